```python
import math
import jax
import jax.numpy as jnp
from jax import lax
import numpy as np

D_MODEL = 1024
BATCH = 2
SEQ = 8192
DEPTH = 2

HEAD_DIM = 64
N_HEADS = D_MODEL // HEAD_DIM
A_HEADS = N_HEADS // 2
B_HEADS = N_HEADS - A_HEADS
B_KV_HEADS = max(1, B_HEADS // 4)
C_HEADS = N_HEADS
C_KV_HEADS = max(1, C_HEADS // 4)
MIX_WIDTH = N_HEADS * HEAD_DIM
A_PATTERNS = ((128, 1), (512, 4), (2048, 16))
A_BLOCK = 64
B_WINDOW = 128
B_BLOCK = 128
C_BLOCK = 128
GRID_W = 64
ROPE_THETA = 10000.0
ROPE_AXIS_DIM = HEAD_DIM // 2
ALIBI_MAX_BIAS = 8.0
RMS_EPS = 1e-6
MOE_GROUPS = 4
MOE_EXPERTS = 8
MOE_TOP_K = 2
MOE_D_FF = D_MODEL // 4
A_W = A_HEADS * HEAD_DIM
B_QW = B_HEADS * HEAD_DIM
B_KVW = B_KV_HEADS * HEAD_DIM
AB_SPLITS = (A_W, 2 * A_W, 3 * A_W, 3 * A_W + B_QW, 3 * A_W + B_QW + B_KVW)
AB_PROJ = 3 * A_W + B_QW + 2 * B_KVW
C_QW = C_HEADS * HEAD_DIM
C_KVW = C_KV_HEADS * HEAD_DIM
C_SPLITS = (C_QW, C_QW + C_KVW)
C_PROJ = C_QW + 2 * C_KVW
N_EVEN = (DEPTH + 1) // 2
N_ODD = DEPTH // 2

kernel_name = 'hybrid_dilated_window_axial_hmoe'


def _rmsnorm(x, g):
    x32 = x.astype(jnp.float32)
    y = x32 * lax.rsqrt(jnp.mean(x32 * x32, axis=-1, keepdims=True) + RMS_EPS)
    return (y * g.astype(jnp.float32)).astype(x.dtype)


def _alibi_slopes(n):
    return jnp.asarray(2.0 ** (-ALIBI_MAX_BIAS * np.arange(1, n + 1) / n), dtype=jnp.float32)


def _banded_attention(q, k, v, half_window, block, dist_scale, slopes, sink=None):
    b, L, H, hd = q.shape
    hkv = k.shape[2]
    grp = H // hkv
    nblk = -(-L // block)
    lp = nblk * block
    kw = block + 2 * half_window
    qp = jnp.pad(q, ((0, 0), (0, lp - L), (0, 0), (0, 0)))
    pad_k = ((0, 0), (half_window, lp - L + half_window), (0, 0), (0, 0))
    kp = jnp.pad(k, pad_k)
    vp = jnp.pad(v, pad_k)
    key_idx = np.arange(nblk)[:, None] * block + np.arange(kw)[None, :]
    kb = kp[:, key_idx]
    vb = vp[:, key_idx]
    qb = qp.reshape(b, nblk, block, hkv, grp, hd)
    s = jnp.einsum('bnqkgd,bnskd->bnkgqs', qb, kb, preferred_element_type=jnp.float32) * (hd ** -0.5)
    qpos = np.arange(nblk)[:, None] * block + np.arange(block)[None, :]
    kpos = key_idx - half_window
    rel = np.abs(kpos[:, None, :] - qpos[:, :, None])
    valid = (rel <= half_window) & (kpos[:, None, :] >= 0) & (kpos[:, None, :] < L)
    dist = jnp.asarray(rel, dtype=jnp.float32) * dist_scale
    slopes_kg = slopes.reshape(hkv, grp)
    s = s - slopes_kg[None, None, :, :, None, None] * dist[None, :, None, None, :, :]
    s = jnp.where(valid[None, :, None, None], s, -jnp.inf)
    m = jnp.max(s, axis=-1, keepdims=True)
    if sink is not None:
        sink_b = sink.astype(jnp.float32).reshape(hkv, grp)[None, None, :, :, None, None]
        m = jnp.maximum(m, sink_b)
    p = jnp.exp(s - m)
    l = jnp.sum(p, axis=-1, keepdims=True)
    if sink is not None:
        l = l + jnp.exp(sink_b - m)
    o = jnp.einsum('bnkgqs,bnskd->bnqkgd', p.astype(v.dtype), vb, preferred_element_type=jnp.float32)
    o = o / jnp.moveaxis(l[..., 0], -1, 2)[..., None]
    lse = jnp.moveaxis((m + jnp.log(l))[..., 0], -1, 2)
    o = o.reshape(b, lp, H, hd)[:, :L]
    lse = lse.reshape(b, lp, H)[:, :L]
    return o, lse


def _to_strided(t, dil):
    b, S, h, hd = t.shape
    return t.reshape(b, S // dil, dil, h, hd).transpose(0, 2, 1, 3, 4).reshape(b * dil, S // dil, h, hd)


def _from_strided(t, b, dil):
    bd, L = t.shape[0], t.shape[1]
    rest = t.shape[2:]
    t = t.reshape((b, dil, L) + rest)
    t = jnp.moveaxis(t, 1, 2)
    return t.reshape((b, L * dil) + rest)


def _dilated_mixture(q, k, v, slopes):
    b = q.shape[0]
    outs, lses = [], []
    for window, dil in A_PATTERNS:
        half = (window // 2) // dil
        o, lse = _banded_attention(_to_strided(q, dil), _to_strided(k, dil), _to_strided(v, dil),
                                   half, A_BLOCK, float(dil), slopes)
        outs.append(_from_strided(o, b, dil))
        lses.append(_from_strided(lse, b, dil))
    w = jax.nn.softmax(jnp.stack(lses, axis=0), axis=0)
    return jnp.sum(w[..., None] * jnp.stack(outs, axis=0), axis=0)


def _mixer_ab(h, w_in, a_qn, a_kn, b_qn, b_kn, b_sink, w_out):
    b, S, _ = h.shape
    proj = jnp.einsum('bsd,de->bse', h, w_in)
    qa, ka, va, qb, kb, vb = jnp.split(proj, AB_SPLITS, axis=-1)
    qa = _rmsnorm(qa.reshape(b, S, A_HEADS, HEAD_DIM), a_qn)
    ka = _rmsnorm(ka.reshape(b, S, A_HEADS, HEAD_DIM), a_kn)
    va = va.reshape(b, S, A_HEADS, HEAD_DIM)
    qb = _rmsnorm(qb.reshape(b, S, B_HEADS, HEAD_DIM), b_qn)
    kb = _rmsnorm(kb.reshape(b, S, B_KV_HEADS, HEAD_DIM), b_kn)
    vb = vb.reshape(b, S, B_KV_HEADS, HEAD_DIM)
    slopes = _alibi_slopes(A_HEADS + B_HEADS)
    oa = _dilated_mixture(qa, ka, va, slopes[0::2])
    ob, _ = _banded_attention(qb, kb, vb, B_WINDOW, B_BLOCK, 1.0, slopes[1::2], sink=b_sink)
    o = jnp.concatenate([oa.reshape(b, S, A_W), ob.reshape(b, S, B_QW)], axis=-1).astype(h.dtype)
    return jnp.einsum('bse,ed->bsd', o, w_out)


def _axial_rope_tables(seq_len):
    rows = seq_len // GRID_W
    row = jnp.broadcast_to(jnp.arange(rows, dtype=jnp.float32)[:, None], (rows, GRID_W)).reshape(-1)
    col = jnp.broadcast_to(jnp.arange(GRID_W, dtype=jnp.float32)[None, :], (rows, GRID_W)).reshape(-1)
    inv_freq = ROPE_THETA ** (-jnp.arange(0, ROPE_AXIS_DIM, 2, dtype=jnp.float32) / ROPE_AXIS_DIM)
    ang_r = row[:, None] * inv_freq[None, :]
    ang_c = col[:, None] * inv_freq[None, :]
    return jnp.cos(ang_r), jnp.sin(ang_r), jnp.cos(ang_c), jnp.sin(ang_c)


def _rotate(t, cos, sin):
    half = t.shape[-1] // 2
    t1, t2 = t[..., :half], t[..., half:]
    c = cos[None, :, None, :]
    s = sin[None, :, None, :]
    return jnp.concatenate([t1 * c - t2 * s, t2 * c + t1 * s], axis=-1)


def _apply_axial_rope(x, cr, sr, cc, sc):
    x32 = x.astype(jnp.float32)
    out = jnp.concatenate([_rotate(x32[..., :ROPE_AXIS_DIM], cr, sr),
                           _rotate(x32[..., ROPE_AXIS_DIM:], cc, sc)], axis=-1)
    return out.astype(x.dtype)


def _dense_blocked_attention(q, k, v):
    b, S, H, hd = q.shape
    hkv = k.shape[2]
    grp = H // hkv
    nq = S // C_BLOCK
    qb = q.reshape(b, nq, C_BLOCK, hkv, grp, hd).transpose(1, 0, 2, 3, 4, 5)

    def one_block(qblk):
        s = jnp.einsum('bqkgd,bskd->bkgqs', qblk, k, preferred_element_type=jnp.float32) * (hd ** -0.5)
        p = jax.nn.softmax(s, axis=-1)
        o = jnp.einsum('bkgqs,bskd->bqkgd', p.astype(v.dtype), v, preferred_element_type=jnp.float32)
        return o.astype(q.dtype)

    o = lax.map(one_block, qb)
    return o.transpose(1, 0, 2, 3, 4, 5).reshape(b, S, H, hd)


def _mixer_c(h, w_in, qn, kn, w_out):
    b, S, _ = h.shape
    proj = jnp.einsum('bsd,de->bse', h, w_in)
    q, k, v = jnp.split(proj, C_SPLITS, axis=-1)
    q = _rmsnorm(q.reshape(b, S, C_HEADS, HEAD_DIM), qn)
    k = _rmsnorm(k.reshape(b, S, C_KV_HEADS, HEAD_DIM), kn)
    v = v.reshape(b, S, C_KV_HEADS, HEAD_DIM)
    cr, sr, cc, sc = _axial_rope_tables(S)
    q = _apply_axial_rope(q, cr, sr, cc, sc)
    k = _apply_axial_rope(k, cr, sr, cc, sc)
    o = _dense_blocked_attention(q, k, v).reshape(b, S, C_QW)
    return jnp.einsum('bse,ed->bsd', o, w_out)


def _hier_moe(h, wg, bg, we, be, w_gate, w_up, w_down):
    b, S, d = h.shape
    xt = h.reshape(b * S, d)
    pg = jax.nn.softmax(jnp.einsum('nd,dg->ng', xt, wg, preferred_element_type=jnp.float32)
                        + bg.astype(jnp.float32), axis=-1)
    g_idx = jnp.argmax(pg, axis=-1)
    onehot_g = jax.nn.one_hot(g_idx, MOE_GROUPS, dtype=jnp.float32)
    p_grp = jnp.sum(pg * onehot_g, axis=-1)
    logits_e = jnp.einsum('nd,gde->nge', xt, we, preferred_element_type=jnp.float32) + be.astype(jnp.float32)[None]
    sel = jnp.einsum('nge,ng->ne', logits_e, onehot_g)
    pe = jax.nn.softmax(sel, axis=-1)
    top_p, top_i = lax.top_k(pe, MOE_TOP_K)
    top_p = top_p / jnp.sum(top_p, axis=-1, keepdims=True)
    w_e = jnp.sum(top_p[..., None] * jax.nn.one_hot(top_i, MOE_EXPERTS, dtype=jnp.float32), axis=1)
    comb = p_grp[:, None, None] * onehot_g[:, :, None] * w_e[:, None, :]
    y = jnp.zeros((b * S, d), jnp.float32)
    for g in range(MOE_GROUPS):
        a = jnp.einsum('nd,edf->nef', xt, w_gate[g], preferred_element_type=jnp.float32)
        u = jnp.einsum('nd,edf->nef', xt, w_up[g], preferred_element_type=jnp.float32)
        act = jax.nn.silu(a) * u * comb[:, g, :, None]
        y = y + jnp.einsum('nef,efd->nd', act.astype(xt.dtype), w_down[g], preferred_element_type=jnp.float32)
    return y.reshape(b, S, d).astype(h.dtype)


def setup_inputs(seed: int = 0) -> dict:
    key = jax.random.key(seed)
    ks = jax.random.split(key, 21)

    def nrm(k, shape, scale):
        return jax.random.normal(k, shape, jnp.float32) * scale

    def gain(k, shape):
        return 1.0 + 0.02 * jax.random.normal(k, shape, jnp.float32)

    return {
        'x': nrm(ks[0], (BATCH, SEQ, D_MODEL), 1.0),
        'mix_norm': gain(ks[1], (DEPTH, D_MODEL)),
        'ffn_norm': gain(ks[2], (DEPTH, D_MODEL)),
        'ab_w_in': nrm(ks[3], (N_EVEN, D_MODEL, AB_PROJ), D_MODEL ** -0.5),
        'a_q_norm': gain(ks[4], (N_EVEN, HEAD_DIM)),
        'a_k_norm': gain(ks[5], (N_EVEN, HEAD_DIM)),
        'b_q_norm': gain(ks[6], (N_EVEN, HEAD_DIM)),
        'b_k_norm': gain(ks[7], (N_EVEN, HEAD_DIM)),
        'b_sink': nrm(ks[8], (N_EVEN, B_HEADS), 0.5),
        'ab_w_out': nrm(ks[9], (N_EVEN, MIX_WIDTH, D_MODEL), MIX_WIDTH ** -0.5),
        'c_w_in': nrm(ks[10], (N_ODD, D_MODEL, C_PROJ), D_MODEL ** -0.5),
        'c_q_norm': gain(ks[11], (N_ODD, HEAD_DIM)),
        'c_k_norm': gain(ks[12], (N_ODD, HEAD_DIM)),
        'c_w_out': nrm(ks[13], (N_ODD, C_QW, D_MODEL), C_QW ** -0.5),
        'moe_group_w': nrm(ks[14], (DEPTH, D_MODEL, MOE_GROUPS), D_MODEL ** -0.5),
        'moe_group_b': nrm(ks[15], (DEPTH, MOE_GROUPS), 0.01),
        'moe_expert_w': nrm(ks[16], (DEPTH, MOE_GROUPS, D_MODEL, MOE_EXPERTS), D_MODEL ** -0.5),
        'moe_expert_b': nrm(ks[17], (DEPTH, MOE_GROUPS, MOE_EXPERTS), 0.01),
        'moe_w_gate': nrm(ks[18], (DEPTH, MOE_GROUPS, MOE_EXPERTS, D_MODEL, MOE_D_FF), D_MODEL ** -0.5),
        'moe_w_up': nrm(ks[19], (DEPTH, MOE_GROUPS, MOE_EXPERTS, D_MODEL, MOE_D_FF), D_MODEL ** -0.5),
        'moe_w_down': nrm(ks[20], (DEPTH, MOE_GROUPS, MOE_EXPERTS, MOE_D_FF, D_MODEL), MOE_D_FF ** -0.5),
    }


def reference(x, mix_norm, ffn_norm, ab_w_in, a_q_norm, a_k_norm, b_q_norm, b_k_norm, b_sink, ab_w_out,
              c_w_in, c_q_norm, c_k_norm, c_w_out, moe_group_w, moe_group_b, moe_expert_w, moe_expert_b,
              moe_w_gate, moe_w_up, moe_w_down):
    for layer in range(DEPTH):
        i = layer // 2
        h = _rmsnorm(x, mix_norm[layer])
        if layer % 2 == 0:
            x = x + _mixer_ab(h, ab_w_in[i], a_q_norm[i], a_k_norm[i], b_q_norm[i], b_k_norm[i],
                              b_sink[i], ab_w_out[i])
        else:
            x = x + _mixer_c(h, c_w_in[i], c_q_norm[i], c_k_norm[i], c_w_out[i])
        h = _rmsnorm(x, ffn_norm[layer])
        x = x + _hier_moe(h, moe_group_w[layer], moe_group_b[layer], moe_expert_w[layer], moe_expert_b[layer],
                          moe_w_gate[layer], moe_w_up[layer], moe_w_down[layer])
    return x
```

```python
import functools
import math

import jax
import jax.numpy as jnp
import numpy as np
from jax import lax
from jax.experimental import pallas as pl
from jax.experimental.pallas import tpu as pltpu

HEAD_DIM = 64
LANES = 128
N_HEADS = 16
A_HEADS = 8
B_HEADS = 8
B_KV_HEADS = 2
C_HEADS = 16
C_KV_HEADS = 4
A_PATTERNS = ((128, 1), (512, 4), (2048, 16))
B_WINDOW = 128
GRID_W = 64
ROPE_THETA = 10000.0
ROPE_AXIS_DIM = HEAD_DIM // 2
ALIBI_MAX_BIAS = 8.0
RMS_EPS = 1e-6
MOE_GROUPS = 4
MOE_EXPERTS = 8
N_EXPERTS = MOE_GROUPS * MOE_EXPERTS
MOE_D_FF = 256
QK_SCALE = HEAD_DIM ** -0.5
NEG_BIG = -1e30
VMEM_LIMIT = 52 * 1024 * 1024

BAND_BQ = 128
DENSE_BQ = 256
DENSE_BK = 512
MOE_TM = 512
ROUTE_E0 = 4

_BF16 = jnp.bfloat16
_F32 = jnp.float32


def _params(*sem):
    return pltpu.CompilerParams(dimension_semantics=sem, vmem_limit_bytes=VMEM_LIMIT)


def _alibi_slopes(n):
    return np.asarray(2.0 ** (-ALIBI_MAX_BIAS * np.arange(1, n + 1) / n), dtype=np.float32)


def _norm_proj_kernel(x_ref, g_ref, w_ref, o_ref):
    x = x_ref[...]
    ms = jnp.mean(x * x, axis=-1, keepdims=True)
    h = (x * lax.rsqrt(ms + RMS_EPS) * g_ref[...]).astype(_BF16)
    o_ref[...] = jnp.dot(h, w_ref[...], preferred_element_type=_F32)


def _norm_proj(x2, g, w, tm=512):
    n, d = x2.shape
    p = w.shape[1]
    return pl.pallas_call(
        _norm_proj_kernel,
        grid=(n // tm,),
        in_specs=[pl.BlockSpec((tm, d), lambda i: (i, 0)),
                  pl.BlockSpec((1, d), lambda i: (0, 0)),
                  pl.BlockSpec((d, p), lambda i: (0, 0))],
        out_specs=pl.BlockSpec((tm, p), lambda i: (i, 0)),
        out_shape=jax.ShapeDtypeStruct((n, p), _F32),
        compiler_params=_params("parallel"),
        name="norm_proj",
    )(x2, g.reshape(1, d), w)


def _head_norm(y, gain):
    lane = lax.broadcasted_iota(jnp.int32, y.shape, 1)
    lo = lane < HEAD_DIM
    y2 = y * y
    ms_lo = jnp.sum(jnp.where(lo, y2, 0.0), axis=-1, keepdims=True) * (1.0 / HEAD_DIM)
    ms_hi = jnp.sum(jnp.where(lo, 0.0, y2), axis=-1, keepdims=True) * (1.0 / HEAD_DIM)
    inv = jnp.where(lo, lax.rsqrt(ms_lo + RMS_EPS), lax.rsqrt(ms_hi + RMS_EPS))
    return y * inv * gain


def _rope(y, cos, sin):
    lane = lax.broadcasted_iota(jnp.int32, y.shape, 1)
    first = (lane % 32) < 16
    partner = jnp.where(first, pltpu.roll(y, LANES - 16, axis=1), pltpu.roll(y, 16, axis=1))
    return y * cos + partner * sin


def _prep_kernel(*refs, ncols, norm, rope, scale, mode, dils, tm):
    it = iter(refs)
    x_ref = next(it)
    g_ref = next(it) if norm else None
    cos_ref = next(it) if rope else None
    sin_ref = next(it) if rope else None
    outs = list(it)
    scr = outs.pop() if mode in ("strided", "stridedT") else None
    for c in range(ncols // LANES):
        sl = slice(c * LANES, (c + 1) * LANES)
        y = x_ref[:, sl]
        if norm:
            y = _head_norm(y, g_ref[...])
        if rope:
            y = _rope(y, cos_ref[...], sin_ref[...])
        if scale != 1.0:
            y = y * scale
        if mode == "plain":
            outs[0][:, sl] = y.astype(_BF16)
        elif mode == "T":
            outs[0][sl, :] = y.T.astype(_BF16)
        elif mode == "heads":
            outs[0][2 * c] = y[:, :HEAD_DIM].astype(_BF16)
            outs[0][2 * c + 1] = y[:, HEAD_DIM:].astype(_BF16)
        else:
            scr[c] = y
    if scr is not None:
        for o_ref, d in zip(outs, dils):
            for r in range(d):
                for c in range(ncols // LANES):
                    sl = slice(c * LANES, (c + 1) * LANES)
                    rows = scr[c, pl.ds(r, tm // d, stride=d), :] if d > 1 else scr[c]
                    if mode == "strided":
                        o_ref[r, :, sl] = rows.astype(_BF16)
                    else:
                        o_ref[r, sl, :] = rows.T.astype(_BF16)


def _prep(proj, bsz, seq, col0, ncols, *, gain=None, rope=None, scale=1.0, mode="plain",
          dils=(), tm=512):
    assert col0 % ncols == 0 and seq % tm == 0
    nt = seq // tm
    cb = col0 // ncols
    ins = [proj]
    in_specs = [pl.BlockSpec((tm, ncols), lambda b, i: (b * nt + i, cb))]
    if gain is not None:
        ins.append(jnp.tile(gain.astype(_F32), LANES // HEAD_DIM).reshape(1, LANES))
        in_specs.append(pl.BlockSpec((1, LANES), lambda b, i: (0, 0)))
    if rope is not None:
        ins += list(rope)
        in_specs += [pl.BlockSpec((tm, LANES), lambda b, i: (i, 0))] * 2
    scratch = []
    if mode == "plain":
        out_shape = [jax.ShapeDtypeStruct((bsz, seq, ncols), _BF16)]
        out_specs = [pl.BlockSpec((None, tm, ncols), lambda b, i: (b, i, 0))]
    elif mode == "T":
        out_shape = [jax.ShapeDtypeStruct((bsz, ncols, seq), _BF16)]
        out_specs = [pl.BlockSpec((None, ncols, tm), lambda b, i: (b, 0, i))]
    elif mode == "heads":
        nh = ncols // HEAD_DIM
        out_shape = [jax.ShapeDtypeStruct((bsz, nh, seq, HEAD_DIM), _BF16)]
        out_specs = [pl.BlockSpec((None, nh, tm, HEAD_DIM), lambda b, i: (b, 0, i, 0))]
    elif mode == "strided":
        out_shape = [jax.ShapeDtypeStruct((bsz, d, seq // d, ncols), _BF16) for d in dils]
        out_specs = [pl.BlockSpec((None, d, tm // d, ncols), lambda b, i: (b, 0, i, 0)) for d in dils]
        scratch = [pltpu.VMEM((ncols // LANES, tm, LANES), _F32)]
    else:
        out_shape = [jax.ShapeDtypeStruct((bsz, d, ncols, seq // d), _BF16) for d in dils]
        out_specs = [pl.BlockSpec((None, d, ncols, tm // d), lambda b, i: (b, 0, 0, i)) for d in dils]
        scratch = [pltpu.VMEM((ncols // LANES, tm, LANES), _F32)]
    kern = functools.partial(_prep_kernel, ncols=ncols, norm=gain is not None, rope=rope is not None,
                             scale=scale, mode=mode, dils=dils, tm=tm)
    out = pl.pallas_call(
        kern, grid=(bsz, nt), in_specs=in_specs, out_specs=out_specs, out_shape=out_shape,
        scratch_shapes=scratch, compiler_params=_params("parallel", "parallel"),
        name="prep_" + mode,
    )(*ins)
    return out if len(out) > 1 else out[0]


def _rope_tables(seq):
    t = np.arange(seq)
    row = (t // GRID_W).astype(np.float32)
    col = (t % GRID_W).astype(np.float32)
    inv_freq = jnp.asarray(ROPE_THETA, _F32) ** (-jnp.arange(0, ROPE_AXIS_DIM, 2, dtype=_F32) / ROPE_AXIS_DIM)
    ang_r = jnp.asarray(row)[:, None] * inv_freq[None, :]
    ang_c = jnp.asarray(col)[:, None] * inv_freq[None, :]
    cr, sr, cc, sc = jnp.cos(ang_r), jnp.sin(ang_r), jnp.cos(ang_c), jnp.sin(ang_c)
    cos = jnp.concatenate([cr, cr, cc, cc], axis=-1)
    sin = jnp.concatenate([-sr, sr, -sc, sc], axis=-1)
    return jnp.tile(cos, (1, 2)), jnp.tile(sin, (1, 2))


def _band_probs(s, q0, seq_len, hw, dist_scale, slopes_rows, sink_rows):
    rows, kw = s.shape
    bq = kw // 3
    qpos = q0 + lax.broadcasted_iota(jnp.int32, (rows, kw), 0) % bq
    kpos = q0 - bq + lax.broadcasted_iota(jnp.int32, (rows, kw), 1)
    rel = jnp.abs(kpos - qpos)
    valid = (rel <= hw) & (kpos >= 0) & (kpos < seq_len)
    s = s - slopes_rows * (rel.astype(_F32) * dist_scale)
    s = jnp.where(valid, s, NEG_BIG)
    m = jnp.max(s, axis=-1, keepdims=True)
    if sink_rows is not None:
        m = jnp.maximum(m, sink_rows)
    p = jnp.exp(s - m)
    l = jnp.sum(p, axis=-1, keepdims=True)
    if sink_rows is not None:
        l = l + jnp.exp(sink_rows - m)
    return p, m, l


def _band_a_kernel(q_ref, kp_ref, kc_ref, kn_ref, vp_ref, vc_ref, vn_ref, o_ref, lse_ref,
                   *, seq_len, hw, dil, slopes):
    bq = q_ref.shape[0]
    q0 = pl.program_id(1) * bq
    kt = jnp.concatenate([kp_ref[...], kc_ref[...], kn_ref[...]], axis=1)
    v = jnp.concatenate([vp_ref[...], vc_ref[...], vn_ref[...]], axis=0)
    q = q_ref[...]
    for h in range(A_HEADS):
        hs = slice(h * HEAD_DIM, (h + 1) * HEAD_DIM)
        s = jnp.dot(q[:, hs], kt[hs, :], preferred_element_type=_F32)
        p, m, l = _band_probs(s, q0, seq_len, hw, float(dil), float(slopes[h]), None)
        o = jnp.dot(p.astype(_BF16), v[:, hs], preferred_element_type=_F32) / l
        o_ref[:, hs] = o
        lse_ref[:, hs] = jnp.broadcast_to(m + jnp.log(l), (bq, HEAD_DIM))


def _band_a(q, kt, v, *, hw, dil, slopes):
    bd, seq_len, w = q.shape
    bq = BAND_BQ
    nb = seq_len // bq
    prev = lambda i: jnp.maximum(i - 1, 0)
    nxt = lambda i: jnp.minimum(i + 1, nb - 1)
    kspec = lambda f: pl.BlockSpec((None, w, bq), lambda b, i: (b, 0, f(i)))
    vspec = lambda f: pl.BlockSpec((None, bq, w), lambda b, i: (b, f(i), 0))
    ident = lambda i: i
    kern = functools.partial(_band_a_kernel, seq_len=seq_len, hw=hw, dil=dil, slopes=slopes)
    return pl.pallas_call(
        kern, grid=(bd, nb),
        in_specs=[vspec(ident), kspec(prev), kspec(ident), kspec(nxt),
                  vspec(prev), vspec(ident), vspec(nxt)],
        out_specs=[vspec(ident), vspec(ident)],
        out_shape=[jax.ShapeDtypeStruct((bd, seq_len, w), _F32)] * 2,
        compiler_params=_params("parallel", "parallel"),
        name="band_a",
    )(q, kt, kt, kt, v, v, v)


def _band_b_kernel(sink_ref, q_ref, kp_ref, kc_ref, kn_ref, vp_ref, vc_ref, vn_ref, o_ref,
                   *, seq_len, hw, slopes):
    bq = q_ref.shape[0]
    g = pl.program_id(1)
    q0 = pl.program_id(2) * bq
    grp = B_HEADS // B_KV_HEADS
    kt = jnp.concatenate([kp_ref[...], kc_ref[...], kn_ref[...]], axis=1)
    v = jnp.concatenate([vp_ref[...], vc_ref[...], vn_ref[...]], axis=0)
    q = q_ref[...]
    q4 = jnp.concatenate([q[:, i * HEAD_DIM:(i + 1) * HEAD_DIM] for i in range(grp)], axis=0)
    s = jnp.dot(q4, kt, preferred_element_type=_F32)
    head = g * grp + lax.broadcasted_iota(jnp.int32, (grp * bq, 1), 0) // bq
    slope_rows = jnp.zeros((grp * bq, 1), _F32)
    sink_rows = jnp.zeros((grp * bq, 1), _F32)
    for h in range(B_HEADS):
        slope_rows = jnp.where(head == h, float(slopes[h]), slope_rows)
        sink_rows = jnp.where(head == h, sink_ref[h], sink_rows)
    p, m, l = _band_probs(s, q0, seq_len, hw, 1.0, slope_rows, sink_rows)
    o = jnp.dot(p.astype(_BF16), v, preferred_element_type=_F32) / l
    o_ref[...] = jnp.concatenate([o[i * bq:(i + 1) * bq] for i in range(grp)], axis=1).astype(_BF16)


def _band_b(q, kt, v, sink, *, hw, slopes):
    bsz, seq_len, w = q.shape
    bq = BAND_BQ
    nb = seq_len // bq
    gw = w // B_KV_HEADS
    prev = lambda i: jnp.maximum(i - 1, 0)
    nxt = lambda i: jnp.minimum(i + 1, nb - 1)
    ident = lambda i: i
    kspec = lambda f: pl.BlockSpec((None, HEAD_DIM, bq), lambda b, g, i: (b, g, f(i)))
    vspec = lambda f: pl.BlockSpec((None, None, bq, HEAD_DIM), lambda b, g, i: (b, g, f(i), 0))
    qspec = pl.BlockSpec((None, bq, gw), lambda b, g, i: (b, i, g))
    kern = functools.partial(_band_b_kernel, seq_len=seq_len, hw=hw, slopes=slopes)
    return pl.pallas_call(
        kern, grid=(bsz, B_KV_HEADS, nb),
        in_specs=[pl.BlockSpec(memory_space=pltpu.SMEM), qspec, kspec(prev), kspec(ident), kspec(nxt),
                  vspec(prev), vspec(ident), vspec(nxt)],
        out_specs=qspec,
        out_shape=jax.ShapeDtypeStruct((bsz, seq_len, w), _BF16),
        compiler_params=_params("parallel", "parallel", "parallel"),
        name="band_b",
    )(sink.astype(_F32), q, kt, kt, kt, v, v, v)


def _mix_a_kernel(o1, l1, o4, l4, o16, l16, out_ref, so4, sl4, so16, sl16, *, tm):
    nc = out_ref.shape[-1] // LANES
    for src, dst, d in ((o4, so4, 4), (l4, sl4, 4), (o16, so16, 16), (l16, sl16, 16)):
        for r in range(d):
            for c in range(nc):
                dst[c, pl.ds(r, tm // d, stride=d), :] = src[r, :, c * LANES:(c + 1) * LANES]
    for c in range(nc):
        sl = slice(c * LANES, (c + 1) * LANES)
        la, lb, lc = l1[0, :, sl], sl4[c], sl16[c]
        m = jnp.maximum(jnp.maximum(la, lb), lc)
        ea, eb, ec = jnp.exp(la - m), jnp.exp(lb - m), jnp.exp(lc - m)
        num = ea * o1[0, :, sl] + eb * so4[c] + ec * so16[c]
        out_ref[:, sl] = (num / (ea + eb + ec)).astype(_BF16)


def _mix_a(branches, bsz, seq, tm=512):
    w = branches[0][0].shape[-1]
    ins, in_specs = [], []
    for (o, lse), (_, d) in zip(branches, A_PATTERNS):
        shp = (bsz, d, seq // d, w)
        spec = pl.BlockSpec((None, d, tm // d, w), lambda b, i: (b, 0, i, 0))
        ins += [o.reshape(shp), lse.reshape(shp)]
        in_specs += [spec, spec]
    return pl.pallas_call(
        functools.partial(_mix_a_kernel, tm=tm), grid=(bsz, seq // tm),
        in_specs=in_specs,
        out_specs=pl.BlockSpec((None, tm, w), lambda b, i: (b, i, 0)),
        out_shape=jax.ShapeDtypeStruct((bsz, seq, w), _BF16),
        scratch_shapes=[pltpu.VMEM((w // LANES, tm, LANES), _F32)] * 4,
        compiler_params=_params("parallel", "parallel"),
        name="mix_a",
    )(*ins)


def _dense_attn_kernel(q_ref, kt_ref, v_ref, o_ref, m_scr, l_scr, acc_scr, *, bk):
    bq = q_ref.shape[0]
    seq = kt_ref.shape[1]
    grp = C_HEADS // C_KV_HEADS
    q = q_ref[...]
    q4 = jnp.concatenate([q[:, i * HEAD_DIM:(i + 1) * HEAD_DIM] for i in range(grp)], axis=0)
    m_scr[...] = jnp.full(m_scr.shape, NEG_BIG, _F32)
    l_scr[...] = jnp.zeros(l_scr.shape, _F32)
    acc_scr[...] = jnp.zeros(acc_scr.shape, _F32)

    def body(j, carry):
        k0 = pl.multiple_of(j * bk, bk)
        s = jnp.dot(q4, kt_ref[:, pl.ds(k0, bk)], preferred_element_type=_F32)
        m_old = m_scr[...]
        m_new = jnp.maximum(m_old, jnp.max(s, axis=-1, keepdims=True))
        alpha = jnp.exp(m_old - m_new)
        p = jnp.exp(s - m_new)
        l_scr[...] = alpha * l_scr[...] + jnp.sum(p, axis=-1, keepdims=True)
        acc_scr[...] = alpha * acc_scr[...] + jnp.dot(
            p.astype(_BF16), v_ref[pl.ds(k0, bk), :], preferred_element_type=_F32)
        m_scr[...] = m_new
        return carry

    lax.fori_loop(0, seq // bk, body, 0)
    o = acc_scr[...] / l_scr[...]
    o_ref[...] = jnp.concatenate([o[i * bq:(i + 1) * bq] for i in range(grp)], axis=1).astype(_BF16)


def _dense_attn(q, kt, v, bq=DENSE_BQ, bk=DENSE_BK):
    bsz, seq, w = q.shape
    bq, bk = min(bq, seq), min(bk, seq)
    grp = C_HEADS // C_KV_HEADS
    gw = grp * HEAD_DIM
    qspec = pl.BlockSpec((None, bq, gw), lambda b, g, i: (b, i, g))
    return pl.pallas_call(
        functools.partial(_dense_attn_kernel, bk=bk), grid=(bsz, C_KV_HEADS, seq // bq),
        in_specs=[qspec,
                  pl.BlockSpec((None, HEAD_DIM, seq), lambda b, g, i: (b, g, 0)),
                  pl.BlockSpec((None, None, seq, HEAD_DIM), lambda b, g, i: (b, g, 0, 0))],
        out_specs=qspec,
        out_shape=jax.ShapeDtypeStruct((bsz, seq, w), _BF16),
        scratch_shapes=[pltpu.VMEM((grp * bq, 1), _F32), pltpu.VMEM((grp * bq, 1), _F32),
                        pltpu.VMEM((grp * bq, HEAD_DIM), _F32)],
        compiler_params=_params("parallel", "parallel", "parallel"),
        name="dense_attn",
    )(q, kt, v)


def _out_proj_kernel(*refs, n_in):
    x_ref = refs[0]
    o_refs = refs[1:1 + n_in]
    w_refs = refs[1 + n_in:1 + 2 * n_in]
    out_ref = refs[-1]
    acc = x_ref[...]
    for o_ref, w_ref in zip(o_refs, w_refs):
        acc = acc + jnp.dot(o_ref[...], w_ref[...], preferred_element_type=_F32)
    out_ref[...] = acc


def _out_proj(x2, os_, ws, tm=512):
    n, d = x2.shape
    in_specs = [pl.BlockSpec((tm, d), lambda i: (i, 0))]
    in_specs += [pl.BlockSpec((tm, o.shape[1]), lambda i: (i, 0)) for o in os_]
    in_specs += [pl.BlockSpec(w.shape, lambda i: (0, 0)) for w in ws]
    return pl.pallas_call(
        functools.partial(_out_proj_kernel, n_in=len(os_)), grid=(n // tm,),
        in_specs=in_specs,
        out_specs=pl.BlockSpec((tm, d), lambda i: (i, 0)),
        out_shape=jax.ShapeDtypeStruct((n, d), _F32),
        compiler_params=_params("parallel"),
        name="out_proj",
    )(x2, *os_, *ws)


def _router_kernel(x_ref, g_ref, w_ref, b_ref, h_ref, route_ref, cnt_ref, carry_ref):
    tm = x_ref.shape[0]

    @pl.when(pl.program_id(0) == 0)
    def _():
        carry_ref[...] = jnp.zeros(carry_ref.shape, _F32)

    x = x_ref[...]
    ms = jnp.mean(x * x, axis=-1, keepdims=True)
    h = x * lax.rsqrt(ms + RMS_EPS) * g_ref[...]
    h_ref[...] = h.astype(_BF16)
    logits = jnp.dot(h, w_ref[...], preferred_element_type=_F32,
                     precision=lax.Precision.HIGHEST) + b_ref[...]
    lane = lax.broadcasted_iota(jnp.int32, (tm, LANES), 1)

    gmask = lane < MOE_GROUPS
    lg = jnp.where(gmask, logits, NEG_BIG)
    mg = jnp.max(lg, axis=-1, keepdims=True)
    zg = jnp.sum(jnp.exp(lg - mg), axis=-1, keepdims=True)
    p_grp = 1.0 / zg
    g_idx = jnp.min(jnp.where(lg == mg, lane, LANES), axis=-1, keepdims=True)

    e_lane = lane - ROUTE_E0
    emask = (e_lane >= 0) & (e_lane < N_EXPERTS) & ((e_lane // MOE_EXPERTS) == g_idx)
    le = jnp.where(emask, logits, NEG_BIG)
    m1 = jnp.max(le, axis=-1, keepdims=True)
    i1 = jnp.min(jnp.where(le == m1, lane, LANES), axis=-1, keepdims=True)
    le2 = jnp.where(lane == i1, NEG_BIG, le)
    m2 = jnp.max(le2, axis=-1, keepdims=True)
    i2 = jnp.min(jnp.where(le2 == m2, lane, LANES), axis=-1, keepdims=True)
    e21 = jnp.exp(m2 - m1)
    c1 = p_grp / (1.0 + e21)
    c2 = p_grp * e21 / (1.0 + e21)

    onehot = jnp.where(lane == i1, 1.0, jnp.where(lane == i2, 1.0, 0.0)).astype(_BF16)
    tri = (lax.broadcasted_iota(jnp.int32, (tm, tm), 0)
           >= lax.broadcasted_iota(jnp.int32, (tm, tm), 1)).astype(_BF16)
    cum = jnp.dot(tri, onehot, preferred_element_type=_F32) + carry_ref[...]
    r1 = jnp.sum(jnp.where(lane == i1, cum, 0.0), axis=-1, keepdims=True) - 1.0
    r2 = jnp.sum(jnp.where(lane == i2, cum, 0.0), axis=-1, keepdims=True) - 1.0
    carry_ref[...] = cum[tm - 1:tm, :]
    cnt_ref[...] = cum[tm - 1:tm, :]

    cols = ((i1 - ROUTE_E0).astype(_F32), (i2 - ROUTE_E0).astype(_F32), c1, c2, r1, r2)
    route = jnp.zeros((tm, LANES), _F32)
    for k, col in enumerate(cols):
        route = jnp.where(lane == k, col, route)
    route_ref[...] = route


def _router(x2, g, w_router, b_router, tm=512):
    n, d = x2.shape
    return pl.pallas_call(
        _router_kernel, grid=(n // tm,),
        in_specs=[pl.BlockSpec((tm, d), lambda i: (i, 0)),
                  pl.BlockSpec((1, d), lambda i: (0, 0)),
                  pl.BlockSpec((d, LANES), lambda i: (0, 0)),
                  pl.BlockSpec((1, LANES), lambda i: (0, 0))],
        out_specs=[pl.BlockSpec((tm, d), lambda i: (i, 0)),
                   pl.BlockSpec((tm, LANES), lambda i: (i, 0)),
                   pl.BlockSpec((1, LANES), lambda i: (0, 0))],
        out_shape=[jax.ShapeDtypeStruct((n, d), _BF16),
                   jax.ShapeDtypeStruct((n, LANES), _F32),
                   jax.ShapeDtypeStruct((1, LANES), _F32)],
        scratch_shapes=[pltpu.VMEM((1, LANES), _F32)],
        compiler_params=_params("arbitrary"),
        name="router",
    )(x2, g.reshape(1, d), w_router, b_router)


def _expert_kernel(te_ref, nv_ref, xs_ref, wg_ref, wu_ref, wd_ref, ys_ref, wgu_s, wd_s):
    j = pl.program_id(0)
    prev = te_ref[jnp.maximum(j - 1, 0)]

    @pl.when((j == 0) | (te_ref[j] != prev))
    def _():
        wgu_s[:, :MOE_D_FF] = wg_ref[...].astype(_BF16)
        wgu_s[:, MOE_D_FF:] = wu_ref[...].astype(_BF16)
        wd_s[...] = wd_ref[...].astype(_BF16)

    @pl.when(j < nv_ref[0])
    def _():
        au = jnp.dot(xs_ref[...], wgu_s[...], preferred_element_type=_F32)
        a, u = au[:, :MOE_D_FF], au[:, MOE_D_FF:]
        act = (a * (1.0 / (1.0 + jnp.exp(-a))) * u).astype(_BF16)
        ys_ref[...] = jnp.dot(act, wd_s[...], preferred_element_type=_F32).astype(_BF16)


def _experts(tile_expert, n_valid, xs, w_gate, w_up, w_down, tm=MOE_TM):
    n_slots, d = xs.shape
    n_tiles = n_slots // tm
    f = w_gate.shape[-1]
    row = lambda j, te, nv: (jnp.minimum(j, nv[0] - 1), 0)
    grid_spec = pltpu.PrefetchScalarGridSpec(
        num_scalar_prefetch=2, grid=(n_tiles,),
        in_specs=[pl.BlockSpec((tm, d), row),
                  pl.BlockSpec((None, d, f), lambda j, te, nv: (te[j], 0, 0)),
                  pl.BlockSpec((None, d, f), lambda j, te, nv: (te[j], 0, 0)),
                  pl.BlockSpec((None, f, d), lambda j, te, nv: (te[j], 0, 0))],
        out_specs=pl.BlockSpec((tm, d), row),
        scratch_shapes=[pltpu.VMEM((d, 2 * f), _BF16), pltpu.VMEM((f, d), _BF16)])
    return pl.pallas_call(
        _expert_kernel, grid_spec=grid_spec,
        out_shape=jax.ShapeDtypeStruct((n_slots, d), _BF16),
        compiler_params=_params("arbitrary"),
        name="experts",
    )(tile_expert, n_valid, xs, w_gate, w_up, w_down)


def _combine_kernel(x_ref, y0_ref, y1_ref, route_ref, o_ref):
    c1 = route_ref[:, 2:3]
    c2 = route_ref[:, 3:4]
    o_ref[...] = x_ref[...] + c1 * y0_ref[...].astype(_F32) + c2 * y1_ref[...].astype(_F32)


def _combine(x2, y0, y1, route, tm=512):
    n, d = x2.shape
    row = pl.BlockSpec((tm, d), lambda i: (i, 0))
    return pl.pallas_call(
        _combine_kernel, grid=(n // tm,),
        in_specs=[row, row, row, pl.BlockSpec((tm, LANES), lambda i: (i, 0))],
        out_specs=row,
        out_shape=jax.ShapeDtypeStruct((n, d), _F32),
        compiler_params=_params("parallel"),
        name="moe_combine",
    )(x2, y0, y1, route)


def _moe(x2, g, wg, bg, we, be, w_gate, w_up, w_down):
    n, d = x2.shape
    tm = MOE_TM
    w_router = jnp.zeros((d, LANES), _F32)
    w_router = w_router.at[:, :MOE_GROUPS].set(wg)
    w_router = w_router.at[:, ROUTE_E0:ROUTE_E0 + N_EXPERTS].set(
        jnp.moveaxis(we, 0, 1).reshape(d, N_EXPERTS))
    b_router = jnp.zeros((1, LANES), _F32)
    b_router = b_router.at[0, :MOE_GROUPS].set(bg)
    b_router = b_router.at[0, ROUTE_E0:ROUTE_E0 + N_EXPERTS].set(be.reshape(-1))
    h, route, cnt = _router(x2, g, w_router, b_router)

    expert = route[:, 0:2].astype(jnp.int32)
    rank = route[:, 4:6].astype(jnp.int32)
    counts = cnt[0, ROUTE_E0:ROUTE_E0 + N_EXPERTS].astype(jnp.int32)
    tiles_per = (counts + tm - 1) // tm
    tiles_end = jnp.cumsum(tiles_per)
    offset = (tiles_end - tiles_per) * tm
    pos = offset[expert] + rank
    n_tiles = (2 * n) // tm + N_EXPERTS
    n_valid = tiles_end[-1]
    tile_ids = jnp.minimum(jnp.arange(n_tiles, dtype=jnp.int32), n_valid - 1)
    tile_expert = jnp.sum(tile_ids[:, None] >= tiles_end[None, :], axis=1).astype(jnp.int32)
    token = jnp.broadcast_to(jnp.arange(n, dtype=jnp.int32)[:, None], (n, 2))
    slot_token = jnp.zeros((n_tiles * tm,), jnp.int32).at[pos.reshape(-1)].set(token.reshape(-1))

    xs = jnp.take(h, slot_token, axis=0)
    ys = _experts(tile_expert, n_valid.reshape(1), xs,
                  w_gate.reshape(N_EXPERTS, d, MOE_D_FF), w_up.reshape(N_EXPERTS, d, MOE_D_FF),
                  w_down.reshape(N_EXPERTS, MOE_D_FF, d))
    y0 = jnp.take(ys, pos[:, 0], axis=0)
    y1 = jnp.take(ys, pos[:, 1], axis=0)
    return _combine(x2, y0, y1, route)


def _mixer_ab(x2, bsz, seq, g, w_in, a_qn, a_kn, b_qn, b_kn, b_sink, w_out):
    aw = A_HEADS * HEAD_DIM
    bqw = B_HEADS * HEAD_DIM
    bkw = B_KV_HEADS * HEAD_DIM
    proj = _norm_proj(x2, g, w_in.astype(_BF16))
    dils = tuple(d for _, d in A_PATTERNS)
    qa = _prep(proj, bsz, seq, 0, aw, gain=a_qn, scale=QK_SCALE, mode="strided", dils=dils)
    ka = _prep(proj, bsz, seq, aw, aw, gain=a_kn, mode="stridedT", dils=dils, tm=2048)
    va = _prep(proj, bsz, seq, 2 * aw, aw, mode="strided", dils=dils)
    qb = _prep(proj, bsz, seq, 3 * aw, bqw, gain=b_qn, scale=QK_SCALE, mode="plain")
    kb = _prep(proj, bsz, seq, 3 * aw + bqw, bkw, gain=b_kn, mode="T")
    vb = _prep(proj, bsz, seq, 3 * aw + bqw + bkw, bkw, mode="heads")

    slopes = _alibi_slopes(A_HEADS + B_HEADS)
    branches = []
    for (window, d), q_d, k_d, v_d in zip(A_PATTERNS, qa, ka, va):
        ld = seq // d
        branches.append(_band_a(q_d.reshape(bsz * d, ld, aw), k_d.reshape(bsz * d, aw, ld),
                                v_d.reshape(bsz * d, ld, aw),
                                hw=(window // 2) // d, dil=d, slopes=slopes[0::2]))
    oa = _mix_a(branches, bsz, seq)
    ob = _band_b(qb, kb, vb, b_sink, hw=B_WINDOW, slopes=slopes[1::2])
    w_out = w_out.astype(_BF16)
    return _out_proj(x2, [oa.reshape(-1, aw), ob.reshape(-1, bqw)], [w_out[:aw], w_out[aw:]])


def _mixer_c(x2, bsz, seq, g, w_in, qn, kn, w_out):
    qw = C_HEADS * HEAD_DIM
    kvw = C_KV_HEADS * HEAD_DIM
    proj = _norm_proj(x2, g, w_in.astype(_BF16))
    rope = _rope_tables(seq)
    q = _prep(proj, bsz, seq, 0, qw, gain=qn, rope=rope, scale=QK_SCALE, mode="plain")
    kt = _prep(proj, bsz, seq, qw, kvw, gain=kn, rope=rope, mode="T")
    v = _prep(proj, bsz, seq, qw + kvw, kvw, mode="heads")
    o = _dense_attn(q, kt, v)
    return _out_proj(x2, [o.reshape(-1, qw)], [w_out.astype(_BF16)])


def kernel(x, mix_norm, ffn_norm, ab_w_in, a_q_norm, a_k_norm, b_q_norm, b_k_norm, b_sink, ab_w_out,
           c_w_in, c_q_norm, c_k_norm, c_w_out, moe_group_w, moe_group_b, moe_expert_w, moe_expert_b,
           moe_w_gate, moe_w_up, moe_w_down):
    bsz, seq, d = x.shape
    x2 = x.reshape(bsz * seq, d)
    depth = mix_norm.shape[0]
    for layer in range(depth):
        i = layer // 2
        if layer % 2 == 0:
            x2 = _mixer_ab(x2, bsz, seq, mix_norm[layer], ab_w_in[i], a_q_norm[i], a_k_norm[i],
                           b_q_norm[i], b_k_norm[i], b_sink[i], ab_w_out[i])
        else:
            x2 = _mixer_c(x2, bsz, seq, mix_norm[layer], c_w_in[i], c_q_norm[i], c_k_norm[i],
                          c_w_out[i])
        x2 = _moe(x2, ffn_norm[layer], moe_group_w[layer], moe_group_b[layer], moe_expert_w[layer],
                  moe_expert_b[layer], moe_w_gate[layer], moe_w_up[layer], moe_w_down[layer])
    return x2.reshape(bsz, seq, d)
```

```python
import functools
import math

import jax
import jax.numpy as jnp
import numpy as np
from jax import lax
from jax.experimental import pallas as pl
from jax.experimental.pallas import tpu as pltpu

HEAD_DIM = 64
LANES = 128
N_HEADS = 16
A_HEADS = 8
B_HEADS = 8
B_KV_HEADS = 2
C_HEADS = 16
C_KV_HEADS = 4
A_PATTERNS = ((128, 1), (512, 4), (2048, 16))
B_WINDOW = 128
GRID_W = 64
ROPE_THETA = 10000.0
ROPE_AXIS_DIM = HEAD_DIM // 2
ALIBI_MAX_BIAS = 8.0
RMS_EPS = 1e-6
MOE_GROUPS = 4
MOE_EXPERTS = 8
N_EXPERTS = MOE_GROUPS * MOE_EXPERTS
MOE_D_FF = 256
QK_SCALE = HEAD_DIM ** -0.5
LOG2E = math.log2(math.e)
SHIFT_MAX = 60.0
NEG_BIG = -1e30
VMEM_LIMIT = 52 * 1024 * 1024

BAND_BQ = 128
DENSE_BQ = 256
DENSE_BK = 2048
DENSE_BK_ONLINE = 512
MOE_TM = 512
ROUTE_E0 = 4

_BF16 = jnp.bfloat16
_F32 = jnp.float32


def _params(*sem):
    return pltpu.CompilerParams(dimension_semantics=sem, vmem_limit_bytes=VMEM_LIMIT)


def _alibi_slopes(n):
    return np.asarray(2.0 ** (-ALIBI_MAX_BIAS * np.arange(1, n + 1) / n), dtype=np.float32)


def _norm_proj_kernel(x_ref, g_ref, w_ref, o_ref):
    x = x_ref[...]
    ms = jnp.mean(x * x, axis=-1, keepdims=True)
    h = (x * lax.rsqrt(ms + RMS_EPS) * g_ref[...]).astype(_BF16)
    o_ref[...] = jnp.dot(h, w_ref[...], preferred_element_type=_F32)


def _norm_proj(x2, g, w, tm=512):
    n, d = x2.shape
    p = w.shape[1]
    return pl.pallas_call(
        _norm_proj_kernel,
        grid=(n // tm,),
        in_specs=[pl.BlockSpec((tm, d), lambda i: (i, 0)),
                  pl.BlockSpec((1, d), lambda i: (0, 0)),
                  pl.BlockSpec((d, p), lambda i: (0, 0))],
        out_specs=pl.BlockSpec((tm, p), lambda i: (i, 0)),
        out_shape=jax.ShapeDtypeStruct((n, p), _F32),
        compiler_params=_params("parallel"),
        name="norm_proj",
    )(x2, g.reshape(1, d), w)


def _head_norm(y, gain):
    lane = lax.broadcasted_iota(jnp.int32, y.shape, 1)
    lo = lane < HEAD_DIM
    y2 = y * y
    ms_lo = jnp.sum(jnp.where(lo, y2, 0.0), axis=-1, keepdims=True) * (1.0 / HEAD_DIM)
    ms_hi = jnp.sum(jnp.where(lo, 0.0, y2), axis=-1, keepdims=True) * (1.0 / HEAD_DIM)
    inv = jnp.where(lo, lax.rsqrt(ms_lo + RMS_EPS), lax.rsqrt(ms_hi + RMS_EPS))
    return y * inv * gain


def _rope(y, cos, sin):
    lane = lax.broadcasted_iota(jnp.int32, y.shape, 1)
    first = (lane % 32) < 16
    partner = jnp.where(first, pltpu.roll(y, LANES - 16, axis=1), pltpu.roll(y, 16, axis=1))
    return y * cos + partner * sin


def _prep_kernel(*refs, ncols, norm, rope, scale, mode, dils, tm):
    it = iter(refs)
    shift_ref = next(it) if mode == "T_ext" else None
    x_ref = next(it)
    g_ref = next(it) if norm else None
    cos_ref = next(it) if rope else None
    sin_ref = next(it) if rope else None
    outs = list(it)
    scr = outs.pop() if mode in ("strided", "stridedT") else None
    for c in range(ncols // LANES):
        sl = slice(c * LANES, (c + 1) * LANES)
        y = x_ref[:, sl]
        if norm:
            y = _head_norm(y, g_ref[...])
        if rope:
            y = _rope(y, cos_ref[...], sin_ref[...])
        if scale != 1.0:
            y = y * scale
        if mode == "plain":
            outs[0][:, sl] = y.astype(_BF16)
        elif mode == "T":
            outs[0][sl, :] = y.T.astype(_BF16)
        elif mode == "heads":
            outs[0][2 * c] = y[:, :HEAD_DIM].astype(_BF16)
            outs[0][2 * c + 1] = y[:, HEAD_DIM:].astype(_BF16)
        elif mode in ("plain_ext", "heads_ext"):
            lane = lax.broadcasted_iota(jnp.int32, y.shape, 1)
            if mode == "plain_ext":
                fill = jnp.where(lane == HEAD_DIM, 1.0, 0.0)
            else:
                fill = jnp.ones(y.shape, _F32)
            for k, yk in enumerate((y, pltpu.roll(y, HEAD_DIM, axis=1))):
                ext = jnp.where(lane < HEAD_DIM, yk, fill).astype(_BF16)
                if mode == "plain_ext":
                    outs[0][:, (2 * c + k) * LANES:(2 * c + k + 1) * LANES] = ext
                else:
                    outs[0][2 * c + k] = ext
        elif mode == "T_ext":
            yt = y.T
            row = lax.broadcasted_iota(jnp.int32, (HEAD_DIM, tm), 0)
            extra = jnp.where(row == 0, -shift_ref[0], 0.0)
            for k in range(2):
                ext = jnp.concatenate([yt[k * HEAD_DIM:(k + 1) * HEAD_DIM], extra], axis=0)
                outs[0][(2 * c + k) * LANES:(2 * c + k + 1) * LANES, :] = ext.astype(_BF16)
        else:
            scr[c] = y
    if scr is not None:
        for o_ref, d in zip(outs, dils):
            for r in range(d):
                for c in range(ncols // LANES):
                    sl = slice(c * LANES, (c + 1) * LANES)
                    rows = scr[c, pl.ds(r, tm // d, stride=d), :] if d > 1 else scr[c]
                    if mode == "strided":
                        o_ref[r, :, sl] = rows.astype(_BF16)
                    else:
                        o_ref[r, sl, :] = rows.T.astype(_BF16)


def _prep(proj, bsz, seq, col0, ncols, *, gain=None, rope=None, scale=1.0, mode="plain",
          dils=(), tm=512, shift=None):
    assert col0 % ncols == 0 and seq % tm == 0
    nt = seq // tm
    cb = col0 // ncols
    ins = [proj]
    in_specs = [pl.BlockSpec((tm, ncols), lambda b, i: (b * nt + i, cb))]
    if mode == "T_ext":
        ins.insert(0, shift.reshape(1).astype(_F32))
        in_specs.insert(0, pl.BlockSpec(memory_space=pltpu.SMEM))
    if gain is not None:
        ins.append(jnp.tile(gain.astype(_F32), LANES // HEAD_DIM).reshape(1, LANES))
        in_specs.append(pl.BlockSpec((1, LANES), lambda b, i: (0, 0)))
    if rope is not None:
        ins += list(rope)
        in_specs += [pl.BlockSpec((tm, LANES), lambda b, i: (i, 0))] * 2
    scratch = []
    if mode == "plain":
        out_shape = [jax.ShapeDtypeStruct((bsz, seq, ncols), _BF16)]
        out_specs = [pl.BlockSpec((None, tm, ncols), lambda b, i: (b, i, 0))]
    elif mode == "T":
        out_shape = [jax.ShapeDtypeStruct((bsz, ncols, seq), _BF16)]
        out_specs = [pl.BlockSpec((None, ncols, tm), lambda b, i: (b, 0, i))]
    elif mode == "heads":
        nh = ncols // HEAD_DIM
        out_shape = [jax.ShapeDtypeStruct((bsz, nh, seq, HEAD_DIM), _BF16)]
        out_specs = [pl.BlockSpec((None, nh, tm, HEAD_DIM), lambda b, i: (b, 0, i, 0))]
    elif mode == "plain_ext":
        out_shape = [jax.ShapeDtypeStruct((bsz, seq, 2 * ncols), _BF16)]
        out_specs = [pl.BlockSpec((None, tm, 2 * ncols), lambda b, i: (b, i, 0))]
    elif mode == "T_ext":
        out_shape = [jax.ShapeDtypeStruct((bsz, 2 * ncols, seq), _BF16)]
        out_specs = [pl.BlockSpec((None, 2 * ncols, tm), lambda b, i: (b, 0, i))]
    elif mode == "heads_ext":
        nh = ncols // HEAD_DIM
        out_shape = [jax.ShapeDtypeStruct((bsz, nh, seq, LANES), _BF16)]
        out_specs = [pl.BlockSpec((None, nh, tm, LANES), lambda b, i: (b, 0, i, 0))]
    elif mode == "strided":
        out_shape = [jax.ShapeDtypeStruct((bsz, d, seq // d, ncols), _BF16) for d in dils]
        out_specs = [pl.BlockSpec((None, d, tm // d, ncols), lambda b, i: (b, 0, i, 0)) for d in dils]
        scratch = [pltpu.VMEM((ncols // LANES, tm, LANES), _F32)]
    else:
        out_shape = [jax.ShapeDtypeStruct((bsz, d, ncols, seq // d), _BF16) for d in dils]
        out_specs = [pl.BlockSpec((None, d, ncols, tm // d), lambda b, i: (b, 0, 0, i)) for d in dils]
        scratch = [pltpu.VMEM((ncols // LANES, tm, LANES), _F32)]
    kern = functools.partial(_prep_kernel, ncols=ncols, norm=gain is not None, rope=rope is not None,
                             scale=scale, mode=mode, dils=dils, tm=tm)
    out = pl.pallas_call(
        kern, grid=(bsz, nt), in_specs=in_specs, out_specs=out_specs, out_shape=out_shape,
        scratch_shapes=scratch, compiler_params=_params("parallel", "parallel"),
        name="prep_" + mode,
    )(*ins)
    return out if len(out) > 1 else out[0]


def _rope_tables(seq):
    t = np.arange(seq)
    row = (t // GRID_W).astype(np.float32)
    col = (t % GRID_W).astype(np.float32)
    inv_freq = jnp.asarray(ROPE_THETA, _F32) ** (-jnp.arange(0, ROPE_AXIS_DIM, 2, dtype=_F32) / ROPE_AXIS_DIM)
    ang_r = jnp.asarray(row)[:, None] * inv_freq[None, :]
    ang_c = jnp.asarray(col)[:, None] * inv_freq[None, :]
    cr, sr, cc, sc = jnp.cos(ang_r), jnp.sin(ang_r), jnp.cos(ang_c), jnp.sin(ang_c)
    cos = jnp.concatenate([cr, cr, cc, cc], axis=-1)
    sin = jnp.concatenate([-sr, sr, -sc, sc], axis=-1)
    return jnp.tile(cos, (1, 2)), jnp.tile(sin, (1, 2))


def _band_probs(s, q0, seq_len, hw, dist_scale, slopes_rows, sink_rows):
    rows, kw = s.shape
    bq = kw // 3
    qpos = q0 + lax.broadcasted_iota(jnp.int32, (rows, kw), 0) % bq
    kpos = q0 - bq + lax.broadcasted_iota(jnp.int32, (rows, kw), 1)
    rel = jnp.abs(kpos - qpos)
    valid = (rel <= hw) & (kpos >= 0) & (kpos < seq_len)
    s = s - slopes_rows * (rel.astype(_F32) * dist_scale)
    s = jnp.where(valid, s, NEG_BIG)
    m = jnp.max(s, axis=-1, keepdims=True)
    if sink_rows is not None:
        m = jnp.maximum(m, sink_rows)
    p = jnp.exp(s - m)
    l = jnp.sum(p, axis=-1, keepdims=True)
    if sink_rows is not None:
        l = l + jnp.exp(sink_rows - m)
    return p, m, l


def _band_a_kernel(q_ref, kp_ref, kc_ref, kn_ref, vp_ref, vc_ref, vn_ref, o_ref, lse_ref,
                   *, seq_len, hw, dil, slopes):
    bq = q_ref.shape[0]
    q0 = pl.program_id(1) * bq
    kt = jnp.concatenate([kp_ref[...], kc_ref[...], kn_ref[...]], axis=1)
    v = jnp.concatenate([vp_ref[...], vc_ref[...], vn_ref[...]], axis=0)
    q = q_ref[...]
    for h in range(A_HEADS):
        hs = slice(h * HEAD_DIM, (h + 1) * HEAD_DIM)
        s = jnp.dot(q[:, hs], kt[hs, :], preferred_element_type=_F32)
        p, m, l = _band_probs(s, q0, seq_len, hw, float(dil), float(slopes[h]), None)
        o = jnp.dot(p.astype(_BF16), v[:, hs], preferred_element_type=_F32) / l
        o_ref[:, hs] = o
        lse_ref[:, hs] = jnp.broadcast_to(m + jnp.log(l), (bq, HEAD_DIM))


def _band_a(q, kt, v, *, hw, dil, slopes):
    bd, seq_len, w = q.shape
    bq = BAND_BQ
    nb = seq_len // bq
    prev = lambda i: jnp.maximum(i - 1, 0)
    nxt = lambda i: jnp.minimum(i + 1, nb - 1)
    kspec = lambda f: pl.BlockSpec((None, w, bq), lambda b, i: (b, 0, f(i)))
    vspec = lambda f: pl.BlockSpec((None, bq, w), lambda b, i: (b, f(i), 0))
    ident = lambda i: i
    kern = functools.partial(_band_a_kernel, seq_len=seq_len, hw=hw, dil=dil, slopes=slopes)
    return pl.pallas_call(
        kern, grid=(bd, nb),
        in_specs=[vspec(ident), kspec(prev), kspec(ident), kspec(nxt),
                  vspec(prev), vspec(ident), vspec(nxt)],
        out_specs=[vspec(ident), vspec(ident)],
        out_shape=[jax.ShapeDtypeStruct((bd, seq_len, w), _F32)] * 2,
        compiler_params=_params("parallel", "parallel"),
        name="band_a",
    )(q, kt, kt, kt, v, v, v)


def _band_b_kernel(sink_ref, q_ref, kp_ref, kc_ref, kn_ref, vp_ref, vc_ref, vn_ref, o_ref,
                   *, seq_len, hw, slopes):
    bq = q_ref.shape[0]
    g = pl.program_id(1)
    q0 = pl.program_id(2) * bq
    grp = B_HEADS // B_KV_HEADS
    kt = jnp.concatenate([kp_ref[...], kc_ref[...], kn_ref[...]], axis=1)
    v = jnp.concatenate([vp_ref[...], vc_ref[...], vn_ref[...]], axis=0)
    q = q_ref[...]
    q4 = jnp.concatenate([q[:, i * HEAD_DIM:(i + 1) * HEAD_DIM] for i in range(grp)], axis=0)
    s = jnp.dot(q4, kt, preferred_element_type=_F32)
    head = g * grp + lax.broadcasted_iota(jnp.int32, (grp * bq, 1), 0) // bq
    slope_rows = jnp.zeros((grp * bq, 1), _F32)
    sink_rows = jnp.zeros((grp * bq, 1), _F32)
    for h in range(B_HEADS):
        slope_rows = jnp.where(head == h, float(slopes[h]), slope_rows)
        sink_rows = jnp.where(head == h, sink_ref[h], sink_rows)
    p, m, l = _band_probs(s, q0, seq_len, hw, 1.0, slope_rows, sink_rows)
    o = jnp.dot(p.astype(_BF16), v, preferred_element_type=_F32) / l
    o_ref[...] = jnp.concatenate([o[i * bq:(i + 1) * bq] for i in range(grp)], axis=1).astype(_BF16)


def _band_b(q, kt, v, sink, *, hw, slopes):
    bsz, seq_len, w = q.shape
    bq = BAND_BQ
    nb = seq_len // bq
    gw = w // B_KV_HEADS
    prev = lambda i: jnp.maximum(i - 1, 0)
    nxt = lambda i: jnp.minimum(i + 1, nb - 1)
    ident = lambda i: i
    kspec = lambda f: pl.BlockSpec((None, HEAD_DIM, bq), lambda b, g, i: (b, g, f(i)))
    vspec = lambda f: pl.BlockSpec((None, None, bq, HEAD_DIM), lambda b, g, i: (b, g, f(i), 0))
    qspec = pl.BlockSpec((None, bq, gw), lambda b, g, i: (b, i, g))
    kern = functools.partial(_band_b_kernel, seq_len=seq_len, hw=hw, slopes=slopes)
    return pl.pallas_call(
        kern, grid=(bsz, B_KV_HEADS, nb),
        in_specs=[pl.BlockSpec(memory_space=pltpu.SMEM), qspec, kspec(prev), kspec(ident), kspec(nxt),
                  vspec(prev), vspec(ident), vspec(nxt)],
        out_specs=qspec,
        out_shape=jax.ShapeDtypeStruct((bsz, seq_len, w), _BF16),
        compiler_params=_params("parallel", "parallel", "parallel"),
        name="band_b",
    )(sink.astype(_F32), q, kt, kt, kt, v, v, v)


def _mix_a_kernel(o1, l1, o4, l4, o16, l16, out_ref, so4, sl4, so16, sl16, *, tm):
    nc = out_ref.shape[-1] // LANES
    for src, dst, d in ((o4, so4, 4), (l4, sl4, 4), (o16, so16, 16), (l16, sl16, 16)):
        for r in range(d):
            for c in range(nc):
                dst[c, pl.ds(r, tm // d, stride=d), :] = src[r, :, c * LANES:(c + 1) * LANES]
    for c in range(nc):
        sl = slice(c * LANES, (c + 1) * LANES)
        la, lb, lc = l1[0, :, sl], sl4[c], sl16[c]
        m = jnp.maximum(jnp.maximum(la, lb), lc)
        ea, eb, ec = jnp.exp(la - m), jnp.exp(lb - m), jnp.exp(lc - m)
        num = ea * o1[0, :, sl] + eb * so4[c] + ec * so16[c]
        out_ref[:, sl] = (num / (ea + eb + ec)).astype(_BF16)


def _mix_a(branches, bsz, seq, tm=512):
    w = branches[0][0].shape[-1]
    ins, in_specs = [], []
    for (o, lse), (_, d) in zip(branches, A_PATTERNS):
        shp = (bsz, d, seq // d, w)
        spec = pl.BlockSpec((None, d, tm // d, w), lambda b, i: (b, 0, i, 0))
        ins += [o.reshape(shp), lse.reshape(shp)]
        in_specs += [spec, spec]
    return pl.pallas_call(
        functools.partial(_mix_a_kernel, tm=tm), grid=(bsz, seq // tm),
        in_specs=in_specs,
        out_specs=pl.BlockSpec((None, tm, w), lambda b, i: (b, i, 0)),
        out_shape=jax.ShapeDtypeStruct((bsz, seq, w), _BF16),
        scratch_shapes=[pltpu.VMEM((w // LANES, tm, LANES), _F32)] * 4,
        compiler_params=_params("parallel", "parallel"),
        name="mix_a",
    )(*ins)


def _dense_attn_kernel(q_ref, kt_ref, v_ref, o_ref, acc_scr, *rest, bk, online):
    seq = kt_ref.shape[1]
    grp = C_HEADS // C_KV_HEADS
    acc_scr[...] = jnp.zeros(acc_scr.shape, _F32)
    if online:
        m_scr, = rest
        m_scr[...] = jnp.full(m_scr.shape, NEG_BIG, _F32)

    def body(j, carry):
        k0 = pl.multiple_of(j * bk, bk)
        kt = kt_ref[:, pl.ds(k0, bk)]
        v = v_ref[pl.ds(k0, bk), :]
        for c in range(grp):
            s = jnp.dot(q_ref[:, c * LANES:(c + 1) * LANES], kt, preferred_element_type=_F32)
            if online:
                m_old = m_scr[c]
                m_new = jnp.maximum(m_old, jnp.max(s, axis=-1, keepdims=True))
                m_scr[c] = m_new
                p = jnp.exp2(s - m_new).astype(_BF16)
                acc_scr[c] = jnp.exp2(m_old - m_new) * acc_scr[c] + jnp.dot(
                    p, v, preferred_element_type=_F32)
            else:
                p = jnp.exp2(s).astype(_BF16)
                acc_scr[c] += jnp.dot(p, v, preferred_element_type=_F32)
        return carry

    lax.fori_loop(0, seq // bk, body, 0)
    for c in range(grp):
        a = acc_scr[c]
        o = a / pltpu.roll(a, HEAD_DIM, axis=1)
        o_ref[:, c * HEAD_DIM:(c + 1) * HEAD_DIM] = o[:, :HEAD_DIM].astype(_BF16)


def _dense_attn(q, kt, v, *, online, bq=DENSE_BQ, bk=DENSE_BK):
    bsz, seq, _ = q.shape
    bq, bk = min(bq, seq), min(DENSE_BK_ONLINE if online else bk, seq)
    grp = C_HEADS // C_KV_HEADS
    scratch = [pltpu.VMEM((grp, bq, LANES), _F32)]
    if online:
        scratch.append(pltpu.VMEM((grp, bq, 1), _F32))
    return pl.pallas_call(
        functools.partial(_dense_attn_kernel, bk=bk, online=online),
        grid=(bsz, C_KV_HEADS, seq // bq),
        in_specs=[pl.BlockSpec((None, bq, grp * LANES), lambda b, g, i: (b, i, g)),
                  pl.BlockSpec((None, LANES, seq), lambda b, g, i: (b, g, 0)),
                  pl.BlockSpec((None, None, seq, LANES), lambda b, g, i: (b, g, 0, 0))],
        out_specs=pl.BlockSpec((None, bq, grp * HEAD_DIM), lambda b, g, i: (b, i, g)),
        out_shape=jax.ShapeDtypeStruct((bsz, seq, C_HEADS * HEAD_DIM), _BF16),
        scratch_shapes=scratch,
        compiler_params=_params("parallel", "parallel", "parallel"),
        name="dense_attn_online" if online else "dense_attn",
    )(q, kt, v)


def _out_proj_kernel(*refs, n_in):
    x_ref = refs[0]
    o_refs = refs[1:1 + n_in]
    w_refs = refs[1 + n_in:1 + 2 * n_in]
    out_ref = refs[-1]
    acc = x_ref[...]
    for o_ref, w_ref in zip(o_refs, w_refs):
        acc = acc + jnp.dot(o_ref[...], w_ref[...], preferred_element_type=_F32)
    out_ref[...] = acc


def _out_proj(x2, os_, ws, tm=512):
    n, d = x2.shape
    in_specs = [pl.BlockSpec((tm, d), lambda i: (i, 0))]
    in_specs += [pl.BlockSpec((tm, o.shape[1]), lambda i: (i, 0)) for o in os_]
    in_specs += [pl.BlockSpec(w.shape, lambda i: (0, 0)) for w in ws]
    return pl.pallas_call(
        functools.partial(_out_proj_kernel, n_in=len(os_)), grid=(n // tm,),
        in_specs=in_specs,
        out_specs=pl.BlockSpec((tm, d), lambda i: (i, 0)),
        out_shape=jax.ShapeDtypeStruct((n, d), _F32),
        compiler_params=_params("parallel"),
        name="out_proj",
    )(x2, *os_, *ws)


def _router_kernel(x_ref, g_ref, w_ref, b_ref, h_ref, route_ref, cnt_ref, carry_ref):
    tm = x_ref.shape[0]

    @pl.when(pl.program_id(0) == 0)
    def _():
        carry_ref[...] = jnp.zeros(carry_ref.shape, _F32)

    x = x_ref[...]
    ms = jnp.mean(x * x, axis=-1, keepdims=True)
    h = x * lax.rsqrt(ms + RMS_EPS) * g_ref[...]
    h_ref[...] = h.astype(_BF16)
    logits = jnp.dot(h, w_ref[...], preferred_element_type=_F32,
                     precision=lax.Precision.HIGHEST) + b_ref[...]
    lane = lax.broadcasted_iota(jnp.int32, (tm, LANES), 1)

    gmask = lane < MOE_GROUPS
    lg = jnp.where(gmask, logits, NEG_BIG)
    mg = jnp.max(lg, axis=-1, keepdims=True)
    zg = jnp.sum(jnp.exp(lg - mg), axis=-1, keepdims=True)
    p_grp = 1.0 / zg
    g_idx = jnp.min(jnp.where(lg == mg, lane, LANES), axis=-1, keepdims=True)

    e_lane = lane - ROUTE_E0
    emask = (e_lane >= 0) & (e_lane < N_EXPERTS) & ((e_lane // MOE_EXPERTS) == g_idx)
    le = jnp.where(emask, logits, NEG_BIG)
    m1 = jnp.max(le, axis=-1, keepdims=True)
    i1 = jnp.min(jnp.where(le == m1, lane, LANES), axis=-1, keepdims=True)
    le2 = jnp.where(lane == i1, NEG_BIG, le)
    m2 = jnp.max(le2, axis=-1, keepdims=True)
    i2 = jnp.min(jnp.where(le2 == m2, lane, LANES), axis=-1, keepdims=True)
    e21 = jnp.exp(m2 - m1)
    c1 = p_grp / (1.0 + e21)
    c2 = p_grp * e21 / (1.0 + e21)

    onehot = jnp.where(lane == i1, 1.0, jnp.where(lane == i2, 1.0, 0.0)).astype(_BF16)
    tri = (lax.broadcasted_iota(jnp.int32, (tm, tm), 0)
           >= lax.broadcasted_iota(jnp.int32, (tm, tm), 1)).astype(_BF16)
    cum = jnp.dot(tri, onehot, preferred_element_type=_F32) + carry_ref[...]
    r1 = jnp.sum(jnp.where(lane == i1, cum, 0.0), axis=-1, keepdims=True) - 1.0
    r2 = jnp.sum(jnp.where(lane == i2, cum, 0.0), axis=-1, keepdims=True) - 1.0
    carry_ref[...] = cum[tm - 1:tm, :]
    cnt_ref[...] = cum[tm - 1:tm, :]

    cols = ((i1 - ROUTE_E0).astype(_F32), (i2 - ROUTE_E0).astype(_F32), c1, c2, r1, r2)
    route = jnp.zeros((tm, LANES), _F32)
    for k, col in enumerate(cols):
        route = jnp.where(lane == k, col, route)
    route_ref[...] = route


def _router(x2, g, w_router, b_router, tm=512):
    n, d = x2.shape
    return pl.pallas_call(
        _router_kernel, grid=(n // tm,),
        in_specs=[pl.BlockSpec((tm, d), lambda i: (i, 0)),
                  pl.BlockSpec((1, d), lambda i: (0, 0)),
                  pl.BlockSpec((d, LANES), lambda i: (0, 0)),
                  pl.BlockSpec((1, LANES), lambda i: (0, 0))],
        out_specs=[pl.BlockSpec((tm, d), lambda i: (i, 0)),
                   pl.BlockSpec((tm, LANES), lambda i: (i, 0)),
                   pl.BlockSpec((1, LANES), lambda i: (0, 0))],
        out_shape=[jax.ShapeDtypeStruct((n, d), _BF16),
                   jax.ShapeDtypeStruct((n, LANES), _F32),
                   jax.ShapeDtypeStruct((1, LANES), _F32)],
        scratch_shapes=[pltpu.VMEM((1, LANES), _F32)],
        compiler_params=_params("arbitrary"),
        name="router",
    )(x2, g.reshape(1, d), w_router, b_router)


def _expert_kernel(te_ref, nv_ref, xs_ref, wg_ref, wu_ref, wd_ref, ys_ref, wgu_s, wd_s):
    j = pl.program_id(0)
    prev = te_ref[jnp.maximum(j - 1, 0)]

    @pl.when((j == 0) | (te_ref[j] != prev))
    def _():
        wgu_s[:, :MOE_D_FF] = wg_ref[...].astype(_BF16)
        wgu_s[:, MOE_D_FF:] = wu_ref[...].astype(_BF16)
        wd_s[...] = wd_ref[...].astype(_BF16)

    @pl.when(j < nv_ref[0])
    def _():
        au = jnp.dot(xs_ref[...], wgu_s[...], preferred_element_type=_F32)
        a, u = au[:, :MOE_D_FF], au[:, MOE_D_FF:]
        act = (a * (1.0 / (1.0 + jnp.exp(-a))) * u).astype(_BF16)
        ys_ref[...] = jnp.dot(act, wd_s[...], preferred_element_type=_F32).astype(_BF16)


def _experts(tile_expert, n_valid, xs, w_gate, w_up, w_down, tm=MOE_TM):
    n_slots, d = xs.shape
    n_tiles = n_slots // tm
    f = w_gate.shape[-1]
    row = lambda j, te, nv: (jnp.minimum(j, nv[0] - 1), 0)
    grid_spec = pltpu.PrefetchScalarGridSpec(
        num_scalar_prefetch=2, grid=(n_tiles,),
        in_specs=[pl.BlockSpec((tm, d), row),
                  pl.BlockSpec((None, d, f), lambda j, te, nv: (te[j], 0, 0)),
                  pl.BlockSpec((None, d, f), lambda j, te, nv: (te[j], 0, 0)),
                  pl.BlockSpec((None, f, d), lambda j, te, nv: (te[j], 0, 0))],
        out_specs=pl.BlockSpec((tm, d), row),
        scratch_shapes=[pltpu.VMEM((d, 2 * f), _BF16), pltpu.VMEM((f, d), _BF16)])
    return pl.pallas_call(
        _expert_kernel, grid_spec=grid_spec,
        out_shape=jax.ShapeDtypeStruct((n_slots, d), _BF16),
        compiler_params=_params("arbitrary"),
        name="experts",
    )(tile_expert, n_valid, xs, w_gate, w_up, w_down)


def _combine_kernel(x_ref, y0_ref, y1_ref, route_ref, o_ref):
    c1 = route_ref[:, 2:3]
    c2 = route_ref[:, 3:4]
    o_ref[...] = x_ref[...] + c1 * y0_ref[...].astype(_F32) + c2 * y1_ref[...].astype(_F32)


def _combine(x2, y0, y1, route, tm=512):
    n, d = x2.shape
    row = pl.BlockSpec((tm, d), lambda i: (i, 0))
    return pl.pallas_call(
        _combine_kernel, grid=(n // tm,),
        in_specs=[row, row, row, pl.BlockSpec((tm, LANES), lambda i: (i, 0))],
        out_specs=row,
        out_shape=jax.ShapeDtypeStruct((n, d), _F32),
        compiler_params=_params("parallel"),
        name="moe_combine",
    )(x2, y0, y1, route)


def _moe(x2, g, wg, bg, we, be, w_gate, w_up, w_down):
    n, d = x2.shape
    tm = MOE_TM
    w_router = jnp.zeros((d, LANES), _F32)
    w_router = w_router.at[:, :MOE_GROUPS].set(wg)
    w_router = w_router.at[:, ROUTE_E0:ROUTE_E0 + N_EXPERTS].set(
        jnp.moveaxis(we, 0, 1).reshape(d, N_EXPERTS))
    b_router = jnp.zeros((1, LANES), _F32)
    b_router = b_router.at[0, :MOE_GROUPS].set(bg)
    b_router = b_router.at[0, ROUTE_E0:ROUTE_E0 + N_EXPERTS].set(be.reshape(-1))
    h, route, cnt = _router(x2, g, w_router, b_router)

    expert = route[:, 0:2].astype(jnp.int32)
    rank = route[:, 4:6].astype(jnp.int32)
    counts = cnt[0, ROUTE_E0:ROUTE_E0 + N_EXPERTS].astype(jnp.int32)
    tiles_per = (counts + tm - 1) // tm
    tiles_end = jnp.cumsum(tiles_per)
    offset = (tiles_end - tiles_per) * tm
    pos = offset[expert] + rank
    n_tiles = (2 * n) // tm + N_EXPERTS
    n_valid = tiles_end[-1]
    tile_ids = jnp.minimum(jnp.arange(n_tiles, dtype=jnp.int32), n_valid - 1)
    tile_expert = jnp.sum(tile_ids[:, None] >= tiles_end[None, :], axis=1).astype(jnp.int32)
    token = jnp.broadcast_to(jnp.arange(n, dtype=jnp.int32)[:, None], (n, 2))
    slot_token = jnp.zeros((n_tiles * tm,), jnp.int32).at[pos.reshape(-1)].set(token.reshape(-1))

    xs = jnp.take(h, slot_token, axis=0)
    ys = _experts(tile_expert, n_valid.reshape(1), xs,
                  w_gate.reshape(N_EXPERTS, d, MOE_D_FF), w_up.reshape(N_EXPERTS, d, MOE_D_FF),
                  w_down.reshape(N_EXPERTS, MOE_D_FF, d))
    y0 = jnp.take(ys, pos[:, 0], axis=0)
    y1 = jnp.take(ys, pos[:, 1], axis=0)
    return _combine(x2, y0, y1, route)


def _mixer_ab(x2, bsz, seq, g, w_in, a_qn, a_kn, b_qn, b_kn, b_sink, w_out):
    aw = A_HEADS * HEAD_DIM
    bqw = B_HEADS * HEAD_DIM
    bkw = B_KV_HEADS * HEAD_DIM
    proj = _norm_proj(x2, g, w_in.astype(_BF16))
    dils = tuple(d for _, d in A_PATTERNS)
    qa = _prep(proj, bsz, seq, 0, aw, gain=a_qn, scale=QK_SCALE, mode="strided", dils=dils)
    ka = _prep(proj, bsz, seq, aw, aw, gain=a_kn, mode="stridedT", dils=dils, tm=2048)
    va = _prep(proj, bsz, seq, 2 * aw, aw, mode="strided", dils=dils)
    qb = _prep(proj, bsz, seq, 3 * aw, bqw, gain=b_qn, scale=QK_SCALE, mode="plain")
    kb = _prep(proj, bsz, seq, 3 * aw + bqw, bkw, gain=b_kn, mode="T")
    vb = _prep(proj, bsz, seq, 3 * aw + bqw + bkw, bkw, mode="heads")

    slopes = _alibi_slopes(A_HEADS + B_HEADS)
    branches = []
    for (window, d), q_d, k_d, v_d in zip(A_PATTERNS, qa, ka, va):
        ld = seq // d
        branches.append(_band_a(q_d.reshape(bsz * d, ld, aw), k_d.reshape(bsz * d, aw, ld),
                                v_d.reshape(bsz * d, ld, aw),
                                hw=(window // 2) // d, dil=d, slopes=slopes[0::2]))
    oa = _mix_a(branches, bsz, seq)
    ob = _band_b(qb, kb, vb, b_sink, hw=B_WINDOW, slopes=slopes[1::2])
    w_out = w_out.astype(_BF16)
    return _out_proj(x2, [oa.reshape(-1, aw), ob.reshape(-1, bqw)], [w_out[:aw], w_out[aw:]])


def _mixer_c(x2, bsz, seq, g, w_in, qn, kn, w_out):
    qw = C_HEADS * HEAD_DIM
    kvw = C_KV_HEADS * HEAD_DIM
    proj = _norm_proj(x2, g, w_in.astype(_BF16))
    rope = _rope_tables(seq)
    bound = HEAD_DIM * QK_SCALE * LOG2E * jnp.max(jnp.abs(qn)) * jnp.max(jnp.abs(kn))
    static_ok = bound <= SHIFT_MAX
    shift = jnp.where(static_ok, bound, 0.0)
    q = _prep(proj, bsz, seq, 0, qw, gain=qn, rope=rope, scale=QK_SCALE * LOG2E, mode="plain_ext")
    kt = _prep(proj, bsz, seq, qw, kvw, gain=kn, rope=rope, mode="T_ext", shift=shift)
    v = _prep(proj, bsz, seq, qw + kvw, kvw, mode="heads_ext")
    o = lax.cond(static_ok,
                 functools.partial(_dense_attn, online=False),
                 functools.partial(_dense_attn, online=True), q, kt, v)
    return _out_proj(x2, [o.reshape(-1, qw)], [w_out.astype(_BF16)])


def kernel(x, mix_norm, ffn_norm, ab_w_in, a_q_norm, a_k_norm, b_q_norm, b_k_norm, b_sink, ab_w_out,
           c_w_in, c_q_norm, c_k_norm, c_w_out, moe_group_w, moe_group_b, moe_expert_w, moe_expert_b,
           moe_w_gate, moe_w_up, moe_w_down):
    bsz, seq, d = x.shape
    x2 = x.reshape(bsz * seq, d)
    depth = mix_norm.shape[0]
    for layer in range(depth):
        i = layer // 2
        if layer % 2 == 0:
            x2 = _mixer_ab(x2, bsz, seq, mix_norm[layer], ab_w_in[i], a_q_norm[i], a_k_norm[i],
                           b_q_norm[i], b_k_norm[i], b_sink[i], ab_w_out[i])
        else:
            x2 = _mixer_c(x2, bsz, seq, mix_norm[layer], c_w_in[i], c_q_norm[i], c_k_norm[i],
                          c_w_out[i])
        x2 = _moe(x2, ffn_norm[layer], moe_group_w[layer], moe_group_b[layer], moe_expert_w[layer],
                  moe_expert_b[layer], moe_w_gate[layer], moe_w_up[layer], moe_w_down[layer])
    return x2.reshape(bsz, seq, d)
```

```python
import functools
import math

import jax
import jax.numpy as jnp
import numpy as np
from jax import lax
from jax.experimental import pallas as pl
from jax.experimental.pallas import tpu as pltpu
from jax.experimental.pallas import tpu_sc as plsc

HEAD_DIM = 64
LANES = 128
N_HEADS = 16
A_HEADS = 8
B_HEADS = 8
B_KV_HEADS = 2
C_HEADS = 16
C_KV_HEADS = 4
A_PATTERNS = ((128, 1), (512, 4), (2048, 16))
B_WINDOW = 128
GRID_W = 64
ROPE_THETA = 10000.0
ROPE_AXIS_DIM = HEAD_DIM // 2
ALIBI_MAX_BIAS = 8.0
RMS_EPS = 1e-6
MOE_GROUPS = 4
MOE_EXPERTS = 8
N_EXPERTS = MOE_GROUPS * MOE_EXPERTS
MOE_D_FF = 256
QK_SCALE = HEAD_DIM ** -0.5
LOG2E = math.log2(math.e)
SHIFT_MAX = 60.0
NEG_BIG = -1e30
VMEM_LIMIT = 52 * 1024 * 1024

BAND_BQ = 128
DENSE_BQ = 256
DENSE_BK = 2048
DENSE_BK_ONLINE = 512
MOE_TM = 512
ROUTE_E0 = 4
PACK_W = 256
SC_WINDOW = 128

_BF16 = jnp.bfloat16
_F32 = jnp.float32


def _params(*sem):
    return pltpu.CompilerParams(dimension_semantics=sem, vmem_limit_bytes=VMEM_LIMIT)


def _alibi_slopes(n):
    return np.asarray(2.0 ** (-ALIBI_MAX_BIAS * np.arange(1, n + 1) / n), dtype=np.float32)


def _norm_proj_kernel(x_ref, g_ref, w_ref, o_ref):
    x = x_ref[...]
    ms = jnp.mean(x * x, axis=-1, keepdims=True)
    h = (x * lax.rsqrt(ms + RMS_EPS) * g_ref[...]).astype(_BF16)
    o_ref[...] = jnp.dot(h, w_ref[...], preferred_element_type=_F32)


def _norm_proj(x2, g, w, tm=512):
    n, d = x2.shape
    p = w.shape[1]
    return pl.pallas_call(
        _norm_proj_kernel,
        grid=(n // tm,),
        in_specs=[pl.BlockSpec((tm, d), lambda i: (i, 0)),
                  pl.BlockSpec((1, d), lambda i: (0, 0)),
                  pl.BlockSpec((d, p), lambda i: (0, 0))],
        out_specs=pl.BlockSpec((tm, p), lambda i: (i, 0)),
        out_shape=jax.ShapeDtypeStruct((n, p), _F32),
        compiler_params=_params("parallel"),
        name="norm_proj",
    )(x2, g.reshape(1, d), w)


def _head_norm(y, gain):
    lane = lax.broadcasted_iota(jnp.int32, y.shape, 1)
    lo = lane < HEAD_DIM
    y2 = y * y
    ms_lo = jnp.sum(jnp.where(lo, y2, 0.0), axis=-1, keepdims=True) * (1.0 / HEAD_DIM)
    ms_hi = jnp.sum(jnp.where(lo, 0.0, y2), axis=-1, keepdims=True) * (1.0 / HEAD_DIM)
    inv = jnp.where(lo, lax.rsqrt(ms_lo + RMS_EPS), lax.rsqrt(ms_hi + RMS_EPS))
    return y * inv * gain


def _rope(y, cos, sin):
    lane = lax.broadcasted_iota(jnp.int32, y.shape, 1)
    first = (lane % 32) < 16
    partner = jnp.where(first, pltpu.roll(y, LANES - 16, axis=1), pltpu.roll(y, 16, axis=1))
    return y * cos + partner * sin


def _prep_kernel(*refs, ncols, norm, rope, scale, mode, dils, tm):
    it = iter(refs)
    shift_ref = next(it) if mode == "T_ext" else None
    x_ref = next(it)
    g_ref = next(it) if norm else None
    cos_ref = next(it) if rope else None
    sin_ref = next(it) if rope else None
    outs = list(it)
    scr = outs.pop() if mode in ("strided", "stridedT") else None
    for c in range(ncols // LANES):
        sl = slice(c * LANES, (c + 1) * LANES)
        y = x_ref[:, sl]
        if norm:
            y = _head_norm(y, g_ref[...])
        if rope:
            y = _rope(y, cos_ref[...], sin_ref[...])
        if scale != 1.0:
            y = y * scale
        if mode == "plain":
            outs[0][:, sl] = y.astype(_BF16)
        elif mode == "T":
            outs[0][sl, :] = y.T.astype(_BF16)
        elif mode == "heads":
            outs[0][2 * c] = y[:, :HEAD_DIM].astype(_BF16)
            outs[0][2 * c + 1] = y[:, HEAD_DIM:].astype(_BF16)
        elif mode in ("plain_ext", "heads_ext"):
            lane = lax.broadcasted_iota(jnp.int32, y.shape, 1)
            if mode == "plain_ext":
                fill = jnp.where(lane == HEAD_DIM, 1.0, 0.0)
            else:
                fill = jnp.ones(y.shape, _F32)
            for k, yk in enumerate((y, pltpu.roll(y, HEAD_DIM, axis=1))):
                ext = jnp.where(lane < HEAD_DIM, yk, fill).astype(_BF16)
                if mode == "plain_ext":
                    outs[0][:, (2 * c + k) * LANES:(2 * c + k + 1) * LANES] = ext
                else:
                    outs[0][2 * c + k] = ext
        elif mode == "T_ext":
            yt = y.T
            row = lax.broadcasted_iota(jnp.int32, (HEAD_DIM, tm), 0)
            extra = jnp.where(row == 0, -shift_ref[0], 0.0)
            for k in range(2):
                ext = jnp.concatenate([yt[k * HEAD_DIM:(k + 1) * HEAD_DIM], extra], axis=0)
                outs[0][(2 * c + k) * LANES:(2 * c + k + 1) * LANES, :] = ext.astype(_BF16)
        else:
            scr[c] = y
    if scr is not None:
        for o_ref, d in zip(outs, dils):
            for r in range(d):
                for c in range(ncols // LANES):
                    sl = slice(c * LANES, (c + 1) * LANES)
                    rows = scr[c, pl.ds(r, tm // d, stride=d), :] if d > 1 else scr[c]
                    if mode == "strided":
                        o_ref[r, :, sl] = rows.astype(_BF16)
                    else:
                        o_ref[r, sl, :] = rows.T.astype(_BF16)


def _prep(proj, bsz, seq, col0, ncols, *, gain=None, rope=None, scale=1.0, mode="plain",
          dils=(), tm=512, shift=None):
    assert col0 % ncols == 0 and seq % tm == 0
    nt = seq // tm
    cb = col0 // ncols
    ins = [proj]
    in_specs = [pl.BlockSpec((tm, ncols), lambda b, i: (b * nt + i, cb))]
    if mode == "T_ext":
        ins.insert(0, shift.reshape(1).astype(_F32))
        in_specs.insert(0, pl.BlockSpec(memory_space=pltpu.SMEM))
    if gain is not None:
        ins.append(jnp.tile(gain.astype(_F32), LANES // HEAD_DIM).reshape(1, LANES))
        in_specs.append(pl.BlockSpec((1, LANES), lambda b, i: (0, 0)))
    if rope is not None:
        ins += list(rope)
        in_specs += [pl.BlockSpec((tm, LANES), lambda b, i: (i, 0))] * 2
    scratch = []
    if mode == "plain":
        out_shape = [jax.ShapeDtypeStruct((bsz, seq, ncols), _BF16)]
        out_specs = [pl.BlockSpec((None, tm, ncols), lambda b, i: (b, i, 0))]
    elif mode == "T":
        out_shape = [jax.ShapeDtypeStruct((bsz, ncols, seq), _BF16)]
        out_specs = [pl.BlockSpec((None, ncols, tm), lambda b, i: (b, 0, i))]
    elif mode == "heads":
        nh = ncols // HEAD_DIM
        out_shape = [jax.ShapeDtypeStruct((bsz, nh, seq, HEAD_DIM), _BF16)]
        out_specs = [pl.BlockSpec((None, nh, tm, HEAD_DIM), lambda b, i: (b, 0, i, 0))]
    elif mode == "plain_ext":
        out_shape = [jax.ShapeDtypeStruct((bsz, seq, 2 * ncols), _BF16)]
        out_specs = [pl.BlockSpec((None, tm, 2 * ncols), lambda b, i: (b, i, 0))]
    elif mode == "T_ext":
        out_shape = [jax.ShapeDtypeStruct((bsz, 2 * ncols, seq), _BF16)]
        out_specs = [pl.BlockSpec((None, 2 * ncols, tm), lambda b, i: (b, 0, i))]
    elif mode == "heads_ext":
        nh = ncols // HEAD_DIM
        out_shape = [jax.ShapeDtypeStruct((bsz, nh, seq, LANES), _BF16)]
        out_specs = [pl.BlockSpec((None, nh, tm, LANES), lambda b, i: (b, 0, i, 0))]
    elif mode == "strided":
        out_shape = [jax.ShapeDtypeStruct((bsz, d, seq // d, ncols), _BF16) for d in dils]
        out_specs = [pl.BlockSpec((None, d, tm // d, ncols), lambda b, i: (b, 0, i, 0)) for d in dils]
        scratch = [pltpu.VMEM((ncols // LANES, tm, LANES), _F32)]
    else:
        out_shape = [jax.ShapeDtypeStruct((bsz, d, ncols, seq // d), _BF16) for d in dils]
        out_specs = [pl.BlockSpec((None, d, ncols, tm // d), lambda b, i: (b, 0, 0, i)) for d in dils]
        scratch = [pltpu.VMEM((ncols // LANES, tm, LANES), _F32)]
    kern = functools.partial(_prep_kernel, ncols=ncols, norm=gain is not None, rope=rope is not None,
                             scale=scale, mode=mode, dils=dils, tm=tm)
    out = pl.pallas_call(
        kern, grid=(bsz, nt), in_specs=in_specs, out_specs=out_specs, out_shape=out_shape,
        scratch_shapes=scratch, compiler_params=_params("parallel", "parallel"),
        name="prep_" + mode,
    )(*ins)
    return out if len(out) > 1 else out[0]


def _rope_tables(seq):
    t = np.arange(seq)
    row = (t // GRID_W).astype(np.float32)
    col = (t % GRID_W).astype(np.float32)
    inv_freq = jnp.asarray(ROPE_THETA, _F32) ** (-jnp.arange(0, ROPE_AXIS_DIM, 2, dtype=_F32) / ROPE_AXIS_DIM)
    ang_r = jnp.asarray(row)[:, None] * inv_freq[None, :]
    ang_c = jnp.asarray(col)[:, None] * inv_freq[None, :]
    cr, sr, cc, sc = jnp.cos(ang_r), jnp.sin(ang_r), jnp.cos(ang_c), jnp.sin(ang_c)
    cos = jnp.concatenate([cr, cr, cc, cc], axis=-1)
    sin = jnp.concatenate([-sr, sr, -sc, sc], axis=-1)
    return jnp.tile(cos, (1, 2)), jnp.tile(sin, (1, 2))


def _band_probs(s, q0, seq_len, hw, dist_scale, slopes_rows, sink_rows):
    rows, kw = s.shape
    bq = kw // 3
    qpos = q0 + lax.broadcasted_iota(jnp.int32, (rows, kw), 0) % bq
    kpos = q0 - bq + lax.broadcasted_iota(jnp.int32, (rows, kw), 1)
    rel = jnp.abs(kpos - qpos)
    valid = (rel <= hw) & (kpos >= 0) & (kpos < seq_len)
    s = s - slopes_rows * (rel.astype(_F32) * dist_scale)
    s = jnp.where(valid, s, NEG_BIG)
    m = jnp.max(s, axis=-1, keepdims=True)
    if sink_rows is not None:
        m = jnp.maximum(m, sink_rows)
    p = jnp.exp(s - m)
    l = jnp.sum(p, axis=-1, keepdims=True)
    if sink_rows is not None:
        l = l + jnp.exp(sink_rows - m)
    return p, m, l


def _band_a_kernel(q_ref, kp_ref, kc_ref, kn_ref, vp_ref, vc_ref, vn_ref, o_ref, lse_ref,
                   *, seq_len, hw, dil, slopes):
    bq = q_ref.shape[0]
    q0 = pl.program_id(1) * bq
    kt = jnp.concatenate([kp_ref[...], kc_ref[...], kn_ref[...]], axis=1)
    v = jnp.concatenate([vp_ref[...], vc_ref[...], vn_ref[...]], axis=0)
    q = q_ref[...]
    for h in range(A_HEADS):
        hs = slice(h * HEAD_DIM, (h + 1) * HEAD_DIM)
        s = jnp.dot(q[:, hs], kt[hs, :], preferred_element_type=_F32)
        p, m, l = _band_probs(s, q0, seq_len, hw, float(dil), float(slopes[h]), None)
        o = jnp.dot(p.astype(_BF16), v[:, hs], preferred_element_type=_F32) / l
        o_ref[:, hs] = o
        lse_ref[:, hs] = jnp.broadcast_to(m + jnp.log(l), (bq, HEAD_DIM))


def _band_a(q, kt, v, *, hw, dil, slopes):
    bd, seq_len, w = q.shape
    bq = BAND_BQ
    nb = seq_len // bq
    prev = lambda i: jnp.maximum(i - 1, 0)
    nxt = lambda i: jnp.minimum(i + 1, nb - 1)
    kspec = lambda f: pl.BlockSpec((None, w, bq), lambda b, i: (b, 0, f(i)))
    vspec = lambda f: pl.BlockSpec((None, bq, w), lambda b, i: (b, f(i), 0))
    ident = lambda i: i
    kern = functools.partial(_band_a_kernel, seq_len=seq_len, hw=hw, dil=dil, slopes=slopes)
    return pl.pallas_call(
        kern, grid=(bd, nb),
        in_specs=[vspec(ident), kspec(prev), kspec(ident), kspec(nxt),
                  vspec(prev), vspec(ident), vspec(nxt)],
        out_specs=[vspec(ident), vspec(ident)],
        out_shape=[jax.ShapeDtypeStruct((bd, seq_len, w), _F32)] * 2,
        compiler_params=_params("parallel", "parallel"),
        name="band_a",
    )(q, kt, kt, kt, v, v, v)


def _band_b_kernel(sink_ref, q_ref, kp_ref, kc_ref, kn_ref, vp_ref, vc_ref, vn_ref, o_ref,
                   *, seq_len, hw, slopes):
    bq = q_ref.shape[0]
    g = pl.program_id(1)
    q0 = pl.program_id(2) * bq
    grp = B_HEADS // B_KV_HEADS
    kt = jnp.concatenate([kp_ref[...], kc_ref[...], kn_ref[...]], axis=1)
    v = jnp.concatenate([vp_ref[...], vc_ref[...], vn_ref[...]], axis=0)
    q = q_ref[...]
    q4 = jnp.concatenate([q[:, i * HEAD_DIM:(i + 1) * HEAD_DIM] for i in range(grp)], axis=0)
    s = jnp.dot(q4, kt, preferred_element_type=_F32)
    head = g * grp + lax.broadcasted_iota(jnp.int32, (grp * bq, 1), 0) // bq
    slope_rows = jnp.zeros((grp * bq, 1), _F32)
    sink_rows = jnp.zeros((grp * bq, 1), _F32)
    for h in range(B_HEADS):
        slope_rows = jnp.where(head == h, float(slopes[h]), slope_rows)
        sink_rows = jnp.where(head == h, sink_ref[h], sink_rows)
    p, m, l = _band_probs(s, q0, seq_len, hw, 1.0, slope_rows, sink_rows)
    o = jnp.dot(p.astype(_BF16), v, preferred_element_type=_F32) / l
    o_ref[...] = jnp.concatenate([o[i * bq:(i + 1) * bq] for i in range(grp)], axis=1).astype(_BF16)


def _band_b(q, kt, v, sink, *, hw, slopes):
    bsz, seq_len, w = q.shape
    bq = BAND_BQ
    nb = seq_len // bq
    gw = w // B_KV_HEADS
    prev = lambda i: jnp.maximum(i - 1, 0)
    nxt = lambda i: jnp.minimum(i + 1, nb - 1)
    ident = lambda i: i
    kspec = lambda f: pl.BlockSpec((None, HEAD_DIM, bq), lambda b, g, i: (b, g, f(i)))
    vspec = lambda f: pl.BlockSpec((None, None, bq, HEAD_DIM), lambda b, g, i: (b, g, f(i), 0))
    qspec = pl.BlockSpec((None, bq, gw), lambda b, g, i: (b, i, g))
    kern = functools.partial(_band_b_kernel, seq_len=seq_len, hw=hw, slopes=slopes)
    return pl.pallas_call(
        kern, grid=(bsz, B_KV_HEADS, nb),
        in_specs=[pl.BlockSpec(memory_space=pltpu.SMEM), qspec, kspec(prev), kspec(ident), kspec(nxt),
                  vspec(prev), vspec(ident), vspec(nxt)],
        out_specs=qspec,
        out_shape=jax.ShapeDtypeStruct((bsz, seq_len, w), _BF16),
        compiler_params=_params("parallel", "parallel", "parallel"),
        name="band_b",
    )(sink.astype(_F32), q, kt, kt, kt, v, v, v)


def _mix_a_kernel(o1, l1, o4, l4, o16, l16, out_ref, so4, sl4, so16, sl16, *, tm):
    nc = out_ref.shape[-1] // LANES
    for src, dst, d in ((o4, so4, 4), (l4, sl4, 4), (o16, so16, 16), (l16, sl16, 16)):
        for r in range(d):
            for c in range(nc):
                dst[c, pl.ds(r, tm // d, stride=d), :] = src[r, :, c * LANES:(c + 1) * LANES]
    for c in range(nc):
        sl = slice(c * LANES, (c + 1) * LANES)
        la, lb, lc = l1[0, :, sl], sl4[c], sl16[c]
        m = jnp.maximum(jnp.maximum(la, lb), lc)
        ea, eb, ec = jnp.exp(la - m), jnp.exp(lb - m), jnp.exp(lc - m)
        num = ea * o1[0, :, sl] + eb * so4[c] + ec * so16[c]
        out_ref[:, sl] = (num / (ea + eb + ec)).astype(_BF16)


def _mix_a(branches, bsz, seq, tm=512):
    w = branches[0][0].shape[-1]
    ins, in_specs = [], []
    for (o, lse), (_, d) in zip(branches, A_PATTERNS):
        shp = (bsz, d, seq // d, w)
        spec = pl.BlockSpec((None, d, tm // d, w), lambda b, i: (b, 0, i, 0))
        ins += [o.reshape(shp), lse.reshape(shp)]
        in_specs += [spec, spec]
    return pl.pallas_call(
        functools.partial(_mix_a_kernel, tm=tm), grid=(bsz, seq // tm),
        in_specs=in_specs,
        out_specs=pl.BlockSpec((None, tm, w), lambda b, i: (b, i, 0)),
        out_shape=jax.ShapeDtypeStruct((bsz, seq, w), _BF16),
        scratch_shapes=[pltpu.VMEM((w // LANES, tm, LANES), _F32)] * 4,
        compiler_params=_params("parallel", "parallel"),
        name="mix_a",
    )(*ins)


def _dense_attn_kernel(q_ref, kt_ref, v_ref, o_ref, acc_scr, *rest, bk, online):
    seq = kt_ref.shape[1]
    grp = C_HEADS // C_KV_HEADS
    acc_scr[...] = jnp.zeros(acc_scr.shape, _F32)
    if online:
        m_scr, = rest
        m_scr[...] = jnp.full(m_scr.shape, NEG_BIG, _F32)

    def body(j, carry):
        k0 = pl.multiple_of(j * bk, bk)
        kt = kt_ref[:, pl.ds(k0, bk)]
        v = v_ref[pl.ds(k0, bk), :]
        for c in range(grp):
            s = jnp.dot(q_ref[:, c * LANES:(c + 1) * LANES], kt, preferred_element_type=_F32)
            if online:
                m_old = m_scr[c]
                m_new = jnp.maximum(m_old, jnp.max(s, axis=-1, keepdims=True))
                m_scr[c] = m_new
                p = jnp.exp2(s - m_new).astype(_BF16)
                acc_scr[c] = jnp.exp2(m_old - m_new) * acc_scr[c] + jnp.dot(
                    p, v, preferred_element_type=_F32)
            else:
                p = jnp.exp2(s).astype(_BF16)
                acc_scr[c] += jnp.dot(p, v, preferred_element_type=_F32)
        return carry

    lax.fori_loop(0, seq // bk, body, 0)
    for c in range(grp):
        a = acc_scr[c]
        o = a / pltpu.roll(a, HEAD_DIM, axis=1)
        o_ref[:, c * HEAD_DIM:(c + 1) * HEAD_DIM] = o[:, :HEAD_DIM].astype(_BF16)


def _dense_attn(q, kt, v, *, online, bq=DENSE_BQ, bk=DENSE_BK):
    bsz, seq, _ = q.shape
    bq, bk = min(bq, seq), min(DENSE_BK_ONLINE if online else bk, seq)
    grp = C_HEADS // C_KV_HEADS
    scratch = [pltpu.VMEM((grp, bq, LANES), _F32)]
    if online:
        scratch.append(pltpu.VMEM((grp, bq, 1), _F32))
    return pl.pallas_call(
        functools.partial(_dense_attn_kernel, bk=bk, online=online),
        grid=(bsz, C_KV_HEADS, seq // bq),
        in_specs=[pl.BlockSpec((None, bq, grp * LANES), lambda b, g, i: (b, i, g)),
                  pl.BlockSpec((None, LANES, seq), lambda b, g, i: (b, g, 0)),
                  pl.BlockSpec((None, None, seq, LANES), lambda b, g, i: (b, g, 0, 0))],
        out_specs=pl.BlockSpec((None, bq, grp * HEAD_DIM), lambda b, g, i: (b, i, g)),
        out_shape=jax.ShapeDtypeStruct((bsz, seq, C_HEADS * HEAD_DIM), _BF16),
        scratch_shapes=scratch,
        compiler_params=_params("parallel", "parallel", "parallel"),
        name="dense_attn_online" if online else "dense_attn",
    )(q, kt, v)


def _out_proj_kernel(*refs, n_in):
    x_ref = refs[0]
    o_refs = refs[1:1 + n_in]
    w_refs = refs[1 + n_in:1 + 2 * n_in]
    out_ref = refs[-1]
    acc = x_ref[...]
    for o_ref, w_ref in zip(o_refs, w_refs):
        acc = acc + jnp.dot(o_ref[...], w_ref[...], preferred_element_type=_F32)
    out_ref[...] = acc


def _out_proj(x2, os_, ws, tm=512):
    n, d = x2.shape
    in_specs = [pl.BlockSpec((tm, d), lambda i: (i, 0))]
    in_specs += [pl.BlockSpec((tm, o.shape[1]), lambda i: (i, 0)) for o in os_]
    in_specs += [pl.BlockSpec(w.shape, lambda i: (0, 0)) for w in ws]
    return pl.pallas_call(
        functools.partial(_out_proj_kernel, n_in=len(os_)), grid=(n // tm,),
        in_specs=in_specs,
        out_specs=pl.BlockSpec((tm, d), lambda i: (i, 0)),
        out_shape=jax.ShapeDtypeStruct((n, d), _F32),
        compiler_params=_params("parallel"),
        name="out_proj",
    )(x2, *os_, *ws)


def _pack_pair(a, b):
    wa = lax.bitcast_convert_type(a.astype(_BF16).astype(_F32), jnp.uint32) >> 16
    wb = lax.bitcast_convert_type(b.astype(_BF16).astype(_F32), jnp.uint32) & jnp.uint32(0xFFFF0000)
    return lax.bitcast_convert_type(wa | wb, jnp.int32)


def _unpack_pair(w):
    u = lax.bitcast_convert_type(w, jnp.uint32)
    return (lax.bitcast_convert_type(u << 16, _F32),
            lax.bitcast_convert_type(u & jnp.uint32(0xFFFF0000), _F32))


def _store_packed(ref, y):
    q = PACK_W
    for j in range(2):
        ref[j] = _pack_pair(y[:, 2 * j * q:(2 * j + 1) * q], y[:, (2 * j + 1) * q:(2 * j + 2) * q])


def _load_packed(ref):
    parts = []
    for j in range(2):
        parts += list(_unpack_pair(ref[j]))
    return jnp.concatenate(parts, axis=1)


def _sc_mesh():
    return plsc.VectorSubcoreMesh(core_axis_name="core", subcore_axis_name="subcore")


def _sc_gather(table, idx):
    n = idx.shape[0]
    d = table.shape[1]

    @pl.kernel(out_type=jax.ShapeDtypeStruct((n, d), table.dtype), mesh=_sc_mesh())
    def gather(x_hbm, i_hbm, o_hbm):
        def body(i_vmem, o_vmem):
            pltpu.sync_copy(x_hbm.at[i_vmem.at[0]], o_vmem)

        pltpu.emit_pipeline(
            body, grid=(n // SC_WINDOW,),
            in_specs=[pl.BlockSpec((1, SC_WINDOW), index_map=lambda i: (0, i))],
            out_specs=[pl.BlockSpec((SC_WINDOW, d), index_map=lambda i: (i, 0))],
            core_axis_name=("core", "subcore"),
            dimension_semantics=(pltpu.PARALLEL,),
        )(i_hbm, o_hbm)

    return gather(table, idx.reshape(1, n))


def _sc_scatter(src, idx, n_out, reps):
    n = idx.shape[0]
    r2, d = src.shape
    nb = r2 // 2 // SC_WINDOW

    @pl.kernel(out_type=jax.ShapeDtypeStruct((n_out, d), src.dtype), mesh=_sc_mesh())
    def scatter(x_hbm, i_hbm, o_hbm):
        def body(x_vmem, i_vmem):
            pltpu.sync_copy(x_vmem, o_hbm.at[i_vmem.at[0]])

        pltpu.emit_pipeline(
            body, grid=(n // SC_WINDOW,),
            in_specs=[pl.BlockSpec((SC_WINDOW, d),
                                   index_map=lambda i: ((i // (reps * nb)) * nb + i % nb, 0)),
                      pl.BlockSpec((1, SC_WINDOW), index_map=lambda i: (0, i))],
            out_specs=[],
            core_axis_name=("core", "subcore"),
            dimension_semantics=(pltpu.PARALLEL,),
        )(x_hbm, i_hbm)

    return scatter(src, idx.reshape(1, n))


def _router_kernel(x_ref, g_ref, w_ref, b_ref, h_ref, route_ref, cnt_ref, carry_ref):
    tm = x_ref.shape[0]

    @pl.when(pl.program_id(0) == 0)
    def _():
        carry_ref[...] = jnp.zeros(carry_ref.shape, _F32)

    x = x_ref[...]
    ms = jnp.mean(x * x, axis=-1, keepdims=True)
    h = x * lax.rsqrt(ms + RMS_EPS) * g_ref[...]
    _store_packed(h_ref, h)
    logits = jnp.dot(h, w_ref[...], preferred_element_type=_F32,
                     precision=lax.Precision.HIGHEST) + b_ref[...]
    lane = lax.broadcasted_iota(jnp.int32, (tm, LANES), 1)

    gmask = lane < MOE_GROUPS
    lg = jnp.where(gmask, logits, NEG_BIG)
    mg = jnp.max(lg, axis=-1, keepdims=True)
    zg = jnp.sum(jnp.exp(lg - mg), axis=-1, keepdims=True)
    p_grp = 1.0 / zg
    g_idx = jnp.min(jnp.where(lg == mg, lane, LANES), axis=-1, keepdims=True)

    e_lane = lane - ROUTE_E0
    emask = (e_lane >= 0) & (e_lane < N_EXPERTS) & ((e_lane // MOE_EXPERTS) == g_idx)
    le = jnp.where(emask, logits, NEG_BIG)
    m1 = jnp.max(le, axis=-1, keepdims=True)
    i1 = jnp.min(jnp.where(le == m1, lane, LANES), axis=-1, keepdims=True)
    le2 = jnp.where(lane == i1, NEG_BIG, le)
    m2 = jnp.max(le2, axis=-1, keepdims=True)
    i2 = jnp.min(jnp.where(le2 == m2, lane, LANES), axis=-1, keepdims=True)
    e21 = jnp.exp(m2 - m1)
    c1 = p_grp / (1.0 + e21)
    c2 = p_grp * e21 / (1.0 + e21)

    onehot = jnp.where(lane == i1, 1.0, jnp.where(lane == i2, 1.0, 0.0)).astype(_BF16)
    tri = (lax.broadcasted_iota(jnp.int32, (tm, tm), 0)
           >= lax.broadcasted_iota(jnp.int32, (tm, tm), 1)).astype(_BF16)
    cum = jnp.dot(tri, onehot, preferred_element_type=_F32) + carry_ref[...]
    r1 = jnp.sum(jnp.where(lane == i1, cum, 0.0), axis=-1, keepdims=True) - 1.0
    r2 = jnp.sum(jnp.where(lane == i2, cum, 0.0), axis=-1, keepdims=True) - 1.0
    carry_ref[...] = cum[tm - 1:tm, :]
    cnt_ref[...] = cum[tm - 1:tm, :]

    cols = ((i1 - ROUTE_E0).astype(_F32), (i2 - ROUTE_E0).astype(_F32), c1, c2, r1, r2)
    route = jnp.zeros((tm, LANES), _F32)
    for k, col in enumerate(cols):
        route = jnp.where(lane == k, col, route)
    route_ref[...] = route


def _router(x2, g, w_router, b_router, tm=512):
    n, d = x2.shape
    return pl.pallas_call(
        _router_kernel, grid=(n // tm,),
        in_specs=[pl.BlockSpec((tm, d), lambda i: (i, 0)),
                  pl.BlockSpec((1, d), lambda i: (0, 0)),
                  pl.BlockSpec((d, LANES), lambda i: (0, 0)),
                  pl.BlockSpec((1, LANES), lambda i: (0, 0))],
        out_specs=[pl.BlockSpec((2, tm, PACK_W), lambda i: (0, i, 0)),
                   pl.BlockSpec((tm, LANES), lambda i: (i, 0)),
                   pl.BlockSpec((1, LANES), lambda i: (0, 0))],
        out_shape=[jax.ShapeDtypeStruct((2, n, PACK_W), jnp.int32),
                   jax.ShapeDtypeStruct((n, LANES), _F32),
                   jax.ShapeDtypeStruct((1, LANES), _F32)],
        scratch_shapes=[pltpu.VMEM((1, LANES), _F32)],
        compiler_params=_params("arbitrary"),
        name="router",
    )(x2, g.reshape(1, d), w_router, b_router)


def _expert_kernel(te_ref, nv_ref, xs_ref, wg_ref, wu_ref, wd_ref, ys_ref, wgu_s, wd_s):
    j = pl.program_id(0)
    prev = te_ref[jnp.maximum(j - 1, 0)]

    @pl.when((j == 0) | (te_ref[j] != prev))
    def _():
        wgu_s[:, :MOE_D_FF] = wg_ref[...].astype(_BF16)
        wgu_s[:, MOE_D_FF:] = wu_ref[...].astype(_BF16)
        wd_s[...] = wd_ref[...].astype(_BF16)

    @pl.when(j < nv_ref[0])
    def _():
        xs = _load_packed(xs_ref).astype(_BF16)
        au = jnp.dot(xs, wgu_s[...], preferred_element_type=_F32)
        a, u = au[:, :MOE_D_FF], au[:, MOE_D_FF:]
        act = (a * (1.0 / (1.0 + jnp.exp(-a))) * u).astype(_BF16)
        _store_packed(ys_ref, jnp.dot(act, wd_s[...], preferred_element_type=_F32))


def _experts(tile_expert, n_valid, xs, w_gate, w_up, w_down, tm=MOE_TM):
    _, n_slots, pw = xs.shape
    n_tiles = n_slots // tm
    d, f = w_gate.shape[-2:]
    row = lambda j, te, nv: (0, jnp.maximum(jnp.minimum(j, nv[0] - 1), 0), 0)
    grid_spec = pltpu.PrefetchScalarGridSpec(
        num_scalar_prefetch=2, grid=(n_tiles,),
        in_specs=[pl.BlockSpec((2, tm, pw), row),
                  pl.BlockSpec((None, d, f), lambda j, te, nv: (te[j], 0, 0)),
                  pl.BlockSpec((None, d, f), lambda j, te, nv: (te[j], 0, 0)),
                  pl.BlockSpec((None, f, d), lambda j, te, nv: (te[j], 0, 0))],
        out_specs=pl.BlockSpec((2, tm, pw), row),
        scratch_shapes=[pltpu.VMEM((d, 2 * f), _BF16), pltpu.VMEM((f, d), _BF16)])
    return pl.pallas_call(
        _expert_kernel, grid_spec=grid_spec,
        out_shape=jax.ShapeDtypeStruct((2, n_slots, pw), jnp.int32),
        compiler_params=_params("arbitrary"),
        name="experts",
    )(tile_expert, n_valid, xs, w_gate, w_up, w_down)


def _combine_kernel(x_ref, y_ref, route_ref, o_ref):
    c1 = route_ref[:, 2:3]
    c2 = route_ref[:, 3:4]
    y0 = _load_packed(y_ref[:, 0])
    y1 = _load_packed(y_ref[:, 1])
    o_ref[...] = x_ref[...] + c1 * y0 + c2 * y1


def _combine(x2, y, route, tm=512):
    n, d = x2.shape
    row = pl.BlockSpec((tm, d), lambda i: (i, 0))
    return pl.pallas_call(
        _combine_kernel, grid=(n // tm,),
        in_specs=[row, pl.BlockSpec((2, 2, tm, PACK_W), lambda i: (0, 0, i, 0)),
                  pl.BlockSpec((tm, LANES), lambda i: (i, 0))],
        out_specs=row,
        out_shape=jax.ShapeDtypeStruct((n, d), _F32),
        compiler_params=_params("parallel"),
        name="moe_combine",
    )(x2, y, route)


def _moe(x2, g, wg, bg, we, be, w_gate, w_up, w_down):
    n, d = x2.shape
    tm = MOE_TM
    w_router = jnp.zeros((d, LANES), _F32)
    w_router = w_router.at[:, :MOE_GROUPS].set(wg)
    w_router = w_router.at[:, ROUTE_E0:ROUTE_E0 + N_EXPERTS].set(
        jnp.moveaxis(we, 0, 1).reshape(d, N_EXPERTS))
    b_router = jnp.zeros((1, LANES), _F32)
    b_router = b_router.at[0, :MOE_GROUPS].set(bg)
    b_router = b_router.at[0, ROUTE_E0:ROUTE_E0 + N_EXPERTS].set(be.reshape(-1))
    h, route, cnt = _router(x2, g, w_router, b_router)

    expert = route[:, 0:2].astype(jnp.int32)
    rank = route[:, 4:6].astype(jnp.int32)
    counts = cnt[0, ROUTE_E0:ROUTE_E0 + N_EXPERTS].astype(jnp.int32)
    tiles_per = (counts + tm - 1) // tm
    tiles_end = jnp.cumsum(tiles_per)
    offset = (tiles_end - tiles_per) * tm
    pos = offset[expert] + rank
    n_tiles = (2 * n) // tm + N_EXPERTS
    n_valid = tiles_end[-1]
    tile_ids = jnp.minimum(jnp.arange(n_tiles, dtype=jnp.int32), n_valid - 1)
    tile_expert = jnp.sum(tile_ids[:, None] >= tiles_end[None, :], axis=1).astype(jnp.int32)
    n_slots = n_tiles * tm
    slot = jnp.concatenate([pos[:, 0], pos[:, 1], pos[:, 0] + n_slots, pos[:, 1] + n_slots])

    xs = _sc_scatter(h.reshape(2 * n, PACK_W), slot, 2 * n_slots, reps=2)
    ys = _experts(tile_expert, n_valid.reshape(1), xs.reshape(2, n_slots, PACK_W),
                  w_gate.reshape(N_EXPERTS, d, MOE_D_FF), w_up.reshape(N_EXPERTS, d, MOE_D_FF),
                  w_down.reshape(N_EXPERTS, MOE_D_FF, d))
    y = _sc_gather(ys.reshape(2 * n_slots, PACK_W), slot)
    return _combine(x2, y.reshape(2, 2, n, PACK_W), route)


def _mixer_ab(x2, bsz, seq, g, w_in, a_qn, a_kn, b_qn, b_kn, b_sink, w_out):
    aw = A_HEADS * HEAD_DIM
    bqw = B_HEADS * HEAD_DIM
    bkw = B_KV_HEADS * HEAD_DIM
    proj = _norm_proj(x2, g, w_in.astype(_BF16))
    dils = tuple(d for _, d in A_PATTERNS)
    qa = _prep(proj, bsz, seq, 0, aw, gain=a_qn, scale=QK_SCALE, mode="strided", dils=dils)
    ka = _prep(proj, bsz, seq, aw, aw, gain=a_kn, mode="stridedT", dils=dils, tm=2048)
    va = _prep(proj, bsz, seq, 2 * aw, aw, mode="strided", dils=dils)
    qb = _prep(proj, bsz, seq, 3 * aw, bqw, gain=b_qn, scale=QK_SCALE, mode="plain")
    kb = _prep(proj, bsz, seq, 3 * aw + bqw, bkw, gain=b_kn, mode="T")
    vb = _prep(proj, bsz, seq, 3 * aw + bqw + bkw, bkw, mode="heads")

    slopes = _alibi_slopes(A_HEADS + B_HEADS)
    branches = []
    for (window, d), q_d, k_d, v_d in zip(A_PATTERNS, qa, ka, va):
        ld = seq // d
        branches.append(_band_a(q_d.reshape(bsz * d, ld, aw), k_d.reshape(bsz * d, aw, ld),
                                v_d.reshape(bsz * d, ld, aw),
                                hw=(window // 2) // d, dil=d, slopes=slopes[0::2]))
    oa = _mix_a(branches, bsz, seq)
    ob = _band_b(qb, kb, vb, b_sink, hw=B_WINDOW, slopes=slopes[1::2])
    w_out = w_out.astype(_BF16)
    return _out_proj(x2, [oa.reshape(-1, aw), ob.reshape(-1, bqw)], [w_out[:aw], w_out[aw:]])


def _mixer_c(x2, bsz, seq, g, w_in, qn, kn, w_out):
    qw = C_HEADS * HEAD_DIM
    kvw = C_KV_HEADS * HEAD_DIM
    proj = _norm_proj(x2, g, w_in.astype(_BF16))
    rope = _rope_tables(seq)
    bound = HEAD_DIM * QK_SCALE * LOG2E * jnp.max(jnp.abs(qn)) * jnp.max(jnp.abs(kn))
    static_ok = bound <= SHIFT_MAX
    shift = jnp.where(static_ok, bound, 0.0)
    q = _prep(proj, bsz, seq, 0, qw, gain=qn, rope=rope, scale=QK_SCALE * LOG2E, mode="plain_ext")
    kt = _prep(proj, bsz, seq, qw, kvw, gain=kn, rope=rope, mode="T_ext", shift=shift)
    v = _prep(proj, bsz, seq, qw + kvw, kvw, mode="heads_ext")
    o = lax.cond(static_ok,
                 functools.partial(_dense_attn, online=False),
                 functools.partial(_dense_attn, online=True), q, kt, v)
    return _out_proj(x2, [o.reshape(-1, qw)], [w_out.astype(_BF16)])


def kernel(x, mix_norm, ffn_norm, ab_w_in, a_q_norm, a_k_norm, b_q_norm, b_k_norm, b_sink, ab_w_out,
           c_w_in, c_q_norm, c_k_norm, c_w_out, moe_group_w, moe_group_b, moe_expert_w, moe_expert_b,
           moe_w_gate, moe_w_up, moe_w_down):
    bsz, seq, d = x.shape
    x2 = x.reshape(bsz * seq, d)
    depth = mix_norm.shape[0]
    for layer in range(depth):
        i = layer // 2
        if layer % 2 == 0:
            x2 = _mixer_ab(x2, bsz, seq, mix_norm[layer], ab_w_in[i], a_q_norm[i], a_k_norm[i],
                           b_q_norm[i], b_k_norm[i], b_sink[i], ab_w_out[i])
        else:
            x2 = _mixer_c(x2, bsz, seq, mix_norm[layer], c_w_in[i], c_q_norm[i], c_k_norm[i],
                          c_w_out[i])
        x2 = _moe(x2, ffn_norm[layer], moe_group_w[layer], moe_group_b[layer], moe_expert_w[layer],
                  moe_expert_b[layer], moe_w_gate[layer], moe_w_up[layer], moe_w_down[layer])
    return x2.reshape(bsz, seq, d)
```

```python
import functools
import math

import jax
import jax.numpy as jnp
import numpy as np
from jax import lax
from jax.experimental import pallas as pl
from jax.experimental.pallas import tpu as pltpu
from jax.experimental.pallas import tpu_sc as plsc

HEAD_DIM = 64
LANES = 128
N_HEADS = 16
A_HEADS = 8
B_HEADS = 8
B_KV_HEADS = 2
C_HEADS = 16
C_KV_HEADS = 4
A_PATTERNS = ((128, 1), (512, 4), (2048, 16))
B_WINDOW = 128
GRID_W = 64
ROPE_THETA = 10000.0
ROPE_AXIS_DIM = HEAD_DIM // 2
ALIBI_MAX_BIAS = 8.0
RMS_EPS = 1e-6
MOE_GROUPS = 4
MOE_EXPERTS = 8
N_EXPERTS = MOE_GROUPS * MOE_EXPERTS
MOE_D_FF = 256
QK_SCALE = HEAD_DIM ** -0.5
LOG2E = math.log2(math.e)
SHIFT_MAX = 60.0
NEG_BIG = -1e30
VMEM_LIMIT = 52 * 1024 * 1024

BAND_BQ = 128
DENSE_BQ = 256
DENSE_BK = 2048
DENSE_BK_ONLINE = 512
MOE_TM = 512
ROUTE_E0 = 4
PACK_W = 256
SC_WINDOW = 128

_BF16 = jnp.bfloat16
_F32 = jnp.float32


def _params(*sem):
    return pltpu.CompilerParams(dimension_semantics=sem, vmem_limit_bytes=VMEM_LIMIT)


def _alibi_slopes(n):
    return np.asarray(2.0 ** (-ALIBI_MAX_BIAS * np.arange(1, n + 1) / n), dtype=np.float32)


def _norm_proj_kernel(x_ref, g_ref, w_ref, o_ref):
    x = x_ref[...]
    ms = jnp.mean(x * x, axis=-1, keepdims=True)
    h = (x * lax.rsqrt(ms + RMS_EPS) * g_ref[...]).astype(_BF16)
    o_ref[...] = jnp.dot(h, w_ref[...], preferred_element_type=_F32)


def _norm_proj(x2, g, w, tm=512):
    n, d = x2.shape
    p = w.shape[1]
    return pl.pallas_call(
        _norm_proj_kernel,
        grid=(n // tm,),
        in_specs=[pl.BlockSpec((tm, d), lambda i: (i, 0)),
                  pl.BlockSpec((1, d), lambda i: (0, 0)),
                  pl.BlockSpec((d, p), lambda i: (0, 0))],
        out_specs=pl.BlockSpec((tm, p), lambda i: (i, 0)),
        out_shape=jax.ShapeDtypeStruct((n, p), _F32),
        compiler_params=_params("parallel"),
        name="norm_proj",
    )(x2, g.reshape(1, d), w)


def _head_norm(y, gain):
    lane = lax.broadcasted_iota(jnp.int32, y.shape, 1)
    lo = lane < HEAD_DIM
    y2 = y * y
    ms_lo = jnp.sum(jnp.where(lo, y2, 0.0), axis=-1, keepdims=True) * (1.0 / HEAD_DIM)
    ms_hi = jnp.sum(jnp.where(lo, 0.0, y2), axis=-1, keepdims=True) * (1.0 / HEAD_DIM)
    inv = jnp.where(lo, lax.rsqrt(ms_lo + RMS_EPS), lax.rsqrt(ms_hi + RMS_EPS))
    return y * inv * gain


def _rope(y, cos, sin):
    lane = lax.broadcasted_iota(jnp.int32, y.shape, 1)
    first = (lane % 32) < 16
    partner = jnp.where(first, pltpu.roll(y, LANES - 16, axis=1), pltpu.roll(y, 16, axis=1))
    return y * cos + partner * sin


def _prep_kernel(*refs, ncols, norm, rope, scale, mode, dils, tm):
    it = iter(refs)
    shift_ref = next(it) if mode == "T_ext" else None
    x_ref = next(it)
    g_ref = next(it) if norm else None
    cos_ref = next(it) if rope else None
    sin_ref = next(it) if rope else None
    outs = list(it)
    scr = outs.pop() if mode in ("strided", "strided_ext", "stridedT") else None
    for c in range(ncols // LANES):
        sl = slice(c * LANES, (c + 1) * LANES)
        y = x_ref[:, sl]
        if norm:
            y = _head_norm(y, g_ref[...])
        if rope:
            y = _rope(y, cos_ref[...], sin_ref[...])
        if scale != 1.0:
            y = y * scale
        if mode == "plain":
            outs[0][:, sl] = y.astype(_BF16)
        elif mode == "T":
            outs[0][sl, :] = y.T.astype(_BF16)
        elif mode == "heads":
            outs[0][2 * c] = y[:, :HEAD_DIM].astype(_BF16)
            outs[0][2 * c + 1] = y[:, HEAD_DIM:].astype(_BF16)
        elif mode in ("plain_ext", "heads_ext"):
            lane = lax.broadcasted_iota(jnp.int32, y.shape, 1)
            if mode == "plain_ext":
                fill = jnp.where(lane == HEAD_DIM, 1.0, 0.0)
            else:
                fill = jnp.ones(y.shape, _F32)
            for k, yk in enumerate((y, pltpu.roll(y, HEAD_DIM, axis=1))):
                ext = jnp.where(lane < HEAD_DIM, yk, fill).astype(_BF16)
                if mode == "plain_ext":
                    outs[0][:, (2 * c + k) * LANES:(2 * c + k + 1) * LANES] = ext
                else:
                    outs[0][2 * c + k] = ext
        elif mode == "T_ext":
            yt = y.T
            row = lax.broadcasted_iota(jnp.int32, (HEAD_DIM, tm), 0)
            extra = jnp.where(row == 0, -shift_ref[0], 0.0)
            for k in range(2):
                ext = jnp.concatenate([yt[k * HEAD_DIM:(k + 1) * HEAD_DIM], extra], axis=0)
                outs[0][(2 * c + k) * LANES:(2 * c + k + 1) * LANES, :] = ext.astype(_BF16)
        else:
            scr[c] = y
    if scr is not None:
        for o_ref, d in zip(outs, dils):
            for r in range(d):
                for c in range(ncols // LANES):
                    sl = slice(c * LANES, (c + 1) * LANES)
                    rows = scr[c, pl.ds(r, tm // d, stride=d), :] if d > 1 else scr[c]
                    if mode == "strided":
                        o_ref[r, :, sl] = rows.astype(_BF16)
                    elif mode == "strided_ext":
                        lane = lax.broadcasted_iota(jnp.int32, rows.shape, 1)
                        for k, rk in enumerate((rows, pltpu.roll(rows, HEAD_DIM, axis=1))):
                            ext = jnp.where(lane < HEAD_DIM, rk, 1.0).astype(_BF16)
                            o_ref[r, :, (2 * c + k) * LANES:(2 * c + k + 1) * LANES] = ext
                    else:
                        o_ref[r, sl, :] = rows.T.astype(_BF16)


def _prep(proj, bsz, seq, col0, ncols, *, gain=None, rope=None, scale=1.0, mode="plain",
          dils=(), tm=512, shift=None):
    assert col0 % ncols == 0 and seq % tm == 0
    nt = seq // tm
    cb = col0 // ncols
    ins = [proj]
    in_specs = [pl.BlockSpec((tm, ncols), lambda b, i: (b * nt + i, cb))]
    if mode == "T_ext":
        ins.insert(0, shift.reshape(1).astype(_F32))
        in_specs.insert(0, pl.BlockSpec(memory_space=pltpu.SMEM))
    if gain is not None:
        ins.append(jnp.tile(gain.astype(_F32), LANES // HEAD_DIM).reshape(1, LANES))
        in_specs.append(pl.BlockSpec((1, LANES), lambda b, i: (0, 0)))
    if rope is not None:
        ins += list(rope)
        in_specs += [pl.BlockSpec((tm, LANES), lambda b, i: (i, 0))] * 2
    scratch = []
    if mode == "plain":
        out_shape = [jax.ShapeDtypeStruct((bsz, seq, ncols), _BF16)]
        out_specs = [pl.BlockSpec((None, tm, ncols), lambda b, i: (b, i, 0))]
    elif mode == "T":
        out_shape = [jax.ShapeDtypeStruct((bsz, ncols, seq), _BF16)]
        out_specs = [pl.BlockSpec((None, ncols, tm), lambda b, i: (b, 0, i))]
    elif mode == "heads":
        nh = ncols // HEAD_DIM
        out_shape = [jax.ShapeDtypeStruct((bsz, nh, seq, HEAD_DIM), _BF16)]
        out_specs = [pl.BlockSpec((None, nh, tm, HEAD_DIM), lambda b, i: (b, 0, i, 0))]
    elif mode == "plain_ext":
        out_shape = [jax.ShapeDtypeStruct((bsz, seq, 2 * ncols), _BF16)]
        out_specs = [pl.BlockSpec((None, tm, 2 * ncols), lambda b, i: (b, i, 0))]
    elif mode == "T_ext":
        out_shape = [jax.ShapeDtypeStruct((bsz, 2 * ncols, seq), _BF16)]
        out_specs = [pl.BlockSpec((None, 2 * ncols, tm), lambda b, i: (b, 0, i))]
    elif mode == "heads_ext":
        nh = ncols // HEAD_DIM
        out_shape = [jax.ShapeDtypeStruct((bsz, nh, seq, LANES), _BF16)]
        out_specs = [pl.BlockSpec((None, nh, tm, LANES), lambda b, i: (b, 0, i, 0))]
    elif mode in ("strided", "strided_ext"):
        wout = ncols if mode == "strided" else 2 * ncols
        out_shape = [jax.ShapeDtypeStruct((bsz, d, seq // d, wout), _BF16) for d in dils]
        out_specs = [pl.BlockSpec((None, d, tm // d, wout), lambda b, i: (b, 0, i, 0)) for d in dils]
        scratch = [pltpu.VMEM((ncols // LANES, tm, LANES), _F32)]
    else:
        out_shape = [jax.ShapeDtypeStruct((bsz, d, ncols, seq // d), _BF16) for d in dils]
        out_specs = [pl.BlockSpec((None, d, ncols, tm // d), lambda b, i: (b, 0, 0, i)) for d in dils]
        scratch = [pltpu.VMEM((ncols // LANES, tm, LANES), _F32)]
    kern = functools.partial(_prep_kernel, ncols=ncols, norm=gain is not None, rope=rope is not None,
                             scale=scale, mode=mode, dils=dils, tm=tm)
    out = pl.pallas_call(
        kern, grid=(bsz, nt), in_specs=in_specs, out_specs=out_specs, out_shape=out_shape,
        scratch_shapes=scratch, compiler_params=_params("parallel", "parallel"),
        name="prep_" + mode,
    )(*ins)
    return out if len(out) > 1 else out[0]


def _rope_tables(seq):
    t = np.arange(seq)
    row = (t // GRID_W).astype(np.float32)
    col = (t % GRID_W).astype(np.float32)
    inv_freq = jnp.asarray(ROPE_THETA, _F32) ** (-jnp.arange(0, ROPE_AXIS_DIM, 2, dtype=_F32) / ROPE_AXIS_DIM)
    ang_r = jnp.asarray(row)[:, None] * inv_freq[None, :]
    ang_c = jnp.asarray(col)[:, None] * inv_freq[None, :]
    cr, sr, cc, sc = jnp.cos(ang_r), jnp.sin(ang_r), jnp.cos(ang_c), jnp.sin(ang_c)
    cos = jnp.concatenate([cr, cr, cc, cc], axis=-1)
    sin = jnp.concatenate([-sr, sr, -sc, sc], axis=-1)
    return jnp.tile(cos, (1, 2)), jnp.tile(sin, (1, 2))


def _band_bias(hw, dist_scale, slopes, shift, stack):
    bq = BAND_BQ
    r = np.arange(bq)[:, None]
    c = np.arange(3 * bq)[None, :]
    rel = np.abs(c - bq - r)
    dist = rel.astype(np.float32) * np.float32(dist_scale)
    ok = np.stack([(rel <= hw) & ~(first & (c < bq)) & ~(last & (c >= 2 * bq))
                   for first, last in ((False, False), (True, False), (False, True), (True, True))])
    alibi = -(np.asarray(slopes, np.float32)[:, None, None] * dist[None]) * np.float32(LOG2E)
    bias = jnp.where(ok[:, None], jnp.asarray(alibi)[None] - shift, NEG_BIG)
    nh = len(slopes)
    return bias.reshape(4, nh // stack, stack * bq, 3 * bq)


def _edge_variant(i, nb):
    return jnp.where(i == 0, 1, 0) + jnp.where(i == nb - 1, 2, 0)


def _band_a_kernel(q_ref, kp_ref, kc_ref, kn_ref, vp_ref, vc_ref, vn_ref, bias_ref, o_ref,
                   *, online):
    kt = jnp.concatenate([kp_ref[...], kc_ref[...], kn_ref[...]], axis=1)
    v = jnp.concatenate([vp_ref[...], vc_ref[...], vn_ref[...]], axis=0)
    q = q_ref[...]
    for h in range(A_HEADS):
        hs = slice(h * HEAD_DIM, (h + 1) * HEAD_DIM)
        es = slice(h * LANES, (h + 1) * LANES)
        s = jnp.dot(q[:, hs], kt[hs, :], preferred_element_type=_F32) + bias_ref[h]
        if online:
            m = jnp.max(s, axis=-1, keepdims=True)
            acc = jnp.dot(jnp.exp2(s - m).astype(_BF16), v[:, es], preferred_element_type=_F32)
            lane = lax.broadcasted_iota(jnp.int32, acc.shape, 1)
            o_ref[:, es] = jnp.where(lane < HEAD_DIM, acc / pltpu.roll(acc, HEAD_DIM, axis=1),
                                     m + jnp.log2(acc))
        else:
            o_ref[:, es] = jnp.dot(jnp.exp2(s).astype(_BF16), v[:, es], preferred_element_type=_F32)


def _band_a(q, kt, v, bias, *, online):
    bd, seq_len, w = q.shape
    we = v.shape[-1]
    bq = BAND_BQ
    nb = seq_len // bq
    prev = lambda i: jnp.maximum(i - 1, 0)
    nxt = lambda i: jnp.minimum(i + 1, nb - 1)
    ident = lambda i: i
    kspec = lambda f: pl.BlockSpec((None, w, bq), lambda b, i: (b, 0, f(i)))
    vspec = lambda f: pl.BlockSpec((None, bq, we), lambda b, i: (b, f(i), 0))
    return pl.pallas_call(
        functools.partial(_band_a_kernel, online=online), grid=(bd, nb),
        in_specs=[pl.BlockSpec((None, bq, w), lambda b, i: (b, i, 0)),
                  kspec(prev), kspec(ident), kspec(nxt), vspec(prev), vspec(ident), vspec(nxt),
                  pl.BlockSpec((None, A_HEADS, bq, 3 * bq),
                               lambda b, i: (_edge_variant(i, nb), 0, 0, 0))],
        out_specs=vspec(ident),
        out_shape=jax.ShapeDtypeStruct((bd, seq_len, we), _F32),
        compiler_params=_params("parallel", "parallel"),
        name="band_a_online" if online else "band_a",
    )(q, kt, kt, kt, v, v, v, bias)


def _band_b_kernel(sink2_ref, sinkp_ref, q_ref, kp_ref, kc_ref, kn_ref, vp_ref, vc_ref, vn_ref,
                   bias_ref, o_ref, *, online):
    bq = q_ref.shape[0]
    g = pl.program_id(1)
    grp = B_HEADS // B_KV_HEADS
    kt = jnp.concatenate([kp_ref[...], kc_ref[...], kn_ref[...]], axis=1)
    v = jnp.concatenate([vp_ref[...], vc_ref[...], vn_ref[...]], axis=0)
    q = q_ref[...]
    q4 = jnp.concatenate([q[:, i * HEAD_DIM:(i + 1) * HEAD_DIM] for i in range(grp)], axis=0)
    s = jnp.dot(q4, kt, preferred_element_type=_F32) + bias_ref[...]
    if online:
        m = jnp.max(s, axis=-1, keepdims=True)
        s = s - m
    acc = jnp.dot(jnp.exp2(s).astype(_BF16), v, preferred_element_type=_F32)
    for i in range(grp):
        a = acc[i * bq:(i + 1) * bq]
        if online:
            mi = m[i * bq:(i + 1) * bq]
            sk = sink2_ref[g * grp + i]
            mm = jnp.maximum(mi, sk)
            a = a * jnp.exp2(mi - mm)
            o = a / (pltpu.roll(a, HEAD_DIM, axis=1) + jnp.exp2(sk - mm))
        else:
            o = a / (pltpu.roll(a, HEAD_DIM, axis=1) + sinkp_ref[g * grp + i])
        o_ref[:, i * HEAD_DIM:(i + 1) * HEAD_DIM] = o[:, :HEAD_DIM].astype(_BF16)


def _band_b(q, kt, v, bias, sink2, *, online):
    bsz, seq_len, w = q.shape
    bq = BAND_BQ
    nb = seq_len // bq
    grp = B_HEADS // B_KV_HEADS
    gw = w // B_KV_HEADS
    prev = lambda i: jnp.maximum(i - 1, 0)
    nxt = lambda i: jnp.minimum(i + 1, nb - 1)
    ident = lambda i: i
    kspec = lambda f: pl.BlockSpec((None, HEAD_DIM, bq), lambda b, g, i: (b, g, f(i)))
    vspec = lambda f: pl.BlockSpec((None, None, bq, LANES), lambda b, g, i: (b, g, f(i), 0))
    qspec = pl.BlockSpec((None, bq, gw), lambda b, g, i: (b, i, g))
    smem = pl.BlockSpec(memory_space=pltpu.SMEM)
    return pl.pallas_call(
        functools.partial(_band_b_kernel, online=online), grid=(bsz, B_KV_HEADS, nb),
        in_specs=[smem, smem, qspec, kspec(prev), kspec(ident), kspec(nxt),
                  vspec(prev), vspec(ident), vspec(nxt),
                  pl.BlockSpec((None, None, grp * bq, 3 * bq),
                               lambda b, g, i: (_edge_variant(i, nb), g, 0, 0))],
        out_specs=qspec,
        out_shape=jax.ShapeDtypeStruct((bsz, seq_len, w), _BF16),
        compiler_params=_params("parallel", "parallel", "parallel"),
        name="band_b_online" if online else "band_b",
    )(sink2, jnp.exp2(sink2), q, kt, kt, kt, v, v, v, bias)


def _mix_a_kernel(r1, r4, r16, out_ref, s4, s16, *, tm, online):
    for src, dst, d in ((r4, s4, 4), (r16, s16, 16)):
        for r in range(d):
            for h in range(A_HEADS):
                dst[h, pl.ds(r, tm // d, stride=d), :] = src[r, :, h * LANES:(h + 1) * LANES]
    for h in range(A_HEADS):
        a, b, c = r1[0, :, h * LANES:(h + 1) * LANES], s4[h], s16[h]
        if online:
            la, lb, lc = (pltpu.roll(t, HEAD_DIM, axis=1) for t in (a, b, c))
            m = jnp.maximum(jnp.maximum(la, lb), lc)
            ea, eb, ec = jnp.exp2(la - m), jnp.exp2(lb - m), jnp.exp2(lc - m)
            o = (ea * a + eb * b + ec * c) / (ea + eb + ec)
        else:
            t = a + b + c
            o = t / pltpu.roll(t, HEAD_DIM, axis=1)
        out_ref[:, h * HEAD_DIM:(h + 1) * HEAD_DIM] = o[:, :HEAD_DIM].astype(_BF16)


def _mix_a(branches, bsz, seq, *, online, tm=512):
    we = branches[0].shape[-1]
    ins, in_specs = [], []
    for res, (_, d) in zip(branches, A_PATTERNS):
        ins.append(res.reshape(bsz, d, seq // d, we))
        in_specs.append(pl.BlockSpec((None, d, tm // d, we), lambda b, i: (b, 0, i, 0)))
    w = A_HEADS * HEAD_DIM
    return pl.pallas_call(
        functools.partial(_mix_a_kernel, tm=tm, online=online), grid=(bsz, seq // tm),
        in_specs=in_specs,
        out_specs=pl.BlockSpec((None, tm, w), lambda b, i: (b, i, 0)),
        out_shape=jax.ShapeDtypeStruct((bsz, seq, w), _BF16),
        scratch_shapes=[pltpu.VMEM((A_HEADS, tm, LANES), _F32)] * 2,
        compiler_params=_params("parallel", "parallel"),
        name="mix_a_online" if online else "mix_a",
    )(*ins)


def _dense_attn_kernel(q_ref, kt_ref, v_ref, o_ref, acc_scr, *rest, bk, online):
    seq = kt_ref.shape[1]
    grp = C_HEADS // C_KV_HEADS
    acc_scr[...] = jnp.zeros(acc_scr.shape, _F32)
    if online:
        m_scr, = rest
        m_scr[...] = jnp.full(m_scr.shape, NEG_BIG, _F32)

    def body(j, carry):
        k0 = pl.multiple_of(j * bk, bk)
        kt = kt_ref[:, pl.ds(k0, bk)]
        v = v_ref[pl.ds(k0, bk), :]
        for c in range(grp):
            s = jnp.dot(q_ref[:, c * LANES:(c + 1) * LANES], kt, preferred_element_type=_F32)
            if online:
                m_old = m_scr[c]
                m_new = jnp.maximum(m_old, jnp.max(s, axis=-1, keepdims=True))
                m_scr[c] = m_new
                p = jnp.exp2(s - m_new).astype(_BF16)
                acc_scr[c] = jnp.exp2(m_old - m_new) * acc_scr[c] + jnp.dot(
                    p, v, preferred_element_type=_F32)
            else:
                p = jnp.exp2(s).astype(_BF16)
                acc_scr[c] += jnp.dot(p, v, preferred_element_type=_F32)
        return carry

    lax.fori_loop(0, seq // bk, body, 0)
    for c in range(grp):
        a = acc_scr[c]
        o = a / pltpu.roll(a, HEAD_DIM, axis=1)
        o_ref[:, c * HEAD_DIM:(c + 1) * HEAD_DIM] = o[:, :HEAD_DIM].astype(_BF16)


def _dense_attn(q, kt, v, *, online, bq=DENSE_BQ, bk=DENSE_BK):
    bsz, seq, _ = q.shape
    bq, bk = min(bq, seq), min(DENSE_BK_ONLINE if online else bk, seq)
    grp = C_HEADS // C_KV_HEADS
    scratch = [pltpu.VMEM((grp, bq, LANES), _F32)]
    if online:
        scratch.append(pltpu.VMEM((grp, bq, 1), _F32))
    return pl.pallas_call(
        functools.partial(_dense_attn_kernel, bk=bk, online=online),
        grid=(bsz, C_KV_HEADS, seq // bq),
        in_specs=[pl.BlockSpec((None, bq, grp * LANES), lambda b, g, i: (b, i, g)),
                  pl.BlockSpec((None, LANES, seq), lambda b, g, i: (b, g, 0)),
                  pl.BlockSpec((None, None, seq, LANES), lambda b, g, i: (b, g, 0, 0))],
        out_specs=pl.BlockSpec((None, bq, grp * HEAD_DIM), lambda b, g, i: (b, i, g)),
        out_shape=jax.ShapeDtypeStruct((bsz, seq, C_HEADS * HEAD_DIM), _BF16),
        scratch_shapes=scratch,
        compiler_params=_params("parallel", "parallel", "parallel"),
        name="dense_attn_online" if online else "dense_attn",
    )(q, kt, v)


def _out_proj_kernel(*refs, n_in):
    x_ref = refs[0]
    o_refs = refs[1:1 + n_in]
    w_refs = refs[1 + n_in:1 + 2 * n_in]
    out_ref = refs[-1]
    acc = x_ref[...]
    for o_ref, w_ref in zip(o_refs, w_refs):
        acc = acc + jnp.dot(o_ref[...], w_ref[...], preferred_element_type=_F32)
    out_ref[...] = acc


def _out_proj(x2, os_, ws, tm=512):
    n, d = x2.shape
    in_specs = [pl.BlockSpec((tm, d), lambda i: (i, 0))]
    in_specs += [pl.BlockSpec((tm, o.shape[1]), lambda i: (i, 0)) for o in os_]
    in_specs += [pl.BlockSpec(w.shape, lambda i: (0, 0)) for w in ws]
    return pl.pallas_call(
        functools.partial(_out_proj_kernel, n_in=len(os_)), grid=(n // tm,),
        in_specs=in_specs,
        out_specs=pl.BlockSpec((tm, d), lambda i: (i, 0)),
        out_shape=jax.ShapeDtypeStruct((n, d), _F32),
        compiler_params=_params("parallel"),
        name="out_proj",
    )(x2, *os_, *ws)


def _pack_pair(a, b):
    wa = lax.bitcast_convert_type(a.astype(_BF16).astype(_F32), jnp.uint32) >> 16
    wb = lax.bitcast_convert_type(b.astype(_BF16).astype(_F32), jnp.uint32) & jnp.uint32(0xFFFF0000)
    return lax.bitcast_convert_type(wa | wb, jnp.int32)


def _unpack_pair(w):
    u = lax.bitcast_convert_type(w, jnp.uint32)
    return (lax.bitcast_convert_type(u << 16, _F32),
            lax.bitcast_convert_type(u & jnp.uint32(0xFFFF0000), _F32))


def _store_packed(ref, y):
    q = PACK_W
    for j in range(2):
        ref[j] = _pack_pair(y[:, 2 * j * q:(2 * j + 1) * q], y[:, (2 * j + 1) * q:(2 * j + 2) * q])


def _load_packed(ref):
    parts = []
    for j in range(2):
        parts += list(_unpack_pair(ref[j]))
    return jnp.concatenate(parts, axis=1)


def _sc_mesh():
    return plsc.VectorSubcoreMesh(core_axis_name="core", subcore_axis_name="subcore")


def _sc_gather(table, idx):
    n = idx.shape[0]
    d = table.shape[1]

    @pl.kernel(out_type=jax.ShapeDtypeStruct((n, d), table.dtype), mesh=_sc_mesh())
    def gather(x_hbm, i_hbm, o_hbm):
        def body(i_vmem, o_vmem):
            pltpu.sync_copy(x_hbm.at[i_vmem.at[0]], o_vmem)

        pltpu.emit_pipeline(
            body, grid=(n // SC_WINDOW,),
            in_specs=[pl.BlockSpec((1, SC_WINDOW), index_map=lambda i: (0, i))],
            out_specs=[pl.BlockSpec((SC_WINDOW, d), index_map=lambda i: (i, 0))],
            core_axis_name=("core", "subcore"),
            dimension_semantics=(pltpu.PARALLEL,),
        )(i_hbm, o_hbm)

    return gather(table, idx.reshape(1, n))


def _sc_scatter(src, idx, n_out, reps):
    n = idx.shape[0]
    r2, d = src.shape
    nb = r2 // 2 // SC_WINDOW

    @pl.kernel(out_type=jax.ShapeDtypeStruct((n_out, d), src.dtype), mesh=_sc_mesh())
    def scatter(x_hbm, i_hbm, o_hbm):
        def body(x_vmem, i_vmem):
            pltpu.sync_copy(x_vmem, o_hbm.at[i_vmem.at[0]])

        pltpu.emit_pipeline(
            body, grid=(n // SC_WINDOW,),
            in_specs=[pl.BlockSpec((SC_WINDOW, d),
                                   index_map=lambda i: ((i // (reps * nb)) * nb + i % nb, 0)),
                      pl.BlockSpec((1, SC_WINDOW), index_map=lambda i: (0, i))],
            out_specs=[],
            core_axis_name=("core", "subcore"),
            dimension_semantics=(pltpu.PARALLEL,),
        )(x_hbm, i_hbm)

    return scatter(src, idx.reshape(1, n))


def _router_kernel(x_ref, g_ref, w_ref, b_ref, h_ref, route_ref, cnt_ref, carry_ref):
    tm = x_ref.shape[0]

    @pl.when(pl.program_id(0) == 0)
    def _():
        carry_ref[...] = jnp.zeros(carry_ref.shape, _F32)

    x = x_ref[...]
    ms = jnp.mean(x * x, axis=-1, keepdims=True)
    h = x * lax.rsqrt(ms + RMS_EPS) * g_ref[...]
    _store_packed(h_ref, h)
    logits = jnp.dot(h, w_ref[...], preferred_element_type=_F32,
                     precision=lax.Precision.HIGHEST) + b_ref[...]
    lane = lax.broadcasted_iota(jnp.int32, (tm, LANES), 1)

    gmask = lane < MOE_GROUPS
    lg = jnp.where(gmask, logits, NEG_BIG)
    mg = jnp.max(lg, axis=-1, keepdims=True)
    zg = jnp.sum(jnp.exp(lg - mg), axis=-1, keepdims=True)
    p_grp = 1.0 / zg
    g_idx = jnp.min(jnp.where(lg == mg, lane, LANES), axis=-1, keepdims=True)

    e_lane = lane - ROUTE_E0
    emask = (e_lane >= 0) & (e_lane < N_EXPERTS) & ((e_lane // MOE_EXPERTS) == g_idx)
    le = jnp.where(emask, logits, NEG_BIG)
    m1 = jnp.max(le, axis=-1, keepdims=True)
    i1 = jnp.min(jnp.where(le == m1, lane, LANES), axis=-1, keepdims=True)
    le2 = jnp.where(lane == i1, NEG_BIG, le)
    m2 = jnp.max(le2, axis=-1, keepdims=True)
    i2 = jnp.min(jnp.where(le2 == m2, lane, LANES), axis=-1, keepdims=True)
    e21 = jnp.exp(m2 - m1)
    c1 = p_grp / (1.0 + e21)
    c2 = p_grp * e21 / (1.0 + e21)

    onehot = jnp.where(lane == i1, 1.0, jnp.where(lane == i2, 1.0, 0.0)).astype(_BF16)
    tri = (lax.broadcasted_iota(jnp.int32, (tm, tm), 0)
           >= lax.broadcasted_iota(jnp.int32, (tm, tm), 1)).astype(_BF16)
    cum = jnp.dot(tri, onehot, preferred_element_type=_F32) + carry_ref[...]
    r1 = jnp.sum(jnp.where(lane == i1, cum, 0.0), axis=-1, keepdims=True) - 1.0
    r2 = jnp.sum(jnp.where(lane == i2, cum, 0.0), axis=-1, keepdims=True) - 1.0
    carry_ref[...] = cum[tm - 1:tm, :]
    cnt_ref[...] = cum[tm - 1:tm, :]

    cols = ((i1 - ROUTE_E0).astype(_F32), (i2 - ROUTE_E0).astype(_F32), c1, c2, r1, r2)
    route = jnp.zeros((tm, LANES), _F32)
    for k, col in enumerate(cols):
        route = jnp.where(lane == k, col, route)
    route_ref[...] = route


def _router(x2, g, w_router, b_router, tm=512):
    n, d = x2.shape
    return pl.pallas_call(
        _router_kernel, grid=(n // tm,),
        in_specs=[pl.BlockSpec((tm, d), lambda i: (i, 0)),
                  pl.BlockSpec((1, d), lambda i: (0, 0)),
                  pl.BlockSpec((d, LANES), lambda i: (0, 0)),
                  pl.BlockSpec((1, LANES), lambda i: (0, 0))],
        out_specs=[pl.BlockSpec((2, tm, PACK_W), lambda i: (0, i, 0)),
                   pl.BlockSpec((tm, LANES), lambda i: (i, 0)),
                   pl.BlockSpec((1, LANES), lambda i: (0, 0))],
        out_shape=[jax.ShapeDtypeStruct((2, n, PACK_W), jnp.int32),
                   jax.ShapeDtypeStruct((n, LANES), _F32),
                   jax.ShapeDtypeStruct((1, LANES), _F32)],
        scratch_shapes=[pltpu.VMEM((1, LANES), _F32)],
        compiler_params=_params("arbitrary"),
        name="router",
    )(x2, g.reshape(1, d), w_router, b_router)


def _expert_kernel(te_ref, nv_ref, xs_ref, wg_ref, wu_ref, wd_ref, ys_ref, wgu_s, wd_s):
    j = pl.program_id(0)
    prev = te_ref[jnp.maximum(j - 1, 0)]

    @pl.when((j == 0) | (te_ref[j] != prev))
    def _():
        wgu_s[:, :MOE_D_FF] = wg_ref[...].astype(_BF16)
        wgu_s[:, MOE_D_FF:] = wu_ref[...].astype(_BF16)
        wd_s[...] = wd_ref[...].astype(_BF16)

    @pl.when(j < nv_ref[0])
    def _():
        xs = _load_packed(xs_ref).astype(_BF16)
        au = jnp.dot(xs, wgu_s[...], preferred_element_type=_F32)
        a, u = au[:, :MOE_D_FF], au[:, MOE_D_FF:]
        act = (a * (1.0 / (1.0 + jnp.exp(-a))) * u).astype(_BF16)
        _store_packed(ys_ref, jnp.dot(act, wd_s[...], preferred_element_type=_F32))


def _experts(tile_expert, n_valid, xs, w_gate, w_up, w_down, tm=MOE_TM):
    _, n_slots, pw = xs.shape
    n_tiles = n_slots // tm
    d, f = w_gate.shape[-2:]
    row = lambda j, te, nv: (0, jnp.maximum(jnp.minimum(j, nv[0] - 1), 0), 0)
    grid_spec = pltpu.PrefetchScalarGridSpec(
        num_scalar_prefetch=2, grid=(n_tiles,),
        in_specs=[pl.BlockSpec((2, tm, pw), row),
                  pl.BlockSpec((None, d, f), lambda j, te, nv: (te[j], 0, 0)),
                  pl.BlockSpec((None, d, f), lambda j, te, nv: (te[j], 0, 0)),
                  pl.BlockSpec((None, f, d), lambda j, te, nv: (te[j], 0, 0))],
        out_specs=pl.BlockSpec((2, tm, pw), row),
        scratch_shapes=[pltpu.VMEM((d, 2 * f), _BF16), pltpu.VMEM((f, d), _BF16)])
    return pl.pallas_call(
        _expert_kernel, grid_spec=grid_spec,
        out_shape=jax.ShapeDtypeStruct((2, n_slots, pw), jnp.int32),
        compiler_params=_params("arbitrary"),
        name="experts",
    )(tile_expert, n_valid, xs, w_gate, w_up, w_down)


def _combine_kernel(x_ref, y_ref, route_ref, o_ref):
    c1 = route_ref[:, 2:3]
    c2 = route_ref[:, 3:4]
    y0 = _load_packed(y_ref[:, 0])
    y1 = _load_packed(y_ref[:, 1])
    o_ref[...] = x_ref[...] + c1 * y0 + c2 * y1


def _combine(x2, y, route, tm=512):
    n, d = x2.shape
    row = pl.BlockSpec((tm, d), lambda i: (i, 0))
    return pl.pallas_call(
        _combine_kernel, grid=(n // tm,),
        in_specs=[row, pl.BlockSpec((2, 2, tm, PACK_W), lambda i: (0, 0, i, 0)),
                  pl.BlockSpec((tm, LANES), lambda i: (i, 0))],
        out_specs=row,
        out_shape=jax.ShapeDtypeStruct((n, d), _F32),
        compiler_params=_params("parallel"),
        name="moe_combine",
    )(x2, y, route)


def _moe(x2, g, wg, bg, we, be, w_gate, w_up, w_down, layer=0):
    n, d = x2.shape
    tm = MOE_TM
    w_router = jnp.zeros((d, LANES), _F32)
    w_router = w_router.at[:, :MOE_GROUPS].set(wg)
    w_router = w_router.at[:, ROUTE_E0:ROUTE_E0 + N_EXPERTS].set(
        jnp.moveaxis(we, 0, 1).reshape(d, N_EXPERTS))
    b_router = jnp.zeros((1, LANES), _F32)
    b_router = b_router.at[0, :MOE_GROUPS].set(bg)
    b_router = b_router.at[0, ROUTE_E0:ROUTE_E0 + N_EXPERTS].set(be.reshape(-1))
    h, route, cnt = _router(x2, g, w_router, b_router)

    expert = route[:, 0:2].astype(jnp.int32)
    rank = route[:, 4:6].astype(jnp.int32)
    counts = cnt[0, ROUTE_E0:ROUTE_E0 + N_EXPERTS].astype(jnp.int32)
    tiles_per = (counts + tm - 1) // tm
    tiles_end = jnp.cumsum(tiles_per)
    offset = (tiles_end - tiles_per) * tm
    pos = offset[expert] + rank
    n_tiles = (2 * n) // tm + N_EXPERTS
    n_valid = tiles_end[-1]
    tile_ids = jnp.minimum(jnp.arange(n_tiles, dtype=jnp.int32), n_valid - 1)
    tile_expert = jnp.sum(tile_ids[:, None] >= tiles_end[None, :], axis=1).astype(jnp.int32)
    n_slots = n_tiles * tm
    slot = jnp.concatenate([pos[:, 0], pos[:, 1], pos[:, 0] + n_slots, pos[:, 1] + n_slots])

    xs = _sc_scatter(h.reshape(2 * n, PACK_W), slot, 2 * n_slots, reps=2)
    ys = _experts(tile_expert + layer * N_EXPERTS, n_valid.reshape(1),
                  xs.reshape(2, n_slots, PACK_W), w_gate.reshape(-1, d, MOE_D_FF),
                  w_up.reshape(-1, d, MOE_D_FF), w_down.reshape(-1, MOE_D_FF, d))
    y = _sc_gather(ys.reshape(2 * n_slots, PACK_W), slot)
    return _combine(x2, y.reshape(2, 2, n, PACK_W), route)


def _mixer_ab(x2, bsz, seq, g, w_in, a_qn, a_kn, b_qn, b_kn, b_sink, w_out):
    aw = A_HEADS * HEAD_DIM
    bqw = B_HEADS * HEAD_DIM
    bkw = B_KV_HEADS * HEAD_DIM
    proj = _norm_proj(x2, g, w_in.astype(_BF16))
    dils = tuple(d for _, d in A_PATTERNS)
    scale2 = QK_SCALE * LOG2E
    qa = _prep(proj, bsz, seq, 0, aw, gain=a_qn, scale=scale2, mode="strided", dils=dils)
    ka = _prep(proj, bsz, seq, aw, aw, gain=a_kn, mode="stridedT", dils=dils, tm=2048)
    va = _prep(proj, bsz, seq, 2 * aw, aw, mode="strided_ext", dils=dils)
    qb = _prep(proj, bsz, seq, 3 * aw, bqw, gain=b_qn, scale=scale2, mode="plain")
    kb = _prep(proj, bsz, seq, 3 * aw + bqw, bkw, gain=b_kn, mode="T")
    vb = _prep(proj, bsz, seq, 3 * aw + bqw + bkw, bkw, mode="heads_ext")

    bound_a = HEAD_DIM * scale2 * jnp.max(jnp.abs(a_qn)) * jnp.max(jnp.abs(a_kn))
    bound_b = HEAD_DIM * scale2 * jnp.max(jnp.abs(b_qn)) * jnp.max(jnp.abs(b_kn))
    static_ok = jnp.maximum(bound_a, bound_b) <= SHIFT_MAX
    shift_a = jnp.where(static_ok, bound_a, 0.0)
    shift_b = jnp.where(static_ok, bound_b, 0.0)
    slopes = _alibi_slopes(A_HEADS + B_HEADS)
    bias_a = [_band_bias((window // 2) // d, float(d), slopes[0::2], shift_a, 1)
              for window, d in A_PATTERNS]
    bias_b = _band_bias(B_WINDOW, 1.0, slopes[1::2], shift_b, B_HEADS // B_KV_HEADS)
    sink2 = b_sink.astype(_F32) * LOG2E - shift_b

    def attend(online, qa, ka, va, qb, kb, vb, bias_a, bias_b, sink2):
        branches = []
        for (_, d), q_d, k_d, v_d, bias_d in zip(A_PATTERNS, qa, ka, va, bias_a):
            ld = seq // d
            branches.append(_band_a(q_d.reshape(bsz * d, ld, aw), k_d.reshape(bsz * d, aw, ld),
                                    v_d.reshape(bsz * d, ld, 2 * aw), bias_d, online=online))
        return (_mix_a(branches, bsz, seq, online=online),
                _band_b(qb, kb, vb, bias_b, sink2, online=online))

    oa, ob = lax.cond(static_ok, functools.partial(attend, False), functools.partial(attend, True),
                      qa, ka, va, qb, kb, vb, bias_a, bias_b, sink2)
    w_out = w_out.astype(_BF16)
    return _out_proj(x2, [oa.reshape(-1, aw), ob.reshape(-1, bqw)], [w_out[:aw], w_out[aw:]])


def _mixer_c(x2, bsz, seq, g, w_in, qn, kn, w_out):
    qw = C_HEADS * HEAD_DIM
    kvw = C_KV_HEADS * HEAD_DIM
    proj = _norm_proj(x2, g, w_in.astype(_BF16))
    rope = _rope_tables(seq)
    bound = HEAD_DIM * QK_SCALE * LOG2E * jnp.max(jnp.abs(qn)) * jnp.max(jnp.abs(kn))
    static_ok = bound <= SHIFT_MAX
    shift = jnp.where(static_ok, bound, 0.0)
    q = _prep(proj, bsz, seq, 0, qw, gain=qn, rope=rope, scale=QK_SCALE * LOG2E, mode="plain_ext")
    kt = _prep(proj, bsz, seq, qw, kvw, gain=kn, rope=rope, mode="T_ext", shift=shift)
    v = _prep(proj, bsz, seq, qw + kvw, kvw, mode="heads_ext")
    o = lax.cond(static_ok,
                 functools.partial(_dense_attn, online=False),
                 functools.partial(_dense_attn, online=True), q, kt, v)
    return _out_proj(x2, [o.reshape(-1, qw)], [w_out.astype(_BF16)])


def kernel(x, mix_norm, ffn_norm, ab_w_in, a_q_norm, a_k_norm, b_q_norm, b_k_norm, b_sink, ab_w_out,
           c_w_in, c_q_norm, c_k_norm, c_w_out, moe_group_w, moe_group_b, moe_expert_w, moe_expert_b,
           moe_w_gate, moe_w_up, moe_w_down):
    bsz, seq, d = x.shape
    x2 = x.reshape(bsz * seq, d)
    depth = mix_norm.shape[0]
    for layer in range(depth):
        i = layer // 2
        if layer % 2 == 0:
            x2 = _mixer_ab(x2, bsz, seq, mix_norm[layer], ab_w_in[i], a_q_norm[i], a_k_norm[i],
                           b_q_norm[i], b_k_norm[i], b_sink[i], ab_w_out[i])
        else:
            x2 = _mixer_c(x2, bsz, seq, mix_norm[layer], c_w_in[i], c_q_norm[i], c_k_norm[i],
                          c_w_out[i])
        x2 = _moe(x2, ffn_norm[layer], moe_group_w[layer], moe_group_b[layer], moe_expert_w[layer],
                  moe_expert_b[layer], moe_w_gate, moe_w_up, moe_w_down, layer)
    return x2.reshape(bsz, seq, d)
```

```python
import functools
import math

import jax
import jax.numpy as jnp
import numpy as np
from jax import lax
from jax.experimental import pallas as pl
from jax.experimental.pallas import tpu as pltpu
from jax.experimental.pallas import tpu_sc as plsc

HEAD_DIM = 64
LANES = 128
N_HEADS = 16
A_HEADS = 8
B_HEADS = 8
B_KV_HEADS = 2
C_HEADS = 16
C_KV_HEADS = 4
A_PATTERNS = ((128, 1), (512, 4), (2048, 16))
B_WINDOW = 128
GRID_W = 64
ROPE_THETA = 10000.0
ROPE_AXIS_DIM = HEAD_DIM // 2
ALIBI_MAX_BIAS = 8.0
RMS_EPS = 1e-6
MOE_GROUPS = 4
MOE_EXPERTS = 8
N_EXPERTS = MOE_GROUPS * MOE_EXPERTS
MOE_D_FF = 256
QK_SCALE = HEAD_DIM ** -0.5
LOG2E = math.log2(math.e)
SHIFT_MAX = 60.0
NEG_BIG = -1e30
VMEM_LIMIT = 52 * 1024 * 1024

BAND_BQ = 256
BAND_KB = 128
BAND_KW = BAND_BQ + 2 * BAND_KB
DENSE_BQ = 256
DENSE_BK = 8192
DENSE_BK_ONLINE = 512
MOE_TM = 512
ROUTE_E0 = 4
PACK_W = 256
SC_WINDOW = 128

_BF16 = jnp.bfloat16
_F32 = jnp.float32


def _params(*sem):
    return pltpu.CompilerParams(dimension_semantics=sem, vmem_limit_bytes=VMEM_LIMIT)


def _alibi_slopes(n):
    return np.asarray(2.0 ** (-ALIBI_MAX_BIAS * np.arange(1, n + 1) / n), dtype=np.float32)


def _norm_proj_kernel(x_ref, g_ref, w_ref, o_ref):
    x = x_ref[...]
    ms = jnp.mean(x * x, axis=-1, keepdims=True)
    h = (x * lax.rsqrt(ms + RMS_EPS) * g_ref[...]).astype(_BF16)
    o_ref[...] = jnp.dot(h, w_ref[...], preferred_element_type=_F32)


def _norm_proj(x2, g, w, tm=512):
    n, d = x2.shape
    p = w.shape[1]
    return pl.pallas_call(
        _norm_proj_kernel,
        grid=(n // tm,),
        in_specs=[pl.BlockSpec((tm, d), lambda i: (i, 0)),
                  pl.BlockSpec((1, d), lambda i: (0, 0)),
                  pl.BlockSpec((d, p), lambda i: (0, 0))],
        out_specs=pl.BlockSpec((tm, p), lambda i: (i, 0)),
        out_shape=jax.ShapeDtypeStruct((n, p), _F32),
        compiler_params=_params("parallel"),
        name="norm_proj",
    )(x2, g.reshape(1, d), w)


def _head_norm(y, gain):
    lane = lax.broadcasted_iota(jnp.int32, y.shape, 1)
    lo = lane < HEAD_DIM
    y2 = y * y
    ms_lo = jnp.sum(jnp.where(lo, y2, 0.0), axis=-1, keepdims=True) * (1.0 / HEAD_DIM)
    ms_hi = jnp.sum(jnp.where(lo, 0.0, y2), axis=-1, keepdims=True) * (1.0 / HEAD_DIM)
    inv = jnp.where(lo, lax.rsqrt(ms_lo + RMS_EPS), lax.rsqrt(ms_hi + RMS_EPS))
    return y * inv * gain


def _rope(y, cos, sin):
    lane = lax.broadcasted_iota(jnp.int32, y.shape, 1)
    first = (lane % 32) < 16
    partner = jnp.where(first, pltpu.roll(y, LANES - 16, axis=1), pltpu.roll(y, 16, axis=1))
    return y * cos + partner * sin


def _prep_kernel(*refs, ncols, norm, rope, scale, mode, dils, tm):
    it = iter(refs)
    shift_ref = next(it) if mode == "T_ext" else None
    x_ref = next(it)
    g_ref = next(it) if norm else None
    cos_ref = next(it) if rope else None
    sin_ref = next(it) if rope else None
    outs = list(it)
    scr = outs.pop() if mode in ("strided", "strided_ext", "stridedT") else None
    for c in range(ncols // LANES):
        sl = slice(c * LANES, (c + 1) * LANES)
        y = x_ref[:, sl]
        if norm:
            y = _head_norm(y, g_ref[...])
        if rope:
            y = _rope(y, cos_ref[...], sin_ref[...])
        if scale != 1.0:
            y = y * scale
        if mode == "plain":
            outs[0][:, sl] = y.astype(_BF16)
        elif mode == "T":
            outs[0][sl, :] = y.T.astype(_BF16)
        elif mode == "heads":
            outs[0][2 * c] = y[:, :HEAD_DIM].astype(_BF16)
            outs[0][2 * c + 1] = y[:, HEAD_DIM:].astype(_BF16)
        elif mode in ("plain_ext", "heads_ext"):
            lane = lax.broadcasted_iota(jnp.int32, y.shape, 1)
            if mode == "plain_ext":
                fill = jnp.where(lane == HEAD_DIM, 1.0, 0.0)
            else:
                fill = jnp.ones(y.shape, _F32)
            for k, yk in enumerate((y, pltpu.roll(y, HEAD_DIM, axis=1))):
                ext = jnp.where(lane < HEAD_DIM, yk, fill).astype(_BF16)
                if mode == "plain_ext":
                    outs[0][:, (2 * c + k) * LANES:(2 * c + k + 1) * LANES] = ext
                else:
                    outs[0][2 * c + k] = ext
        elif mode == "T_ext":
            yt = y.T
            row = lax.broadcasted_iota(jnp.int32, (HEAD_DIM, tm), 0)
            extra = jnp.where(row == 0, -shift_ref[0], 0.0)
            for k in range(2):
                ext = jnp.concatenate([yt[k * HEAD_DIM:(k + 1) * HEAD_DIM], extra], axis=0)
                outs[0][(2 * c + k) * LANES:(2 * c + k + 1) * LANES, :] = ext.astype(_BF16)
        else:
            scr[c] = y
    if scr is not None:
        for o_ref, d in zip(outs, dils):
            for r in range(d):
                for c in range(ncols // LANES):
                    sl = slice(c * LANES, (c + 1) * LANES)
                    rows = scr[c, pl.ds(r, tm // d, stride=d), :] if d > 1 else scr[c]
                    if mode == "strided":
                        o_ref[r, :, sl] = rows.astype(_BF16)
                    elif mode == "strided_ext":
                        lane = lax.broadcasted_iota(jnp.int32, rows.shape, 1)
                        for k, rk in enumerate((rows, pltpu.roll(rows, HEAD_DIM, axis=1))):
                            ext = jnp.where(lane < HEAD_DIM, rk, 1.0).astype(_BF16)
                            o_ref[r, :, (2 * c + k) * LANES:(2 * c + k + 1) * LANES] = ext
                    else:
                        o_ref[r, sl, :] = rows.T.astype(_BF16)


def _prep(proj, bsz, seq, col0, ncols, *, gain=None, rope=None, scale=1.0, mode="plain",
          dils=(), tm=512, shift=None):
    assert col0 % ncols == 0 and seq % tm == 0
    nt = seq // tm
    cb = col0 // ncols
    ins = [proj]
    in_specs = [pl.BlockSpec((tm, ncols), lambda b, i: (b * nt + i, cb))]
    if mode == "T_ext":
        ins.insert(0, shift.reshape(1).astype(_F32))
        in_specs.insert(0, pl.BlockSpec(memory_space=pltpu.SMEM))
    if gain is not None:
        ins.append(jnp.tile(gain.astype(_F32), LANES // HEAD_DIM).reshape(1, LANES))
        in_specs.append(pl.BlockSpec((1, LANES), lambda b, i: (0, 0)))
    if rope is not None:
        ins += list(rope)
        in_specs += [pl.BlockSpec((tm, LANES), lambda b, i: (i, 0))] * 2
    scratch = []
    if mode == "plain":
        out_shape = [jax.ShapeDtypeStruct((bsz, seq, ncols), _BF16)]
        out_specs = [pl.BlockSpec((None, tm, ncols), lambda b, i: (b, i, 0))]
    elif mode == "T":
        out_shape = [jax.ShapeDtypeStruct((bsz, ncols, seq), _BF16)]
        out_specs = [pl.BlockSpec((None, ncols, tm), lambda b, i: (b, 0, i))]
    elif mode == "heads":
        nh = ncols // HEAD_DIM
        out_shape = [jax.ShapeDtypeStruct((bsz, nh, seq, HEAD_DIM), _BF16)]
        out_specs = [pl.BlockSpec((None, nh, tm, HEAD_DIM), lambda b, i: (b, 0, i, 0))]
    elif mode == "plain_ext":
        out_shape = [jax.ShapeDtypeStruct((bsz, seq, 2 * ncols), _BF16)]
        out_specs = [pl.BlockSpec((None, tm, 2 * ncols), lambda b, i: (b, i, 0))]
    elif mode == "T_ext":
        out_shape = [jax.ShapeDtypeStruct((bsz, 2 * ncols, seq), _BF16)]
        out_specs = [pl.BlockSpec((None, 2 * ncols, tm), lambda b, i: (b, 0, i))]
    elif mode == "heads_ext":
        nh = ncols // HEAD_DIM
        out_shape = [jax.ShapeDtypeStruct((bsz, nh, seq, LANES), _BF16)]
        out_specs = [pl.BlockSpec((None, nh, tm, LANES), lambda b, i: (b, 0, i, 0))]
    elif mode in ("strided", "strided_ext"):
        wout = ncols if mode == "strided" else 2 * ncols
        out_shape = [jax.ShapeDtypeStruct((bsz, d, seq // d, wout), _BF16) for d in dils]
        out_specs = [pl.BlockSpec((None, d, tm // d, wout), lambda b, i: (b, 0, i, 0)) for d in dils]
        scratch = [pltpu.VMEM((ncols // LANES, tm, LANES), _F32)]
    else:
        out_shape = [jax.ShapeDtypeStruct((bsz, d, ncols, seq // d), _BF16) for d in dils]
        out_specs = [pl.BlockSpec((None, d, ncols, tm // d), lambda b, i: (b, 0, 0, i)) for d in dils]
        scratch = [pltpu.VMEM((ncols // LANES, tm, LANES), _F32)]
    kern = functools.partial(_prep_kernel, ncols=ncols, norm=gain is not None, rope=rope is not None,
                             scale=scale, mode=mode, dils=dils, tm=tm)
    out = pl.pallas_call(
        kern, grid=(bsz, nt), in_specs=in_specs, out_specs=out_specs, out_shape=out_shape,
        scratch_shapes=scratch, compiler_params=_params("parallel", "parallel"),
        name="prep_" + mode,
    )(*ins)
    return out if len(out) > 1 else out[0]


def _rope_tables(seq):
    t = np.arange(seq)
    row = (t // GRID_W).astype(np.float32)
    col = (t % GRID_W).astype(np.float32)
    inv_freq = jnp.asarray(ROPE_THETA, _F32) ** (-jnp.arange(0, ROPE_AXIS_DIM, 2, dtype=_F32) / ROPE_AXIS_DIM)
    ang_r = jnp.asarray(row)[:, None] * inv_freq[None, :]
    ang_c = jnp.asarray(col)[:, None] * inv_freq[None, :]
    cr, sr, cc, sc = jnp.cos(ang_r), jnp.sin(ang_r), jnp.cos(ang_c), jnp.sin(ang_c)
    cos = jnp.concatenate([cr, cr, cc, cc], axis=-1)
    sin = jnp.concatenate([-sr, sr, -sc, sc], axis=-1)
    return jnp.tile(cos, (1, 2)), jnp.tile(sin, (1, 2))


def _band_bias(hw, dist_scale, slopes, shift, stack):
    bq, kb = BAND_BQ, BAND_KB
    assert hw <= kb
    r = np.arange(bq)[:, None]
    c = np.arange(BAND_KW)[None, :]
    rel = np.abs(c - kb - r)
    dist = rel.astype(np.float32) * np.float32(dist_scale)
    ok = np.stack([(rel <= hw) & ~(first & (c < kb)) & ~(last & (c >= kb + bq))
                   for first, last in ((False, False), (True, False), (False, True), (True, True))])
    alibi = -(np.asarray(slopes, np.float32)[:, None, None] * dist[None]) * np.float32(LOG2E)
    bias = jnp.where(ok[:, None], jnp.asarray(alibi)[None] - shift, NEG_BIG)
    nh = len(slopes)
    return bias.reshape(4, nh // stack, stack * bq, BAND_KW)


def _edge_variant(i, nb):
    return jnp.where(i == 0, 1, 0) + jnp.where(i == nb - 1, 2, 0)


def _window_blocks(nb):
    per = BAND_BQ // BAND_KB
    last = nb * per - 1
    fns = [lambda i: jnp.maximum(per * i - 1, 0)]
    fns += [functools.partial(lambda j, i: per * i + j, j) for j in range(per)]
    fns += [lambda i: jnp.minimum(per * i + per, last)]
    return fns


def _band_a_kernel(q_ref, *refs, online):
    npc = BAND_KW // BAND_KB
    k_refs, v_refs, (bias_ref, o_ref) = refs[:npc], refs[npc:2 * npc], refs[2 * npc:]
    kt = jnp.concatenate([r[...] for r in k_refs], axis=1)
    v = jnp.concatenate([r[...] for r in v_refs], axis=0)
    q = q_ref[...]
    for h in range(A_HEADS):
        hs = slice(h * HEAD_DIM, (h + 1) * HEAD_DIM)
        es = slice(h * LANES, (h + 1) * LANES)
        s = jnp.dot(q[:, hs], kt[hs, :], preferred_element_type=_F32) + bias_ref[h]
        if online:
            m = jnp.max(s, axis=-1, keepdims=True)
            acc = jnp.dot(jnp.exp2(s - m).astype(_BF16), v[:, es], preferred_element_type=_F32)
            lane = lax.broadcasted_iota(jnp.int32, acc.shape, 1)
            o_ref[:, es] = jnp.where(lane < HEAD_DIM, acc / pltpu.roll(acc, HEAD_DIM, axis=1),
                                     m + jnp.log2(acc))
        else:
            o_ref[:, es] = jnp.dot(jnp.exp2(s).astype(_BF16), v[:, es], preferred_element_type=_F32)


def _band_a(q, kt, v, bias, *, online):
    bd, seq_len, w = q.shape
    we = v.shape[-1]
    bq, kb = BAND_BQ, BAND_KB
    nb = seq_len // bq
    blocks = _window_blocks(nb)
    kspec = lambda f: pl.BlockSpec((None, w, kb), lambda b, i: (b, 0, f(i)))
    vspec = lambda f: pl.BlockSpec((None, kb, we), lambda b, i: (b, f(i), 0))
    return pl.pallas_call(
        functools.partial(_band_a_kernel, online=online), grid=(bd, nb),
        in_specs=[pl.BlockSpec((None, bq, w), lambda b, i: (b, i, 0))]
        + [kspec(f) for f in blocks] + [vspec(f) for f in blocks]
        + [pl.BlockSpec((None, A_HEADS, bq, BAND_KW),
                        lambda b, i: (_edge_variant(i, nb), 0, 0, 0))],
        out_specs=pl.BlockSpec((None, bq, we), lambda b, i: (b, i, 0)),
        out_shape=jax.ShapeDtypeStruct((bd, seq_len, we), _F32),
        compiler_params=_params("parallel", "parallel"),
        name="band_a_online" if online else "band_a",
    )(q, *([kt] * len(blocks)), *([v] * len(blocks)), bias)


def _band_b_kernel(sink2_ref, sinkp_ref, q_ref, *refs, online):
    bq = q_ref.shape[0]
    g = pl.program_id(1)
    grp = B_HEADS // B_KV_HEADS
    npc = BAND_KW // BAND_KB
    k_refs, v_refs, (bias_ref, o_ref) = refs[:npc], refs[npc:2 * npc], refs[2 * npc:]
    kt = jnp.concatenate([r[...] for r in k_refs], axis=1)
    v = jnp.concatenate([r[...] for r in v_refs], axis=0)
    q = q_ref[...]
    q4 = jnp.concatenate([q[:, i * HEAD_DIM:(i + 1) * HEAD_DIM] for i in range(grp)], axis=0)
    s = jnp.dot(q4, kt, preferred_element_type=_F32) + bias_ref[...]
    if online:
        m = jnp.max(s, axis=-1, keepdims=True)
        s = s - m
    acc = jnp.dot(jnp.exp2(s).astype(_BF16), v, preferred_element_type=_F32)
    for i in range(grp):
        a = acc[i * bq:(i + 1) * bq]
        if online:
            mi = m[i * bq:(i + 1) * bq]
            sk = sink2_ref[g * grp + i]
            mm = jnp.maximum(mi, sk)
            a = a * jnp.exp2(mi - mm)
            o = a / (pltpu.roll(a, HEAD_DIM, axis=1) + jnp.exp2(sk - mm))
        else:
            o = a / (pltpu.roll(a, HEAD_DIM, axis=1) + sinkp_ref[g * grp + i])
        o_ref[:, i * HEAD_DIM:(i + 1) * HEAD_DIM] = o[:, :HEAD_DIM].astype(_BF16)


def _band_b(q, kt, v, bias, sink2, *, online):
    bsz, seq_len, w = q.shape
    bq, kb = BAND_BQ, BAND_KB
    nb = seq_len // bq
    grp = B_HEADS // B_KV_HEADS
    gw = w // B_KV_HEADS
    blocks = _window_blocks(nb)
    kspec = lambda f: pl.BlockSpec((None, HEAD_DIM, kb), lambda b, g, i: (b, g, f(i)))
    vspec = lambda f: pl.BlockSpec((None, None, kb, LANES), lambda b, g, i: (b, g, f(i), 0))
    qspec = pl.BlockSpec((None, bq, gw), lambda b, g, i: (b, i, g))
    smem = pl.BlockSpec(memory_space=pltpu.SMEM)
    return pl.pallas_call(
        functools.partial(_band_b_kernel, online=online), grid=(bsz, B_KV_HEADS, nb),
        in_specs=[smem, smem, qspec] + [kspec(f) for f in blocks] + [vspec(f) for f in blocks]
        + [pl.BlockSpec((None, None, grp * bq, BAND_KW),
                        lambda b, g, i: (_edge_variant(i, nb), g, 0, 0))],
        out_specs=qspec,
        out_shape=jax.ShapeDtypeStruct((bsz, seq_len, w), _BF16),
        compiler_params=_params("parallel", "parallel", "parallel"),
        name="band_b_online" if online else "band_b",
    )(sink2, jnp.exp2(sink2), q, *([kt] * len(blocks)), *([v] * len(blocks)), bias)


def _mix_a_kernel(r1, r4, r16, out_ref, s4, s16, *, tm, online):
    for src, dst, d in ((r4, s4, 4), (r16, s16, 16)):
        for r in range(d):
            for h in range(A_HEADS):
                dst[h, pl.ds(r, tm // d, stride=d), :] = src[r, :, h * LANES:(h + 1) * LANES]
    for h in range(A_HEADS):
        a, b, c = r1[0, :, h * LANES:(h + 1) * LANES], s4[h], s16[h]
        if online:
            la, lb, lc = (pltpu.roll(t, HEAD_DIM, axis=1) for t in (a, b, c))
            m = jnp.maximum(jnp.maximum(la, lb), lc)
            ea, eb, ec = jnp.exp2(la - m), jnp.exp2(lb - m), jnp.exp2(lc - m)
            o = (ea * a + eb * b + ec * c) / (ea + eb + ec)
        else:
            t = a + b + c
            o = t / pltpu.roll(t, HEAD_DIM, axis=1)
        out_ref[:, h * HEAD_DIM:(h + 1) * HEAD_DIM] = o[:, :HEAD_DIM].astype(_BF16)


def _mix_a(branches, bsz, seq, *, online, tm=512):
    we = branches[0].shape[-1]
    ins, in_specs = [], []
    for res, (_, d) in zip(branches, A_PATTERNS):
        ins.append(res.reshape(bsz, d, seq // d, we))
        in_specs.append(pl.BlockSpec((None, d, tm // d, we), lambda b, i: (b, 0, i, 0)))
    w = A_HEADS * HEAD_DIM
    return pl.pallas_call(
        functools.partial(_mix_a_kernel, tm=tm, online=online), grid=(bsz, seq // tm),
        in_specs=in_specs,
        out_specs=pl.BlockSpec((None, tm, w), lambda b, i: (b, i, 0)),
        out_shape=jax.ShapeDtypeStruct((bsz, seq, w), _BF16),
        scratch_shapes=[pltpu.VMEM((A_HEADS, tm, LANES), _F32)] * 2,
        compiler_params=_params("parallel", "parallel"),
        name="mix_a_online" if online else "mix_a",
    )(*ins)


def _dense_attn_kernel(q_ref, kt_ref, v_ref, o_ref, acc_scr, *rest, bk, online):
    seq = kt_ref.shape[1]
    grp = C_HEADS // C_KV_HEADS
    acc_scr[...] = jnp.zeros(acc_scr.shape, _F32)
    if online:
        m_scr, = rest
        m_scr[...] = jnp.full(m_scr.shape, NEG_BIG, _F32)

    def body(j, carry):
        k0 = pl.multiple_of(j * bk, bk)
        kt = kt_ref[:, pl.ds(k0, bk)]
        v = v_ref[pl.ds(k0, bk), :]
        for c in range(grp):
            s = jnp.dot(q_ref[:, c * LANES:(c + 1) * LANES], kt, preferred_element_type=_F32)
            if online:
                m_old = m_scr[c]
                m_new = jnp.maximum(m_old, jnp.max(s, axis=-1, keepdims=True))
                m_scr[c] = m_new
                p = jnp.exp2(s - m_new).astype(_BF16)
                acc_scr[c] = jnp.exp2(m_old - m_new) * acc_scr[c] + jnp.dot(
                    p, v, preferred_element_type=_F32)
            else:
                p = jnp.exp2(s).astype(_BF16)
                acc_scr[c] += jnp.dot(p, v, preferred_element_type=_F32)
        return carry

    lax.fori_loop(0, seq // bk, body, 0)
    for c in range(grp):
        a = acc_scr[c]
        o = a / pltpu.roll(a, HEAD_DIM, axis=1)
        o_ref[:, c * HEAD_DIM:(c + 1) * HEAD_DIM] = o[:, :HEAD_DIM].astype(_BF16)


def _dense_attn(q, kt, v, *, online, bq=DENSE_BQ, bk=DENSE_BK):
    bsz, seq, _ = q.shape
    bq, bk = min(bq, seq), min(DENSE_BK_ONLINE if online else bk, seq)
    grp = C_HEADS // C_KV_HEADS
    scratch = [pltpu.VMEM((grp, bq, LANES), _F32)]
    if online:
        scratch.append(pltpu.VMEM((grp, bq, 1), _F32))
    return pl.pallas_call(
        functools.partial(_dense_attn_kernel, bk=bk, online=online),
        grid=(bsz, C_KV_HEADS, seq // bq),
        in_specs=[pl.BlockSpec((None, bq, grp * LANES), lambda b, g, i: (b, i, g)),
                  pl.BlockSpec((None, LANES, seq), lambda b, g, i: (b, g, 0)),
                  pl.BlockSpec((None, None, seq, LANES), lambda b, g, i: (b, g, 0, 0))],
        out_specs=pl.BlockSpec((None, bq, grp * HEAD_DIM), lambda b, g, i: (b, i, g)),
        out_shape=jax.ShapeDtypeStruct((bsz, seq, C_HEADS * HEAD_DIM), _BF16),
        scratch_shapes=scratch,
        compiler_params=_params("parallel", "parallel", "parallel"),
        name="dense_attn_online" if online else "dense_attn",
    )(q, kt, v)


def _out_proj_kernel(*refs, n_in):
    x_ref = refs[0]
    o_refs = refs[1:1 + n_in]
    w_refs = refs[1 + n_in:1 + 2 * n_in]
    out_ref = refs[-1]
    acc = x_ref[...]
    for o_ref, w_ref in zip(o_refs, w_refs):
        acc = acc + jnp.dot(o_ref[...], w_ref[...], preferred_element_type=_F32)
    out_ref[...] = acc


def _out_proj(x2, os_, ws, tm=512):
    n, d = x2.shape
    in_specs = [pl.BlockSpec((tm, d), lambda i: (i, 0))]
    in_specs += [pl.BlockSpec((tm, o.shape[1]), lambda i: (i, 0)) for o in os_]
    in_specs += [pl.BlockSpec(w.shape, lambda i: (0, 0)) for w in ws]
    return pl.pallas_call(
        functools.partial(_out_proj_kernel, n_in=len(os_)), grid=(n // tm,),
        in_specs=in_specs,
        out_specs=pl.BlockSpec((tm, d), lambda i: (i, 0)),
        out_shape=jax.ShapeDtypeStruct((n, d), _F32),
        compiler_params=_params("parallel"),
        name="out_proj",
    )(x2, *os_, *ws)


def _pack_pair(a, b):
    wa = lax.bitcast_convert_type(a.astype(_BF16).astype(_F32), jnp.uint32) >> 16
    wb = lax.bitcast_convert_type(b.astype(_BF16).astype(_F32), jnp.uint32) & jnp.uint32(0xFFFF0000)
    return lax.bitcast_convert_type(wa | wb, jnp.int32)


def _unpack_pair(w):
    u = lax.bitcast_convert_type(w, jnp.uint32)
    return (lax.bitcast_convert_type(u << 16, _F32),
            lax.bitcast_convert_type(u & jnp.uint32(0xFFFF0000), _F32))


def _store_packed(ref, y):
    q = PACK_W
    for j in range(2):
        ref[j] = _pack_pair(y[:, 2 * j * q:(2 * j + 1) * q], y[:, (2 * j + 1) * q:(2 * j + 2) * q])


def _load_packed(ref):
    parts = []
    for j in range(2):
        parts += list(_unpack_pair(ref[j]))
    return jnp.concatenate(parts, axis=1)


def _sc_mesh():
    return plsc.VectorSubcoreMesh(core_axis_name="core", subcore_axis_name="subcore")


def _sc_gather(table, idx):
    n = idx.shape[0]
    d = table.shape[1]

    @pl.kernel(out_type=jax.ShapeDtypeStruct((n, d), table.dtype), mesh=_sc_mesh())
    def gather(x_hbm, i_hbm, o_hbm):
        def body(i_vmem, o_vmem):
            pltpu.sync_copy(x_hbm.at[i_vmem.at[0]], o_vmem)

        pltpu.emit_pipeline(
            body, grid=(n // SC_WINDOW,),
            in_specs=[pl.BlockSpec((1, SC_WINDOW), index_map=lambda i: (0, i))],
            out_specs=[pl.BlockSpec((SC_WINDOW, d), index_map=lambda i: (i, 0))],
            core_axis_name=("core", "subcore"),
            dimension_semantics=(pltpu.PARALLEL,),
        )(i_hbm, o_hbm)

    return gather(table, idx.reshape(1, n))


def _sc_scatter(src, idx, n_out, reps):
    n = idx.shape[0]
    r2, d = src.shape
    nb = r2 // 2 // SC_WINDOW

    @pl.kernel(out_type=jax.ShapeDtypeStruct((n_out, d), src.dtype), mesh=_sc_mesh())
    def scatter(x_hbm, i_hbm, o_hbm):
        def body(x_vmem, i_vmem):
            pltpu.sync_copy(x_vmem, o_hbm.at[i_vmem.at[0]])

        pltpu.emit_pipeline(
            body, grid=(n // SC_WINDOW,),
            in_specs=[pl.BlockSpec((SC_WINDOW, d),
                                   index_map=lambda i: ((i // (reps * nb)) * nb + i % nb, 0)),
                      pl.BlockSpec((1, SC_WINDOW), index_map=lambda i: (0, i))],
            out_specs=[],
            core_axis_name=("core", "subcore"),
            dimension_semantics=(pltpu.PARALLEL,),
        )(x_hbm, i_hbm)

    return scatter(src, idx.reshape(1, n))


def _router_kernel(x_ref, g_ref, w_ref, b_ref, h_ref, route_ref, cnt_ref, carry_ref):
    tm = x_ref.shape[0]

    @pl.when(pl.program_id(0) == 0)
    def _():
        carry_ref[...] = jnp.zeros(carry_ref.shape, _F32)

    x = x_ref[...]
    ms = jnp.mean(x * x, axis=-1, keepdims=True)
    h = x * lax.rsqrt(ms + RMS_EPS) * g_ref[...]
    _store_packed(h_ref, h)
    logits = jnp.dot(h, w_ref[...], preferred_element_type=_F32,
                     precision=lax.Precision.HIGHEST) + b_ref[...]
    lane = lax.broadcasted_iota(jnp.int32, (tm, LANES), 1)

    gmask = lane < MOE_GROUPS
    lg = jnp.where(gmask, logits, NEG_BIG)
    mg = jnp.max(lg, axis=-1, keepdims=True)
    zg = jnp.sum(jnp.exp(lg - mg), axis=-1, keepdims=True)
    p_grp = 1.0 / zg
    g_idx = jnp.min(jnp.where(lg == mg, lane, LANES), axis=-1, keepdims=True)

    e_lane = lane - ROUTE_E0
    emask = (e_lane >= 0) & (e_lane < N_EXPERTS) & ((e_lane // MOE_EXPERTS) == g_idx)
    le = jnp.where(emask, logits, NEG_BIG)
    m1 = jnp.max(le, axis=-1, keepdims=True)
    i1 = jnp.min(jnp.where(le == m1, lane, LANES), axis=-1, keepdims=True)
    le2 = jnp.where(lane == i1, NEG_BIG, le)
    m2 = jnp.max(le2, axis=-1, keepdims=True)
    i2 = jnp.min(jnp.where(le2 == m2, lane, LANES), axis=-1, keepdims=True)
    e21 = jnp.exp(m2 - m1)
    c1 = p_grp / (1.0 + e21)
    c2 = p_grp * e21 / (1.0 + e21)

    onehot = jnp.where(lane == i1, 1.0, jnp.where(lane == i2, 1.0, 0.0)).astype(_BF16)
    tri = (lax.broadcasted_iota(jnp.int32, (tm, tm), 0)
           >= lax.broadcasted_iota(jnp.int32, (tm, tm), 1)).astype(_BF16)
    cum = jnp.dot(tri, onehot, preferred_element_type=_F32) + carry_ref[...]
    r1 = jnp.sum(jnp.where(lane == i1, cum, 0.0), axis=-1, keepdims=True) - 1.0
    r2 = jnp.sum(jnp.where(lane == i2, cum, 0.0), axis=-1, keepdims=True) - 1.0
    carry_ref[...] = cum[tm - 1:tm, :]
    cnt_ref[...] = cum[tm - 1:tm, :]

    cols = ((i1 - ROUTE_E0).astype(_F32), (i2 - ROUTE_E0).astype(_F32), c1, c2, r1, r2)
    route = jnp.zeros((tm, LANES), _F32)
    for k, col in enumerate(cols):
        route = jnp.where(lane == k, col, route)
    route_ref[...] = route


def _router(x2, g, w_router, b_router, tm=512):
    n, d = x2.shape
    return pl.pallas_call(
        _router_kernel, grid=(n // tm,),
        in_specs=[pl.BlockSpec((tm, d), lambda i: (i, 0)),
                  pl.BlockSpec((1, d), lambda i: (0, 0)),
                  pl.BlockSpec((d, LANES), lambda i: (0, 0)),
                  pl.BlockSpec((1, LANES), lambda i: (0, 0))],
        out_specs=[pl.BlockSpec((2, tm, PACK_W), lambda i: (0, i, 0)),
                   pl.BlockSpec((tm, LANES), lambda i: (i, 0)),
                   pl.BlockSpec((1, LANES), lambda i: (0, 0))],
        out_shape=[jax.ShapeDtypeStruct((2, n, PACK_W), jnp.int32),
                   jax.ShapeDtypeStruct((n, LANES), _F32),
                   jax.ShapeDtypeStruct((1, LANES), _F32)],
        scratch_shapes=[pltpu.VMEM((1, LANES), _F32)],
        compiler_params=_params("arbitrary"),
        name="router",
    )(x2, g.reshape(1, d), w_router, b_router)


def _expert_kernel(te_ref, nv_ref, xs_ref, wg_ref, wu_ref, wd_ref, ys_ref, wgu_s, wd_s):
    j = pl.program_id(0)
    prev = te_ref[jnp.maximum(j - 1, 0)]

    @pl.when((j == 0) | (te_ref[j] != prev))
    def _():
        wgu_s[:, :MOE_D_FF] = wg_ref[...].astype(_BF16)
        wgu_s[:, MOE_D_FF:] = wu_ref[...].astype(_BF16)
        wd_s[...] = wd_ref[...].astype(_BF16)

    @pl.when(j < nv_ref[0])
    def _():
        xs = _load_packed(xs_ref).astype(_BF16)
        au = jnp.dot(xs, wgu_s[...], preferred_element_type=_F32)
        a, u = au[:, :MOE_D_FF], au[:, MOE_D_FF:]
        act = (a * (1.0 / (1.0 + jnp.exp(-a))) * u).astype(_BF16)
        _store_packed(ys_ref, jnp.dot(act, wd_s[...], preferred_element_type=_F32))


def _experts(tile_expert, n_valid, xs, w_gate, w_up, w_down, tm=MOE_TM):
    _, n_slots, pw = xs.shape
    n_tiles = n_slots // tm
    d, f = w_gate.shape[-2:]
    row = lambda j, te, nv: (0, jnp.maximum(jnp.minimum(j, nv[0] - 1), 0), 0)
    grid_spec = pltpu.PrefetchScalarGridSpec(
        num_scalar_prefetch=2, grid=(n_tiles,),
        in_specs=[pl.BlockSpec((2, tm, pw), row),
                  pl.BlockSpec((None, d, f), lambda j, te, nv: (te[j], 0, 0)),
                  pl.BlockSpec((None, d, f), lambda j, te, nv: (te[j], 0, 0)),
                  pl.BlockSpec((None, f, d), lambda j, te, nv: (te[j], 0, 0))],
        out_specs=pl.BlockSpec((2, tm, pw), row),
        scratch_shapes=[pltpu.VMEM((d, 2 * f), _BF16), pltpu.VMEM((f, d), _BF16)])
    return pl.pallas_call(
        _expert_kernel, grid_spec=grid_spec,
        out_shape=jax.ShapeDtypeStruct((2, n_slots, pw), jnp.int32),
        compiler_params=_params("arbitrary"),
        name="experts",
    )(tile_expert, n_valid, xs, w_gate, w_up, w_down)


def _combine_kernel(x_ref, y_ref, route_ref, o_ref):
    c1 = route_ref[:, 2:3]
    c2 = route_ref[:, 3:4]
    y0 = _load_packed(y_ref[:, 0])
    y1 = _load_packed(y_ref[:, 1])
    o_ref[...] = x_ref[...] + c1 * y0 + c2 * y1


def _combine(x2, y, route, tm=512):
    n, d = x2.shape
    row = pl.BlockSpec((tm, d), lambda i: (i, 0))
    return pl.pallas_call(
        _combine_kernel, grid=(n // tm,),
        in_specs=[row, pl.BlockSpec((2, 2, tm, PACK_W), lambda i: (0, 0, i, 0)),
                  pl.BlockSpec((tm, LANES), lambda i: (i, 0))],
        out_specs=row,
        out_shape=jax.ShapeDtypeStruct((n, d), _F32),
        compiler_params=_params("parallel"),
        name="moe_combine",
    )(x2, y, route)


def _moe(x2, g, wg, bg, we, be, w_gate, w_up, w_down, layer=0):
    n, d = x2.shape
    tm = MOE_TM
    w_router = jnp.zeros((d, LANES), _F32)
    w_router = w_router.at[:, :MOE_GROUPS].set(wg)
    w_router = w_router.at[:, ROUTE_E0:ROUTE_E0 + N_EXPERTS].set(
        jnp.moveaxis(we, 0, 1).reshape(d, N_EXPERTS))
    b_router = jnp.zeros((1, LANES), _F32)
    b_router = b_router.at[0, :MOE_GROUPS].set(bg)
    b_router = b_router.at[0, ROUTE_E0:ROUTE_E0 + N_EXPERTS].set(be.reshape(-1))
    h, route, cnt = _router(x2, g, w_router, b_router)

    expert = route[:, 0:2].astype(jnp.int32)
    rank = route[:, 4:6].astype(jnp.int32)
    counts = cnt[0, ROUTE_E0:ROUTE_E0 + N_EXPERTS].astype(jnp.int32)
    tiles_per = (counts + tm - 1) // tm
    tiles_end = jnp.cumsum(tiles_per)
    offset = (tiles_end - tiles_per) * tm
    pos = offset[expert] + rank
    n_tiles = (2 * n) // tm + N_EXPERTS
    n_valid = tiles_end[-1]
    tile_ids = jnp.minimum(jnp.arange(n_tiles, dtype=jnp.int32), n_valid - 1)
    tile_expert = jnp.sum(tile_ids[:, None] >= tiles_end[None, :], axis=1).astype(jnp.int32)
    n_slots = n_tiles * tm
    slot = jnp.concatenate([pos[:, 0], pos[:, 1], pos[:, 0] + n_slots, pos[:, 1] + n_slots])

    xs = _sc_scatter(h.reshape(2 * n, PACK_W), slot, 2 * n_slots, reps=2)
    ys = _experts(tile_expert + layer * N_EXPERTS, n_valid.reshape(1),
                  xs.reshape(2, n_slots, PACK_W), w_gate.reshape(-1, d, MOE_D_FF),
                  w_up.reshape(-1, d, MOE_D_FF), w_down.reshape(-1, MOE_D_FF, d))
    y = _sc_gather(ys.reshape(2 * n_slots, PACK_W), slot)
    return _combine(x2, y.reshape(2, 2, n, PACK_W), route)


def _mixer_ab(x2, bsz, seq, g, w_in, a_qn, a_kn, b_qn, b_kn, b_sink, w_out):
    aw = A_HEADS * HEAD_DIM
    bqw = B_HEADS * HEAD_DIM
    bkw = B_KV_HEADS * HEAD_DIM
    proj = _norm_proj(x2, g, w_in.astype(_BF16))
    dils = tuple(d for _, d in A_PATTERNS)
    scale2 = QK_SCALE * LOG2E
    qa = _prep(proj, bsz, seq, 0, aw, gain=a_qn, scale=scale2, mode="strided", dils=dils)
    ka = _prep(proj, bsz, seq, aw, aw, gain=a_kn, mode="stridedT", dils=dils, tm=2048)
    va = _prep(proj, bsz, seq, 2 * aw, aw, mode="strided_ext", dils=dils)
    qb = _prep(proj, bsz, seq, 3 * aw, bqw, gain=b_qn, scale=scale2, mode="plain")
    kb = _prep(proj, bsz, seq, 3 * aw + bqw, bkw, gain=b_kn, mode="T")
    vb = _prep(proj, bsz, seq, 3 * aw + bqw + bkw, bkw, mode="heads_ext")

    bound_a = HEAD_DIM * scale2 * jnp.max(jnp.abs(a_qn)) * jnp.max(jnp.abs(a_kn))
    bound_b = HEAD_DIM * scale2 * jnp.max(jnp.abs(b_qn)) * jnp.max(jnp.abs(b_kn))
    static_ok = jnp.maximum(bound_a, bound_b) <= SHIFT_MAX
    shift_a = jnp.where(static_ok, bound_a, 0.0)
    shift_b = jnp.where(static_ok, bound_b, 0.0)
    slopes = _alibi_slopes(A_HEADS + B_HEADS)
    bias_a = [_band_bias((window // 2) // d, float(d), slopes[0::2], shift_a, 1)
              for window, d in A_PATTERNS]
    bias_b = _band_bias(B_WINDOW, 1.0, slopes[1::2], shift_b, B_HEADS // B_KV_HEADS)
    sink2 = b_sink.astype(_F32) * LOG2E - shift_b

    def attend(online, qa, ka, va, qb, kb, vb, bias_a, bias_b, sink2):
        branches = []
        for (_, d), q_d, k_d, v_d, bias_d in zip(A_PATTERNS, qa, ka, va, bias_a):
            ld = seq // d
            branches.append(_band_a(q_d.reshape(bsz * d, ld, aw), k_d.reshape(bsz * d, aw, ld),
                                    v_d.reshape(bsz * d, ld, 2 * aw), bias_d, online=online))
        return (_mix_a(branches, bsz, seq, online=online),
                _band_b(qb, kb, vb, bias_b, sink2, online=online))

    oa, ob = lax.cond(static_ok, functools.partial(attend, False), functools.partial(attend, True),
                      qa, ka, va, qb, kb, vb, bias_a, bias_b, sink2)
    w_out = w_out.astype(_BF16)
    return _out_proj(x2, [oa.reshape(-1, aw), ob.reshape(-1, bqw)], [w_out[:aw], w_out[aw:]])


def _mixer_c(x2, bsz, seq, g, w_in, qn, kn, w_out):
    qw = C_HEADS * HEAD_DIM
    kvw = C_KV_HEADS * HEAD_DIM
    proj = _norm_proj(x2, g, w_in.astype(_BF16))
    rope = _rope_tables(seq)
    bound = HEAD_DIM * QK_SCALE * LOG2E * jnp.max(jnp.abs(qn)) * jnp.max(jnp.abs(kn))
    static_ok = bound <= SHIFT_MAX
    shift = jnp.where(static_ok, bound, 0.0)
    q = _prep(proj, bsz, seq, 0, qw, gain=qn, rope=rope, scale=QK_SCALE * LOG2E, mode="plain_ext")
    kt = _prep(proj, bsz, seq, qw, kvw, gain=kn, rope=rope, mode="T_ext", shift=shift)
    v = _prep(proj, bsz, seq, qw + kvw, kvw, mode="heads_ext")
    o = lax.cond(static_ok,
                 functools.partial(_dense_attn, online=False),
                 functools.partial(_dense_attn, online=True), q, kt, v)
    return _out_proj(x2, [o.reshape(-1, qw)], [w_out.astype(_BF16)])


def kernel(x, mix_norm, ffn_norm, ab_w_in, a_q_norm, a_k_norm, b_q_norm, b_k_norm, b_sink, ab_w_out,
           c_w_in, c_q_norm, c_k_norm, c_w_out, moe_group_w, moe_group_b, moe_expert_w, moe_expert_b,
           moe_w_gate, moe_w_up, moe_w_down):
    bsz, seq, d = x.shape
    x2 = x.reshape(bsz * seq, d)
    depth = mix_norm.shape[0]
    for layer in range(depth):
        i = layer // 2
        if layer % 2 == 0:
            x2 = _mixer_ab(x2, bsz, seq, mix_norm[layer], ab_w_in[i], a_q_norm[i], a_k_norm[i],
                           b_q_norm[i], b_k_norm[i], b_sink[i], ab_w_out[i])
        else:
            x2 = _mixer_c(x2, bsz, seq, mix_norm[layer], c_w_in[i], c_q_norm[i], c_k_norm[i],
                          c_w_out[i])
        x2 = _moe(x2, ffn_norm[layer], moe_group_w[layer], moe_group_b[layer], moe_expert_w[layer],
                  moe_expert_b[layer], moe_w_gate, moe_w_up, moe_w_down, layer)
    return x2.reshape(bsz, seq, d)
```

```python
import functools
import math

import jax
import jax.numpy as jnp
import numpy as np
from jax import lax
from jax.experimental import pallas as pl
from jax.experimental.pallas import tpu as pltpu
from jax.experimental.pallas import tpu_sc as plsc

HEAD_DIM = 64
LANES = 128
N_HEADS = 16
A_HEADS = 8
B_HEADS = 8
B_KV_HEADS = 2
C_HEADS = 16
C_KV_HEADS = 4
A_PATTERNS = ((128, 1), (512, 4), (2048, 16))
B_WINDOW = 128
GRID_W = 64
ROPE_THETA = 10000.0
ROPE_AXIS_DIM = HEAD_DIM // 2
ALIBI_MAX_BIAS = 8.0
RMS_EPS = 1e-6
MOE_GROUPS = 4
MOE_EXPERTS = 8
N_EXPERTS = MOE_GROUPS * MOE_EXPERTS
MOE_D_FF = 256
QK_SCALE = HEAD_DIM ** -0.5
LOG2E = math.log2(math.e)
SHIFT_MAX = 60.0
NEG_BIG = -1e30
VMEM_LIMIT = 52 * 1024 * 1024

BAND_BQ = 256
BAND_KB = 128
BAND_KW = BAND_BQ + 2 * BAND_KB
DIL_BQ = 256
DIL_REACH = max(w // 2 for w, _ in A_PATTERNS)
DIL_KW = DIL_BQ + 2 * DIL_REACH
DIL_CHUNK = 768
DENSE_BQ = 256
DENSE_BK = 8192
DENSE_BK_ONLINE = 512
MOE_TM = 512
ROUTE_E0 = 4
ROUTE_ROWS = 48
PACK_W = 256
SC_WINDOW = 128

_BF16 = jnp.bfloat16
_F32 = jnp.float32


def _params(*sem):
    return pltpu.CompilerParams(dimension_semantics=sem, vmem_limit_bytes=VMEM_LIMIT)


def _alibi_slopes(n):
    return np.asarray(2.0 ** (-ALIBI_MAX_BIAS * np.arange(1, n + 1) / n), dtype=np.float32)


def _norm_proj_kernel(x_ref, g_ref, w_ref, o_ref):
    x = x_ref[...]
    ms = jnp.mean(x * x, axis=-1, keepdims=True)
    h = (x * lax.rsqrt(ms + RMS_EPS) * g_ref[...]).astype(_BF16)
    o_ref[...] = jnp.dot(h, w_ref[...], preferred_element_type=_F32)


def _norm_proj(x2, g, w, tm=512):
    n, d = x2.shape
    p = w.shape[1]
    return pl.pallas_call(
        _norm_proj_kernel,
        grid=(n // tm,),
        in_specs=[pl.BlockSpec((tm, d), lambda i: (i, 0)),
                  pl.BlockSpec((1, d), lambda i: (0, 0)),
                  pl.BlockSpec((d, p), lambda i: (0, 0))],
        out_specs=pl.BlockSpec((tm, p), lambda i: (i, 0)),
        out_shape=jax.ShapeDtypeStruct((n, p), _F32),
        compiler_params=_params("parallel"),
        name="norm_proj",
    )(x2, g.reshape(1, d), w)


def _head_norm(y, gain):
    lane = lax.broadcasted_iota(jnp.int32, y.shape, 1)
    lo = lane < HEAD_DIM
    y2 = y * y
    ms_lo = jnp.sum(jnp.where(lo, y2, 0.0), axis=-1, keepdims=True) * (1.0 / HEAD_DIM)
    ms_hi = jnp.sum(jnp.where(lo, 0.0, y2), axis=-1, keepdims=True) * (1.0 / HEAD_DIM)
    inv = jnp.where(lo, lax.rsqrt(ms_lo + RMS_EPS), lax.rsqrt(ms_hi + RMS_EPS))
    return y * inv * gain


def _rope(y, cos, sin):
    lane = lax.broadcasted_iota(jnp.int32, y.shape, 1)
    first = (lane % 32) < 16
    partner = jnp.where(first, pltpu.roll(y, LANES - 16, axis=1), pltpu.roll(y, 16, axis=1))
    return y * cos + partner * sin


def _prep_kernel(*refs, ncols, norm, rope, scale, mode, tm):
    it = iter(refs)
    shift_ref = next(it) if mode == "T_ext" else None
    x_ref = next(it)
    g_ref = next(it) if norm else None
    cos_ref = next(it) if rope else None
    sin_ref = next(it) if rope else None
    outs = list(it)
    for c in range(ncols // LANES):
        sl = slice(c * LANES, (c + 1) * LANES)
        y = x_ref[:, sl]
        if norm:
            y = _head_norm(y, g_ref[...])
        if rope:
            y = _rope(y, cos_ref[...], sin_ref[...])
        if scale != 1.0:
            y = y * scale
        if mode == "plain":
            outs[0][:, sl] = y.astype(_BF16)
        elif mode == "T":
            outs[0][sl, :] = y.T.astype(_BF16)
        elif mode in ("plain_ext", "heads_ext"):
            lane = lax.broadcasted_iota(jnp.int32, y.shape, 1)
            if mode == "plain_ext":
                fill = jnp.where(lane == HEAD_DIM, 1.0, 0.0)
            else:
                fill = jnp.ones(y.shape, _F32)
            for k, yk in enumerate((y, pltpu.roll(y, HEAD_DIM, axis=1))):
                ext = jnp.where(lane < HEAD_DIM, yk, fill).astype(_BF16)
                if mode == "plain_ext":
                    outs[0][:, (2 * c + k) * LANES:(2 * c + k + 1) * LANES] = ext
                else:
                    outs[0][2 * c + k] = ext
        else:
            assert mode == "T_ext"
            yt = y.T
            row = lax.broadcasted_iota(jnp.int32, (HEAD_DIM, tm), 0)
            extra = jnp.where(row == 0, -shift_ref[0], 0.0)
            for k in range(2):
                ext = jnp.concatenate([yt[k * HEAD_DIM:(k + 1) * HEAD_DIM], extra], axis=0)
                outs[0][(2 * c + k) * LANES:(2 * c + k + 1) * LANES, :] = ext.astype(_BF16)


def _prep(proj, bsz, seq, col0, ncols, *, gain=None, rope=None, scale=1.0, mode="plain",
          tm=512, shift=None):
    assert col0 % ncols == 0 and seq % tm == 0
    nt = seq // tm
    cb = col0 // ncols
    ins = [proj]
    in_specs = [pl.BlockSpec((tm, ncols), lambda b, i: (b * nt + i, cb))]
    if mode == "T_ext":
        ins.insert(0, shift.reshape(1).astype(_F32))
        in_specs.insert(0, pl.BlockSpec(memory_space=pltpu.SMEM))
    if gain is not None:
        ins.append(jnp.tile(gain.astype(_F32), LANES // HEAD_DIM).reshape(1, LANES))
        in_specs.append(pl.BlockSpec((1, LANES), lambda b, i: (0, 0)))
    if rope is not None:
        ins += list(rope)
        in_specs += [pl.BlockSpec((tm, LANES), lambda b, i: (i, 0))] * 2
    if mode == "plain":
        out_shape = jax.ShapeDtypeStruct((bsz, seq, ncols), _BF16)
        out_spec = pl.BlockSpec((None, tm, ncols), lambda b, i: (b, i, 0))
    elif mode == "T":
        out_shape = jax.ShapeDtypeStruct((bsz, ncols, seq), _BF16)
        out_spec = pl.BlockSpec((None, ncols, tm), lambda b, i: (b, 0, i))
    elif mode == "plain_ext":
        out_shape = jax.ShapeDtypeStruct((bsz, seq, 2 * ncols), _BF16)
        out_spec = pl.BlockSpec((None, tm, 2 * ncols), lambda b, i: (b, i, 0))
    elif mode == "T_ext":
        out_shape = jax.ShapeDtypeStruct((bsz, 2 * ncols, seq), _BF16)
        out_spec = pl.BlockSpec((None, 2 * ncols, tm), lambda b, i: (b, 0, i))
    else:
        assert mode == "heads_ext"
        nh = ncols // HEAD_DIM
        out_shape = jax.ShapeDtypeStruct((bsz, nh, seq, LANES), _BF16)
        out_spec = pl.BlockSpec((None, nh, tm, LANES), lambda b, i: (b, 0, i, 0))
    kern = functools.partial(_prep_kernel, ncols=ncols, norm=gain is not None, rope=rope is not None,
                             scale=scale, mode=mode, tm=tm)
    return pl.pallas_call(
        kern, grid=(bsz, nt), in_specs=in_specs, out_specs=out_spec, out_shape=out_shape,
        compiler_params=_params("parallel", "parallel"),
        name="prep_" + mode,
    )(*ins)


def _rope_tables(seq):
    t = np.arange(seq)
    row = (t // GRID_W).astype(np.float32)
    col = (t % GRID_W).astype(np.float32)
    inv_freq = jnp.asarray(ROPE_THETA, _F32) ** (-jnp.arange(0, ROPE_AXIS_DIM, 2, dtype=_F32) / ROPE_AXIS_DIM)
    ang_r = jnp.asarray(row)[:, None] * inv_freq[None, :]
    ang_c = jnp.asarray(col)[:, None] * inv_freq[None, :]
    cr, sr, cc, sc = jnp.cos(ang_r), jnp.sin(ang_r), jnp.cos(ang_c), jnp.sin(ang_c)
    cos = jnp.concatenate([cr, cr, cc, cc], axis=-1)
    sin = jnp.concatenate([-sr, sr, -sc, sc], axis=-1)
    return jnp.tile(cos, (1, 2)), jnp.tile(sin, (1, 2))


def _band_bias(hw, dist_scale, slopes, shift, stack):
    bq, kb = BAND_BQ, BAND_KB
    assert hw <= kb
    r = np.arange(bq)[:, None]
    c = np.arange(BAND_KW)[None, :]
    rel = np.abs(c - kb - r)
    dist = rel.astype(np.float32) * np.float32(dist_scale)
    ok = np.stack([(rel <= hw) & ~(first & (c < kb)) & ~(last & (c >= kb + bq))
                   for first, last in ((False, False), (True, False), (False, True), (True, True))])
    alibi = -(np.asarray(slopes, np.float32)[:, None, None] * dist[None]) * np.float32(LOG2E)
    bias = jnp.where(ok[:, None], jnp.asarray(alibi)[None] - shift, NEG_BIG)
    nh = len(slopes)
    return bias.reshape(4, nh // stack, stack * bq, BAND_KW)


def _edge_variant(i, nb):
    return jnp.where(i == 0, 1, 0) + jnp.where(i == nb - 1, 2, 0)


def _window_blocks(nb):
    per = BAND_BQ // BAND_KB
    last = nb * per - 1
    fns = [lambda i: jnp.maximum(per * i - 1, 0)]
    fns += [functools.partial(lambda j, i: per * i + j, j) for j in range(per)]
    fns += [lambda i: jnp.minimum(per * i + per, last)]
    return fns


def _dil_bias(slopes, shift):
    r = jnp.arange(DIL_BQ, dtype=jnp.int32)[:, None]
    u = jnp.arange(DIL_BQ + 4 * DIL_REACH, dtype=jnp.int32)[None, :]
    delta = u - 2 * DIL_REACH - r
    dist = jnp.abs(delta)
    mult = sum(((delta % d == 0) & (dist <= ((w // 2) // d) * d)).astype(_F32) for w, d in A_PATTERNS)
    alibi = -(jnp.asarray(slopes, _F32)[:, None, None] * dist.astype(_F32)[None]) * LOG2E
    return jnp.where((mult > 0)[None], jnp.log2(jnp.maximum(mult, 1.0))[None] + alibi - shift, NEG_BIG)


def _dil_attn_kernel(q_ref, kt_ref, v_ref, bias_ref, o_ref, *, online):
    bq = q_ref.shape[0]
    seq = kt_ref.shape[1]
    q0 = pl.program_id(2) * bq
    w0 = pl.multiple_of(jnp.clip(q0 - DIL_REACH, 0, seq - DIL_KW), bq)
    u0 = pl.multiple_of(w0 - q0 + 2 * DIL_REACH, bq)
    q = q_ref[...]
    outs = []
    for j in range(2):
        rows = slice(j * HEAD_DIM, (j + 1) * HEAD_DIM)
        scores = []
        for c in range(DIL_KW // DIL_CHUNK):
            ks = pl.ds(pl.multiple_of(w0 + c * DIL_CHUNK, bq), DIL_CHUNK)
            us = pl.ds(pl.multiple_of(u0 + c * DIL_CHUNK, bq), DIL_CHUNK)
            scores.append(jnp.dot(q[:, rows], kt_ref[rows, ks], preferred_element_type=_F32)
                          + bias_ref[j, :, us])
        if online:
            m = functools.reduce(jnp.maximum, [jnp.max(s, axis=-1, keepdims=True) for s in scores])
            scores = [s - m for s in scores]
        acc = jnp.zeros((bq, LANES), _F32)
        for c, s in enumerate(scores):
            ks = pl.ds(pl.multiple_of(w0 + c * DIL_CHUNK, bq), DIL_CHUNK)
            acc += jnp.dot(jnp.exp2(s).astype(_BF16), v_ref[j, ks, :], preferred_element_type=_F32)
        outs.append((acc / pltpu.roll(acc, HEAD_DIM, axis=1))[:, :HEAD_DIM])
    o_ref[...] = jnp.concatenate(outs, axis=1).astype(_BF16)


def _dil_attn(q, kt, v, bias, *, online):
    bsz, seq, w = q.shape
    bq = DIL_BQ
    assert seq >= DIL_KW and seq % bq == 0
    return pl.pallas_call(
        functools.partial(_dil_attn_kernel, online=online), grid=(bsz, A_HEADS // 2, seq // bq),
        in_specs=[pl.BlockSpec((None, bq, LANES), lambda b, p, i: (b, i, p)),
                  pl.BlockSpec((None, LANES, seq), lambda b, p, i: (b, p, 0)),
                  pl.BlockSpec((None, 2, seq, LANES), lambda b, p, i: (b, p, 0, 0)),
                  pl.BlockSpec((2, bq, bias.shape[-1]), lambda b, p, i: (p, 0, 0))],
        out_specs=pl.BlockSpec((None, bq, LANES), lambda b, p, i: (b, i, p)),
        out_shape=jax.ShapeDtypeStruct((bsz, seq, w), _BF16),
        compiler_params=_params("parallel", "parallel", "parallel"),
        name="dil_attn_online" if online else "dil_attn",
    )(q, kt, v, bias)


def _band_b_kernel(sink2_ref, sinkp_ref, q_ref, *refs, online):
    bq = q_ref.shape[0]
    g = pl.program_id(1)
    grp = B_HEADS // B_KV_HEADS
    npc = BAND_KW // BAND_KB
    k_refs, v_refs, (bias_ref, o_ref) = refs[:npc], refs[npc:2 * npc], refs[2 * npc:]
    kt = jnp.concatenate([r[...] for r in k_refs], axis=1)
    v = jnp.concatenate([r[...] for r in v_refs], axis=0)
    q = q_ref[...]
    q4 = jnp.concatenate([q[:, i * HEAD_DIM:(i + 1) * HEAD_DIM] for i in range(grp)], axis=0)
    s = jnp.dot(q4, kt, preferred_element_type=_F32) + bias_ref[...]
    if online:
        m = jnp.max(s, axis=-1, keepdims=True)
        s = s - m
    acc = jnp.dot(jnp.exp2(s).astype(_BF16), v, preferred_element_type=_F32)
    for i in range(grp):
        a = acc[i * bq:(i + 1) * bq]
        if online:
            mi = m[i * bq:(i + 1) * bq]
            sk = sink2_ref[g * grp + i]
            mm = jnp.maximum(mi, sk)
            a = a * jnp.exp2(mi - mm)
            o = a / (pltpu.roll(a, HEAD_DIM, axis=1) + jnp.exp2(sk - mm))
        else:
            o = a / (pltpu.roll(a, HEAD_DIM, axis=1) + sinkp_ref[g * grp + i])
        o_ref[:, i * HEAD_DIM:(i + 1) * HEAD_DIM] = o[:, :HEAD_DIM].astype(_BF16)


def _band_b(q, kt, v, bias, sink2, *, online):
    bsz, seq_len, w = q.shape
    bq, kb = BAND_BQ, BAND_KB
    nb = seq_len // bq
    grp = B_HEADS // B_KV_HEADS
    gw = w // B_KV_HEADS
    blocks = _window_blocks(nb)
    kspec = lambda f: pl.BlockSpec((None, HEAD_DIM, kb), lambda b, g, i: (b, g, f(i)))
    vspec = lambda f: pl.BlockSpec((None, None, kb, LANES), lambda b, g, i: (b, g, f(i), 0))
    qspec = pl.BlockSpec((None, bq, gw), lambda b, g, i: (b, i, g))
    smem = pl.BlockSpec(memory_space=pltpu.SMEM)
    return pl.pallas_call(
        functools.partial(_band_b_kernel, online=online), grid=(bsz, B_KV_HEADS, nb),
        in_specs=[smem, smem, qspec] + [kspec(f) for f in blocks] + [vspec(f) for f in blocks]
        + [pl.BlockSpec((None, None, grp * bq, BAND_KW),
                        lambda b, g, i: (_edge_variant(i, nb), g, 0, 0))],
        out_specs=qspec,
        out_shape=jax.ShapeDtypeStruct((bsz, seq_len, w), _BF16),
        compiler_params=_params("parallel", "parallel", "parallel"),
        name="band_b_online" if online else "band_b",
    )(sink2, jnp.exp2(sink2), q, *([kt] * len(blocks)), *([v] * len(blocks)), bias)


def _dense_attn_kernel(q_ref, kt_ref, v_ref, o_ref, acc_scr, *rest, bk, online):
    seq = kt_ref.shape[1]
    grp = C_HEADS // C_KV_HEADS
    acc_scr[...] = jnp.zeros(acc_scr.shape, _F32)
    if online:
        m_scr, = rest
        m_scr[...] = jnp.full(m_scr.shape, NEG_BIG, _F32)

    def body(j, carry):
        k0 = pl.multiple_of(j * bk, bk)
        kt = kt_ref[:, pl.ds(k0, bk)]
        v = v_ref[pl.ds(k0, bk), :]
        for c in range(grp):
            s = jnp.dot(q_ref[:, c * LANES:(c + 1) * LANES], kt, preferred_element_type=_F32)
            if online:
                m_old = m_scr[c]
                m_new = jnp.maximum(m_old, jnp.max(s, axis=-1, keepdims=True))
                m_scr[c] = m_new
                p = jnp.exp2(s - m_new).astype(_BF16)
                acc_scr[c] = jnp.exp2(m_old - m_new) * acc_scr[c] + jnp.dot(
                    p, v, preferred_element_type=_F32)
            else:
                p = jnp.exp2(s).astype(_BF16)
                acc_scr[c] += jnp.dot(p, v, preferred_element_type=_F32)
        return carry

    lax.fori_loop(0, seq // bk, body, 0)
    for c in range(grp):
        a = acc_scr[c]
        o = a / pltpu.roll(a, HEAD_DIM, axis=1)
        o_ref[:, c * HEAD_DIM:(c + 1) * HEAD_DIM] = o[:, :HEAD_DIM].astype(_BF16)


def _dense_attn(q, kt, v, *, online, bq=DENSE_BQ, bk=DENSE_BK):
    bsz, seq, _ = q.shape
    bq, bk = min(bq, seq), min(DENSE_BK_ONLINE if online else bk, seq)
    grp = C_HEADS // C_KV_HEADS
    scratch = [pltpu.VMEM((grp, bq, LANES), _F32)]
    if online:
        scratch.append(pltpu.VMEM((grp, bq, 1), _F32))
    return pl.pallas_call(
        functools.partial(_dense_attn_kernel, bk=bk, online=online),
        grid=(bsz, C_KV_HEADS, seq // bq),
        in_specs=[pl.BlockSpec((None, bq, grp * LANES), lambda b, g, i: (b, i, g)),
                  pl.BlockSpec((None, LANES, seq), lambda b, g, i: (b, g, 0)),
                  pl.BlockSpec((None, None, seq, LANES), lambda b, g, i: (b, g, 0, 0))],
        out_specs=pl.BlockSpec((None, bq, grp * HEAD_DIM), lambda b, g, i: (b, i, g)),
        out_shape=jax.ShapeDtypeStruct((bsz, seq, C_HEADS * HEAD_DIM), _BF16),
        scratch_shapes=scratch,
        compiler_params=_params("parallel", "parallel", "parallel"),
        name="dense_attn_online" if online else "dense_attn",
    )(q, kt, v)


def _out_proj_kernel(*refs, n_in):
    x_ref = refs[0]
    o_refs = refs[1:1 + n_in]
    w_refs = refs[1 + n_in:1 + 2 * n_in]
    out_ref = refs[-1]
    acc = x_ref[...]
    for o_ref, w_ref in zip(o_refs, w_refs):
        acc = acc + jnp.dot(o_ref[...], w_ref[...], preferred_element_type=_F32)
    out_ref[...] = acc


def _out_proj(x2, os_, ws, tm=512):
    n, d = x2.shape
    in_specs = [pl.BlockSpec((tm, d), lambda i: (i, 0))]
    in_specs += [pl.BlockSpec((tm, o.shape[1]), lambda i: (i, 0)) for o in os_]
    in_specs += [pl.BlockSpec(w.shape, lambda i: (0, 0)) for w in ws]
    return pl.pallas_call(
        functools.partial(_out_proj_kernel, n_in=len(os_)), grid=(n // tm,),
        in_specs=in_specs,
        out_specs=pl.BlockSpec((tm, d), lambda i: (i, 0)),
        out_shape=jax.ShapeDtypeStruct((n, d), _F32),
        compiler_params=_params("parallel"),
        name="out_proj",
    )(x2, *os_, *ws)


def _pack_pair(a, b):
    wa = lax.bitcast_convert_type(a.astype(_BF16).astype(_F32), jnp.uint32) >> 16
    wb = lax.bitcast_convert_type(b.astype(_BF16).astype(_F32), jnp.uint32) & jnp.uint32(0xFFFF0000)
    return lax.bitcast_convert_type(wa | wb, jnp.int32)


def _unpack_pair(w):
    u = lax.bitcast_convert_type(w, jnp.uint32)
    return (lax.bitcast_convert_type(u << 16, _F32),
            lax.bitcast_convert_type(u & jnp.uint32(0xFFFF0000), _F32))


def _store_packed(ref, y):
    q = PACK_W
    for j in range(2):
        ref[j] = _pack_pair(y[:, 2 * j * q:(2 * j + 1) * q], y[:, (2 * j + 1) * q:(2 * j + 2) * q])


def _load_packed(ref):
    parts = []
    for j in range(2):
        parts += list(_unpack_pair(ref[j]))
    return jnp.concatenate(parts, axis=1)


def _sc_mesh():
    return plsc.VectorSubcoreMesh(core_axis_name="core", subcore_axis_name="subcore")


def _sc_gather(table, idx):
    n = idx.shape[0]
    d = table.shape[1]

    @pl.kernel(out_type=jax.ShapeDtypeStruct((n, d), table.dtype), mesh=_sc_mesh())
    def gather(x_hbm, i_hbm, o_hbm):
        def body(i_vmem, o_vmem):
            pltpu.sync_copy(x_hbm.at[i_vmem.at[0]], o_vmem)

        pltpu.emit_pipeline(
            body, grid=(n // SC_WINDOW,),
            in_specs=[pl.BlockSpec((1, SC_WINDOW), index_map=lambda i: (0, i))],
            out_specs=[pl.BlockSpec((SC_WINDOW, d), index_map=lambda i: (i, 0))],
            core_axis_name=("core", "subcore"),
            dimension_semantics=(pltpu.PARALLEL,),
        )(i_hbm, o_hbm)

    return gather(table, idx.reshape(1, n))


def _sc_scatter(src, idx, n_out, reps):
    n = idx.shape[0]
    r2, d = src.shape
    nb = r2 // 2 // SC_WINDOW

    @pl.kernel(out_type=jax.ShapeDtypeStruct((n_out, d), src.dtype), mesh=_sc_mesh())
    def scatter(x_hbm, i_hbm, o_hbm):
        def body(x_vmem, i_vmem):
            pltpu.sync_copy(x_vmem, o_hbm.at[i_vmem.at[0]])

        pltpu.emit_pipeline(
            body, grid=(n // SC_WINDOW,),
            in_specs=[pl.BlockSpec((SC_WINDOW, d),
                                   index_map=lambda i: ((i // (reps * nb)) * nb + i % nb, 0)),
                      pl.BlockSpec((1, SC_WINDOW), index_map=lambda i: (0, i))],
            out_specs=[],
            core_axis_name=("core", "subcore"),
            dimension_semantics=(pltpu.PARALLEL,),
        )(x_hbm, i_hbm)

    return scatter(src, idx.reshape(1, n))


def _router_kernel(x_ref, g_ref, w_ref, b_ref, h_ref, route_ref, cnt_ref, carry_ref):
    tm = x_ref.shape[0]

    @pl.when(pl.program_id(0) == 0)
    def _():
        carry_ref[...] = jnp.zeros(carry_ref.shape, _F32)

    x = x_ref[...]
    ms = jnp.mean(x * x, axis=-1, keepdims=True)
    h = x * lax.rsqrt(ms + RMS_EPS) * g_ref[...]
    _store_packed(h_ref, h)
    h_hi = h.astype(_BF16)
    h_lo = (h - h_hi.astype(_F32)).astype(_BF16)
    hw = (jnp.dot(h_hi, w_ref[...], preferred_element_type=_F32)
          + jnp.dot(h_lo, w_ref[...], preferred_element_type=_F32))
    logits = hw[:, :LANES] + hw[:, LANES:] + b_ref[...]
    lt = logits.T[:ROUTE_ROWS]
    row = lax.broadcasted_iota(jnp.int32, (ROUTE_ROWS, tm), 0)

    lg = jnp.where(row < MOE_GROUPS, lt, NEG_BIG)
    mg = jnp.max(lg, axis=0, keepdims=True)
    zg = jnp.sum(jnp.exp(lg - mg), axis=0, keepdims=True)
    p_grp = 1.0 / zg
    g_idx = jnp.min(jnp.where(lg == mg, row, ROUTE_ROWS), axis=0, keepdims=True)

    e_row = row - ROUTE_E0
    emask = (e_row >= 0) & (e_row < N_EXPERTS) & ((e_row >> 3) == g_idx)
    le = jnp.where(emask, lt, NEG_BIG)
    m1 = jnp.max(le, axis=0, keepdims=True)
    i1 = jnp.min(jnp.where(le == m1, row, ROUTE_ROWS), axis=0, keepdims=True)
    le2 = jnp.where(row == i1, NEG_BIG, le)
    m2 = jnp.max(le2, axis=0, keepdims=True)
    i2 = jnp.min(jnp.where(le2 == m2, row, ROUTE_ROWS), axis=0, keepdims=True)
    e21 = jnp.exp(m2 - m1)
    c1 = p_grp / (1.0 + e21)
    c2 = p_grp * e21 / (1.0 + e21)

    onehot = jnp.where(row == i1, 1.0, jnp.where(row == i2, 1.0, 0.0)).astype(_BF16)
    upper = (lax.broadcasted_iota(jnp.int32, (tm, tm), 0)
             <= lax.broadcasted_iota(jnp.int32, (tm, tm), 1)).astype(_BF16)
    cum = jnp.dot(onehot, upper, preferred_element_type=_F32) + carry_ref[...]
    r1 = jnp.sum(jnp.where(row == i1, cum, 0.0), axis=0, keepdims=True) - 1.0
    r2 = jnp.sum(jnp.where(row == i2, cum, 0.0), axis=0, keepdims=True) - 1.0
    total = jnp.max(cum, axis=1, keepdims=True)
    carry_ref[...] = total
    cnt_ref[...] = jnp.broadcast_to(total, cnt_ref.shape)

    rows = ((i1 - ROUTE_E0).astype(_F32), (i2 - ROUTE_E0).astype(_F32), c1, c2, r1, r2)
    rrow = lax.broadcasted_iota(jnp.int32, (8, tm), 0)
    route = jnp.zeros((8, tm), _F32)
    for k, val in enumerate(rows):
        route = jnp.where(rrow == k, val, route)
    route_ref[...] = route


def _router(x2, g, w_router, b_router, tm=512):
    n, d = x2.shape
    w_hi = w_router.astype(_BF16)
    w_lo = (w_router - w_hi.astype(_F32)).astype(_BF16)
    return pl.pallas_call(
        _router_kernel, grid=(n // tm,),
        in_specs=[pl.BlockSpec((tm, d), lambda i: (i, 0)),
                  pl.BlockSpec((1, d), lambda i: (0, 0)),
                  pl.BlockSpec((d, 2 * LANES), lambda i: (0, 0)),
                  pl.BlockSpec((1, LANES), lambda i: (0, 0))],
        out_specs=[pl.BlockSpec((2, tm, PACK_W), lambda i: (0, i, 0)),
                   pl.BlockSpec((8, tm), lambda i: (0, i)),
                   pl.BlockSpec((ROUTE_ROWS, LANES), lambda i: (0, 0))],
        out_shape=[jax.ShapeDtypeStruct((2, n, PACK_W), jnp.int32),
                   jax.ShapeDtypeStruct((8, n), _F32),
                   jax.ShapeDtypeStruct((ROUTE_ROWS, LANES), _F32)],
        scratch_shapes=[pltpu.VMEM((ROUTE_ROWS, 1), _F32)],
        compiler_params=_params("arbitrary"),
        name="router",
    )(x2, g.reshape(1, d), jnp.concatenate([w_hi, w_lo], axis=1), b_router)


def _expert_kernel(te_ref, nv_ref, xs_ref, wg_ref, wu_ref, wd_ref, ys_ref, wgu_s, wd_s):
    j = pl.program_id(0)
    prev = te_ref[jnp.maximum(j - 1, 0)]

    @pl.when((j == 0) | (te_ref[j] != prev))
    def _():
        wgu_s[:, :MOE_D_FF] = wg_ref[...].astype(_BF16)
        wgu_s[:, MOE_D_FF:] = wu_ref[...].astype(_BF16)
        wd_s[...] = wd_ref[...].astype(_BF16)

    @pl.when(j < nv_ref[0])
    def _():
        xs = _load_packed(xs_ref).astype(_BF16)
        au = jnp.dot(xs, wgu_s[...], preferred_element_type=_F32)
        a, u = au[:, :MOE_D_FF], au[:, MOE_D_FF:]
        act = (a * (1.0 / (1.0 + jnp.exp(-a))) * u).astype(_BF16)
        _store_packed(ys_ref, jnp.dot(act, wd_s[...], preferred_element_type=_F32))


def _experts(tile_expert, n_valid, xs, w_gate, w_up, w_down, tm=MOE_TM):
    _, n_slots, pw = xs.shape
    n_tiles = n_slots // tm
    d, f = w_gate.shape[-2:]
    row = lambda j, te, nv: (0, jnp.maximum(jnp.minimum(j, nv[0] - 1), 0), 0)
    grid_spec = pltpu.PrefetchScalarGridSpec(
        num_scalar_prefetch=2, grid=(n_tiles,),
        in_specs=[pl.BlockSpec((2, tm, pw), row),
                  pl.BlockSpec((None, d, f), lambda j, te, nv: (te[j], 0, 0)),
                  pl.BlockSpec((None, d, f), lambda j, te, nv: (te[j], 0, 0)),
                  pl.BlockSpec((None, f, d), lambda j, te, nv: (te[j], 0, 0))],
        out_specs=pl.BlockSpec((2, tm, pw), row),
        scratch_shapes=[pltpu.VMEM((d, 2 * f), _BF16), pltpu.VMEM((f, d), _BF16)])
    return pl.pallas_call(
        _expert_kernel, grid_spec=grid_spec,
        out_shape=jax.ShapeDtypeStruct((2, n_slots, pw), jnp.int32),
        compiler_params=_params("arbitrary"),
        name="experts",
    )(tile_expert, n_valid, xs, w_gate, w_up, w_down)


def _combine_kernel(x_ref, y_ref, route_ref, o_ref):
    coef = route_ref[...].T
    c1 = coef[:, 2:3]
    c2 = coef[:, 3:4]
    y0 = _load_packed(y_ref[:, 0])
    y1 = _load_packed(y_ref[:, 1])
    o_ref[...] = x_ref[...] + c1 * y0 + c2 * y1


def _combine(x2, y, route, tm=512):
    n, d = x2.shape
    row = pl.BlockSpec((tm, d), lambda i: (i, 0))
    return pl.pallas_call(
        _combine_kernel, grid=(n // tm,),
        in_specs=[row, pl.BlockSpec((2, 2, tm, PACK_W), lambda i: (0, 0, i, 0)),
                  pl.BlockSpec((8, tm), lambda i: (0, i))],
        out_specs=row,
        out_shape=jax.ShapeDtypeStruct((n, d), _F32),
        compiler_params=_params("parallel"),
        name="moe_combine",
    )(x2, y, route)


def _moe(x2, g, wg, bg, we, be, w_gate, w_up, w_down, layer=0):
    n, d = x2.shape
    tm = MOE_TM
    w_router = jnp.zeros((d, LANES), _F32)
    w_router = w_router.at[:, :MOE_GROUPS].set(wg)
    w_router = w_router.at[:, ROUTE_E0:ROUTE_E0 + N_EXPERTS].set(
        jnp.moveaxis(we, 0, 1).reshape(d, N_EXPERTS))
    b_router = jnp.zeros((1, LANES), _F32)
    b_router = b_router.at[0, :MOE_GROUPS].set(bg)
    b_router = b_router.at[0, ROUTE_E0:ROUTE_E0 + N_EXPERTS].set(be.reshape(-1))
    h, route, cnt = _router(x2, g, w_router, b_router)

    expert = route[0:2].astype(jnp.int32)
    rank = route[4:6].astype(jnp.int32)
    counts = cnt[ROUTE_E0:ROUTE_E0 + N_EXPERTS, 0].astype(jnp.int32)
    tiles_per = (counts + tm - 1) // tm
    tiles_end = jnp.cumsum(tiles_per)
    offset = (tiles_end - tiles_per) * tm
    pos = offset[expert] + rank
    n_tiles = (2 * n) // tm + N_EXPERTS
    n_valid = tiles_end[-1]
    tile_ids = jnp.minimum(jnp.arange(n_tiles, dtype=jnp.int32), n_valid - 1)
    tile_expert = jnp.sum(tile_ids[:, None] >= tiles_end[None, :], axis=1).astype(jnp.int32)
    n_slots = n_tiles * tm
    slot = jnp.concatenate([pos[0], pos[1], pos[0] + n_slots, pos[1] + n_slots])

    xs = _sc_scatter(h.reshape(2 * n, PACK_W), slot, 2 * n_slots, reps=2)
    ys = _experts(tile_expert + layer * N_EXPERTS, n_valid.reshape(1),
                  xs.reshape(2, n_slots, PACK_W), w_gate.reshape(-1, d, MOE_D_FF),
                  w_up.reshape(-1, d, MOE_D_FF), w_down.reshape(-1, MOE_D_FF, d))
    y = _sc_gather(ys.reshape(2 * n_slots, PACK_W), slot)
    return _combine(x2, y.reshape(2, 2, n, PACK_W), route)


def _mixer_ab(x2, bsz, seq, g, w_in, a_qn, a_kn, b_qn, b_kn, b_sink, w_out):
    aw = A_HEADS * HEAD_DIM
    bqw = B_HEADS * HEAD_DIM
    bkw = B_KV_HEADS * HEAD_DIM
    proj = _norm_proj(x2, g, w_in.astype(_BF16))
    scale2 = QK_SCALE * LOG2E
    qa = _prep(proj, bsz, seq, 0, aw, gain=a_qn, scale=scale2, mode="plain")
    ka = _prep(proj, bsz, seq, aw, aw, gain=a_kn, mode="T")
    va = _prep(proj, bsz, seq, 2 * aw, aw, mode="heads_ext")
    qb = _prep(proj, bsz, seq, 3 * aw, bqw, gain=b_qn, scale=scale2, mode="plain")
    kb = _prep(proj, bsz, seq, 3 * aw + bqw, bkw, gain=b_kn, mode="T")
    vb = _prep(proj, bsz, seq, 3 * aw + bqw + bkw, bkw, mode="heads_ext")

    bound_a = HEAD_DIM * scale2 * jnp.max(jnp.abs(a_qn)) * jnp.max(jnp.abs(a_kn))
    bound_b = HEAD_DIM * scale2 * jnp.max(jnp.abs(b_qn)) * jnp.max(jnp.abs(b_kn))
    static_ok = jnp.maximum(bound_a, bound_b) <= SHIFT_MAX
    shift_a = jnp.where(static_ok, bound_a, 0.0)
    shift_b = jnp.where(static_ok, bound_b, 0.0)
    slopes = _alibi_slopes(A_HEADS + B_HEADS)
    bias_a = _dil_bias(slopes[0::2], shift_a)
    bias_b = _band_bias(B_WINDOW, 1.0, slopes[1::2], shift_b, B_HEADS // B_KV_HEADS)
    sink2 = b_sink.astype(_F32) * LOG2E - shift_b

    def attend(online, qa, ka, va, qb, kb, vb, bias_a, bias_b, sink2):
        return (_dil_attn(qa, ka, va, bias_a, online=online),
                _band_b(qb, kb, vb, bias_b, sink2, online=online))

    oa, ob = lax.cond(static_ok, functools.partial(attend, False), functools.partial(attend, True),
                      qa, ka, va, qb, kb, vb, bias_a, bias_b, sink2)
    w_out = w_out.astype(_BF16)
    return _out_proj(x2, [oa.reshape(-1, aw), ob.reshape(-1, bqw)], [w_out[:aw], w_out[aw:]])


def _mixer_c(x2, bsz, seq, g, w_in, qn, kn, w_out):
    qw = C_HEADS * HEAD_DIM
    kvw = C_KV_HEADS * HEAD_DIM
    proj = _norm_proj(x2, g, w_in.astype(_BF16))
    rope = _rope_tables(seq)
    bound = HEAD_DIM * QK_SCALE * LOG2E * jnp.max(jnp.abs(qn)) * jnp.max(jnp.abs(kn))
    static_ok = bound <= SHIFT_MAX
    shift = jnp.where(static_ok, bound, 0.0)
    q = _prep(proj, bsz, seq, 0, qw, gain=qn, rope=rope, scale=QK_SCALE * LOG2E, mode="plain_ext")
    kt = _prep(proj, bsz, seq, qw, kvw, gain=kn, rope=rope, mode="T_ext", shift=shift)
    v = _prep(proj, bsz, seq, qw + kvw, kvw, mode="heads_ext")
    o = lax.cond(static_ok,
                 functools.partial(_dense_attn, online=False),
                 functools.partial(_dense_attn, online=True), q, kt, v)
    return _out_proj(x2, [o.reshape(-1, qw)], [w_out.astype(_BF16)])


def kernel(x, mix_norm, ffn_norm, ab_w_in, a_q_norm, a_k_norm, b_q_norm, b_k_norm, b_sink, ab_w_out,
           c_w_in, c_q_norm, c_k_norm, c_w_out, moe_group_w, moe_group_b, moe_expert_w, moe_expert_b,
           moe_w_gate, moe_w_up, moe_w_down):
    bsz, seq, d = x.shape
    x2 = x.reshape(bsz * seq, d)
    depth = mix_norm.shape[0]
    for layer in range(depth):
        i = layer // 2
        if layer % 2 == 0:
            x2 = _mixer_ab(x2, bsz, seq, mix_norm[layer], ab_w_in[i], a_q_norm[i], a_k_norm[i],
                           b_q_norm[i], b_k_norm[i], b_sink[i], ab_w_out[i])
        else:
            x2 = _mixer_c(x2, bsz, seq, mix_norm[layer], c_w_in[i], c_q_norm[i], c_k_norm[i],
                          c_w_out[i])
        x2 = _moe(x2, ffn_norm[layer], moe_group_w[layer], moe_group_b[layer], moe_expert_w[layer],
                  moe_expert_b[layer], moe_w_gate, moe_w_up, moe_w_down, layer)
    return x2.reshape(bsz, seq, d)
```

```python
import functools
import math
from typing import NamedTuple

import jax
import jax.numpy as jnp
import numpy as np
from jax import lax
from jax.experimental import pallas as pl
from jax.experimental.pallas import tpu as pltpu
from jax.experimental.pallas import tpu_sc as plsc

HEAD_DIM = 64
LANES = 128
N_HEADS = 16
A_HEADS = 8
B_HEADS = 8
B_KV_HEADS = 2
C_HEADS = 16
C_KV_HEADS = 4
A_PATTERNS = ((128, 1), (512, 4), (2048, 16))
B_WINDOW = 128
GRID_W = 64
ROPE_THETA = 10000.0
ROPE_AXIS_DIM = HEAD_DIM // 2
ALIBI_MAX_BIAS = 8.0
RMS_EPS = 1e-6
MOE_GROUPS = 4
MOE_EXPERTS = 8
N_EXPERTS = MOE_GROUPS * MOE_EXPERTS
MOE_D_FF = 256
QK_SCALE = HEAD_DIM ** -0.5
LOG2E = math.log2(math.e)
SHIFT_MAX = 60.0
NEG_BIG = -1e30
VMEM_LIMIT = 52 * 1024 * 1024

BAND_BQ = 256
BAND_KB = 128
BAND_KW = BAND_BQ + 2 * BAND_KB
DIL_BQ = 256
DIL_REACH = max(w // 2 for w, _ in A_PATTERNS)
DIL_KW = DIL_BQ + 2 * DIL_REACH
DIL_CHUNK = 768
DENSE_BQ = 256
DENSE_BK = 8192
DENSE_BK_ONLINE = 512
MOE_TM = 512
ROUTE_E0 = 4
ROUTE_ROWS = 48
PACK_W = 256
SC_WINDOW = 128

_BF16 = jnp.bfloat16
_F32 = jnp.float32


def _params(*sem):
    return pltpu.CompilerParams(dimension_semantics=sem, vmem_limit_bytes=VMEM_LIMIT)


def _alibi_slopes(n):
    return np.asarray(2.0 ** (-ALIBI_MAX_BIAS * np.arange(1, n + 1) / n), dtype=np.float32)


def _head_norm(y, gain, head_ones):
    y2 = y * y
    hi = y2.astype(_BF16)
    lo = (y2 - hi.astype(_F32)).astype(_BF16)
    ss = (jnp.dot(hi, head_ones, preferred_element_type=_F32)
          + jnp.dot(lo, head_ones, preferred_element_type=_F32))
    return y * lax.rsqrt(ss * (1.0 / HEAD_DIM) + RMS_EPS) * gain


def _rope(y, cos, sin):
    lane = lax.broadcasted_iota(jnp.int32, y.shape, 1)
    first = (lane % 32) < 16
    partner = jnp.where(first, pltpu.roll(y, LANES - 16, axis=1), pltpu.roll(y, 16, axis=1))
    return y * cos + partner * sin


class _Seg(NamedTuple):
    col0: int
    ncols: int
    norm: bool
    rope: bool
    scale: float
    mode: str


def _prep_slab(y, c, o_ref, seg, gain_ref, cos_ref, sin_ref, shift_ref, head_ones):
    rows = y.shape[0]
    if seg.norm:
        y = _head_norm(y, gain_ref[...], head_ones)
    if seg.rope:
        y = _rope(y, cos_ref[...], sin_ref[...])
    if seg.scale != 1.0:
        y = y * seg.scale
    sl = slice(c * LANES, (c + 1) * LANES)
    if seg.mode == "plain":
        o_ref[:, sl] = y.astype(_BF16)
    elif seg.mode == "T":
        o_ref[sl, :] = y.T.astype(_BF16)
    elif seg.mode in ("plain_ext", "heads_ext"):
        lane = lax.broadcasted_iota(jnp.int32, y.shape, 1)
        if seg.mode == "plain_ext":
            fill = jnp.where(lane == HEAD_DIM, 1.0, 0.0)
        else:
            fill = jnp.ones(y.shape, _F32)
        for k, yk in enumerate((y, pltpu.roll(y, HEAD_DIM, axis=1))):
            ext = jnp.where(lane < HEAD_DIM, yk, fill).astype(_BF16)
            if seg.mode == "plain_ext":
                o_ref[:, (2 * c + k) * LANES:(2 * c + k + 1) * LANES] = ext
            else:
                o_ref[2 * c + k] = ext
    else:
        assert seg.mode == "T_ext"
        yt = y.T
        row = lax.broadcasted_iota(jnp.int32, (HEAD_DIM, rows), 0)
        extra = jnp.where(row == 0, -shift_ref[0], 0.0)
        for k in range(2):
            ext = jnp.concatenate([yt[k * HEAD_DIM:(k + 1) * HEAD_DIM], extra], axis=0)
            o_ref[(2 * c + k) * LANES:(2 * c + k + 1) * LANES, :] = ext.astype(_BF16)


def _proj_prep_kernel(*refs, segs):
    it = iter(refs)
    shift_ref = next(it) if any(s.mode == "T_ext" for s in segs) else None
    x_ref, g_ref, w_ref = next(it), next(it), next(it)
    gain_refs = [next(it) if s.norm else None for s in segs]
    cos_ref, sin_ref = (next(it), next(it)) if any(s.rope for s in segs) else (None, None)
    outs = list(it)
    x = x_ref[...]
    ms = jnp.mean(x * x, axis=-1, keepdims=True)
    h = (x * lax.rsqrt(ms + RMS_EPS) * g_ref[...]).astype(_BF16)
    proj = jnp.dot(h, w_ref[...], preferred_element_type=_F32)
    head_ones = (lax.broadcasted_iota(jnp.int32, (LANES, LANES), 0) // HEAD_DIM
                 == lax.broadcasted_iota(jnp.int32, (LANES, LANES), 1) // HEAD_DIM).astype(_BF16)
    for seg, gain_ref, o_ref in zip(segs, gain_refs, outs):
        for c in range(seg.ncols // LANES):
            y = proj[:, seg.col0 + c * LANES:seg.col0 + (c + 1) * LANES]
            _prep_slab(y, c, o_ref, seg, gain_ref, cos_ref, sin_ref, shift_ref, head_ones)


def _proj_prep(x2, bsz, seq, g, w, segs, gains, rope=None, shift=None, tm=512):
    n, d = x2.shape
    p = w.shape[1]
    nt = seq // tm
    assert seq % tm == 0
    const = lambda shape: pl.BlockSpec(shape, lambda i: (0,) * len(shape))
    ins, in_specs = [], []
    if shift is not None:
        ins.append(shift.reshape(1).astype(_F32))
        in_specs.append(pl.BlockSpec(memory_space=pltpu.SMEM))
    ins += [x2, g.reshape(1, d), w]
    in_specs += [pl.BlockSpec((tm, d), lambda i: (i, 0)), const((1, d)), const((d, p))]
    for seg, gain in zip(segs, gains):
        if seg.norm:
            ins.append(jnp.tile(gain.astype(_F32), LANES // HEAD_DIM).reshape(1, LANES))
            in_specs.append(const((1, LANES)))
    if rope is not None:
        ins += list(rope)
        in_specs += [pl.BlockSpec((tm, LANES), lambda i: (i % nt, 0))] * 2
    out_shape, out_specs = [], []
    for seg in segs:
        nc = seg.ncols
        if seg.mode in ("plain", "plain_ext"):
            wout = nc if seg.mode == "plain" else 2 * nc
            out_shape.append(jax.ShapeDtypeStruct((bsz, seq, wout), _BF16))
            out_specs.append(pl.BlockSpec((None, tm, wout), lambda i: (i // nt, i % nt, 0)))
        elif seg.mode in ("T", "T_ext"):
            wout = nc if seg.mode == "T" else 2 * nc
            out_shape.append(jax.ShapeDtypeStruct((bsz, wout, seq), _BF16))
            out_specs.append(pl.BlockSpec((None, wout, tm), lambda i: (i // nt, 0, i % nt)))
        else:
            assert seg.mode == "heads_ext"
            nh = nc // HEAD_DIM
            out_shape.append(jax.ShapeDtypeStruct((bsz, nh, seq, LANES), _BF16))
            out_specs.append(pl.BlockSpec((None, nh, tm, LANES), lambda i: (i // nt, 0, i % nt, 0)))
    return pl.pallas_call(
        functools.partial(_proj_prep_kernel, segs=tuple(segs)), grid=(n // tm,),
        in_specs=in_specs, out_specs=out_specs, out_shape=out_shape,
        compiler_params=_params("parallel"),
        name="proj_prep",
    )(*ins)


def _rope_tables(seq):
    t = np.arange(seq)
    row = (t // GRID_W).astype(np.float32)
    col = (t % GRID_W).astype(np.float32)
    inv_freq = jnp.asarray(ROPE_THETA, _F32) ** (-jnp.arange(0, ROPE_AXIS_DIM, 2, dtype=_F32) / ROPE_AXIS_DIM)
    ang_r = jnp.asarray(row)[:, None] * inv_freq[None, :]
    ang_c = jnp.asarray(col)[:, None] * inv_freq[None, :]
    cr, sr, cc, sc = jnp.cos(ang_r), jnp.sin(ang_r), jnp.cos(ang_c), jnp.sin(ang_c)
    cos = jnp.concatenate([cr, cr, cc, cc], axis=-1)
    sin = jnp.concatenate([-sr, sr, -sc, sc], axis=-1)
    return jnp.tile(cos, (1, 2)), jnp.tile(sin, (1, 2))


def _band_bias(hw, dist_scale, slopes, shift, stack):
    bq, kb = BAND_BQ, BAND_KB
    assert hw <= kb
    r = np.arange(bq)[:, None]
    c = np.arange(BAND_KW)[None, :]
    rel = np.abs(c - kb - r)
    dist = rel.astype(np.float32) * np.float32(dist_scale)
    ok = np.stack([(rel <= hw) & ~(first & (c < kb)) & ~(last & (c >= kb + bq))
                   for first, last in ((False, False), (True, False), (False, True), (True, True))])
    alibi = -(np.asarray(slopes, np.float32)[:, None, None] * dist[None]) * np.float32(LOG2E)
    bias = jnp.where(ok[:, None], jnp.asarray(alibi)[None] - shift, NEG_BIG)
    nh = len(slopes)
    return bias.reshape(4, nh // stack, stack * bq, BAND_KW)


def _edge_variant(i, nb):
    return jnp.where(i == 0, 1, 0) + jnp.where(i == nb - 1, 2, 0)


def _window_blocks(nb):
    per = BAND_BQ // BAND_KB
    last = nb * per - 1
    fns = [lambda i: jnp.maximum(per * i - 1, 0)]
    fns += [functools.partial(lambda j, i: per * i + j, j) for j in range(per)]
    fns += [lambda i: jnp.minimum(per * i + per, last)]
    return fns


def _dil_bias(slopes, shift):
    r = jnp.arange(DIL_BQ, dtype=jnp.int32)[:, None]
    u = jnp.arange(DIL_BQ + 4 * DIL_REACH, dtype=jnp.int32)[None, :]
    delta = u - 2 * DIL_REACH - r
    dist = jnp.abs(delta)
    mult = sum(((delta % d == 0) & (dist <= ((w // 2) // d) * d)).astype(_F32) for w, d in A_PATTERNS)
    alibi = -(jnp.asarray(slopes, _F32)[:, None, None] * dist.astype(_F32)[None]) * LOG2E
    return jnp.where((mult > 0)[None], jnp.log2(jnp.maximum(mult, 1.0))[None] + alibi - shift, NEG_BIG)


def _dil_attn_kernel(q_ref, kt_ref, v_ref, bias_ref, o_ref, *, online):
    bq = q_ref.shape[0]
    seq = kt_ref.shape[1]
    q0 = pl.program_id(2) * bq
    w0 = pl.multiple_of(jnp.clip(q0 - DIL_REACH, 0, seq - DIL_KW), bq)
    u0 = pl.multiple_of(w0 - q0 + 2 * DIL_REACH, bq)
    q = q_ref[...]
    outs = []
    for j in range(2):
        rows = slice(j * HEAD_DIM, (j + 1) * HEAD_DIM)
        scores = []
        for c in range(DIL_KW // DIL_CHUNK):
            ks = pl.ds(pl.multiple_of(w0 + c * DIL_CHUNK, bq), DIL_CHUNK)
            us = pl.ds(pl.multiple_of(u0 + c * DIL_CHUNK, bq), DIL_CHUNK)
            scores.append(jnp.dot(q[:, rows], kt_ref[rows, ks], preferred_element_type=_F32)
                          + bias_ref[j, :, us])
        if online:
            m = functools.reduce(jnp.maximum, [jnp.max(s, axis=-1, keepdims=True) for s in scores])
            scores = [s - m for s in scores]
        acc = jnp.zeros((bq, LANES), _F32)
        for c, s in enumerate(scores):
            ks = pl.ds(pl.multiple_of(w0 + c * DIL_CHUNK, bq), DIL_CHUNK)
            acc += jnp.dot(jnp.exp2(s).astype(_BF16), v_ref[j, ks, :], preferred_element_type=_F32)
        outs.append((acc / pltpu.roll(acc, HEAD_DIM, axis=1))[:, :HEAD_DIM])
    o_ref[...] = jnp.concatenate(outs, axis=1).astype(_BF16)


def _dil_attn(q, kt, v, bias, *, online):
    bsz, seq, w = q.shape
    bq = DIL_BQ
    assert seq >= DIL_KW and seq % bq == 0
    return pl.pallas_call(
        functools.partial(_dil_attn_kernel, online=online), grid=(A_HEADS // 2, bsz, seq // bq),
        in_specs=[pl.BlockSpec((None, bq, LANES), lambda p, b, i: (b, i, p)),
                  pl.BlockSpec((None, LANES, seq), lambda p, b, i: (b, p, 0)),
                  pl.BlockSpec((None, 2, seq, LANES), lambda p, b, i: (b, p, 0, 0)),
                  pl.BlockSpec((2, bq, bias.shape[-1]), lambda p, b, i: (p, 0, 0))],
        out_specs=pl.BlockSpec((None, bq, LANES), lambda p, b, i: (b, i, p)),
        out_shape=jax.ShapeDtypeStruct((bsz, seq, w), _BF16),
        compiler_params=_params("parallel", "parallel", "parallel"),
        name="dil_attn_online" if online else "dil_attn",
    )(q, kt, v, bias)


def _band_b_kernel(sink2_ref, sinkp_ref, q_ref, *refs, online):
    bq = q_ref.shape[0]
    g = pl.program_id(1)
    grp = B_HEADS // B_KV_HEADS
    npc = BAND_KW // BAND_KB
    k_refs, v_refs, (bias_ref, o_ref) = refs[:npc], refs[npc:2 * npc], refs[2 * npc:]
    kt = jnp.concatenate([r[...] for r in k_refs], axis=1)
    v = jnp.concatenate([r[...] for r in v_refs], axis=0)
    q = q_ref[...]
    q4 = jnp.concatenate([q[:, i * HEAD_DIM:(i + 1) * HEAD_DIM] for i in range(grp)], axis=0)
    s = jnp.dot(q4, kt, preferred_element_type=_F32) + bias_ref[...]
    if online:
        m = jnp.max(s, axis=-1, keepdims=True)
        s = s - m
    acc = jnp.dot(jnp.exp2(s).astype(_BF16), v, preferred_element_type=_F32)
    for i in range(grp):
        a = acc[i * bq:(i + 1) * bq]
        if online:
            mi = m[i * bq:(i + 1) * bq]
            sk = sink2_ref[g * grp + i]
            mm = jnp.maximum(mi, sk)
            a = a * jnp.exp2(mi - mm)
            o = a / (pltpu.roll(a, HEAD_DIM, axis=1) + jnp.exp2(sk - mm))
        else:
            o = a / (pltpu.roll(a, HEAD_DIM, axis=1) + sinkp_ref[g * grp + i])
        o_ref[:, i * HEAD_DIM:(i + 1) * HEAD_DIM] = o[:, :HEAD_DIM].astype(_BF16)


def _band_b(q, kt, v, bias, sink2, *, online):
    bsz, seq_len, w = q.shape
    bq, kb = BAND_BQ, BAND_KB
    nb = seq_len // bq
    grp = B_HEADS // B_KV_HEADS
    gw = w // B_KV_HEADS
    blocks = _window_blocks(nb)
    kspec = lambda f: pl.BlockSpec((None, HEAD_DIM, kb), lambda b, g, i: (b, g, f(i)))
    vspec = lambda f: pl.BlockSpec((None, None, kb, LANES), lambda b, g, i: (b, g, f(i), 0))
    qspec = pl.BlockSpec((None, bq, gw), lambda b, g, i: (b, i, g))
    smem = pl.BlockSpec(memory_space=pltpu.SMEM)
    return pl.pallas_call(
        functools.partial(_band_b_kernel, online=online), grid=(bsz, B_KV_HEADS, nb),
        in_specs=[smem, smem, qspec] + [kspec(f) for f in blocks] + [vspec(f) for f in blocks]
        + [pl.BlockSpec((None, None, grp * bq, BAND_KW),
                        lambda b, g, i: (_edge_variant(i, nb), g, 0, 0))],
        out_specs=qspec,
        out_shape=jax.ShapeDtypeStruct((bsz, seq_len, w), _BF16),
        compiler_params=_params("parallel", "parallel", "parallel"),
        name="band_b_online" if online else "band_b",
    )(sink2, jnp.exp2(sink2), q, *([kt] * len(blocks)), *([v] * len(blocks)), bias)


def _dense_attn_kernel(q_ref, kt_ref, v_ref, o_ref, acc_scr, *rest, bk, online):
    seq = kt_ref.shape[1]
    grp = C_HEADS // C_KV_HEADS
    acc_scr[...] = jnp.zeros(acc_scr.shape, _F32)
    if online:
        m_scr, = rest
        m_scr[...] = jnp.full(m_scr.shape, NEG_BIG, _F32)

    def body(j, carry):
        k0 = pl.multiple_of(j * bk, bk)
        kt = kt_ref[:, pl.ds(k0, bk)]
        v = v_ref[pl.ds(k0, bk), :]
        for c in range(grp):
            s = jnp.dot(q_ref[:, c * LANES:(c + 1) * LANES], kt, preferred_element_type=_F32)
            if online:
                m_old = m_scr[c]
                m_new = jnp.maximum(m_old, jnp.max(s, axis=-1, keepdims=True))
                m_scr[c] = m_new
                p = jnp.exp2(s - m_new).astype(_BF16)
                acc_scr[c] = jnp.exp2(m_old - m_new) * acc_scr[c] + jnp.dot(
                    p, v, preferred_element_type=_F32)
            else:
                p = jnp.exp2(s).astype(_BF16)
                acc_scr[c] += jnp.dot(p, v, preferred_element_type=_F32)
        return carry

    lax.fori_loop(0, seq // bk, body, 0)
    for c in range(grp):
        a = acc_scr[c]
        o = a / pltpu.roll(a, HEAD_DIM, axis=1)
        o_ref[:, c * HEAD_DIM:(c + 1) * HEAD_DIM] = o[:, :HEAD_DIM].astype(_BF16)


def _dense_attn(q, kt, v, *, online, bq=DENSE_BQ, bk=DENSE_BK):
    bsz, seq, _ = q.shape
    bq, bk = min(bq, seq), min(DENSE_BK_ONLINE if online else bk, seq)
    grp = C_HEADS // C_KV_HEADS
    scratch = [pltpu.VMEM((grp, bq, LANES), _F32)]
    if online:
        scratch.append(pltpu.VMEM((grp, bq, 1), _F32))
    return pl.pallas_call(
        functools.partial(_dense_attn_kernel, bk=bk, online=online),
        grid=(bsz, C_KV_HEADS, seq // bq),
        in_specs=[pl.BlockSpec((None, bq, grp * LANES), lambda b, g, i: (b, i, g)),
                  pl.BlockSpec((None, LANES, seq), lambda b, g, i: (b, g, 0)),
                  pl.BlockSpec((None, None, seq, LANES), lambda b, g, i: (b, g, 0, 0))],
        out_specs=pl.BlockSpec((None, bq, grp * HEAD_DIM), lambda b, g, i: (b, i, g)),
        out_shape=jax.ShapeDtypeStruct((bsz, seq, C_HEADS * HEAD_DIM), _BF16),
        scratch_shapes=scratch,
        compiler_params=_params("parallel", "parallel", "parallel"),
        name="dense_attn_online" if online else "dense_attn",
    )(q, kt, v)


def _out_proj_kernel(*refs, n_in):
    x_ref = refs[0]
    o_refs = refs[1:1 + n_in]
    w_refs = refs[1 + n_in:1 + 2 * n_in]
    out_ref = refs[-1]
    acc = x_ref[...]
    for o_ref, w_ref in zip(o_refs, w_refs):
        acc = acc + jnp.dot(o_ref[...], w_ref[...], preferred_element_type=_F32)
    out_ref[...] = acc


def _out_proj(x2, os_, ws, tm=512):
    n, d = x2.shape
    in_specs = [pl.BlockSpec((tm, d), lambda i: (i, 0))]
    in_specs += [pl.BlockSpec((tm, o.shape[1]), lambda i: (i, 0)) for o in os_]
    in_specs += [pl.BlockSpec(w.shape, lambda i: (0, 0)) for w in ws]
    return pl.pallas_call(
        functools.partial(_out_proj_kernel, n_in=len(os_)), grid=(n // tm,),
        in_specs=in_specs,
        out_specs=pl.BlockSpec((tm, d), lambda i: (i, 0)),
        out_shape=jax.ShapeDtypeStruct((n, d), _F32),
        compiler_params=_params("parallel"),
        name="out_proj",
    )(x2, *os_, *ws)


def _pack_pair(a, b):
    wa = lax.bitcast_convert_type(a.astype(_BF16).astype(_F32), jnp.uint32) >> 16
    wb = lax.bitcast_convert_type(b.astype(_BF16).astype(_F32), jnp.uint32) & jnp.uint32(0xFFFF0000)
    return lax.bitcast_convert_type(wa | wb, jnp.int32)


def _unpack_pair(w):
    u = lax.bitcast_convert_type(w, jnp.uint32)
    return (lax.bitcast_convert_type(u << 16, _F32),
            lax.bitcast_convert_type(u & jnp.uint32(0xFFFF0000), _F32))


def _store_packed(ref, y):
    q = PACK_W
    for j in range(2):
        ref[j] = _pack_pair(y[:, 2 * j * q:(2 * j + 1) * q], y[:, (2 * j + 1) * q:(2 * j + 2) * q])


def _load_packed(ref):
    parts = []
    for j in range(2):
        parts += list(_unpack_pair(ref[j]))
    return jnp.concatenate(parts, axis=1)


def _sc_mesh():
    return plsc.VectorSubcoreMesh(core_axis_name="core", subcore_axis_name="subcore")


def _sc_gather(table, idx):
    n = idx.shape[0]
    d = table.shape[1]

    @pl.kernel(out_type=jax.ShapeDtypeStruct((n, d), table.dtype), mesh=_sc_mesh())
    def gather(x_hbm, i_hbm, o_hbm):
        def body(i_vmem, o_vmem):
            pltpu.sync_copy(x_hbm.at[i_vmem.at[0]], o_vmem)

        pltpu.emit_pipeline(
            body, grid=(n // SC_WINDOW,),
            in_specs=[pl.BlockSpec((1, SC_WINDOW), index_map=lambda i: (0, i))],
            out_specs=[pl.BlockSpec((SC_WINDOW, d), index_map=lambda i: (i, 0))],
            core_axis_name=("core", "subcore"),
            dimension_semantics=(pltpu.PARALLEL,),
        )(i_hbm, o_hbm)

    return gather(table, idx.reshape(1, n))


def _sc_scatter(src, idx, n_out, reps):
    n = idx.shape[0]
    r2, d = src.shape
    nb = r2 // 2 // SC_WINDOW

    @pl.kernel(out_type=jax.ShapeDtypeStruct((n_out, d), src.dtype), mesh=_sc_mesh())
    def scatter(x_hbm, i_hbm, o_hbm):
        def body(x_vmem, i_vmem):
            pltpu.sync_copy(x_vmem, o_hbm.at[i_vmem.at[0]])

        pltpu.emit_pipeline(
            body, grid=(n // SC_WINDOW,),
            in_specs=[pl.BlockSpec((SC_WINDOW, d),
                                   index_map=lambda i: ((i // (reps * nb)) * nb + i % nb, 0)),
                      pl.BlockSpec((1, SC_WINDOW), index_map=lambda i: (0, i))],
            out_specs=[],
            core_axis_name=("core", "subcore"),
            dimension_semantics=(pltpu.PARALLEL,),
        )(x_hbm, i_hbm)

    return scatter(src, idx.reshape(1, n))


def _router_kernel(x_ref, g_ref, w_ref, b_ref, h_ref, route_ref, cnt_ref, carry_ref):
    tm = x_ref.shape[0]

    @pl.when(pl.program_id(0) == 0)
    def _():
        carry_ref[...] = jnp.zeros(carry_ref.shape, _F32)

    x = x_ref[...]
    ms = jnp.mean(x * x, axis=-1, keepdims=True)
    h = x * lax.rsqrt(ms + RMS_EPS) * g_ref[...]
    _store_packed(h_ref, h)
    h_hi = h.astype(_BF16)
    h_lo = (h - h_hi.astype(_F32)).astype(_BF16)
    hw = (jnp.dot(h_hi, w_ref[...], preferred_element_type=_F32)
          + jnp.dot(h_lo, w_ref[...], preferred_element_type=_F32))
    logits = hw[:, :LANES] + hw[:, LANES:] + b_ref[...]
    lt = logits.T[:ROUTE_ROWS]
    row = lax.broadcasted_iota(jnp.int32, (ROUTE_ROWS, tm), 0)

    lg = jnp.where(row < MOE_GROUPS, lt, NEG_BIG)
    mg = jnp.max(lg, axis=0, keepdims=True)
    zg = jnp.sum(jnp.exp(lg - mg), axis=0, keepdims=True)
    p_grp = 1.0 / zg
    g_idx = jnp.min(jnp.where(lg == mg, row, ROUTE_ROWS), axis=0, keepdims=True)

    e_row = row - ROUTE_E0
    emask = (e_row >= 0) & (e_row < N_EXPERTS) & ((e_row >> 3) == g_idx)
    le = jnp.where(emask, lt, NEG_BIG)
    m1 = jnp.max(le, axis=0, keepdims=True)
    i1 = jnp.min(jnp.where(le == m1, row, ROUTE_ROWS), axis=0, keepdims=True)
    le2 = jnp.where(row == i1, NEG_BIG, le)
    m2 = jnp.max(le2, axis=0, keepdims=True)
    i2 = jnp.min(jnp.where(le2 == m2, row, ROUTE_ROWS), axis=0, keepdims=True)
    e21 = jnp.exp(m2 - m1)
    c1 = p_grp / (1.0 + e21)
    c2 = p_grp * e21 / (1.0 + e21)

    onehot = jnp.where(row == i1, 1.0, jnp.where(row == i2, 1.0, 0.0)).astype(_BF16)
    upper = (lax.broadcasted_iota(jnp.int32, (tm, tm), 0)
             <= lax.broadcasted_iota(jnp.int32, (tm, tm), 1)).astype(_BF16)
    cum = jnp.dot(onehot, upper, preferred_element_type=_F32) + carry_ref[...]
    r1 = jnp.sum(jnp.where(row == i1, cum, 0.0), axis=0, keepdims=True) - 1.0
    r2 = jnp.sum(jnp.where(row == i2, cum, 0.0), axis=0, keepdims=True) - 1.0
    total = jnp.max(cum, axis=1, keepdims=True)
    carry_ref[...] = total
    cnt_ref[...] = jnp.broadcast_to(total, cnt_ref.shape)

    rows = ((i1 - ROUTE_E0).astype(_F32), (i2 - ROUTE_E0).astype(_F32), c1, c2, r1, r2)
    rrow = lax.broadcasted_iota(jnp.int32, (8, tm), 0)
    route = jnp.zeros((8, tm), _F32)
    for k, val in enumerate(rows):
        route = jnp.where(rrow == k, val, route)
    route_ref[...] = route


def _router(x2, g, w_router, b_router, tm=512):
    n, d = x2.shape
    w_hi = w_router.astype(_BF16)
    w_lo = (w_router - w_hi.astype(_F32)).astype(_BF16)
    return pl.pallas_call(
        _router_kernel, grid=(n // tm,),
        in_specs=[pl.BlockSpec((tm, d), lambda i: (i, 0)),
                  pl.BlockSpec((1, d), lambda i: (0, 0)),
                  pl.BlockSpec((d, 2 * LANES), lambda i: (0, 0)),
                  pl.BlockSpec((1, LANES), lambda i: (0, 0))],
        out_specs=[pl.BlockSpec((2, tm, PACK_W), lambda i: (0, i, 0)),
                   pl.BlockSpec((8, tm), lambda i: (0, i)),
                   pl.BlockSpec((ROUTE_ROWS, LANES), lambda i: (0, 0))],
        out_shape=[jax.ShapeDtypeStruct((2, n, PACK_W), jnp.int32),
                   jax.ShapeDtypeStruct((8, n), _F32),
                   jax.ShapeDtypeStruct((ROUTE_ROWS, LANES), _F32)],
        scratch_shapes=[pltpu.VMEM((ROUTE_ROWS, 1), _F32)],
        compiler_params=_params("arbitrary"),
        name="router",
    )(x2, g.reshape(1, d), jnp.concatenate([w_hi, w_lo], axis=1), b_router)


def _expert_kernel(te_ref, nv_ref, xs_ref, wg_ref, wu_ref, wd_ref, ys_ref, wgu_s, wd_s):
    j = pl.program_id(0)
    prev = te_ref[jnp.maximum(j - 1, 0)]

    @pl.when((j == 0) | (te_ref[j] != prev))
    def _():
        wgu_s[:, :MOE_D_FF] = wg_ref[...].astype(_BF16)
        wgu_s[:, MOE_D_FF:] = wu_ref[...].astype(_BF16)
        wd_s[...] = wd_ref[...].astype(_BF16)

    @pl.when(j < nv_ref[0])
    def _():
        xs = _load_packed(xs_ref).astype(_BF16)
        au = jnp.dot(xs, wgu_s[...], preferred_element_type=_F32)
        a, u = au[:, :MOE_D_FF], au[:, MOE_D_FF:]
        act = (a * (1.0 / (1.0 + jnp.exp(-a))) * u).astype(_BF16)
        _store_packed(ys_ref, jnp.dot(act, wd_s[...], preferred_element_type=_F32))


def _experts(tile_expert, n_valid, xs, w_gate, w_up, w_down, tm=MOE_TM):
    _, n_slots, pw = xs.shape
    n_tiles = n_slots // tm
    d, f = w_gate.shape[-2:]
    row = lambda j, te, nv: (0, jnp.maximum(jnp.minimum(j, nv[0] - 1), 0), 0)
    grid_spec = pltpu.PrefetchScalarGridSpec(
        num_scalar_prefetch=2, grid=(n_tiles,),
        in_specs=[pl.BlockSpec((2, tm, pw), row),
                  pl.BlockSpec((None, d, f), lambda j, te, nv: (te[j], 0, 0)),
                  pl.BlockSpec((None, d, f), lambda j, te, nv: (te[j], 0, 0)),
                  pl.BlockSpec((None, f, d), lambda j, te, nv: (te[j], 0, 0))],
        out_specs=pl.BlockSpec((2, tm, pw), row),
        scratch_shapes=[pltpu.VMEM((d, 2 * f), _BF16), pltpu.VMEM((f, d), _BF16)])
    return pl.pallas_call(
        _expert_kernel, grid_spec=grid_spec,
        out_shape=jax.ShapeDtypeStruct((2, n_slots, pw), jnp.int32),
        compiler_params=_params("arbitrary"),
        name="experts",
    )(tile_expert, n_valid, xs, w_gate, w_up, w_down)


def _combine_kernel(x_ref, y_ref, route_ref, o_ref):
    coef = route_ref[...].T
    c1 = coef[:, 2:3]
    c2 = coef[:, 3:4]
    y0 = _load_packed(y_ref[:, 0])
    y1 = _load_packed(y_ref[:, 1])
    o_ref[...] = x_ref[...] + c1 * y0 + c2 * y1


def _combine(x2, y, route, tm=512):
    n, d = x2.shape
    row = pl.BlockSpec((tm, d), lambda i: (i, 0))
    return pl.pallas_call(
        _combine_kernel, grid=(n // tm,),
        in_specs=[row, pl.BlockSpec((2, 2, tm, PACK_W), lambda i: (0, 0, i, 0)),
                  pl.BlockSpec((8, tm), lambda i: (0, i))],
        out_specs=row,
        out_shape=jax.ShapeDtypeStruct((n, d), _F32),
        compiler_params=_params("parallel"),
        name="moe_combine",
    )(x2, y, route)


def _slot_kernel(offset_ref, route_ref, slot_ref, *, n_slots):
    expert = route_ref[0:2, :].astype(jnp.int32)
    pos = route_ref[4:6, :].astype(jnp.int32)
    for e in range(N_EXPERTS):
        pos = pos + jnp.where(expert == e, offset_ref[e], 0)
    slot_ref[0:2, :] = pos
    slot_ref[2:4, :] = pos + n_slots


def _slots(offset, route, n_slots):
    n = route.shape[1]
    return pl.pallas_call(
        functools.partial(_slot_kernel, n_slots=n_slots),
        in_specs=[pl.BlockSpec(memory_space=pltpu.SMEM),
                  pl.BlockSpec((8, n), lambda: (0, 0))],
        out_specs=pl.BlockSpec((4, n), lambda: (0, 0)),
        out_shape=jax.ShapeDtypeStruct((4, n), jnp.int32),
        name="moe_slots",
    )(offset, route)


def _moe(x2, g, wg, bg, we, be, w_gate, w_up, w_down, layer=0):
    n, d = x2.shape
    tm = MOE_TM
    w_router = jnp.zeros((d, LANES), _F32)
    w_router = w_router.at[:, :MOE_GROUPS].set(wg)
    w_router = w_router.at[:, ROUTE_E0:ROUTE_E0 + N_EXPERTS].set(
        jnp.moveaxis(we, 0, 1).reshape(d, N_EXPERTS))
    b_router = jnp.zeros((1, LANES), _F32)
    b_router = b_router.at[0, :MOE_GROUPS].set(bg)
    b_router = b_router.at[0, ROUTE_E0:ROUTE_E0 + N_EXPERTS].set(be.reshape(-1))
    h, route, cnt = _router(x2, g, w_router, b_router)

    counts = cnt[ROUTE_E0:ROUTE_E0 + N_EXPERTS, 0].astype(jnp.int32)
    tiles_per = (counts + tm - 1) // tm
    tiles_end = jnp.cumsum(tiles_per)
    offset = (tiles_end - tiles_per) * tm
    n_tiles = (2 * n) // tm + N_EXPERTS
    n_valid = tiles_end[-1]
    tile_ids = jnp.minimum(jnp.arange(n_tiles, dtype=jnp.int32), n_valid - 1)
    tile_expert = jnp.sum(tile_ids[:, None] >= tiles_end[None, :], axis=1).astype(jnp.int32)
    n_slots = n_tiles * tm
    slot = _slots(offset, route, n_slots).reshape(-1)

    xs = _sc_scatter(h.reshape(2 * n, PACK_W), slot, 2 * n_slots, reps=2)
    ys = _experts(tile_expert + layer * N_EXPERTS, n_valid.reshape(1),
                  xs.reshape(2, n_slots, PACK_W), w_gate.reshape(-1, d, MOE_D_FF),
                  w_up.reshape(-1, d, MOE_D_FF), w_down.reshape(-1, MOE_D_FF, d))
    y = _sc_gather(ys.reshape(2 * n_slots, PACK_W), slot)
    return _combine(x2, y.reshape(2, 2, n, PACK_W), route)


def _mixer_ab(x2, bsz, seq, g, w_in, a_qn, a_kn, b_qn, b_kn, b_sink, w_out):
    aw = A_HEADS * HEAD_DIM
    bqw = B_HEADS * HEAD_DIM
    bkw = B_KV_HEADS * HEAD_DIM
    scale2 = QK_SCALE * LOG2E
    segs = [_Seg(0, aw, True, False, scale2, "plain"),
            _Seg(aw, aw, True, False, 1.0, "T"),
            _Seg(2 * aw, aw, False, False, 1.0, "heads_ext"),
            _Seg(3 * aw, bqw, True, False, scale2, "plain"),
            _Seg(3 * aw + bqw, bkw, True, False, 1.0, "T"),
            _Seg(3 * aw + bqw + bkw, bkw, False, False, 1.0, "heads_ext")]
    qa, ka, va, qb, kb, vb = _proj_prep(x2, bsz, seq, g, w_in.astype(_BF16), segs,
                                        [a_qn, a_kn, None, b_qn, b_kn, None])

    bound_a = HEAD_DIM * scale2 * jnp.max(jnp.abs(a_qn)) * jnp.max(jnp.abs(a_kn))
    bound_b = HEAD_DIM * scale2 * jnp.max(jnp.abs(b_qn)) * jnp.max(jnp.abs(b_kn))
    static_ok = jnp.maximum(bound_a, bound_b) <= SHIFT_MAX
    shift_a = jnp.where(static_ok, bound_a, 0.0)
    shift_b = jnp.where(static_ok, bound_b, 0.0)
    slopes = _alibi_slopes(A_HEADS + B_HEADS)
    bias_a = _dil_bias(slopes[0::2], shift_a)
    bias_b = _band_bias(B_WINDOW, 1.0, slopes[1::2], shift_b, B_HEADS // B_KV_HEADS)
    sink2 = b_sink.astype(_F32) * LOG2E - shift_b

    def attend(online, qa, ka, va, qb, kb, vb, bias_a, bias_b, sink2):
        return (_dil_attn(qa, ka, va, bias_a, online=online),
                _band_b(qb, kb, vb, bias_b, sink2, online=online))

    oa, ob = lax.cond(static_ok, functools.partial(attend, False), functools.partial(attend, True),
                      qa, ka, va, qb, kb, vb, bias_a, bias_b, sink2)
    w_out = w_out.astype(_BF16)
    return _out_proj(x2, [oa.reshape(-1, aw), ob.reshape(-1, bqw)], [w_out[:aw], w_out[aw:]])


def _mixer_c(x2, bsz, seq, g, w_in, qn, kn, w_out):
    qw = C_HEADS * HEAD_DIM
    kvw = C_KV_HEADS * HEAD_DIM
    rope = _rope_tables(seq)
    bound = HEAD_DIM * QK_SCALE * LOG2E * jnp.max(jnp.abs(qn)) * jnp.max(jnp.abs(kn))
    static_ok = bound <= SHIFT_MAX
    shift = jnp.where(static_ok, bound, 0.0)
    segs = [_Seg(0, qw, True, True, QK_SCALE * LOG2E, "plain_ext"),
            _Seg(qw, kvw, True, True, 1.0, "T_ext"),
            _Seg(qw + kvw, kvw, False, False, 1.0, "heads_ext")]
    q, kt, v = _proj_prep(x2, bsz, seq, g, w_in.astype(_BF16), segs, [qn, kn, None],
                          rope=rope, shift=shift)
    o = lax.cond(static_ok,
                 functools.partial(_dense_attn, online=False),
                 functools.partial(_dense_attn, online=True), q, kt, v)
    return _out_proj(x2, [o.reshape(-1, qw)], [w_out.astype(_BF16)])


def kernel(x, mix_norm, ffn_norm, ab_w_in, a_q_norm, a_k_norm, b_q_norm, b_k_norm, b_sink, ab_w_out,
           c_w_in, c_q_norm, c_k_norm, c_w_out, moe_group_w, moe_group_b, moe_expert_w, moe_expert_b,
           moe_w_gate, moe_w_up, moe_w_down):
    bsz, seq, d = x.shape
    x2 = x.reshape(bsz * seq, d)
    depth = mix_norm.shape[0]
    for layer in range(depth):
        i = layer // 2
        if layer % 2 == 0:
            x2 = _mixer_ab(x2, bsz, seq, mix_norm[layer], ab_w_in[i], a_q_norm[i], a_k_norm[i],
                           b_q_norm[i], b_k_norm[i], b_sink[i], ab_w_out[i])
        else:
            x2 = _mixer_c(x2, bsz, seq, mix_norm[layer], c_w_in[i], c_q_norm[i], c_k_norm[i],
                          c_w_out[i])
        x2 = _moe(x2, ffn_norm[layer], moe_group_w[layer], moe_group_b[layer], moe_expert_w[layer],
                  moe_expert_b[layer], moe_w_gate, moe_w_up, moe_w_down, layer)
    return x2.reshape(bsz, seq, d)
```

```python
import functools
import math
from typing import NamedTuple

import jax
import jax.numpy as jnp
import numpy as np
from jax import lax
from jax.experimental import pallas as pl
from jax.experimental.pallas import tpu as pltpu
from jax.experimental.pallas import tpu_sc as plsc

HEAD_DIM = 64
LANES = 128
N_HEADS = 16
A_HEADS = 8
B_HEADS = 8
B_KV_HEADS = 2
C_HEADS = 16
C_KV_HEADS = 4
A_PATTERNS = ((128, 1), (512, 4), (2048, 16))
B_WINDOW = 128
GRID_W = 64
ROPE_THETA = 10000.0
ROPE_AXIS_DIM = HEAD_DIM // 2
ALIBI_MAX_BIAS = 8.0
RMS_EPS = 1e-6
MOE_GROUPS = 4
MOE_EXPERTS = 8
N_EXPERTS = MOE_GROUPS * MOE_EXPERTS
MOE_D_FF = 256
QK_SCALE = HEAD_DIM ** -0.5
LOG2E = math.log2(math.e)
SHIFT_MAX = 60.0
NEG_BIG = -1e30
VMEM_LIMIT = 52 * 1024 * 1024

BAND_BQ = 256
BAND_KB = 128
BAND_KW = BAND_BQ + 2 * BAND_KB
DIL_BQ = 256
DIL_REACH = max(w // 2 for w, _ in A_PATTERNS)
DIL_KW = DIL_BQ + 2 * DIL_REACH
DIL_CHUNK = 768
DENSE_BQ = 512
DENSE_ROWS = 256
DENSE_BK = 8192
DENSE_BK_ONLINE = 512
MOE_TM = 512
MOE_CHAINS = 1
ROUTE_E0 = 4
ROUTE_ROWS = 48
PACK_W = 256
SC_WINDOW = 128

_BF16 = jnp.bfloat16
_F32 = jnp.float32


def _params(*sem):
    return pltpu.CompilerParams(dimension_semantics=sem, vmem_limit_bytes=VMEM_LIMIT)


def _alibi_slopes(n):
    return np.asarray(2.0 ** (-ALIBI_MAX_BIAS * np.arange(1, n + 1) / n), dtype=np.float32)


def _head_norm(y, gain, head_ones):
    y2 = y * y
    hi = y2.astype(_BF16)
    lo = (y2 - hi.astype(_F32)).astype(_BF16)
    ss = (jnp.dot(hi, head_ones, preferred_element_type=_F32)
          + jnp.dot(lo, head_ones, preferred_element_type=_F32))
    return y * lax.rsqrt(ss * (1.0 / HEAD_DIM) + RMS_EPS) * gain


def _rope(y, cos, sin):
    lane = lax.broadcasted_iota(jnp.int32, y.shape, 1)
    first = (lane % 32) < 16
    partner = jnp.where(first, pltpu.roll(y, LANES - 16, axis=1), pltpu.roll(y, 16, axis=1))
    return y * cos + partner * sin


class _Seg(NamedTuple):
    col0: int
    ncols: int
    norm: bool
    rope: bool
    scale: float
    mode: str


def _prep_slab(y, c, o_ref, seg, gain_ref, cos_ref, sin_ref, shift_ref, head_ones):
    rows = y.shape[0]
    if seg.norm:
        y = _head_norm(y, gain_ref[...], head_ones)
    if seg.rope:
        y = _rope(y, cos_ref[...], sin_ref[...])
    if seg.scale != 1.0:
        y = y * seg.scale
    sl = slice(c * LANES, (c + 1) * LANES)
    if seg.mode == "plain":
        o_ref[:, sl] = y.astype(_BF16)
    elif seg.mode == "T":
        o_ref[sl, :] = y.T.astype(_BF16)
    elif seg.mode in ("plain_ext", "heads_ext"):
        lane = lax.broadcasted_iota(jnp.int32, y.shape, 1)
        if seg.mode == "plain_ext":
            fill = jnp.where(lane == HEAD_DIM, 1.0, 0.0)
        else:
            fill = jnp.ones(y.shape, _F32)
        for k, yk in enumerate((y, pltpu.roll(y, HEAD_DIM, axis=1))):
            ext = jnp.where(lane < HEAD_DIM, yk, fill).astype(_BF16)
            if seg.mode == "plain_ext":
                o_ref[:, (2 * c + k) * LANES:(2 * c + k + 1) * LANES] = ext
            else:
                o_ref[2 * c + k] = ext
    else:
        assert seg.mode == "T_ext"
        yt = y.T
        row = lax.broadcasted_iota(jnp.int32, (HEAD_DIM, rows), 0)
        extra = jnp.where(row == 0, -shift_ref[0], 0.0)
        for k in range(2):
            ext = jnp.concatenate([yt[k * HEAD_DIM:(k + 1) * HEAD_DIM], extra], axis=0)
            o_ref[(2 * c + k) * LANES:(2 * c + k + 1) * LANES, :] = ext.astype(_BF16)


def _combine_tile(x_ref, y_ref, route_ref):
    coef = route_ref[...].T
    y0 = _load_packed(y_ref[:, 0])
    y1 = _load_packed(y_ref[:, 1])
    return x_ref[...] + coef[:, 2:3] * y0 + coef[:, 3:4] * y1


def _proj_prep_kernel(*refs, segs, combine):
    it = iter(refs)
    shift_ref = next(it) if any(s.mode == "T_ext" for s in segs) else None
    x_ref, g_ref, w_ref = next(it), next(it), next(it)
    y_ref, route_ref = (next(it), next(it)) if combine else (None, None)
    gain_refs = [next(it) if s.norm else None for s in segs]
    cos_ref, sin_ref = (next(it), next(it)) if any(s.rope for s in segs) else (None, None)
    outs = list(it)
    if combine:
        x = _combine_tile(x_ref, y_ref, route_ref)
        outs.pop(0)[...] = x
    else:
        x = x_ref[...]
    ms = jnp.mean(x * x, axis=-1, keepdims=True)
    h = (x * lax.rsqrt(ms + RMS_EPS) * g_ref[...]).astype(_BF16)
    proj = jnp.dot(h, w_ref[...], preferred_element_type=_F32)
    head_ones = (lax.broadcasted_iota(jnp.int32, (LANES, LANES), 0) // HEAD_DIM
                 == lax.broadcasted_iota(jnp.int32, (LANES, LANES), 1) // HEAD_DIM).astype(_BF16)
    for seg, gain_ref, o_ref in zip(segs, gain_refs, outs):
        for c in range(seg.ncols // LANES):
            y = proj[:, seg.col0 + c * LANES:seg.col0 + (c + 1) * LANES]
            _prep_slab(y, c, o_ref, seg, gain_ref, cos_ref, sin_ref, shift_ref, head_ones)


def _proj_prep(x2, bsz, seq, g, w, segs, gains, rope=None, shift=None, pending=None, tm=512):
    n, d = x2.shape
    p = w.shape[1]
    nt = seq // tm
    assert seq % tm == 0
    const = lambda shape: pl.BlockSpec(shape, lambda i: (0,) * len(shape))
    row = pl.BlockSpec((tm, d), lambda i: (i, 0))
    ins, in_specs = [], []
    if shift is not None:
        ins.append(shift.reshape(1).astype(_F32))
        in_specs.append(pl.BlockSpec(memory_space=pltpu.SMEM))
    ins += [x2, g.reshape(1, d), w]
    in_specs += [row, const((1, d)), const((d, p))]
    if pending is not None:
        ins += list(pending)
        in_specs += [pl.BlockSpec((2, 2, tm, PACK_W), lambda i: (0, 0, i, 0)),
                     pl.BlockSpec((8, tm), lambda i: (0, i))]
    for seg, gain in zip(segs, gains):
        if seg.norm:
            ins.append(jnp.tile(gain.astype(_F32), LANES // HEAD_DIM).reshape(1, LANES))
            in_specs.append(const((1, LANES)))
    if rope is not None:
        ins += list(rope)
        in_specs += [pl.BlockSpec((tm, LANES), lambda i: (i % nt, 0))] * 2
    out_shape, out_specs = [], []
    if pending is not None:
        out_shape.append(jax.ShapeDtypeStruct((n, d), _F32))
        out_specs.append(row)
    for seg in segs:
        nc = seg.ncols
        if seg.mode in ("plain", "plain_ext"):
            wout = nc if seg.mode == "plain" else 2 * nc
            out_shape.append(jax.ShapeDtypeStruct((bsz, seq, wout), _BF16))
            out_specs.append(pl.BlockSpec((None, tm, wout), lambda i: (i // nt, i % nt, 0)))
        elif seg.mode in ("T", "T_ext"):
            wout = nc if seg.mode == "T" else 2 * nc
            out_shape.append(jax.ShapeDtypeStruct((bsz, wout, seq), _BF16))
            out_specs.append(pl.BlockSpec((None, wout, tm), lambda i: (i // nt, 0, i % nt)))
        else:
            assert seg.mode == "heads_ext"
            nh = nc // HEAD_DIM
            out_shape.append(jax.ShapeDtypeStruct((bsz, nh, seq, LANES), _BF16))
            out_specs.append(pl.BlockSpec((None, nh, tm, LANES), lambda i: (i // nt, 0, i % nt, 0)))
    return pl.pallas_call(
        functools.partial(_proj_prep_kernel, segs=tuple(segs), combine=pending is not None),
        grid=(n // tm,),
        in_specs=in_specs, out_specs=out_specs, out_shape=out_shape,
        compiler_params=_params("parallel"),
        name="proj_prep",
    )(*ins)


def _rope_tables(seq):
    t = np.arange(seq)
    row = (t // GRID_W).astype(np.float32)
    col = (t % GRID_W).astype(np.float32)
    inv_freq = jnp.asarray(ROPE_THETA, _F32) ** (-jnp.arange(0, ROPE_AXIS_DIM, 2, dtype=_F32) / ROPE_AXIS_DIM)
    ang_r = jnp.asarray(row)[:, None] * inv_freq[None, :]
    ang_c = jnp.asarray(col)[:, None] * inv_freq[None, :]
    cr, sr, cc, sc = jnp.cos(ang_r), jnp.sin(ang_r), jnp.cos(ang_c), jnp.sin(ang_c)
    cos = jnp.concatenate([cr, cr, cc, cc], axis=-1)
    sin = jnp.concatenate([-sr, sr, -sc, sc], axis=-1)
    return jnp.tile(cos, (1, 2)), jnp.tile(sin, (1, 2))


def _band_bias(hw, dist_scale, slopes, shift, stack):
    bq, kb = BAND_BQ, BAND_KB
    assert hw <= kb
    r = np.arange(bq)[:, None]
    c = np.arange(BAND_KW)[None, :]
    rel = np.abs(c - kb - r)
    dist = rel.astype(np.float32) * np.float32(dist_scale)
    ok = np.stack([(rel <= hw) & ~(first & (c < kb)) & ~(last & (c >= kb + bq))
                   for first, last in ((False, False), (True, False), (False, True), (True, True))])
    alibi = -(np.asarray(slopes, np.float32)[:, None, None] * dist[None]) * np.float32(LOG2E)
    bias = jnp.where(ok[:, None], jnp.asarray(alibi)[None] - shift, NEG_BIG)
    nh = len(slopes)
    return bias.reshape(4, nh // stack, stack * bq, BAND_KW)


def _edge_variant(i, nb):
    return jnp.where(i == 0, 1, 0) + jnp.where(i == nb - 1, 2, 0)


def _window_blocks(nb):
    per = BAND_BQ // BAND_KB
    last = nb * per - 1
    fns = [lambda i: jnp.maximum(per * i - 1, 0)]
    fns += [functools.partial(lambda j, i: per * i + j, j) for j in range(per)]
    fns += [lambda i: jnp.minimum(per * i + per, last)]
    return fns


def _dil_bias(slopes, shift):
    r = jnp.arange(DIL_BQ, dtype=jnp.int32)[:, None]
    u = jnp.arange(DIL_BQ + 4 * DIL_REACH, dtype=jnp.int32)[None, :]
    delta = u - 2 * DIL_REACH - r
    dist = jnp.abs(delta)
    mult = sum(((delta % d == 0) & (dist <= ((w // 2) // d) * d)).astype(_F32) for w, d in A_PATTERNS)
    alibi = -(jnp.asarray(slopes, _F32)[:, None, None] * dist.astype(_F32)[None]) * LOG2E
    return jnp.where((mult > 0)[None], jnp.log2(jnp.maximum(mult, 1.0))[None] + alibi - shift, NEG_BIG)


def _dil_attn_kernel(q_ref, kt_ref, v_ref, bias_ref, o_ref, *, online):
    bq = q_ref.shape[0]
    seq = kt_ref.shape[1]
    q0 = pl.program_id(2) * bq
    w0 = pl.multiple_of(jnp.clip(q0 - DIL_REACH, 0, seq - DIL_KW), bq)
    u0 = pl.multiple_of(w0 - q0 + 2 * DIL_REACH, bq)
    q = q_ref[...]
    outs = []
    for j in range(2):
        rows = slice(j * HEAD_DIM, (j + 1) * HEAD_DIM)
        scores = []
        for c in range(DIL_KW // DIL_CHUNK):
            ks = pl.ds(pl.multiple_of(w0 + c * DIL_CHUNK, bq), DIL_CHUNK)
            us = pl.ds(pl.multiple_of(u0 + c * DIL_CHUNK, bq), DIL_CHUNK)
            scores.append(jnp.dot(q[:, rows], kt_ref[rows, ks], preferred_element_type=_F32)
                          + bias_ref[j, :, us])
        if online:
            m = functools.reduce(jnp.maximum, [jnp.max(s, axis=-1, keepdims=True) for s in scores])
            scores = [s - m for s in scores]
        acc = jnp.zeros((bq, LANES), _F32)
        for c, s in enumerate(scores):
            ks = pl.ds(pl.multiple_of(w0 + c * DIL_CHUNK, bq), DIL_CHUNK)
            acc += jnp.dot(jnp.exp2(s).astype(_BF16), v_ref[j, ks, :], preferred_element_type=_F32)
        outs.append((acc / pltpu.roll(acc, HEAD_DIM, axis=1))[:, :HEAD_DIM])
    o_ref[...] = jnp.concatenate(outs, axis=1).astype(_BF16)


def _dil_attn(q, kt, v, bias, *, online):
    bsz, seq, w = q.shape
    bq = DIL_BQ
    assert seq >= DIL_KW and seq % bq == 0
    return pl.pallas_call(
        functools.partial(_dil_attn_kernel, online=online), grid=(A_HEADS // 2, bsz, seq // bq),
        in_specs=[pl.BlockSpec((None, bq, LANES), lambda p, b, i: (b, i, p)),
                  pl.BlockSpec((None, LANES, seq), lambda p, b, i: (b, p, 0)),
                  pl.BlockSpec((None, 2, seq, LANES), lambda p, b, i: (b, p, 0, 0)),
                  pl.BlockSpec((2, bq, bias.shape[-1]), lambda p, b, i: (p, 0, 0))],
        out_specs=pl.BlockSpec((None, bq, LANES), lambda p, b, i: (b, i, p)),
        out_shape=jax.ShapeDtypeStruct((bsz, seq, w), _BF16),
        compiler_params=_params("parallel", "parallel", "parallel"),
        name="dil_attn_online" if online else "dil_attn",
    )(q, kt, v, bias)


def _band_b_kernel(sink2_ref, sinkp_ref, q_ref, *refs, online):
    bq = q_ref.shape[0]
    g = pl.program_id(1)
    grp = B_HEADS // B_KV_HEADS
    npc = BAND_KW // BAND_KB
    k_refs, v_refs, (bias_ref, o_ref) = refs[:npc], refs[npc:2 * npc], refs[2 * npc:]
    kt = jnp.concatenate([r[...] for r in k_refs], axis=1)
    v = jnp.concatenate([r[...] for r in v_refs], axis=0)
    q = q_ref[...]
    q4 = jnp.concatenate([q[:, i * HEAD_DIM:(i + 1) * HEAD_DIM] for i in range(grp)], axis=0)
    s = jnp.dot(q4, kt, preferred_element_type=_F32) + bias_ref[...]
    if online:
        m = jnp.max(s, axis=-1, keepdims=True)
        s = s - m
    acc = jnp.dot(jnp.exp2(s).astype(_BF16), v, preferred_element_type=_F32)
    for i in range(grp):
        a = acc[i * bq:(i + 1) * bq]
        if online:
            mi = m[i * bq:(i + 1) * bq]
            sk = sink2_ref[g * grp + i]
            mm = jnp.maximum(mi, sk)
            a = a * jnp.exp2(mi - mm)
            o = a / (pltpu.roll(a, HEAD_DIM, axis=1) + jnp.exp2(sk - mm))
        else:
            o = a / (pltpu.roll(a, HEAD_DIM, axis=1) + sinkp_ref[g * grp + i])
        o_ref[:, i * HEAD_DIM:(i + 1) * HEAD_DIM] = o[:, :HEAD_DIM].astype(_BF16)


def _band_b(q, kt, v, bias, sink2, *, online):
    bsz, seq_len, w = q.shape
    bq, kb = BAND_BQ, BAND_KB
    nb = seq_len // bq
    grp = B_HEADS // B_KV_HEADS
    gw = w // B_KV_HEADS
    blocks = _window_blocks(nb)
    kspec = lambda f: pl.BlockSpec((None, HEAD_DIM, kb), lambda b, g, i: (b, g, f(i)))
    vspec = lambda f: pl.BlockSpec((None, None, kb, LANES), lambda b, g, i: (b, g, f(i), 0))
    qspec = pl.BlockSpec((None, bq, gw), lambda b, g, i: (b, i, g))
    smem = pl.BlockSpec(memory_space=pltpu.SMEM)
    return pl.pallas_call(
        functools.partial(_band_b_kernel, online=online), grid=(bsz, B_KV_HEADS, nb),
        in_specs=[smem, smem, qspec] + [kspec(f) for f in blocks] + [vspec(f) for f in blocks]
        + [pl.BlockSpec((None, None, grp * bq, BAND_KW),
                        lambda b, g, i: (_edge_variant(i, nb), g, 0, 0))],
        out_specs=qspec,
        out_shape=jax.ShapeDtypeStruct((bsz, seq_len, w), _BF16),
        compiler_params=_params("parallel", "parallel", "parallel"),
        name="band_b_online" if online else "band_b",
    )(sink2, jnp.exp2(sink2), q, *([kt] * len(blocks)), *([v] * len(blocks)), bias)


def _dense_attn_kernel(q_ref, kt_ref, v_ref, o_ref, acc_scr, *rest, bk, online):
    seq = kt_ref.shape[1]
    grp = C_HEADS // C_KV_HEADS
    acc_scr[...] = jnp.zeros(acc_scr.shape, _F32)
    if online:
        m_scr, = rest
        m_scr[...] = jnp.full(m_scr.shape, NEG_BIG, _F32)

    def body(j, carry):
        k0 = pl.multiple_of(j * bk, bk)
        kt = kt_ref[:, pl.ds(k0, bk)]
        v = v_ref[pl.ds(k0, bk), :]
        for c in range(grp):
            for r in range(q_ref.shape[0] // DENSE_ROWS):
                rs = slice(r * DENSE_ROWS, (r + 1) * DENSE_ROWS)
                s = jnp.dot(q_ref[rs, c * LANES:(c + 1) * LANES], kt, preferred_element_type=_F32)
                if online:
                    m_old = m_scr[c, rs]
                    m_new = jnp.maximum(m_old, jnp.max(s, axis=-1, keepdims=True))
                    m_scr[c, rs] = m_new
                    p = jnp.exp2(s - m_new).astype(_BF16)
                    acc_scr[c, rs] = jnp.exp2(m_old - m_new) * acc_scr[c, rs] + jnp.dot(
                        p, v, preferred_element_type=_F32)
                else:
                    p = jnp.exp2(s).astype(_BF16)
                    acc_scr[c, rs] += jnp.dot(p, v, preferred_element_type=_F32)
        return carry

    lax.fori_loop(0, seq // bk, body, 0)
    for c in range(grp):
        a = acc_scr[c]
        o = a / pltpu.roll(a, HEAD_DIM, axis=1)
        o_ref[:, c * HEAD_DIM:(c + 1) * HEAD_DIM] = o[:, :HEAD_DIM].astype(_BF16)


def _dense_attn(q, kt, v, *, online, bq=DENSE_BQ, bk=DENSE_BK):
    bsz, seq, _ = q.shape
    bq, bk = min(bq, seq), min(DENSE_BK_ONLINE if online else bk, seq)
    grp = C_HEADS // C_KV_HEADS
    scratch = [pltpu.VMEM((grp, bq, LANES), _F32)]
    if online:
        scratch.append(pltpu.VMEM((grp, bq, 1), _F32))
    return pl.pallas_call(
        functools.partial(_dense_attn_kernel, bk=bk, online=online),
        grid=(bsz, C_KV_HEADS, seq // bq),
        in_specs=[pl.BlockSpec((None, bq, grp * LANES), lambda b, g, i: (b, i, g)),
                  pl.BlockSpec((None, LANES, seq), lambda b, g, i: (b, g, 0)),
                  pl.BlockSpec((None, None, seq, LANES), lambda b, g, i: (b, g, 0, 0))],
        out_specs=pl.BlockSpec((None, bq, grp * HEAD_DIM), lambda b, g, i: (b, i, g)),
        out_shape=jax.ShapeDtypeStruct((bsz, seq, C_HEADS * HEAD_DIM), _BF16),
        scratch_shapes=scratch,
        compiler_params=_params("parallel", "parallel", "parallel"),
        name="dense_attn_online" if online else "dense_attn",
    )(q, kt, v)


def _pack_pair(a, b):
    wa = lax.bitcast_convert_type(a.astype(_BF16).astype(_F32), jnp.uint32) >> 16
    wb = lax.bitcast_convert_type(b.astype(_BF16).astype(_F32), jnp.uint32) & jnp.uint32(0xFFFF0000)
    return lax.bitcast_convert_type(wa | wb, jnp.int32)


def _unpack_pair(w):
    u = lax.bitcast_convert_type(w, jnp.uint32)
    return (lax.bitcast_convert_type(u << 16, _F32),
            lax.bitcast_convert_type(u & jnp.uint32(0xFFFF0000), _F32))


def _store_packed(ref, y):
    q = PACK_W
    for j in range(2):
        ref[j] = _pack_pair(y[:, 2 * j * q:(2 * j + 1) * q], y[:, (2 * j + 1) * q:(2 * j + 2) * q])


def _load_packed(ref):
    parts = []
    for j in range(2):
        parts += list(_unpack_pair(ref[j]))
    return jnp.concatenate(parts, axis=1)


def _sc_mesh():
    return plsc.VectorSubcoreMesh(core_axis_name="core", subcore_axis_name="subcore")


def _sc_gather(table, idx):
    n = idx.shape[0]
    d = table.shape[1]

    @pl.kernel(out_type=jax.ShapeDtypeStruct((n, d), table.dtype), mesh=_sc_mesh())
    def gather(x_hbm, i_hbm, o_hbm):
        def body(i_vmem, o_vmem):
            pltpu.sync_copy(x_hbm.at[i_vmem.at[0]], o_vmem)

        pltpu.emit_pipeline(
            body, grid=(n // SC_WINDOW,),
            in_specs=[pl.BlockSpec((1, SC_WINDOW), index_map=lambda i: (0, i))],
            out_specs=[pl.BlockSpec((SC_WINDOW, d), index_map=lambda i: (i, 0))],
            core_axis_name=("core", "subcore"),
            dimension_semantics=(pltpu.PARALLEL,),
        )(i_hbm, o_hbm)

    return gather(table, idx.reshape(1, n))


def _sc_scatter(src, idx, n_out, reps):
    n = idx.shape[0]
    r2, d = src.shape
    nb = r2 // 2 // SC_WINDOW

    @pl.kernel(out_type=jax.ShapeDtypeStruct((n_out, d), src.dtype), mesh=_sc_mesh())
    def scatter(x_hbm, i_hbm, o_hbm):
        def body(x_vmem, i_vmem):
            pltpu.sync_copy(x_vmem, o_hbm.at[i_vmem.at[0]])

        pltpu.emit_pipeline(
            body, grid=(n // SC_WINDOW,),
            in_specs=[pl.BlockSpec((SC_WINDOW, d),
                                   index_map=lambda i: ((i // (reps * nb)) * nb + i % nb, 0)),
                      pl.BlockSpec((1, SC_WINDOW), index_map=lambda i: (0, i))],
            out_specs=[],
            core_axis_name=("core", "subcore"),
            dimension_semantics=(pltpu.PARALLEL,),
        )(x_hbm, i_hbm)

    return scatter(src, idx.reshape(1, n))


def _route_tile(x, g_ref, w_ref, b_ref, h_ref, route_ref, cnt_ref, carry_ref):
    tm = x.shape[0]

    @pl.when(pl.program_id(0) == 0)
    def _():
        carry_ref[...] = jnp.zeros(carry_ref.shape, _F32)

    ms = jnp.mean(x * x, axis=-1, keepdims=True)
    h = x * lax.rsqrt(ms + RMS_EPS) * g_ref[...]
    _store_packed(h_ref, h)
    h_hi = h.astype(_BF16)
    h_lo = (h - h_hi.astype(_F32)).astype(_BF16)
    hw = (jnp.dot(h_hi, w_ref[...], preferred_element_type=_F32)
          + jnp.dot(h_lo, w_ref[...], preferred_element_type=_F32))
    logits = hw[:, :LANES] + hw[:, LANES:] + b_ref[...]
    lt = logits.T[:ROUTE_ROWS]
    row = lax.broadcasted_iota(jnp.int32, (ROUTE_ROWS, tm), 0)

    lg = jnp.where(row < MOE_GROUPS, lt, NEG_BIG)
    mg = jnp.max(lg, axis=0, keepdims=True)
    zg = jnp.sum(jnp.exp(lg - mg), axis=0, keepdims=True)
    p_grp = 1.0 / zg
    g_idx = jnp.min(jnp.where(lg == mg, row, ROUTE_ROWS), axis=0, keepdims=True)

    e_row = row - ROUTE_E0
    emask = (e_row >= 0) & (e_row < N_EXPERTS) & ((e_row >> 3) == g_idx)
    le = jnp.where(emask, lt, NEG_BIG)
    m1 = jnp.max(le, axis=0, keepdims=True)
    i1 = jnp.min(jnp.where(le == m1, row, ROUTE_ROWS), axis=0, keepdims=True)
    le2 = jnp.where(row == i1, NEG_BIG, le)
    m2 = jnp.max(le2, axis=0, keepdims=True)
    i2 = jnp.min(jnp.where(le2 == m2, row, ROUTE_ROWS), axis=0, keepdims=True)
    e21 = jnp.exp(m2 - m1)
    c1 = p_grp / (1.0 + e21)
    c2 = p_grp * e21 / (1.0 + e21)

    onehot = jnp.where(row == i1, 1.0, jnp.where(row == i2, 1.0, 0.0)).astype(_BF16)
    upper = (lax.broadcasted_iota(jnp.int32, (tm, tm), 0)
             <= lax.broadcasted_iota(jnp.int32, (tm, tm), 1)).astype(_BF16)
    cum = jnp.dot(onehot, upper, preferred_element_type=_F32) + carry_ref[...]
    r1 = jnp.sum(jnp.where(row == i1, cum, 0.0), axis=0, keepdims=True) - 1.0
    r2 = jnp.sum(jnp.where(row == i2, cum, 0.0), axis=0, keepdims=True) - 1.0
    total = jnp.max(cum, axis=1, keepdims=True)
    carry_ref[...] = total
    cnt_ref[...] = jnp.broadcast_to(total, cnt_ref.shape)

    rows = ((i1 - ROUTE_E0).astype(_F32), (i2 - ROUTE_E0).astype(_F32), c1, c2, r1, r2)
    rrow = lax.broadcasted_iota(jnp.int32, (8, tm), 0)
    route = jnp.zeros((8, tm), _F32)
    for k, val in enumerate(rows):
        route = jnp.where(rrow == k, val, route)
    route_ref[...] = route


def _out_router_kernel(*refs, n_in):
    x_ref = refs[0]
    o_refs = refs[1:1 + n_in]
    w_refs = refs[1 + n_in:1 + 2 * n_in]
    g_ref, wr_ref, br_ref, xnew_ref = refs[1 + 2 * n_in:5 + 2 * n_in]
    acc = x_ref[...]
    for o_ref, w_ref in zip(o_refs, w_refs):
        acc = acc + jnp.dot(o_ref[...], w_ref[...], preferred_element_type=_F32)
    xnew_ref[...] = acc
    _route_tile(acc, g_ref, wr_ref, br_ref, *refs[5 + 2 * n_in:])


def _out_router(x2, os_, ws, g, w_router, b_router, tm=512):
    n, d = x2.shape
    w_hi = w_router.astype(_BF16)
    w_lo = (w_router - w_hi.astype(_F32)).astype(_BF16)
    const = lambda shape: pl.BlockSpec(shape, lambda i: (0,) * len(shape))
    row = pl.BlockSpec((tm, d), lambda i: (i, 0))
    in_specs = [row] + [pl.BlockSpec((tm, o.shape[1]), lambda i: (i, 0)) for o in os_]
    in_specs += [const(w.shape) for w in ws]
    in_specs += [const((1, d)), const((d, 2 * LANES)), const((1, LANES))]
    return pl.pallas_call(
        functools.partial(_out_router_kernel, n_in=len(os_)), grid=(n // tm,),
        in_specs=in_specs,
        out_specs=[row, pl.BlockSpec((2, tm, PACK_W), lambda i: (0, i, 0)),
                   pl.BlockSpec((8, tm), lambda i: (0, i)), const((ROUTE_ROWS, LANES))],
        out_shape=[jax.ShapeDtypeStruct((n, d), _F32),
                   jax.ShapeDtypeStruct((2, n, PACK_W), jnp.int32),
                   jax.ShapeDtypeStruct((8, n), _F32),
                   jax.ShapeDtypeStruct((ROUTE_ROWS, LANES), _F32)],
        scratch_shapes=[pltpu.VMEM((ROUTE_ROWS, 1), _F32)],
        compiler_params=_params("arbitrary"),
        name="out_router",
    )(x2, *os_, *ws, g.reshape(1, d), jnp.concatenate([w_hi, w_lo], axis=1), b_router)


def _expert_kernel(te_ref, nv_ref, xs_ref, wg_ref, wu_ref, wd_ref, ys_ref, wgu_s, wd_s):
    j = pl.program_id(0)
    prev = te_ref[jnp.maximum(j - 1, 0)]

    @pl.when((j == 0) | (te_ref[j] != prev))
    def _():
        wgu_s[:, :MOE_D_FF] = wg_ref[...].astype(_BF16)
        wgu_s[:, MOE_D_FF:] = wu_ref[...].astype(_BF16)
        wd_s[...] = wd_ref[...].astype(_BF16)

    @pl.when(j < nv_ref[0])
    def _():
        rows = xs_ref.shape[1] // MOE_CHAINS
        for r in range(MOE_CHAINS):
            rs = slice(r * rows, (r + 1) * rows)
            xs = _load_packed(xs_ref[:, rs]).astype(_BF16)
            au = jnp.dot(xs, wgu_s[...], preferred_element_type=_F32)
            a, u = au[:, :MOE_D_FF], au[:, MOE_D_FF:]
            act = (a * (1.0 / (1.0 + jnp.exp(-a))) * u).astype(_BF16)
            _store_packed(ys_ref.at[:, rs], jnp.dot(act, wd_s[...], preferred_element_type=_F32))


def _experts(tile_expert, n_valid, xs, w_gate, w_up, w_down, tm=MOE_TM):
    _, n_slots, pw = xs.shape
    n_tiles = n_slots // tm
    d, f = w_gate.shape[-2:]
    row = lambda j, te, nv: (0, jnp.maximum(jnp.minimum(j, nv[0] - 1), 0), 0)
    grid_spec = pltpu.PrefetchScalarGridSpec(
        num_scalar_prefetch=2, grid=(n_tiles,),
        in_specs=[pl.BlockSpec((2, tm, pw), row),
                  pl.BlockSpec((None, d, f), lambda j, te, nv: (te[j], 0, 0)),
                  pl.BlockSpec((None, d, f), lambda j, te, nv: (te[j], 0, 0)),
                  pl.BlockSpec((None, f, d), lambda j, te, nv: (te[j], 0, 0))],
        out_specs=pl.BlockSpec((2, tm, pw), row),
        scratch_shapes=[pltpu.VMEM((d, 2 * f), _BF16), pltpu.VMEM((f, d), _BF16)])
    return pl.pallas_call(
        _expert_kernel, grid_spec=grid_spec,
        out_shape=jax.ShapeDtypeStruct((2, n_slots, pw), jnp.int32),
        compiler_params=_params("arbitrary"),
        name="experts",
    )(tile_expert, n_valid, xs, w_gate, w_up, w_down)


def _combine_kernel(x_ref, y_ref, route_ref, o_ref):
    o_ref[...] = _combine_tile(x_ref, y_ref, route_ref)


def _combine(x2, y, route, tm=512):
    n, d = x2.shape
    row = pl.BlockSpec((tm, d), lambda i: (i, 0))
    return pl.pallas_call(
        _combine_kernel, grid=(n // tm,),
        in_specs=[row, pl.BlockSpec((2, 2, tm, PACK_W), lambda i: (0, 0, i, 0)),
                  pl.BlockSpec((8, tm), lambda i: (0, i))],
        out_specs=row,
        out_shape=jax.ShapeDtypeStruct((n, d), _F32),
        compiler_params=_params("parallel"),
        name="moe_combine",
    )(x2, y, route)


def _slot_kernel(offset_ref, route_ref, slot_ref, *, n_slots):
    expert = route_ref[0:2, :].astype(jnp.int32)
    pos = route_ref[4:6, :].astype(jnp.int32)
    for e in range(N_EXPERTS):
        pos = pos + jnp.where(expert == e, offset_ref[e], 0)
    slot_ref[0:2, :] = pos
    slot_ref[2:4, :] = pos + n_slots


def _slots(offset, route, n_slots):
    n = route.shape[1]
    return pl.pallas_call(
        functools.partial(_slot_kernel, n_slots=n_slots),
        in_specs=[pl.BlockSpec(memory_space=pltpu.SMEM),
                  pl.BlockSpec((8, n), lambda: (0, 0))],
        out_specs=pl.BlockSpec((4, n), lambda: (0, 0)),
        out_shape=jax.ShapeDtypeStruct((4, n), jnp.int32),
        name="moe_slots",
    )(offset, route)


def _router_params(wg, bg, we, be):
    d = wg.shape[0]
    w_router = jnp.zeros((d, LANES), _F32)
    w_router = w_router.at[:, :MOE_GROUPS].set(wg)
    w_router = w_router.at[:, ROUTE_E0:ROUTE_E0 + N_EXPERTS].set(
        jnp.moveaxis(we, 0, 1).reshape(d, N_EXPERTS))
    b_router = jnp.zeros((1, LANES), _F32)
    b_router = b_router.at[0, :MOE_GROUPS].set(bg)
    b_router = b_router.at[0, ROUTE_E0:ROUTE_E0 + N_EXPERTS].set(be.reshape(-1))
    return w_router, b_router


def _moe_experts(h, route, cnt, w_gate, w_up, w_down, layer):
    n = h.shape[1]
    d = w_gate.shape[-2]
    tm = MOE_TM
    counts = cnt[ROUTE_E0:ROUTE_E0 + N_EXPERTS, 0].astype(jnp.int32)
    tiles_per = (counts + tm - 1) // tm
    tiles_end = jnp.cumsum(tiles_per)
    offset = (tiles_end - tiles_per) * tm
    n_tiles = (2 * n) // tm + N_EXPERTS
    n_valid = tiles_end[-1]
    tile_ids = jnp.minimum(jnp.arange(n_tiles, dtype=jnp.int32), n_valid - 1)
    tile_expert = jnp.sum(tile_ids[:, None] >= tiles_end[None, :], axis=1).astype(jnp.int32)
    n_slots = n_tiles * tm
    slot = _slots(offset, route, n_slots).reshape(-1)

    xs = _sc_scatter(h.reshape(2 * n, PACK_W), slot, 2 * n_slots, reps=2)
    ys = _experts(tile_expert + layer * N_EXPERTS, n_valid.reshape(1),
                  xs.reshape(2, n_slots, PACK_W), w_gate.reshape(-1, d, MOE_D_FF),
                  w_up.reshape(-1, d, MOE_D_FF), w_down.reshape(-1, MOE_D_FF, d))
    y = _sc_gather(ys.reshape(2 * n_slots, PACK_W), slot)
    return y.reshape(2, 2, n, PACK_W)


def _proj_prep_pending(x2, pending, *args, **kwargs):
    outs = _proj_prep(x2, *args, pending=pending, **kwargs)
    return (x2, outs) if pending is None else (outs[0], outs[1:])


def _mixer_ab(x2, pending, router, bsz, seq, g, w_in, a_qn, a_kn, b_qn, b_kn, b_sink, w_out):
    aw = A_HEADS * HEAD_DIM
    bqw = B_HEADS * HEAD_DIM
    bkw = B_KV_HEADS * HEAD_DIM
    scale2 = QK_SCALE * LOG2E
    segs = [_Seg(0, aw, True, False, scale2, "plain"),
            _Seg(aw, aw, True, False, 1.0, "T"),
            _Seg(2 * aw, aw, False, False, 1.0, "heads_ext"),
            _Seg(3 * aw, bqw, True, False, scale2, "plain"),
            _Seg(3 * aw + bqw, bkw, True, False, 1.0, "T"),
            _Seg(3 * aw + bqw + bkw, bkw, False, False, 1.0, "heads_ext")]
    x2, (qa, ka, va, qb, kb, vb) = _proj_prep_pending(
        x2, pending, bsz, seq, g, w_in.astype(_BF16), segs, [a_qn, a_kn, None, b_qn, b_kn, None])

    bound_a = HEAD_DIM * scale2 * jnp.max(jnp.abs(a_qn)) * jnp.max(jnp.abs(a_kn))
    bound_b = HEAD_DIM * scale2 * jnp.max(jnp.abs(b_qn)) * jnp.max(jnp.abs(b_kn))
    static_ok = jnp.maximum(bound_a, bound_b) <= SHIFT_MAX
    shift_a = jnp.where(static_ok, bound_a, 0.0)
    shift_b = jnp.where(static_ok, bound_b, 0.0)
    slopes = _alibi_slopes(A_HEADS + B_HEADS)
    bias_a = _dil_bias(slopes[0::2], shift_a)
    bias_b = _band_bias(B_WINDOW, 1.0, slopes[1::2], shift_b, B_HEADS // B_KV_HEADS)
    sink2 = b_sink.astype(_F32) * LOG2E - shift_b

    def attend(online, qa, ka, va, qb, kb, vb, bias_a, bias_b, sink2):
        return (_dil_attn(qa, ka, va, bias_a, online=online),
                _band_b(qb, kb, vb, bias_b, sink2, online=online))

    oa, ob = lax.cond(static_ok, functools.partial(attend, False), functools.partial(attend, True),
                      qa, ka, va, qb, kb, vb, bias_a, bias_b, sink2)
    w_out = w_out.astype(_BF16)
    return _out_router(x2, [oa.reshape(-1, aw), ob.reshape(-1, bqw)], [w_out[:aw], w_out[aw:]],
                       *router)


def _mixer_c(x2, pending, router, bsz, seq, g, w_in, qn, kn, w_out):
    qw = C_HEADS * HEAD_DIM
    kvw = C_KV_HEADS * HEAD_DIM
    rope = _rope_tables(seq)
    bound = HEAD_DIM * QK_SCALE * LOG2E * jnp.max(jnp.abs(qn)) * jnp.max(jnp.abs(kn))
    static_ok = bound <= SHIFT_MAX
    shift = jnp.where(static_ok, bound, 0.0)
    segs = [_Seg(0, qw, True, True, QK_SCALE * LOG2E, "plain_ext"),
            _Seg(qw, kvw, True, True, 1.0, "T_ext"),
            _Seg(qw + kvw, kvw, False, False, 1.0, "heads_ext")]
    x2, (q, kt, v) = _proj_prep_pending(x2, pending, bsz, seq, g, w_in.astype(_BF16), segs,
                                        [qn, kn, None], rope=rope, shift=shift)
    o = lax.cond(static_ok,
                 functools.partial(_dense_attn, online=False),
                 functools.partial(_dense_attn, online=True), q, kt, v)
    return _out_router(x2, [o.reshape(-1, qw)], [w_out.astype(_BF16)], *router)


def kernel(x, mix_norm, ffn_norm, ab_w_in, a_q_norm, a_k_norm, b_q_norm, b_k_norm, b_sink, ab_w_out,
           c_w_in, c_q_norm, c_k_norm, c_w_out, moe_group_w, moe_group_b, moe_expert_w, moe_expert_b,
           moe_w_gate, moe_w_up, moe_w_down):
    bsz, seq, d = x.shape
    x2 = x.reshape(bsz * seq, d)
    depth = mix_norm.shape[0]
    pending = None
    for layer in range(depth):
        i = layer // 2
        router = (ffn_norm[layer],) + _router_params(moe_group_w[layer], moe_group_b[layer],
                                                     moe_expert_w[layer], moe_expert_b[layer])
        if layer % 2 == 0:
            x2, h, route, cnt = _mixer_ab(x2, pending, router, bsz, seq, mix_norm[layer], ab_w_in[i],
                                          a_q_norm[i], a_k_norm[i], b_q_norm[i], b_k_norm[i],
                                          b_sink[i], ab_w_out[i])
        else:
            x2, h, route, cnt = _mixer_c(x2, pending, router, bsz, seq, mix_norm[layer], c_w_in[i],
                                         c_q_norm[i], c_k_norm[i], c_w_out[i])
        pending = (_moe_experts(h, route, cnt, moe_w_gate, moe_w_up, moe_w_down, layer), route)
    return _combine(x2, *pending).reshape(bsz, seq, d)
```

```python
import functools
import math
from typing import NamedTuple

import jax
import jax.numpy as jnp
import numpy as np
from jax import lax
from jax.experimental import pallas as pl
from jax.experimental.pallas import tpu as pltpu
from jax.experimental.pallas import tpu_sc as plsc

HEAD_DIM = 64
LANES = 128
N_HEADS = 16
A_HEADS = 8
B_HEADS = 8
B_KV_HEADS = 2
C_HEADS = 16
C_KV_HEADS = 4
A_PATTERNS = ((128, 1), (512, 4), (2048, 16))
B_WINDOW = 128
GRID_W = 64
ROPE_THETA = 10000.0
ROPE_AXIS_DIM = HEAD_DIM // 2
ALIBI_MAX_BIAS = 8.0
RMS_EPS = 1e-6
MOE_GROUPS = 4
MOE_EXPERTS = 8
N_EXPERTS = MOE_GROUPS * MOE_EXPERTS
MOE_D_FF = 256
QK_SCALE = HEAD_DIM ** -0.5
LOG2E = math.log2(math.e)
SHIFT_MAX = 60.0
NEG_BIG = -1e30
VMEM_LIMIT = 52 * 1024 * 1024

BAND_BQ = 256
BAND_KB = 128
BAND_KW = BAND_BQ + 2 * BAND_KB
DIL_BQ = 256
DIL_REACH = max(w // 2 for w, _ in A_PATTERNS)
DIL_KW = DIL_BQ + 2 * DIL_REACH
DIL_CHUNK = 768
DENSE_BQ = 256
DENSE_KCH = 2048
MOE_TM = 512
MOE_CHAINS = 1
ROUTE_E0 = 4
ROUTE_CHAINS = 1
ROUTE_ROWS = 48
PACK_W = 256
SC_WINDOW = 128

_BF16 = jnp.bfloat16
_F32 = jnp.float32


def _params(*sem):
    return pltpu.CompilerParams(dimension_semantics=sem, vmem_limit_bytes=VMEM_LIMIT)


def _alibi_slopes(n):
    return np.asarray(2.0 ** (-ALIBI_MAX_BIAS * np.arange(1, n + 1) / n), dtype=np.float32)


def _head_norm(y, gain, head_ones):
    y2 = y * y
    hi = y2.astype(_BF16)
    lo = (y2 - hi.astype(_F32)).astype(_BF16)
    ss = (jnp.dot(hi, head_ones, preferred_element_type=_F32)
          + jnp.dot(lo, head_ones, preferred_element_type=_F32))
    return y * lax.rsqrt(ss * (1.0 / HEAD_DIM) + RMS_EPS) * gain


def _rope(y, cos, sin):
    lane = lax.broadcasted_iota(jnp.int32, y.shape, 1)
    first = (lane % 32) < 16
    partner = jnp.where(first, pltpu.roll(y, LANES - 16, axis=1), pltpu.roll(y, 16, axis=1))
    return y * cos + partner * sin


class _Seg(NamedTuple):
    col0: int
    ncols: int
    norm: bool
    rope: bool
    scale: float
    mode: str


def _prep_slab(y, c, o_ref, seg, gain_ref, cos_ref, sin_ref, shift_ref, head_ones):
    rows = y.shape[0]
    if seg.norm:
        y = _head_norm(y, gain_ref[...], head_ones)
    if seg.rope:
        y = _rope(y, cos_ref[...], sin_ref[...])
    if seg.scale != 1.0:
        y = y * seg.scale
    sl = slice(c * LANES, (c + 1) * LANES)
    if seg.mode == "plain":
        o_ref[:, sl] = y.astype(_BF16)
    elif seg.mode == "T":
        o_ref[sl, :] = y.T.astype(_BF16)
    elif seg.mode in ("heads_ones", "heads_shift"):
        lane = lax.broadcasted_iota(jnp.int32, y.shape, 1)
        if seg.mode == "heads_ones":
            fill = jnp.ones(y.shape, _F32)
        else:
            fill = jnp.where(lane == HEAD_DIM, -shift_ref[0], 0.0)
        for k, yk in enumerate((y, pltpu.roll(y, HEAD_DIM, axis=1))):
            o_ref[2 * c + k] = jnp.where(lane < HEAD_DIM, yk, fill).astype(_BF16)
    else:
        assert seg.mode in ("T_one", "T_ones")
        yt = y.T
        if seg.mode == "T_one":
            row = lax.broadcasted_iota(jnp.int32, (HEAD_DIM, rows), 0)
            extra = jnp.where(row == 0, 1.0, 0.0)
        else:
            extra = jnp.ones((HEAD_DIM, rows), _F32)
        for k in range(2):
            ext = jnp.concatenate([yt[k * HEAD_DIM:(k + 1) * HEAD_DIM], extra], axis=0)
            o_ref[(2 * c + k) * LANES:(2 * c + k + 1) * LANES, :] = ext.astype(_BF16)


def _combine_tile(x_ref, y_ref, route_ref):
    coef = route_ref[...].T
    y0 = _load_packed(y_ref[:, 0])
    y1 = _load_packed(y_ref[:, 1])
    return x_ref[...] + coef[:, 2:3] * y0 + coef[:, 3:4] * y1


def _proj_prep_kernel(*refs, segs, combine):
    it = iter(refs)
    shift_ref = next(it) if any(s.mode == "heads_shift" for s in segs) else None
    x_ref, g_ref, w_ref = next(it), next(it), next(it)
    y_ref, route_ref = (next(it), next(it)) if combine else (None, None)
    gain_refs = [next(it) if s.norm else None for s in segs]
    cos_ref, sin_ref = (next(it), next(it)) if any(s.rope for s in segs) else (None, None)
    outs = list(it)
    if combine:
        x = _combine_tile(x_ref, y_ref, route_ref)
        outs.pop(0)[...] = x
    else:
        x = x_ref[...]
    ms = jnp.mean(x * x, axis=-1, keepdims=True)
    h = (x * lax.rsqrt(ms + RMS_EPS) * g_ref[...]).astype(_BF16)
    proj = jnp.dot(h, w_ref[...], preferred_element_type=_F32)
    head_ones = (lax.broadcasted_iota(jnp.int32, (LANES, LANES), 0) // HEAD_DIM
                 == lax.broadcasted_iota(jnp.int32, (LANES, LANES), 1) // HEAD_DIM).astype(_BF16)
    for seg, gain_ref, o_ref in zip(segs, gain_refs, outs):
        for c in range(seg.ncols // LANES):
            y = proj[:, seg.col0 + c * LANES:seg.col0 + (c + 1) * LANES]
            _prep_slab(y, c, o_ref, seg, gain_ref, cos_ref, sin_ref, shift_ref, head_ones)


def _proj_prep(x2, bsz, seq, g, w, segs, gains, rope=None, shift=None, pending=None, tm=512):
    n, d = x2.shape
    p = w.shape[1]
    nt = seq // tm
    assert seq % tm == 0
    const = lambda shape: pl.BlockSpec(shape, lambda i: (0,) * len(shape))
    row = pl.BlockSpec((tm, d), lambda i: (i, 0))
    ins, in_specs = [], []
    if shift is not None:
        ins.append(shift.reshape(1).astype(_F32))
        in_specs.append(pl.BlockSpec(memory_space=pltpu.SMEM))
    ins += [x2, g.reshape(1, d), w]
    in_specs += [row, const((1, d)), const((d, p))]
    if pending is not None:
        ins += list(pending)
        in_specs += [pl.BlockSpec((2, 2, tm, PACK_W), lambda i: (0, 0, i, 0)),
                     pl.BlockSpec((8, tm), lambda i: (0, i))]
    for seg, gain in zip(segs, gains):
        if seg.norm:
            ins.append(jnp.tile(gain.astype(_F32), LANES // HEAD_DIM).reshape(1, LANES))
            in_specs.append(const((1, LANES)))
    if rope is not None:
        ins += list(rope)
        in_specs += [pl.BlockSpec((tm, LANES), lambda i: (i % nt, 0))] * 2
    out_shape, out_specs = [], []
    if pending is not None:
        out_shape.append(jax.ShapeDtypeStruct((n, d), _F32))
        out_specs.append(row)
    for seg in segs:
        nc = seg.ncols
        if seg.mode == "plain":
            out_shape.append(jax.ShapeDtypeStruct((bsz, seq, nc), _BF16))
            out_specs.append(pl.BlockSpec((None, tm, nc), lambda i: (i // nt, i % nt, 0)))
        elif seg.mode in ("T", "T_one", "T_ones"):
            wout = nc if seg.mode == "T" else 2 * nc
            out_shape.append(jax.ShapeDtypeStruct((bsz, wout, seq), _BF16))
            out_specs.append(pl.BlockSpec((None, wout, tm), lambda i: (i // nt, 0, i % nt)))
        else:
            assert seg.mode in ("heads_ones", "heads_shift")
            nh = nc // HEAD_DIM
            out_shape.append(jax.ShapeDtypeStruct((bsz, nh, seq, LANES), _BF16))
            out_specs.append(pl.BlockSpec((None, nh, tm, LANES), lambda i: (i // nt, 0, i % nt, 0)))
    return pl.pallas_call(
        functools.partial(_proj_prep_kernel, segs=tuple(segs), combine=pending is not None),
        grid=(n // tm,),
        in_specs=in_specs, out_specs=out_specs, out_shape=out_shape,
        compiler_params=_params("parallel"),
        name="proj_prep",
    )(*ins)


def _rope_tables(seq):
    t = np.arange(seq)
    row = (t // GRID_W).astype(np.float32)
    col = (t % GRID_W).astype(np.float32)
    inv_freq = jnp.asarray(ROPE_THETA, _F32) ** (-jnp.arange(0, ROPE_AXIS_DIM, 2, dtype=_F32) / ROPE_AXIS_DIM)
    ang_r = jnp.asarray(row)[:, None] * inv_freq[None, :]
    ang_c = jnp.asarray(col)[:, None] * inv_freq[None, :]
    cr, sr, cc, sc = jnp.cos(ang_r), jnp.sin(ang_r), jnp.cos(ang_c), jnp.sin(ang_c)
    cos = jnp.concatenate([cr, cr, cc, cc], axis=-1)
    sin = jnp.concatenate([-sr, sr, -sc, sc], axis=-1)
    return jnp.tile(cos, (1, 2)), jnp.tile(sin, (1, 2))


def _band_bias(hw, dist_scale, slopes, shift, stack):
    bq, kb = BAND_BQ, BAND_KB
    assert hw <= kb
    r = np.arange(bq)[:, None]
    c = np.arange(BAND_KW)[None, :]
    rel = np.abs(c - kb - r)
    dist = rel.astype(np.float32) * np.float32(dist_scale)
    ok = np.stack([(rel <= hw) & ~(first & (c < kb)) & ~(last & (c >= kb + bq))
                   for first, last in ((False, False), (True, False), (False, True), (True, True))])
    alibi = -(np.asarray(slopes, np.float32)[:, None, None] * dist[None]) * np.float32(LOG2E)
    bias = jnp.where(ok[:, None], jnp.asarray(alibi)[None] - shift, NEG_BIG)
    nh = len(slopes)
    return bias.reshape(4, nh // stack, stack * bq, BAND_KW)


def _edge_variant(i, nb):
    return jnp.where(i == 0, 1, 0) + jnp.where(i == nb - 1, 2, 0)


def _window_blocks(nb):
    per = BAND_BQ // BAND_KB
    last = nb * per - 1
    fns = [lambda i: jnp.maximum(per * i - 1, 0)]
    fns += [functools.partial(lambda j, i: per * i + j, j) for j in range(per)]
    fns += [lambda i: jnp.minimum(per * i + per, last)]
    return fns


def _dil_bias(slopes, shift):
    r = jnp.arange(DIL_BQ, dtype=jnp.int32)[:, None]
    u = jnp.arange(DIL_BQ + 4 * DIL_REACH, dtype=jnp.int32)[None, :]
    delta = u - 2 * DIL_REACH - r
    dist = jnp.abs(delta)
    mult = sum(((delta % d == 0) & (dist <= ((w // 2) // d) * d)).astype(_F32) for w, d in A_PATTERNS)
    alibi = -(jnp.asarray(slopes, _F32)[:, None, None] * dist.astype(_F32)[None]) * LOG2E
    return jnp.where((mult > 0)[None], jnp.log2(jnp.maximum(mult, 1.0))[None] + alibi - shift, NEG_BIG)


def _dil_attn_kernel(q_ref, kt_ref, v_ref, bias_ref, o_ref, *, online):
    bq = q_ref.shape[0]
    seq = kt_ref.shape[1]
    q0 = pl.program_id(2) * bq
    w0 = pl.multiple_of(jnp.clip(q0 - DIL_REACH, 0, seq - DIL_KW), bq)
    u0 = pl.multiple_of(w0 - q0 + 2 * DIL_REACH, bq)
    q = q_ref[...]
    outs = []
    for j in range(2):
        rows = slice(j * HEAD_DIM, (j + 1) * HEAD_DIM)
        scores = []
        for c in range(DIL_KW // DIL_CHUNK):
            ks = pl.ds(pl.multiple_of(w0 + c * DIL_CHUNK, bq), DIL_CHUNK)
            us = pl.ds(pl.multiple_of(u0 + c * DIL_CHUNK, bq), DIL_CHUNK)
            scores.append(jnp.dot(q[:, rows], kt_ref[rows, ks], preferred_element_type=_F32)
                          + bias_ref[j, :, us])
        if online:
            m = functools.reduce(jnp.maximum, [jnp.max(s, axis=-1, keepdims=True) for s in scores])
            scores = [s - m for s in scores]
        acc = jnp.zeros((bq, LANES), _F32)
        for c, s in enumerate(scores):
            ks = pl.ds(pl.multiple_of(w0 + c * DIL_CHUNK, bq), DIL_CHUNK)
            acc += jnp.dot(jnp.exp2(s).astype(_BF16), v_ref[j, ks, :], preferred_element_type=_F32)
        outs.append((acc / pltpu.roll(acc, HEAD_DIM, axis=1))[:, :HEAD_DIM])
    o_ref[...] = jnp.concatenate(outs, axis=1).astype(_BF16)


def _dil_attn(q, kt, v, bias, *, online):
    bsz, seq, w = q.shape
    bq = DIL_BQ
    assert seq >= DIL_KW and seq % bq == 0
    return pl.pallas_call(
        functools.partial(_dil_attn_kernel, online=online), grid=(A_HEADS // 2, bsz, seq // bq),
        in_specs=[pl.BlockSpec((None, bq, LANES), lambda p, b, i: (b, i, p)),
                  pl.BlockSpec((None, LANES, seq), lambda p, b, i: (b, p, 0)),
                  pl.BlockSpec((None, 2, seq, LANES), lambda p, b, i: (b, p, 0, 0)),
                  pl.BlockSpec((2, bq, bias.shape[-1]), lambda p, b, i: (p, 0, 0))],
        out_specs=pl.BlockSpec((None, bq, LANES), lambda p, b, i: (b, i, p)),
        out_shape=jax.ShapeDtypeStruct((bsz, seq, w), _BF16),
        compiler_params=_params("parallel", "parallel", "parallel"),
        name="dil_attn_online" if online else "dil_attn",
    )(q, kt, v, bias)


def _band_b_kernel(sink2_ref, sinkp_ref, q_ref, *refs, online):
    bq = q_ref.shape[0]
    g = pl.program_id(1)
    grp = B_HEADS // B_KV_HEADS
    npc = BAND_KW // BAND_KB
    k_refs, v_refs, (bias_ref, o_ref) = refs[:npc], refs[npc:2 * npc], refs[2 * npc:]
    kt = jnp.concatenate([r[...] for r in k_refs], axis=1)
    v = jnp.concatenate([r[...] for r in v_refs], axis=0)
    q = q_ref[...]
    q4 = jnp.concatenate([q[:, i * HEAD_DIM:(i + 1) * HEAD_DIM] for i in range(grp)], axis=0)
    s = jnp.dot(q4, kt, preferred_element_type=_F32) + bias_ref[...]
    if online:
        m = jnp.max(s, axis=-1, keepdims=True)
        s = s - m
    acc = jnp.dot(jnp.exp2(s).astype(_BF16), v, preferred_element_type=_F32)
    for i in range(grp):
        a = acc[i * bq:(i + 1) * bq]
        if online:
            mi = m[i * bq:(i + 1) * bq]
            sk = sink2_ref[g * grp + i]
            mm = jnp.maximum(mi, sk)
            a = a * jnp.exp2(mi - mm)
            o = a / (pltpu.roll(a, HEAD_DIM, axis=1) + jnp.exp2(sk - mm))
        else:
            o = a / (pltpu.roll(a, HEAD_DIM, axis=1) + sinkp_ref[g * grp + i])
        o_ref[:, i * HEAD_DIM:(i + 1) * HEAD_DIM] = o[:, :HEAD_DIM].astype(_BF16)


def _band_b(q, kt, v, bias, sink2, *, online):
    bsz, seq_len, w = q.shape
    bq, kb = BAND_BQ, BAND_KB
    nb = seq_len // bq
    grp = B_HEADS // B_KV_HEADS
    gw = w // B_KV_HEADS
    blocks = _window_blocks(nb)
    kspec = lambda f: pl.BlockSpec((None, HEAD_DIM, kb), lambda b, g, i: (b, g, f(i)))
    vspec = lambda f: pl.BlockSpec((None, None, kb, LANES), lambda b, g, i: (b, g, f(i), 0))
    qspec = pl.BlockSpec((None, bq, gw), lambda b, g, i: (b, i, g))
    smem = pl.BlockSpec(memory_space=pltpu.SMEM)
    return pl.pallas_call(
        functools.partial(_band_b_kernel, online=online), grid=(bsz, B_KV_HEADS, nb),
        in_specs=[smem, smem, qspec] + [kspec(f) for f in blocks] + [vspec(f) for f in blocks]
        + [pl.BlockSpec((None, None, grp * bq, BAND_KW),
                        lambda b, g, i: (_edge_variant(i, nb), g, 0, 0))],
        out_specs=qspec,
        out_shape=jax.ShapeDtypeStruct((bsz, seq_len, w), _BF16),
        compiler_params=_params("parallel", "parallel", "parallel"),
        name="band_b_online" if online else "band_b",
    )(sink2, jnp.exp2(sink2), q, *([kt] * len(blocks)), *([v] * len(blocks)), bias)


def _dense_attn_kernel(qt_ref, k_ref, vt_ref, o_ref, *, kch, online):
    bq = qt_ref.shape[1]
    seq = k_ref.shape[0]
    for c in range(C_HEADS // C_KV_HEADS):
        qt = qt_ref[c * LANES:(c + 1) * LANES, :]
        acc = jnp.zeros((LANES, bq), _F32)
        m = jnp.full((1, bq), NEG_BIG, _F32)
        for j in range(seq // kch):
            ks = slice(j * kch, (j + 1) * kch)
            st = jnp.dot(k_ref[ks, :], qt, preferred_element_type=_F32)
            if online:
                m_new = jnp.maximum(m, jnp.max(st, axis=0, keepdims=True))
                acc = acc * jnp.exp2(m - m_new)
                st = st - m_new
                m = m_new
            acc = acc + jnp.dot(vt_ref[:, ks], jnp.exp2(st).astype(_BF16),
                                preferred_element_type=_F32)
        o = acc[:HEAD_DIM] / acc[HEAD_DIM:]
        o_ref[:, c * HEAD_DIM:(c + 1) * HEAD_DIM] = o.T.astype(_BF16)


def _dense_attn(qt, k, vt, *, online, bq=DENSE_BQ, kch=DENSE_KCH):
    bsz, _, seq = qt.shape
    bq, kch = min(bq, seq), min(kch, seq)
    grp = C_HEADS // C_KV_HEADS
    return pl.pallas_call(
        functools.partial(_dense_attn_kernel, kch=kch, online=online),
        grid=(bsz, C_KV_HEADS, seq // bq),
        in_specs=[pl.BlockSpec((None, grp * LANES, bq), lambda b, g, i: (b, g, i)),
                  pl.BlockSpec((None, None, seq, LANES), lambda b, g, i: (b, g, 0, 0)),
                  pl.BlockSpec((None, LANES, seq), lambda b, g, i: (b, g, 0))],
        out_specs=pl.BlockSpec((None, bq, grp * HEAD_DIM), lambda b, g, i: (b, i, g)),
        out_shape=jax.ShapeDtypeStruct((bsz, seq, C_HEADS * HEAD_DIM), _BF16),
        compiler_params=_params("parallel", "parallel", "parallel"),
        name="dense_attn_online" if online else "dense_attn",
    )(qt, k, vt)


def _pack_pair(a, b):
    wa = lax.bitcast_convert_type(a.astype(_BF16).astype(_F32), jnp.uint32) >> 16
    wb = lax.bitcast_convert_type(b.astype(_BF16).astype(_F32), jnp.uint32) & jnp.uint32(0xFFFF0000)
    return lax.bitcast_convert_type(wa | wb, jnp.int32)


def _unpack_pair(w):
    u = lax.bitcast_convert_type(w, jnp.uint32)
    return (lax.bitcast_convert_type(u << 16, _F32),
            lax.bitcast_convert_type(u & jnp.uint32(0xFFFF0000), _F32))


def _store_packed(ref, y):
    q = PACK_W
    for j in range(2):
        ref[j] = _pack_pair(y[:, 2 * j * q:(2 * j + 1) * q], y[:, (2 * j + 1) * q:(2 * j + 2) * q])


def _load_packed(ref):
    parts = []
    for j in range(2):
        parts += list(_unpack_pair(ref[j]))
    return jnp.concatenate(parts, axis=1)


def _sc_mesh():
    return plsc.VectorSubcoreMesh(core_axis_name="core", subcore_axis_name="subcore")


def _sc_gather(table, idx):
    n = idx.shape[0]
    d = table.shape[1]

    @pl.kernel(out_type=jax.ShapeDtypeStruct((n, d), table.dtype), mesh=_sc_mesh())
    def gather(x_hbm, i_hbm, o_hbm):
        def body(i_vmem, o_vmem):
            pltpu.sync_copy(x_hbm.at[i_vmem.at[0]], o_vmem)

        pltpu.emit_pipeline(
            body, grid=(n // SC_WINDOW,),
            in_specs=[pl.BlockSpec((1, SC_WINDOW), index_map=lambda i: (0, i))],
            out_specs=[pl.BlockSpec((SC_WINDOW, d), index_map=lambda i: (i, 0))],
            core_axis_name=("core", "subcore"),
            dimension_semantics=(pltpu.PARALLEL,),
        )(i_hbm, o_hbm)

    return gather(table, idx.reshape(1, n))


def _sc_scatter(src, idx, n_out, reps):
    n = idx.shape[0]
    r2, d = src.shape
    nb = r2 // 2 // SC_WINDOW

    @pl.kernel(out_type=jax.ShapeDtypeStruct((n_out, d), src.dtype), mesh=_sc_mesh())
    def scatter(x_hbm, i_hbm, o_hbm):
        def body(x_vmem, i_vmem):
            pltpu.sync_copy(x_vmem, o_hbm.at[i_vmem.at[0]])

        pltpu.emit_pipeline(
            body, grid=(n // SC_WINDOW,),
            in_specs=[pl.BlockSpec((SC_WINDOW, d),
                                   index_map=lambda i: ((i // (reps * nb)) * nb + i % nb, 0)),
                      pl.BlockSpec((1, SC_WINDOW), index_map=lambda i: (0, i))],
            out_specs=[],
            core_axis_name=("core", "subcore"),
            dimension_semantics=(pltpu.PARALLEL,),
        )(x_hbm, i_hbm)

    return scatter(src, idx.reshape(1, n))


def _route_tile(x, rs, g_ref, w_ref, b_ref, h_ref, route_ref, cnt_ref, carry_ref):
    tm = x.shape[0]
    ms = jnp.mean(x * x, axis=-1, keepdims=True)
    h = x * lax.rsqrt(ms + RMS_EPS) * g_ref[...]
    _store_packed(h_ref.at[:, rs], h)
    h_hi = h.astype(_BF16)
    h_lo = (h - h_hi.astype(_F32)).astype(_BF16)
    hw = (jnp.dot(h_hi, w_ref[...], preferred_element_type=_F32)
          + jnp.dot(h_lo, w_ref[...], preferred_element_type=_F32))
    logits = hw[:, :LANES] + hw[:, LANES:] + b_ref[...]
    lt = logits.T[:ROUTE_ROWS]
    row = lax.broadcasted_iota(jnp.int32, (ROUTE_ROWS, tm), 0)

    lg = jnp.where(row < MOE_GROUPS, lt, NEG_BIG)
    mg = jnp.max(lg, axis=0, keepdims=True)
    zg = jnp.sum(jnp.exp(lg - mg), axis=0, keepdims=True)
    p_grp = 1.0 / zg
    g_idx = jnp.min(jnp.where(lg == mg, row, ROUTE_ROWS), axis=0, keepdims=True)

    e_row = row - ROUTE_E0
    emask = (e_row >= 0) & (e_row < N_EXPERTS) & ((e_row >> 3) == g_idx)
    le = jnp.where(emask, lt, NEG_BIG)
    m1 = jnp.max(le, axis=0, keepdims=True)
    i1 = jnp.min(jnp.where(le == m1, row, ROUTE_ROWS), axis=0, keepdims=True)
    le2 = jnp.where(row == i1, NEG_BIG, le)
    m2 = jnp.max(le2, axis=0, keepdims=True)
    i2 = jnp.min(jnp.where(le2 == m2, row, ROUTE_ROWS), axis=0, keepdims=True)
    e21 = jnp.exp(m2 - m1)
    c1 = p_grp / (1.0 + e21)
    c2 = p_grp * e21 / (1.0 + e21)

    onehot = jnp.where(row == i1, 1.0, jnp.where(row == i2, 1.0, 0.0)).astype(_BF16)
    upper = (lax.broadcasted_iota(jnp.int32, (tm, tm), 0)
             <= lax.broadcasted_iota(jnp.int32, (tm, tm), 1)).astype(_BF16)
    cum = jnp.dot(onehot, upper, preferred_element_type=_F32) + carry_ref[...]
    r1 = jnp.sum(jnp.where(row == i1, cum, 0.0), axis=0, keepdims=True) - 1.0
    r2 = jnp.sum(jnp.where(row == i2, cum, 0.0), axis=0, keepdims=True) - 1.0
    total = jnp.max(cum, axis=1, keepdims=True)
    carry_ref[...] = total
    cnt_ref[...] = jnp.broadcast_to(total, cnt_ref.shape)

    rows = ((i1 - ROUTE_E0).astype(_F32), (i2 - ROUTE_E0).astype(_F32), c1, c2, r1, r2)
    rrow = lax.broadcasted_iota(jnp.int32, (8, tm), 0)
    route = jnp.zeros((8, tm), _F32)
    for k, val in enumerate(rows):
        route = jnp.where(rrow == k, val, route)
    route_ref[:, rs] = route


def _out_router_kernel(*refs, n_in):
    x_ref = refs[0]
    o_refs = refs[1:1 + n_in]
    w_refs = refs[1 + n_in:1 + 2 * n_in]
    g_ref, wr_ref, br_ref, xnew_ref = refs[1 + 2 * n_in:5 + 2 * n_in]
    route_refs = refs[5 + 2 * n_in:]
    carry_ref = route_refs[-1]

    @pl.when(pl.program_id(0) == 0)
    def _():
        carry_ref[...] = jnp.zeros(carry_ref.shape, _F32)

    rows = x_ref.shape[0] // ROUTE_CHAINS
    for r in range(ROUTE_CHAINS):
        rs = slice(r * rows, (r + 1) * rows)
        acc = x_ref[rs]
        for o_ref, w_ref in zip(o_refs, w_refs):
            acc = acc + jnp.dot(o_ref[rs], w_ref[...], preferred_element_type=_F32)
        xnew_ref[rs] = acc
        _route_tile(acc, rs, g_ref, wr_ref, br_ref, *route_refs)


def _out_router(x2, os_, ws, g, w_router, b_router, tm=512):
    n, d = x2.shape
    w_hi = w_router.astype(_BF16)
    w_lo = (w_router - w_hi.astype(_F32)).astype(_BF16)
    const = lambda shape: pl.BlockSpec(shape, lambda i: (0,) * len(shape))
    row = pl.BlockSpec((tm, d), lambda i: (i, 0))
    in_specs = [row] + [pl.BlockSpec((tm, o.shape[1]), lambda i: (i, 0)) for o in os_]
    in_specs += [const(w.shape) for w in ws]
    in_specs += [const((1, d)), const((d, 2 * LANES)), const((1, LANES))]
    return pl.pallas_call(
        functools.partial(_out_router_kernel, n_in=len(os_)), grid=(n // tm,),
        in_specs=in_specs,
        out_specs=[row, pl.BlockSpec((2, tm, PACK_W), lambda i: (0, i, 0)),
                   pl.BlockSpec((8, tm), lambda i: (0, i)), const((ROUTE_ROWS, LANES))],
        out_shape=[jax.ShapeDtypeStruct((n, d), _F32),
                   jax.ShapeDtypeStruct((2, n, PACK_W), jnp.int32),
                   jax.ShapeDtypeStruct((8, n), _F32),
                   jax.ShapeDtypeStruct((ROUTE_ROWS, LANES), _F32)],
        scratch_shapes=[pltpu.VMEM((ROUTE_ROWS, 1), _F32)],
        compiler_params=_params("arbitrary"),
        name="out_router",
    )(x2, *os_, *ws, g.reshape(1, d), jnp.concatenate([w_hi, w_lo], axis=1), b_router)


def _expert_kernel(te_ref, nv_ref, xs_ref, wg_ref, wu_ref, wd_ref, ys_ref, wgu_s, wd_s):
    j = pl.program_id(0)
    prev = te_ref[jnp.maximum(j - 1, 0)]

    @pl.when((j == 0) | (te_ref[j] != prev))
    def _():
        wgu_s[:, :MOE_D_FF] = wg_ref[...].astype(_BF16)
        wgu_s[:, MOE_D_FF:] = wu_ref[...].astype(_BF16)
        wd_s[...] = wd_ref[...].astype(_BF16)

    @pl.when(j < nv_ref[0])
    def _():
        rows = xs_ref.shape[1] // MOE_CHAINS
        for r in range(MOE_CHAINS):
            rs = slice(r * rows, (r + 1) * rows)
            xs = _load_packed(xs_ref[:, rs]).astype(_BF16)
            au = jnp.dot(xs, wgu_s[...], preferred_element_type=_F32)
            a, u = au[:, :MOE_D_FF], au[:, MOE_D_FF:]
            act = (a * (1.0 / (1.0 + jnp.exp(-a))) * u).astype(_BF16)
            _store_packed(ys_ref.at[:, rs], jnp.dot(act, wd_s[...], preferred_element_type=_F32))


def _experts(tile_expert, n_valid, xs, w_gate, w_up, w_down, tm=MOE_TM):
    _, n_slots, pw = xs.shape
    n_tiles = n_slots // tm
    d, f = w_gate.shape[-2:]
    row = lambda j, te, nv: (0, jnp.maximum(jnp.minimum(j, nv[0] - 1), 0), 0)
    grid_spec = pltpu.PrefetchScalarGridSpec(
        num_scalar_prefetch=2, grid=(n_tiles,),
        in_specs=[pl.BlockSpec((2, tm, pw), row),
                  pl.BlockSpec((None, d, f), lambda j, te, nv: (te[j], 0, 0)),
                  pl.BlockSpec((None, d, f), lambda j, te, nv: (te[j], 0, 0)),
                  pl.BlockSpec((None, f, d), lambda j, te, nv: (te[j], 0, 0))],
        out_specs=pl.BlockSpec((2, tm, pw), row),
        scratch_shapes=[pltpu.VMEM((d, 2 * f), _BF16), pltpu.VMEM((f, d), _BF16)])
    return pl.pallas_call(
        _expert_kernel, grid_spec=grid_spec,
        out_shape=jax.ShapeDtypeStruct((2, n_slots, pw), jnp.int32),
        compiler_params=_params("arbitrary"),
        name="experts",
    )(tile_expert, n_valid, xs, w_gate, w_up, w_down)


def _combine_kernel(x_ref, y_ref, route_ref, o_ref):
    o_ref[...] = _combine_tile(x_ref, y_ref, route_ref)


def _combine(x2, y, route, tm=512):
    n, d = x2.shape
    row = pl.BlockSpec((tm, d), lambda i: (i, 0))
    return pl.pallas_call(
        _combine_kernel, grid=(n // tm,),
        in_specs=[row, pl.BlockSpec((2, 2, tm, PACK_W), lambda i: (0, 0, i, 0)),
                  pl.BlockSpec((8, tm), lambda i: (0, i))],
        out_specs=row,
        out_shape=jax.ShapeDtypeStruct((n, d), _F32),
        compiler_params=_params("parallel"),
        name="moe_combine",
    )(x2, y, route)


def _slot_kernel(offset_ref, route_ref, slot_ref, *, n_slots):
    expert = route_ref[0:2, :].astype(jnp.int32)
    pos = route_ref[4:6, :].astype(jnp.int32)
    for e in range(N_EXPERTS):
        pos = pos + jnp.where(expert == e, offset_ref[e], 0)
    slot_ref[0:2, :] = pos
    slot_ref[2:4, :] = pos + n_slots


def _slots(offset, route, n_slots):
    n = route.shape[1]
    return pl.pallas_call(
        functools.partial(_slot_kernel, n_slots=n_slots),
        in_specs=[pl.BlockSpec(memory_space=pltpu.SMEM),
                  pl.BlockSpec((8, n), lambda: (0, 0))],
        out_specs=pl.BlockSpec((4, n), lambda: (0, 0)),
        out_shape=jax.ShapeDtypeStruct((4, n), jnp.int32),
        name="moe_slots",
    )(offset, route)


def _router_params(wg, bg, we, be):
    d = wg.shape[0]
    w_router = jnp.zeros((d, LANES), _F32)
    w_router = w_router.at[:, :MOE_GROUPS].set(wg)
    w_router = w_router.at[:, ROUTE_E0:ROUTE_E0 + N_EXPERTS].set(
        jnp.moveaxis(we, 0, 1).reshape(d, N_EXPERTS))
    b_router = jnp.zeros((1, LANES), _F32)
    b_router = b_router.at[0, :MOE_GROUPS].set(bg)
    b_router = b_router.at[0, ROUTE_E0:ROUTE_E0 + N_EXPERTS].set(be.reshape(-1))
    return w_router, b_router


def _moe_experts(h, route, cnt, w_gate, w_up, w_down, layer):
    n = h.shape[1]
    d = w_gate.shape[-2]
    tm = MOE_TM
    counts = cnt[ROUTE_E0:ROUTE_E0 + N_EXPERTS, 0].astype(jnp.int32)
    tiles_per = (counts + tm - 1) // tm
    tiles_end = jnp.cumsum(tiles_per)
    offset = (tiles_end - tiles_per) * tm
    n_tiles = (2 * n) // tm + N_EXPERTS
    n_valid = tiles_end[-1]
    tile_ids = jnp.minimum(jnp.arange(n_tiles, dtype=jnp.int32), n_valid - 1)
    tile_expert = jnp.sum(tile_ids[:, None] >= tiles_end[None, :], axis=1).astype(jnp.int32)
    n_slots = n_tiles * tm
    slot = _slots(offset, route, n_slots).reshape(-1)

    xs = _sc_scatter(h.reshape(2 * n, PACK_W), slot, 2 * n_slots, reps=2)
    ys = _experts(tile_expert + layer * N_EXPERTS, n_valid.reshape(1),
                  xs.reshape(2, n_slots, PACK_W), w_gate.reshape(-1, d, MOE_D_FF),
                  w_up.reshape(-1, d, MOE_D_FF), w_down.reshape(-1, MOE_D_FF, d))
    y = _sc_gather(ys.reshape(2 * n_slots, PACK_W), slot)
    return y.reshape(2, 2, n, PACK_W)


def _proj_prep_pending(x2, pending, *args, **kwargs):
    outs = _proj_prep(x2, *args, pending=pending, **kwargs)
    return (x2, outs) if pending is None else (outs[0], outs[1:])


def _mixer_ab(x2, pending, router, bsz, seq, g, w_in, a_qn, a_kn, b_qn, b_kn, b_sink, w_out):
    aw = A_HEADS * HEAD_DIM
    bqw = B_HEADS * HEAD_DIM
    bkw = B_KV_HEADS * HEAD_DIM
    scale2 = QK_SCALE * LOG2E
    segs = [_Seg(0, aw, True, False, scale2, "plain"),
            _Seg(aw, aw, True, False, 1.0, "T"),
            _Seg(2 * aw, aw, False, False, 1.0, "heads_ones"),
            _Seg(3 * aw, bqw, True, False, scale2, "plain"),
            _Seg(3 * aw + bqw, bkw, True, False, 1.0, "T"),
            _Seg(3 * aw + bqw + bkw, bkw, False, False, 1.0, "heads_ones")]
    x2, (qa, ka, va, qb, kb, vb) = _proj_prep_pending(
        x2, pending, bsz, seq, g, w_in.astype(_BF16), segs, [a_qn, a_kn, None, b_qn, b_kn, None])

    bound_a = HEAD_DIM * scale2 * jnp.max(jnp.abs(a_qn)) * jnp.max(jnp.abs(a_kn))
    bound_b = HEAD_DIM * scale2 * jnp.max(jnp.abs(b_qn)) * jnp.max(jnp.abs(b_kn))
    static_ok = jnp.maximum(bound_a, bound_b) <= SHIFT_MAX
    shift_a = jnp.where(static_ok, bound_a, 0.0)
    shift_b = jnp.where(static_ok, bound_b, 0.0)
    slopes = _alibi_slopes(A_HEADS + B_HEADS)
    bias_a = _dil_bias(slopes[0::2], shift_a)
    bias_b = _band_bias(B_WINDOW, 1.0, slopes[1::2], shift_b, B_HEADS // B_KV_HEADS)
    sink2 = b_sink.astype(_F32) * LOG2E - shift_b

    def attend(online, qa, ka, va, qb, kb, vb, bias_a, bias_b, sink2):
        return (_dil_attn(qa, ka, va, bias_a, online=online),
                _band_b(qb, kb, vb, bias_b, sink2, online=online))

    oa, ob = lax.cond(static_ok, functools.partial(attend, False), functools.partial(attend, True),
                      qa, ka, va, qb, kb, vb, bias_a, bias_b, sink2)
    w_out = w_out.astype(_BF16)
    return _out_router(x2, [oa.reshape(-1, aw), ob.reshape(-1, bqw)], [w_out[:aw], w_out[aw:]],
                       *router)


def _mixer_c(x2, pending, router, bsz, seq, g, w_in, qn, kn, w_out):
    qw = C_HEADS * HEAD_DIM
    kvw = C_KV_HEADS * HEAD_DIM
    rope = _rope_tables(seq)
    bound = HEAD_DIM * QK_SCALE * LOG2E * jnp.max(jnp.abs(qn)) * jnp.max(jnp.abs(kn))
    static_ok = bound <= SHIFT_MAX
    shift = jnp.where(static_ok, bound, 0.0)
    segs = [_Seg(0, qw, True, True, QK_SCALE * LOG2E, "T_one"),
            _Seg(qw, kvw, True, True, 1.0, "heads_shift"),
            _Seg(qw + kvw, kvw, False, False, 1.0, "T_ones")]
    x2, (qt, k, vt) = _proj_prep_pending(x2, pending, bsz, seq, g, w_in.astype(_BF16), segs,
                                         [qn, kn, None], rope=rope, shift=shift)
    o = lax.cond(static_ok,
                 functools.partial(_dense_attn, online=False),
                 functools.partial(_dense_attn, online=True), qt, k, vt)
    return _out_router(x2, [o.reshape(-1, qw)], [w_out.astype(_BF16)], *router)


def kernel(x, mix_norm, ffn_norm, ab_w_in, a_q_norm, a_k_norm, b_q_norm, b_k_norm, b_sink, ab_w_out,
           c_w_in, c_q_norm, c_k_norm, c_w_out, moe_group_w, moe_group_b, moe_expert_w, moe_expert_b,
           moe_w_gate, moe_w_up, moe_w_down):
    bsz, seq, d = x.shape
    x2 = x.reshape(bsz * seq, d)
    depth = mix_norm.shape[0]
    pending = None
    for layer in range(depth):
        i = layer // 2
        router = (ffn_norm[layer],) + _router_params(moe_group_w[layer], moe_group_b[layer],
                                                     moe_expert_w[layer], moe_expert_b[layer])
        if layer % 2 == 0:
            x2, h, route, cnt = _mixer_ab(x2, pending, router, bsz, seq, mix_norm[layer], ab_w_in[i],
                                          a_q_norm[i], a_k_norm[i], b_q_norm[i], b_k_norm[i],
                                          b_sink[i], ab_w_out[i])
        else:
            x2, h, route, cnt = _mixer_c(x2, pending, router, bsz, seq, mix_norm[layer], c_w_in[i],
                                         c_q_norm[i], c_k_norm[i], c_w_out[i])
        pending = (_moe_experts(h, route, cnt, moe_w_gate, moe_w_up, moe_w_down, layer), route)
    return _combine(x2, *pending).reshape(bsz, seq, d)
```

```python
import functools
import math
from typing import NamedTuple

import jax
import jax.numpy as jnp
import numpy as np
from jax import lax
from jax.experimental import pallas as pl
from jax.experimental.pallas import tpu as pltpu
from jax.experimental.pallas import tpu_sc as plsc

HEAD_DIM = 64
LANES = 128
N_HEADS = 16
A_HEADS = 8
B_HEADS = 8
B_KV_HEADS = 2
C_HEADS = 16
C_KV_HEADS = 4
A_PATTERNS = ((128, 1), (512, 4), (2048, 16))
B_WINDOW = 128
GRID_W = 64
ROPE_THETA = 10000.0
ROPE_AXIS_DIM = HEAD_DIM // 2
ALIBI_MAX_BIAS = 8.0
RMS_EPS = 1e-6
MOE_GROUPS = 4
MOE_EXPERTS = 8
N_EXPERTS = MOE_GROUPS * MOE_EXPERTS
MOE_D_FF = 256
QK_SCALE = HEAD_DIM ** -0.5
LOG2E = math.log2(math.e)
SHIFT_MAX = 60.0
NEG_BIG = -1e30
VMEM_LIMIT = 52 * 1024 * 1024

BAND_BQ = 256
BAND_KB = 128
BAND_KW = BAND_BQ + 2 * BAND_KB
DIL_BQ = 256
DIL_REACH = max(w // 2 for w, _ in A_PATTERNS)
DIL_KW = DIL_BQ + 2 * DIL_REACH
DIL_CHUNK = 768
DENSE_BQ = 256
DENSE_KCH = 2048
MOE_TM = 512
MOE_CHAINS = 1
ROUTE_E0 = 4
ROUTE_CHAINS = 1
ROUTE_ROWS = 48
PACK_W = 256
SC_WINDOW = 128

_BF16 = jnp.bfloat16
_F32 = jnp.float32


def _params(*sem):
    return pltpu.CompilerParams(dimension_semantics=sem, vmem_limit_bytes=VMEM_LIMIT)


def _alibi_slopes(n):
    return np.asarray(2.0 ** (-ALIBI_MAX_BIAS * np.arange(1, n + 1) / n), dtype=np.float32)


def _head_norm(y, gain, head_ones):
    y2 = y * y
    hi = y2.astype(_BF16)
    lo = (y2 - hi.astype(_F32)).astype(_BF16)
    ss = (jnp.dot(hi, head_ones, preferred_element_type=_F32)
          + jnp.dot(lo, head_ones, preferred_element_type=_F32))
    return y * lax.rsqrt(ss * (1.0 / HEAD_DIM) + RMS_EPS) * gain


def _rope(y, cos, sin):
    lane = lax.broadcasted_iota(jnp.int32, y.shape, 1)
    first = (lane % 32) < 16
    partner = jnp.where(first, pltpu.roll(y, LANES - 16, axis=1), pltpu.roll(y, 16, axis=1))
    return y * cos + partner * sin


class _Seg(NamedTuple):
    col0: int
    ncols: int
    norm: bool
    rope: bool
    scale: float
    mode: str


def _prep_slab(y, c, o_ref, seg, gain_ref, cos_ref, sin_ref, shift_ref, head_ones):
    rows = y.shape[0]
    if seg.norm:
        y = _head_norm(y, gain_ref[...], head_ones)
    if seg.rope:
        y = _rope(y, cos_ref[...], sin_ref[...])
    if seg.scale != 1.0:
        y = y * seg.scale
    sl = slice(c * LANES, (c + 1) * LANES)
    if seg.mode == "plain":
        o_ref[:, sl] = y.astype(_BF16)
    elif seg.mode == "T":
        o_ref[sl, :] = y.T.astype(_BF16)
    elif seg.mode in ("heads_ones", "heads_shift"):
        lane = lax.broadcasted_iota(jnp.int32, y.shape, 1)
        if seg.mode == "heads_ones":
            fill = jnp.ones(y.shape, _F32)
        else:
            fill = jnp.where(lane == HEAD_DIM, -shift_ref[0], 0.0)
        for k, yk in enumerate((y, pltpu.roll(y, HEAD_DIM, axis=1))):
            o_ref[2 * c + k] = jnp.where(lane < HEAD_DIM, yk, fill).astype(_BF16)
    else:
        assert seg.mode in ("T_one", "T_ones")
        yt = y.T
        if seg.mode == "T_one":
            row = lax.broadcasted_iota(jnp.int32, (HEAD_DIM, rows), 0)
            extra = jnp.where(row == 0, 1.0, 0.0)
        else:
            extra = jnp.ones((HEAD_DIM, rows), _F32)
        for k in range(2):
            ext = jnp.concatenate([yt[k * HEAD_DIM:(k + 1) * HEAD_DIM], extra], axis=0)
            o_ref[(2 * c + k) * LANES:(2 * c + k + 1) * LANES, :] = ext.astype(_BF16)


def _combine_tile(x_ref, y_ref, route_ref):
    coef = route_ref[...].T
    y0 = _load_packed(y_ref[:, 0])
    y1 = _load_packed(y_ref[:, 1])
    return x_ref[...] + coef[:, 2:3] * y0 + coef[:, 3:4] * y1


def _proj_prep_kernel(*refs, segs, combine):
    it = iter(refs)
    shift_ref = next(it) if any(s.mode == "heads_shift" for s in segs) else None
    x_ref, g_ref, w_ref = next(it), next(it), next(it)
    y_ref, route_ref = (next(it), next(it)) if combine else (None, None)
    gain_refs = [next(it) if s.norm else None for s in segs]
    cos_ref, sin_ref = (next(it), next(it)) if any(s.rope for s in segs) else (None, None)
    outs = list(it)
    if combine:
        x = _combine_tile(x_ref, y_ref, route_ref)
        outs.pop(0)[...] = x
    else:
        x = x_ref[...]
    ms = jnp.mean(x * x, axis=-1, keepdims=True)
    h = (x * lax.rsqrt(ms + RMS_EPS) * g_ref[...]).astype(_BF16)
    proj = jnp.dot(h, w_ref[...], preferred_element_type=_F32)
    head_ones = (lax.broadcasted_iota(jnp.int32, (LANES, LANES), 0) // HEAD_DIM
                 == lax.broadcasted_iota(jnp.int32, (LANES, LANES), 1) // HEAD_DIM).astype(_BF16)
    for seg, gain_ref, o_ref in zip(segs, gain_refs, outs):
        for c in range(seg.ncols // LANES):
            y = proj[:, seg.col0 + c * LANES:seg.col0 + (c + 1) * LANES]
            _prep_slab(y, c, o_ref, seg, gain_ref, cos_ref, sin_ref, shift_ref, head_ones)


def _proj_prep(x2, bsz, seq, g, w, segs, gains, rope=None, shift=None, pending=None, tm=512):
    n, d = x2.shape
    p = w.shape[1]
    nt = seq // tm
    assert seq % tm == 0
    const = lambda shape: pl.BlockSpec(shape, lambda i: (0,) * len(shape))
    row = pl.BlockSpec((tm, d), lambda i: (i, 0))
    ins, in_specs = [], []
    if shift is not None:
        ins.append(shift.reshape(1).astype(_F32))
        in_specs.append(pl.BlockSpec(memory_space=pltpu.SMEM))
    ins += [x2, g.reshape(1, d), w]
    in_specs += [row, const((1, d)), const((d, p))]
    if pending is not None:
        ins += list(pending)
        in_specs += [pl.BlockSpec((2, 2, tm, PACK_W), lambda i: (0, 0, i, 0)),
                     pl.BlockSpec((8, tm), lambda i: (0, i))]
    for seg, gain in zip(segs, gains):
        if seg.norm:
            ins.append(jnp.tile(gain.astype(_F32), LANES // HEAD_DIM).reshape(1, LANES))
            in_specs.append(const((1, LANES)))
    if rope is not None:
        ins += list(rope)
        in_specs += [pl.BlockSpec((tm, LANES), lambda i: (i % nt, 0))] * 2
    out_shape, out_specs = [], []
    if pending is not None:
        out_shape.append(jax.ShapeDtypeStruct((n, d), _F32))
        out_specs.append(row)
    for seg in segs:
        nc = seg.ncols
        if seg.mode == "plain":
            out_shape.append(jax.ShapeDtypeStruct((bsz, seq, nc), _BF16))
            out_specs.append(pl.BlockSpec((None, tm, nc), lambda i: (i // nt, i % nt, 0)))
        elif seg.mode in ("T", "T_one", "T_ones"):
            wout = nc if seg.mode == "T" else 2 * nc
            out_shape.append(jax.ShapeDtypeStruct((bsz, wout, seq), _BF16))
            out_specs.append(pl.BlockSpec((None, wout, tm), lambda i: (i // nt, 0, i % nt)))
        else:
            assert seg.mode in ("heads_ones", "heads_shift")
            nh = nc // HEAD_DIM
            out_shape.append(jax.ShapeDtypeStruct((bsz, nh, seq, LANES), _BF16))
            out_specs.append(pl.BlockSpec((None, nh, tm, LANES), lambda i: (i // nt, 0, i % nt, 0)))
    return pl.pallas_call(
        functools.partial(_proj_prep_kernel, segs=tuple(segs), combine=pending is not None),
        grid=(n // tm,),
        in_specs=in_specs, out_specs=out_specs, out_shape=out_shape,
        compiler_params=_params("parallel"),
        name="proj_prep",
    )(*ins)


def _rope_tables(seq):
    t = np.arange(seq)
    row = (t // GRID_W).astype(np.float32)
    col = (t % GRID_W).astype(np.float32)
    inv_freq = np.float32(ROPE_THETA) ** (-np.arange(0, ROPE_AXIS_DIM, 2, dtype=np.float32)
                                          / np.float32(ROPE_AXIS_DIM))
    ang_r = row[:, None] * inv_freq[None, :]
    ang_c = col[:, None] * inv_freq[None, :]
    cr, sr, cc, sc = np.cos(ang_r), np.sin(ang_r), np.cos(ang_c), np.sin(ang_c)
    cos = np.concatenate([cr, cr, cc, cc], axis=-1)
    sin = np.concatenate([-sr, sr, -sc, sc], axis=-1)
    return (jnp.asarray(np.tile(cos, (1, 2)), _F32), jnp.asarray(np.tile(sin, (1, 2)), _F32))


def _band_bias(hw, dist_scale, slopes, shift, stack):
    bq, kb = BAND_BQ, BAND_KB
    assert hw <= kb
    r = np.arange(bq)[:, None]
    c = np.arange(BAND_KW)[None, :]
    rel = np.abs(c - kb - r)
    dist = rel.astype(np.float32) * np.float32(dist_scale)
    ok = np.stack([(rel <= hw) & ~(first & (c < kb)) & ~(last & (c >= kb + bq))
                   for first, last in ((False, False), (True, False), (False, True), (True, True))])
    alibi = -(np.asarray(slopes, np.float32)[:, None, None] * dist[None]) * np.float32(LOG2E)
    bias = jnp.where(ok[:, None], jnp.asarray(alibi)[None] - shift, NEG_BIG)
    nh = len(slopes)
    return bias.reshape(4, nh // stack, stack * bq, BAND_KW)


def _edge_variant(i, nb):
    return jnp.where(i == 0, 1, 0) + jnp.where(i == nb - 1, 2, 0)


def _window_blocks(nb):
    per = BAND_BQ // BAND_KB
    last = nb * per - 1
    fns = [lambda i: jnp.maximum(per * i - 1, 0)]
    fns += [functools.partial(lambda j, i: per * i + j, j) for j in range(per)]
    fns += [lambda i: jnp.minimum(per * i + per, last)]
    return fns


def _dil_bias(slopes, shift):
    r = jnp.arange(DIL_BQ, dtype=jnp.int32)[:, None]
    u = jnp.arange(DIL_BQ + 4 * DIL_REACH, dtype=jnp.int32)[None, :]
    delta = u - 2 * DIL_REACH - r
    dist = jnp.abs(delta)
    mult = sum(((delta % d == 0) & (dist <= ((w // 2) // d) * d)).astype(_F32) for w, d in A_PATTERNS)
    alibi = -(jnp.asarray(slopes, _F32)[:, None, None] * dist.astype(_F32)[None]) * LOG2E
    return jnp.where((mult > 0)[None], jnp.log2(jnp.maximum(mult, 1.0))[None] + alibi - shift, NEG_BIG)


def _dil_attn_kernel(q_ref, kt_ref, v_ref, bias_ref, o_ref, *, online):
    bq = q_ref.shape[0]
    seq = kt_ref.shape[1]
    q0 = pl.program_id(2) * bq
    w0 = pl.multiple_of(jnp.clip(q0 - DIL_REACH, 0, seq - DIL_KW), bq)
    u0 = pl.multiple_of(w0 - q0 + 2 * DIL_REACH, bq)
    q = q_ref[...]
    outs = []
    for j in range(2):
        rows = slice(j * HEAD_DIM, (j + 1) * HEAD_DIM)
        scores = []
        for c in range(DIL_KW // DIL_CHUNK):
            ks = pl.ds(pl.multiple_of(w0 + c * DIL_CHUNK, bq), DIL_CHUNK)
            us = pl.ds(pl.multiple_of(u0 + c * DIL_CHUNK, bq), DIL_CHUNK)
            scores.append(jnp.dot(q[:, rows], kt_ref[rows, ks], preferred_element_type=_F32)
                          + bias_ref[j, :, us])
        if online:
            m = functools.reduce(jnp.maximum, [jnp.max(s, axis=-1, keepdims=True) for s in scores])
            scores = [s - m for s in scores]
        acc = jnp.zeros((bq, LANES), _F32)
        for c, s in enumerate(scores):
            ks = pl.ds(pl.multiple_of(w0 + c * DIL_CHUNK, bq), DIL_CHUNK)
            acc += jnp.dot(jnp.exp2(s).astype(_BF16), v_ref[j, ks, :], preferred_element_type=_F32)
        outs.append((acc / pltpu.roll(acc, HEAD_DIM, axis=1))[:, :HEAD_DIM])
    o_ref[...] = jnp.concatenate(outs, axis=1).astype(_BF16)


def _dil_attn(q, kt, v, bias, *, online):
    bsz, seq, w = q.shape
    bq = DIL_BQ
    assert seq >= DIL_KW and seq % bq == 0
    return pl.pallas_call(
        functools.partial(_dil_attn_kernel, online=online), grid=(A_HEADS // 2, bsz, seq // bq),
        in_specs=[pl.BlockSpec((None, bq, LANES), lambda p, b, i: (b, i, p)),
                  pl.BlockSpec((None, LANES, seq), lambda p, b, i: (b, p, 0)),
                  pl.BlockSpec((None, 2, seq, LANES), lambda p, b, i: (b, p, 0, 0)),
                  pl.BlockSpec((2, bq, bias.shape[-1]), lambda p, b, i: (p, 0, 0))],
        out_specs=pl.BlockSpec((None, bq, LANES), lambda p, b, i: (b, i, p)),
        out_shape=jax.ShapeDtypeStruct((bsz, seq, w), _BF16),
        compiler_params=_params("parallel", "parallel", "parallel"),
        name="dil_attn_online" if online else "dil_attn",
    )(q, kt, v, bias)


def _band_b_kernel(sink2_ref, sinkp_ref, q_ref, *refs, online):
    bq = q_ref.shape[0]
    g = pl.program_id(1)
    grp = B_HEADS // B_KV_HEADS
    npc = BAND_KW // BAND_KB
    k_refs, v_refs, (bias_ref, o_ref) = refs[:npc], refs[npc:2 * npc], refs[2 * npc:]
    kt = jnp.concatenate([r[...] for r in k_refs], axis=1)
    v = jnp.concatenate([r[...] for r in v_refs], axis=0)
    q = q_ref[...]
    q4 = jnp.concatenate([q[:, i * HEAD_DIM:(i + 1) * HEAD_DIM] for i in range(grp)], axis=0)
    s = jnp.dot(q4, kt, preferred_element_type=_F32) + bias_ref[...]
    if online:
        m = jnp.max(s, axis=-1, keepdims=True)
        s = s - m
    acc = jnp.dot(jnp.exp2(s).astype(_BF16), v, preferred_element_type=_F32)
    for i in range(grp):
        a = acc[i * bq:(i + 1) * bq]
        if online:
            mi = m[i * bq:(i + 1) * bq]
            sk = sink2_ref[g * grp + i]
            mm = jnp.maximum(mi, sk)
            a = a * jnp.exp2(mi - mm)
            o = a / (pltpu.roll(a, HEAD_DIM, axis=1) + jnp.exp2(sk - mm))
        else:
            o = a / (pltpu.roll(a, HEAD_DIM, axis=1) + sinkp_ref[g * grp + i])
        o_ref[:, i * HEAD_DIM:(i + 1) * HEAD_DIM] = o[:, :HEAD_DIM].astype(_BF16)


def _band_b(q, kt, v, bias, sink2, *, online):
    bsz, seq_len, w = q.shape
    bq, kb = BAND_BQ, BAND_KB
    nb = seq_len // bq
    grp = B_HEADS // B_KV_HEADS
    gw = w // B_KV_HEADS
    blocks = _window_blocks(nb)
    kspec = lambda f: pl.BlockSpec((None, HEAD_DIM, kb), lambda b, g, i: (b, g, f(i)))
    vspec = lambda f: pl.BlockSpec((None, None, kb, LANES), lambda b, g, i: (b, g, f(i), 0))
    qspec = pl.BlockSpec((None, bq, gw), lambda b, g, i: (b, i, g))
    smem = pl.BlockSpec(memory_space=pltpu.SMEM)
    return pl.pallas_call(
        functools.partial(_band_b_kernel, online=online), grid=(bsz, B_KV_HEADS, nb),
        in_specs=[smem, smem, qspec] + [kspec(f) for f in blocks] + [vspec(f) for f in blocks]
        + [pl.BlockSpec((None, None, grp * bq, BAND_KW),
                        lambda b, g, i: (_edge_variant(i, nb), g, 0, 0))],
        out_specs=qspec,
        out_shape=jax.ShapeDtypeStruct((bsz, seq_len, w), _BF16),
        compiler_params=_params("parallel", "parallel", "parallel"),
        name="band_b_online" if online else "band_b",
    )(sink2, jnp.exp2(sink2), q, *([kt] * len(blocks)), *([v] * len(blocks)), bias)


def _dense_attn_kernel(qt_ref, k_ref, vt_ref, o_ref, *, kch, online):
    bq = qt_ref.shape[1]
    seq = k_ref.shape[0]
    for c in range(C_HEADS // C_KV_HEADS):
        qt = qt_ref[c * LANES:(c + 1) * LANES, :]
        acc = jnp.zeros((LANES, bq), _F32)
        m = jnp.full((1, bq), NEG_BIG, _F32)
        for j in range(seq // kch):
            ks = slice(j * kch, (j + 1) * kch)
            st = jnp.dot(k_ref[ks, :], qt, preferred_element_type=_F32)
            if online:
                m_new = jnp.maximum(m, jnp.max(st, axis=0, keepdims=True))
                acc = acc * jnp.exp2(m - m_new)
                st = st - m_new
                m = m_new
            acc = acc + jnp.dot(vt_ref[:, ks], jnp.exp2(st).astype(_BF16),
                                preferred_element_type=_F32)
        o = acc[:HEAD_DIM] / acc[HEAD_DIM:]
        o_ref[:, c * HEAD_DIM:(c + 1) * HEAD_DIM] = o.T.astype(_BF16)


def _dense_attn(qt, k, vt, *, online, bq=DENSE_BQ, kch=DENSE_KCH):
    bsz, _, seq = qt.shape
    bq, kch = min(bq, seq), min(kch, seq)
    grp = C_HEADS // C_KV_HEADS
    return pl.pallas_call(
        functools.partial(_dense_attn_kernel, kch=kch, online=online),
        grid=(bsz, C_KV_HEADS, seq // bq),
        in_specs=[pl.BlockSpec((None, grp * LANES, bq), lambda b, g, i: (b, g, i)),
                  pl.BlockSpec((None, None, seq, LANES), lambda b, g, i: (b, g, 0, 0)),
                  pl.BlockSpec((None, LANES, seq), lambda b, g, i: (b, g, 0))],
        out_specs=pl.BlockSpec((None, bq, grp * HEAD_DIM), lambda b, g, i: (b, i, g)),
        out_shape=jax.ShapeDtypeStruct((bsz, seq, C_HEADS * HEAD_DIM), _BF16),
        compiler_params=_params("parallel", "parallel", "parallel"),
        name="dense_attn_online" if online else "dense_attn",
    )(qt, k, vt)


def _pack_pair(a, b):
    wa = lax.bitcast_convert_type(a.astype(_BF16).astype(_F32), jnp.uint32) >> 16
    wb = lax.bitcast_convert_type(b.astype(_BF16).astype(_F32), jnp.uint32) & jnp.uint32(0xFFFF0000)
    return lax.bitcast_convert_type(wa | wb, jnp.int32)


def _unpack_pair(w):
    u = lax.bitcast_convert_type(w, jnp.uint32)
    return (lax.bitcast_convert_type(u << 16, _F32),
            lax.bitcast_convert_type(u & jnp.uint32(0xFFFF0000), _F32))


def _store_packed(ref, y):
    q = PACK_W
    for j in range(2):
        ref[j] = _pack_pair(y[:, 2 * j * q:(2 * j + 1) * q], y[:, (2 * j + 1) * q:(2 * j + 2) * q])


def _load_packed(ref):
    parts = []
    for j in range(2):
        parts += list(_unpack_pair(ref[j]))
    return jnp.concatenate(parts, axis=1)


def _sc_mesh():
    return plsc.VectorSubcoreMesh(core_axis_name="core", subcore_axis_name="subcore")


def _sc_gather(table, idx):
    n = idx.shape[0]
    d = table.shape[1]

    @pl.kernel(out_type=jax.ShapeDtypeStruct((n, d), table.dtype), mesh=_sc_mesh())
    def gather(x_hbm, i_hbm, o_hbm):
        def body(i_vmem, o_vmem):
            pltpu.sync_copy(x_hbm.at[i_vmem.at[0]], o_vmem)

        pltpu.emit_pipeline(
            body, grid=(n // SC_WINDOW,),
            in_specs=[pl.BlockSpec((1, SC_WINDOW), index_map=lambda i: (0, i))],
            out_specs=[pl.BlockSpec((SC_WINDOW, d), index_map=lambda i: (i, 0))],
            core_axis_name=("core", "subcore"),
            dimension_semantics=(pltpu.PARALLEL,),
        )(i_hbm, o_hbm)

    return gather(table, idx.reshape(1, n))


def _sc_scatter(src, idx, n_out, reps):
    n = idx.shape[0]
    r2, d = src.shape
    nb = r2 // 2 // SC_WINDOW

    @pl.kernel(out_type=jax.ShapeDtypeStruct((n_out, d), src.dtype), mesh=_sc_mesh())
    def scatter(x_hbm, i_hbm, o_hbm):
        def body(x_vmem, i_vmem):
            pltpu.sync_copy(x_vmem, o_hbm.at[i_vmem.at[0]])

        pltpu.emit_pipeline(
            body, grid=(n // SC_WINDOW,),
            in_specs=[pl.BlockSpec((SC_WINDOW, d),
                                   index_map=lambda i: ((i // (reps * nb)) * nb + i % nb, 0)),
                      pl.BlockSpec((1, SC_WINDOW), index_map=lambda i: (0, i))],
            out_specs=[],
            core_axis_name=("core", "subcore"),
            dimension_semantics=(pltpu.PARALLEL,),
        )(x_hbm, i_hbm)

    return scatter(src, idx.reshape(1, n))


def _route_tile(x, rs, g_ref, w_ref, b_ref, h_ref, route_ref, cnt_ref, carry_ref):
    tm = x.shape[0]
    ms = jnp.mean(x * x, axis=-1, keepdims=True)
    h = x * lax.rsqrt(ms + RMS_EPS) * g_ref[...]
    _store_packed(h_ref.at[:, rs], h)
    h_hi = h.astype(_BF16)
    h_lo = (h - h_hi.astype(_F32)).astype(_BF16)
    hw = (jnp.dot(h_hi, w_ref[...], preferred_element_type=_F32)
          + jnp.dot(h_lo, w_ref[...], preferred_element_type=_F32))
    logits = hw[:, :LANES] + hw[:, LANES:] + b_ref[...]
    lt = logits.T[:ROUTE_ROWS]
    row = lax.broadcasted_iota(jnp.int32, (ROUTE_ROWS, tm), 0)

    lg = jnp.where(row < MOE_GROUPS, lt, NEG_BIG)
    mg = jnp.max(lg, axis=0, keepdims=True)
    zg = jnp.sum(jnp.exp(lg - mg), axis=0, keepdims=True)
    p_grp = 1.0 / zg
    g_idx = jnp.min(jnp.where(lg == mg, row, ROUTE_ROWS), axis=0, keepdims=True)

    e_row = row - ROUTE_E0
    emask = (e_row >= 0) & (e_row < N_EXPERTS) & ((e_row >> 3) == g_idx)
    le = jnp.where(emask, lt, NEG_BIG)
    m1 = jnp.max(le, axis=0, keepdims=True)
    i1 = jnp.min(jnp.where(le == m1, row, ROUTE_ROWS), axis=0, keepdims=True)
    le2 = jnp.where(row == i1, NEG_BIG, le)
    m2 = jnp.max(le2, axis=0, keepdims=True)
    i2 = jnp.min(jnp.where(le2 == m2, row, ROUTE_ROWS), axis=0, keepdims=True)
    e21 = jnp.exp(m2 - m1)
    c1 = p_grp / (1.0 + e21)
    c2 = p_grp * e21 / (1.0 + e21)

    onehot = jnp.where(row == i1, 1.0, jnp.where(row == i2, 1.0, 0.0)).astype(_BF16)
    upper = (lax.broadcasted_iota(jnp.int32, (tm, tm), 0)
             <= lax.broadcasted_iota(jnp.int32, (tm, tm), 1)).astype(_BF16)
    cum = jnp.dot(onehot, upper, preferred_element_type=_F32) + carry_ref[...]
    r1 = jnp.sum(jnp.where(row == i1, cum, 0.0), axis=0, keepdims=True) - 1.0
    r2 = jnp.sum(jnp.where(row == i2, cum, 0.0), axis=0, keepdims=True) - 1.0
    total = jnp.max(cum, axis=1, keepdims=True)
    carry_ref[...] = total
    cnt_ref[...] = jnp.broadcast_to(total, cnt_ref.shape)

    rows = ((i1 - ROUTE_E0).astype(_F32), (i2 - ROUTE_E0).astype(_F32), c1, c2, r1, r2)
    rrow = lax.broadcasted_iota(jnp.int32, (8, tm), 0)
    route = jnp.zeros((8, tm), _F32)
    for k, val in enumerate(rows):
        route = jnp.where(rrow == k, val, route)
    route_ref[:, rs] = route


def _out_router_kernel(*refs, n_in):
    x_ref = refs[0]
    o_refs = refs[1:1 + n_in]
    w_refs = refs[1 + n_in:1 + 2 * n_in]
    g_ref, wr_ref, br_ref, xnew_ref = refs[1 + 2 * n_in:5 + 2 * n_in]
    route_refs = refs[5 + 2 * n_in:]
    carry_ref = route_refs[-1]

    @pl.when(pl.program_id(0) == 0)
    def _():
        carry_ref[...] = jnp.zeros(carry_ref.shape, _F32)

    rows = x_ref.shape[0] // ROUTE_CHAINS
    for r in range(ROUTE_CHAINS):
        rs = slice(r * rows, (r + 1) * rows)
        acc = x_ref[rs]
        for o_ref, w_ref in zip(o_refs, w_refs):
            acc = acc + jnp.dot(o_ref[rs], w_ref[...], preferred_element_type=_F32)
        xnew_ref[rs] = acc
        _route_tile(acc, rs, g_ref, wr_ref, br_ref, *route_refs)


def _out_router(x2, os_, ws, g, w_router, b_router, tm=1024):
    n, d = x2.shape
    w_hi = w_router.astype(_BF16)
    w_lo = (w_router - w_hi.astype(_F32)).astype(_BF16)
    const = lambda shape: pl.BlockSpec(shape, lambda i: (0,) * len(shape))
    row = pl.BlockSpec((tm, d), lambda i: (i, 0))
    in_specs = [row] + [pl.BlockSpec((tm, o.shape[1]), lambda i: (i, 0)) for o in os_]
    in_specs += [const(w.shape) for w in ws]
    in_specs += [const((1, d)), const((d, 2 * LANES)), const((1, LANES))]
    return pl.pallas_call(
        functools.partial(_out_router_kernel, n_in=len(os_)), grid=(n // tm,),
        in_specs=in_specs,
        out_specs=[row, pl.BlockSpec((2, tm, PACK_W), lambda i: (0, i, 0)),
                   pl.BlockSpec((8, tm), lambda i: (0, i)), const((ROUTE_ROWS, LANES))],
        out_shape=[jax.ShapeDtypeStruct((n, d), _F32),
                   jax.ShapeDtypeStruct((2, n, PACK_W), jnp.int32),
                   jax.ShapeDtypeStruct((8, n), _F32),
                   jax.ShapeDtypeStruct((ROUTE_ROWS, LANES), _F32)],
        scratch_shapes=[pltpu.VMEM((ROUTE_ROWS, 1), _F32)],
        compiler_params=_params("arbitrary"),
        name="out_router",
    )(x2, *os_, *ws, g.reshape(1, d), jnp.concatenate([w_hi, w_lo], axis=1), b_router)


def _expert_kernel(te_ref, nv_ref, xs_ref, wg_ref, wu_ref, wd_ref, ys_ref, wgu_s, wd_s):
    j = pl.program_id(0)
    prev = te_ref[jnp.maximum(j - 1, 0)]

    @pl.when((j == 0) | (te_ref[j] != prev))
    def _():
        wgu_s[:, :MOE_D_FF] = wg_ref[...].astype(_BF16)
        wgu_s[:, MOE_D_FF:] = wu_ref[...].astype(_BF16)
        wd_s[...] = wd_ref[...].astype(_BF16)

    @pl.when(j < nv_ref[0])
    def _():
        rows = xs_ref.shape[1] // MOE_CHAINS
        for r in range(MOE_CHAINS):
            rs = slice(r * rows, (r + 1) * rows)
            xs = _load_packed(xs_ref[:, rs]).astype(_BF16)
            au = jnp.dot(xs, wgu_s[...], preferred_element_type=_F32)
            a, u = au[:, :MOE_D_FF], au[:, MOE_D_FF:]
            act = (a * (1.0 / (1.0 + jnp.exp(-a))) * u).astype(_BF16)
            _store_packed(ys_ref.at[:, rs], jnp.dot(act, wd_s[...], preferred_element_type=_F32))


def _experts(tile_expert, n_valid, xs, w_gate, w_up, w_down, tm=MOE_TM):
    _, n_slots, pw = xs.shape
    n_tiles = n_slots // tm
    d, f = w_gate.shape[-2:]
    row = lambda j, te, nv: (0, jnp.maximum(jnp.minimum(j, nv[0] - 1), 0), 0)
    grid_spec = pltpu.PrefetchScalarGridSpec(
        num_scalar_prefetch=2, grid=(n_tiles,),
        in_specs=[pl.BlockSpec((2, tm, pw), row),
                  pl.BlockSpec((None, d, f), lambda j, te, nv: (te[j], 0, 0)),
                  pl.BlockSpec((None, d, f), lambda j, te, nv: (te[j], 0, 0)),
                  pl.BlockSpec((None, f, d), lambda j, te, nv: (te[j], 0, 0))],
        out_specs=pl.BlockSpec((2, tm, pw), row),
        scratch_shapes=[pltpu.VMEM((d, 2 * f), _BF16), pltpu.VMEM((f, d), _BF16)])
    return pl.pallas_call(
        _expert_kernel, grid_spec=grid_spec,
        out_shape=jax.ShapeDtypeStruct((2, n_slots, pw), jnp.int32),
        compiler_params=_params("arbitrary"),
        name="experts",
    )(tile_expert, n_valid, xs, w_gate, w_up, w_down)


def _combine_kernel(x_ref, y_ref, route_ref, o_ref):
    o_ref[...] = _combine_tile(x_ref, y_ref, route_ref)


def _combine(x2, y, route, tm=512):
    n, d = x2.shape
    row = pl.BlockSpec((tm, d), lambda i: (i, 0))
    return pl.pallas_call(
        _combine_kernel, grid=(n // tm,),
        in_specs=[row, pl.BlockSpec((2, 2, tm, PACK_W), lambda i: (0, 0, i, 0)),
                  pl.BlockSpec((8, tm), lambda i: (0, i))],
        out_specs=row,
        out_shape=jax.ShapeDtypeStruct((n, d), _F32),
        compiler_params=_params("parallel"),
        name="moe_combine",
    )(x2, y, route)


def _slot_kernel(offset_ref, route_ref, slot_ref, *, n_slots):
    expert = route_ref[0:2, :].astype(jnp.int32)
    pos = route_ref[4:6, :].astype(jnp.int32)
    for e in range(N_EXPERTS):
        pos = pos + jnp.where(expert == e, offset_ref[e], 0)
    slot_ref[0:2, :] = pos
    slot_ref[2:4, :] = pos + n_slots


def _slots(offset, route, n_slots):
    n = route.shape[1]
    return pl.pallas_call(
        functools.partial(_slot_kernel, n_slots=n_slots),
        in_specs=[pl.BlockSpec(memory_space=pltpu.SMEM),
                  pl.BlockSpec((8, n), lambda: (0, 0))],
        out_specs=pl.BlockSpec((4, n), lambda: (0, 0)),
        out_shape=jax.ShapeDtypeStruct((4, n), jnp.int32),
        name="moe_slots",
    )(offset, route)


def _router_params(wg, bg, we, be):
    d = wg.shape[0]
    w_router = jnp.zeros((d, LANES), _F32)
    w_router = w_router.at[:, :MOE_GROUPS].set(wg)
    w_router = w_router.at[:, ROUTE_E0:ROUTE_E0 + N_EXPERTS].set(
        jnp.moveaxis(we, 0, 1).reshape(d, N_EXPERTS))
    b_router = jnp.zeros((1, LANES), _F32)
    b_router = b_router.at[0, :MOE_GROUPS].set(bg)
    b_router = b_router.at[0, ROUTE_E0:ROUTE_E0 + N_EXPERTS].set(be.reshape(-1))
    return w_router, b_router


def _moe_experts(h, route, cnt, w_gate, w_up, w_down, layer):
    n = h.shape[1]
    d = w_gate.shape[-2]
    tm = MOE_TM
    counts = cnt[ROUTE_E0:ROUTE_E0 + N_EXPERTS, 0].astype(jnp.int32)
    tiles_per = (counts + tm - 1) // tm
    tiles_end = jnp.cumsum(tiles_per)
    offset = (tiles_end - tiles_per) * tm
    n_tiles = (2 * n) // tm + N_EXPERTS
    n_valid = tiles_end[-1]
    tile_ids = jnp.minimum(jnp.arange(n_tiles, dtype=jnp.int32), n_valid - 1)
    tile_expert = jnp.sum(tile_ids[:, None] >= tiles_end[None, :], axis=1).astype(jnp.int32)
    n_slots = n_tiles * tm
    slot = _slots(offset, route, n_slots).reshape(-1)

    xs = _sc_scatter(h.reshape(2 * n, PACK_W), slot, 2 * n_slots, reps=2)
    ys = _experts(tile_expert + layer * N_EXPERTS, n_valid.reshape(1),
                  xs.reshape(2, n_slots, PACK_W), w_gate.reshape(-1, d, MOE_D_FF),
                  w_up.reshape(-1, d, MOE_D_FF), w_down.reshape(-1, MOE_D_FF, d))
    y = _sc_gather(ys.reshape(2 * n_slots, PACK_W), slot)
    return y.reshape(2, 2, n, PACK_W)


def _proj_prep_pending(x2, pending, *args, **kwargs):
    outs = _proj_prep(x2, *args, pending=pending, **kwargs)
    return (x2, outs) if pending is None else (outs[0], outs[1:])


def _mixer_ab(x2, pending, router, bsz, seq, g, w_in, a_qn, a_kn, b_qn, b_kn, b_sink, w_out):
    aw = A_HEADS * HEAD_DIM
    bqw = B_HEADS * HEAD_DIM
    bkw = B_KV_HEADS * HEAD_DIM
    scale2 = QK_SCALE * LOG2E
    segs = [_Seg(0, aw, True, False, scale2, "plain"),
            _Seg(aw, aw, True, False, 1.0, "T"),
            _Seg(2 * aw, aw, False, False, 1.0, "heads_ones"),
            _Seg(3 * aw, bqw, True, False, scale2, "plain"),
            _Seg(3 * aw + bqw, bkw, True, False, 1.0, "T"),
            _Seg(3 * aw + bqw + bkw, bkw, False, False, 1.0, "heads_ones")]
    x2, (qa, ka, va, qb, kb, vb) = _proj_prep_pending(
        x2, pending, bsz, seq, g, w_in.astype(_BF16), segs, [a_qn, a_kn, None, b_qn, b_kn, None])

    bound_a = HEAD_DIM * scale2 * jnp.max(jnp.abs(a_qn)) * jnp.max(jnp.abs(a_kn))
    bound_b = HEAD_DIM * scale2 * jnp.max(jnp.abs(b_qn)) * jnp.max(jnp.abs(b_kn))
    static_ok = jnp.maximum(bound_a, bound_b) <= SHIFT_MAX
    shift_a = jnp.where(static_ok, bound_a, 0.0)
    shift_b = jnp.where(static_ok, bound_b, 0.0)
    slopes = _alibi_slopes(A_HEADS + B_HEADS)
    bias_a = _dil_bias(slopes[0::2], shift_a)
    bias_b = _band_bias(B_WINDOW, 1.0, slopes[1::2], shift_b, B_HEADS // B_KV_HEADS)
    sink2 = b_sink.astype(_F32) * LOG2E - shift_b

    def attend(online, qa, ka, va, qb, kb, vb, bias_a, bias_b, sink2):
        return (_dil_attn(qa, ka, va, bias_a, online=online),
                _band_b(qb, kb, vb, bias_b, sink2, online=online))

    oa, ob = lax.cond(static_ok, functools.partial(attend, False), functools.partial(attend, True),
                      qa, ka, va, qb, kb, vb, bias_a, bias_b, sink2)
    w_out = w_out.astype(_BF16)
    return _out_router(x2, [oa.reshape(-1, aw), ob.reshape(-1, bqw)], [w_out[:aw], w_out[aw:]],
                       *router)


def _mixer_c(x2, pending, router, bsz, seq, g, w_in, qn, kn, w_out):
    qw = C_HEADS * HEAD_DIM
    kvw = C_KV_HEADS * HEAD_DIM
    rope = _rope_tables(seq)
    bound = HEAD_DIM * QK_SCALE * LOG2E * jnp.max(jnp.abs(qn)) * jnp.max(jnp.abs(kn))
    static_ok = bound <= SHIFT_MAX
    shift = jnp.where(static_ok, bound, 0.0)
    segs = [_Seg(0, qw, True, True, QK_SCALE * LOG2E, "T_one"),
            _Seg(qw, kvw, True, True, 1.0, "heads_shift"),
            _Seg(qw + kvw, kvw, False, False, 1.0, "T_ones")]
    x2, (qt, k, vt) = _proj_prep_pending(x2, pending, bsz, seq, g, w_in.astype(_BF16), segs,
                                         [qn, kn, None], rope=rope, shift=shift)
    o = lax.cond(static_ok,
                 functools.partial(_dense_attn, online=False),
                 functools.partial(_dense_attn, online=True), qt, k, vt)
    return _out_router(x2, [o.reshape(-1, qw)], [w_out.astype(_BF16)], *router)


def kernel(x, mix_norm, ffn_norm, ab_w_in, a_q_norm, a_k_norm, b_q_norm, b_k_norm, b_sink, ab_w_out,
           c_w_in, c_q_norm, c_k_norm, c_w_out, moe_group_w, moe_group_b, moe_expert_w, moe_expert_b,
           moe_w_gate, moe_w_up, moe_w_down):
    bsz, seq, d = x.shape
    x2 = x.reshape(bsz * seq, d)
    depth = mix_norm.shape[0]
    pending = None
    for layer in range(depth):
        i = layer // 2
        router = (ffn_norm[layer],) + _router_params(moe_group_w[layer], moe_group_b[layer],
                                                     moe_expert_w[layer], moe_expert_b[layer])
        if layer % 2 == 0:
            x2, h, route, cnt = _mixer_ab(x2, pending, router, bsz, seq, mix_norm[layer], ab_w_in[i],
                                          a_q_norm[i], a_k_norm[i], b_q_norm[i], b_k_norm[i],
                                          b_sink[i], ab_w_out[i])
        else:
            x2, h, route, cnt = _mixer_c(x2, pending, router, bsz, seq, mix_norm[layer], c_w_in[i],
                                         c_q_norm[i], c_k_norm[i], c_w_out[i])
        pending = (_moe_experts(h, route, cnt, moe_w_gate, moe_w_up, moe_w_down, layer), route)
    return _combine(x2, *pending).reshape(bsz, seq, d)
```

```python
import functools
import math
from typing import NamedTuple

import jax
import jax.numpy as jnp
import numpy as np
from jax import lax
from jax.experimental import pallas as pl
from jax.experimental.pallas import tpu as pltpu
from jax.experimental.pallas import tpu_sc as plsc

HEAD_DIM = 64
LANES = 128
N_HEADS = 16
A_HEADS = 8
B_HEADS = 8
B_KV_HEADS = 2
C_HEADS = 16
C_KV_HEADS = 4
A_PATTERNS = ((128, 1), (512, 4), (2048, 16))
B_WINDOW = 128
GRID_W = 64
ROPE_THETA = 10000.0
ROPE_AXIS_DIM = HEAD_DIM // 2
ALIBI_MAX_BIAS = 8.0
RMS_EPS = 1e-6
MOE_GROUPS = 4
MOE_EXPERTS = 8
N_EXPERTS = MOE_GROUPS * MOE_EXPERTS
MOE_D_FF = 256
QK_SCALE = HEAD_DIM ** -0.5
LOG2E = math.log2(math.e)
SHIFT_MAX = 60.0
NEG_BIG = -1e30
VMEM_LIMIT = 52 * 1024 * 1024

BAND_BQ = 256
BAND_KB = 128
BAND_KW = BAND_BQ + 2 * BAND_KB
DIL_BQ = 256
DIL_REACH = max(w // 2 for w, _ in A_PATTERNS)
DIL_KW = DIL_BQ + 2 * DIL_REACH
DIL_CHUNK = 768
DENSE_BQ = 1024
DENSE_KCH = 2048
MOE_TM = 512
MOE_CHAINS = 1
ROUTE_E0 = 4
ROUTE_CHAINS = 1
ROUTE_ROWS = 48
PACK_W = 256
SC_WINDOW = 128

_BF16 = jnp.bfloat16
_F32 = jnp.float32


def _params(*sem):
    return pltpu.CompilerParams(dimension_semantics=sem, vmem_limit_bytes=VMEM_LIMIT)


def _alibi_slopes(n):
    return np.asarray(2.0 ** (-ALIBI_MAX_BIAS * np.arange(1, n + 1) / n), dtype=np.float32)


def _head_norm(y, gain, head_ones):
    y2 = y * y
    hi = y2.astype(_BF16)
    lo = (y2 - hi.astype(_F32)).astype(_BF16)
    ss = (jnp.dot(hi, head_ones, preferred_element_type=_F32)
          + jnp.dot(lo, head_ones, preferred_element_type=_F32))
    return y * lax.rsqrt(ss * (1.0 / HEAD_DIM) + RMS_EPS) * gain


def _rope(y, cos, sin):
    lane = lax.broadcasted_iota(jnp.int32, y.shape, 1)
    first = (lane % 32) < 16
    partner = jnp.where(first, pltpu.roll(y, LANES - 16, axis=1), pltpu.roll(y, 16, axis=1))
    return y * cos + partner * sin


class _Seg(NamedTuple):
    col0: int
    ncols: int
    norm: bool
    rope: bool
    scale: float
    mode: str


def _prep_slab(y, c, o_ref, seg, gain_ref, cos_ref, sin_ref, shift_ref, head_ones):
    rows = y.shape[0]
    if seg.norm:
        y = _head_norm(y, gain_ref[...], head_ones)
    if seg.rope:
        y = _rope(y, cos_ref[...], sin_ref[...])
    if seg.scale != 1.0:
        y = y * seg.scale
    sl = slice(c * LANES, (c + 1) * LANES)
    if seg.mode == "plain":
        o_ref[:, sl] = y.astype(_BF16)
    elif seg.mode == "T":
        o_ref[sl, :] = y.T.astype(_BF16)
    elif seg.mode in ("heads_ones", "heads_shift"):
        lane = lax.broadcasted_iota(jnp.int32, y.shape, 1)
        if seg.mode == "heads_ones":
            fill = jnp.ones(y.shape, _F32)
        else:
            fill = jnp.where(lane == HEAD_DIM, -shift_ref[0], 0.0)
        for k, yk in enumerate((y, pltpu.roll(y, HEAD_DIM, axis=1))):
            o_ref[2 * c + k] = jnp.where(lane < HEAD_DIM, yk, fill).astype(_BF16)
    else:
        assert seg.mode in ("T_one", "T_ones")
        yt = y.T
        if seg.mode == "T_one":
            row = lax.broadcasted_iota(jnp.int32, (HEAD_DIM, rows), 0)
            extra = jnp.where(row == 0, 1.0, 0.0)
        else:
            extra = jnp.ones((HEAD_DIM, rows), _F32)
        for k in range(2):
            ext = jnp.concatenate([yt[k * HEAD_DIM:(k + 1) * HEAD_DIM], extra], axis=0)
            o_ref[(2 * c + k) * LANES:(2 * c + k + 1) * LANES, :] = ext.astype(_BF16)


def _combine_tile(x_ref, y_ref, route_ref):
    coef = route_ref[...].T
    y0 = _load_packed(y_ref[:, 0])
    y1 = _load_packed(y_ref[:, 1])
    return x_ref[...] + coef[:, 2:3] * y0 + coef[:, 3:4] * y1


def _proj_prep_kernel(*refs, segs, combine):
    it = iter(refs)
    shift_ref = next(it) if any(s.mode == "heads_shift" for s in segs) else None
    x_ref, g_ref, w_ref = next(it), next(it), next(it)
    y_ref, route_ref = (next(it), next(it)) if combine else (None, None)
    gain_refs = [next(it) if s.norm else None for s in segs]
    cos_ref, sin_ref = (next(it), next(it)) if any(s.rope for s in segs) else (None, None)
    outs = list(it)
    if combine:
        x = _combine_tile(x_ref, y_ref, route_ref)
        outs.pop(0)[...] = x
    else:
        x = x_ref[...]
    ms = jnp.mean(x * x, axis=-1, keepdims=True)
    h = (x * lax.rsqrt(ms + RMS_EPS) * g_ref[...]).astype(_BF16)
    proj = jnp.dot(h, w_ref[...], preferred_element_type=_F32)
    head_ones = (lax.broadcasted_iota(jnp.int32, (LANES, LANES), 0) // HEAD_DIM
                 == lax.broadcasted_iota(jnp.int32, (LANES, LANES), 1) // HEAD_DIM).astype(_BF16)
    for seg, gain_ref, o_ref in zip(segs, gain_refs, outs):
        for c in range(seg.ncols // LANES):
            y = proj[:, seg.col0 + c * LANES:seg.col0 + (c + 1) * LANES]
            _prep_slab(y, c, o_ref, seg, gain_ref, cos_ref, sin_ref, shift_ref, head_ones)


def _proj_prep(x2, bsz, seq, g, w, segs, gains, rope=None, shift=None, pending=None, tm=512):
    n, d = x2.shape
    p = w.shape[1]
    nt = seq // tm
    assert seq % tm == 0
    const = lambda shape: pl.BlockSpec(shape, lambda i: (0,) * len(shape))
    row = pl.BlockSpec((tm, d), lambda i: (i, 0))
    ins, in_specs = [], []
    if shift is not None:
        ins.append(shift.reshape(1).astype(_F32))
        in_specs.append(pl.BlockSpec(memory_space=pltpu.SMEM))
    ins += [x2, g.reshape(1, d), w]
    in_specs += [row, const((1, d)), const((d, p))]
    if pending is not None:
        ins += list(pending)
        in_specs += [pl.BlockSpec((2, 2, tm, PACK_W), lambda i: (0, 0, i, 0)),
                     pl.BlockSpec((8, tm), lambda i: (0, i))]
    for seg, gain in zip(segs, gains):
        if seg.norm:
            ins.append(jnp.tile(gain.astype(_F32), LANES // HEAD_DIM).reshape(1, LANES))
            in_specs.append(const((1, LANES)))
    if rope is not None:
        ins += list(rope)
        in_specs += [pl.BlockSpec((tm, LANES), lambda i: (i % nt, 0))] * 2
    out_shape, out_specs = [], []
    if pending is not None:
        out_shape.append(jax.ShapeDtypeStruct((n, d), _F32))
        out_specs.append(row)
    for seg in segs:
        nc = seg.ncols
        if seg.mode == "plain":
            out_shape.append(jax.ShapeDtypeStruct((bsz, seq, nc), _BF16))
            out_specs.append(pl.BlockSpec((None, tm, nc), lambda i: (i // nt, i % nt, 0)))
        elif seg.mode in ("T", "T_one", "T_ones"):
            wout = nc if seg.mode == "T" else 2 * nc
            out_shape.append(jax.ShapeDtypeStruct((bsz, wout, seq), _BF16))
            out_specs.append(pl.BlockSpec((None, wout, tm), lambda i: (i // nt, 0, i % nt)))
        else:
            assert seg.mode in ("heads_ones", "heads_shift")
            nh = nc // HEAD_DIM
            out_shape.append(jax.ShapeDtypeStruct((bsz, nh, seq, LANES), _BF16))
            out_specs.append(pl.BlockSpec((None, nh, tm, LANES), lambda i: (i // nt, 0, i % nt, 0)))
    return pl.pallas_call(
        functools.partial(_proj_prep_kernel, segs=tuple(segs), combine=pending is not None),
        grid=(n // tm,),
        in_specs=in_specs, out_specs=out_specs, out_shape=out_shape,
        compiler_params=_params("parallel"),
        name="proj_prep",
    )(*ins)


def _rope_tables(seq):
    t = np.arange(seq)
    row = (t // GRID_W).astype(np.float32)
    col = (t % GRID_W).astype(np.float32)
    inv_freq = np.float32(ROPE_THETA) ** (-np.arange(0, ROPE_AXIS_DIM, 2, dtype=np.float32)
                                          / np.float32(ROPE_AXIS_DIM))
    ang_r = row[:, None] * inv_freq[None, :]
    ang_c = col[:, None] * inv_freq[None, :]
    cr, sr, cc, sc = np.cos(ang_r), np.sin(ang_r), np.cos(ang_c), np.sin(ang_c)
    cos = np.concatenate([cr, cr, cc, cc], axis=-1)
    sin = np.concatenate([-sr, sr, -sc, sc], axis=-1)
    return (jnp.asarray(np.tile(cos, (1, 2)), _F32), jnp.asarray(np.tile(sin, (1, 2)), _F32))


def _band_bias(hw, dist_scale, slopes, shift, stack):
    bq, kb = BAND_BQ, BAND_KB
    assert hw <= kb
    r = np.arange(bq)[:, None]
    c = np.arange(BAND_KW)[None, :]
    rel = np.abs(c - kb - r)
    dist = rel.astype(np.float32) * np.float32(dist_scale)
    ok = np.stack([(rel <= hw) & ~(first & (c < kb)) & ~(last & (c >= kb + bq))
                   for first, last in ((False, False), (True, False), (False, True), (True, True))])
    alibi = -(np.asarray(slopes, np.float32)[:, None, None] * dist[None]) * np.float32(LOG2E)
    bias = jnp.where(ok[:, None], jnp.asarray(alibi)[None] - shift, NEG_BIG)
    nh = len(slopes)
    return bias.reshape(4, nh // stack, stack * bq, BAND_KW)


def _edge_variant(i, nb):
    return jnp.where(i == 0, 1, 0) + jnp.where(i == nb - 1, 2, 0)


def _window_blocks(nb):
    per = BAND_BQ // BAND_KB
    last = nb * per - 1
    fns = [lambda i: jnp.maximum(per * i - 1, 0)]
    fns += [functools.partial(lambda j, i: per * i + j, j) for j in range(per)]
    fns += [lambda i: jnp.minimum(per * i + per, last)]
    return fns


def _dil_bias(slopes, shift):
    r = jnp.arange(DIL_BQ, dtype=jnp.int32)[:, None]
    u = jnp.arange(DIL_BQ + 4 * DIL_REACH, dtype=jnp.int32)[None, :]
    delta = u - 2 * DIL_REACH - r
    dist = jnp.abs(delta)
    mult = sum(((delta % d == 0) & (dist <= ((w // 2) // d) * d)).astype(_F32) for w, d in A_PATTERNS)
    alibi = -(jnp.asarray(slopes, _F32)[:, None, None] * dist.astype(_F32)[None]) * LOG2E
    return jnp.where((mult > 0)[None], jnp.log2(jnp.maximum(mult, 1.0))[None] + alibi - shift, NEG_BIG)


def _dil_attn_kernel(q_ref, kt_ref, v_ref, bias_ref, o_ref, *, online):
    bq = q_ref.shape[0]
    seq = kt_ref.shape[1]
    q0 = pl.program_id(2) * bq
    w0 = pl.multiple_of(jnp.clip(q0 - DIL_REACH, 0, seq - DIL_KW), bq)
    u0 = pl.multiple_of(w0 - q0 + 2 * DIL_REACH, bq)
    q = q_ref[...]
    outs = []
    for j in range(2):
        rows = slice(j * HEAD_DIM, (j + 1) * HEAD_DIM)
        scores = []
        for c in range(DIL_KW // DIL_CHUNK):
            ks = pl.ds(pl.multiple_of(w0 + c * DIL_CHUNK, bq), DIL_CHUNK)
            us = pl.ds(pl.multiple_of(u0 + c * DIL_CHUNK, bq), DIL_CHUNK)
            scores.append(jnp.dot(q[:, rows], kt_ref[rows, ks], preferred_element_type=_F32)
                          + bias_ref[j, :, us])
        if online:
            m = functools.reduce(jnp.maximum, [jnp.max(s, axis=-1, keepdims=True) for s in scores])
            scores = [s - m for s in scores]
        acc = jnp.zeros((bq, LANES), _F32)
        for c, s in enumerate(scores):
            ks = pl.ds(pl.multiple_of(w0 + c * DIL_CHUNK, bq), DIL_CHUNK)
            acc += jnp.dot(jnp.exp2(s).astype(_BF16), v_ref[j, ks, :], preferred_element_type=_F32)
        outs.append((acc / pltpu.roll(acc, HEAD_DIM, axis=1))[:, :HEAD_DIM])
    o_ref[...] = jnp.concatenate(outs, axis=1).astype(_BF16)


def _dil_attn(q, kt, v, bias, *, online):
    bsz, seq, w = q.shape
    bq = DIL_BQ
    assert seq >= DIL_KW and seq % bq == 0
    return pl.pallas_call(
        functools.partial(_dil_attn_kernel, online=online), grid=(A_HEADS // 2, bsz, seq // bq),
        in_specs=[pl.BlockSpec((None, bq, LANES), lambda p, b, i: (b, i, p)),
                  pl.BlockSpec((None, LANES, seq), lambda p, b, i: (b, p, 0)),
                  pl.BlockSpec((None, 2, seq, LANES), lambda p, b, i: (b, p, 0, 0)),
                  pl.BlockSpec((2, bq, bias.shape[-1]), lambda p, b, i: (p, 0, 0))],
        out_specs=pl.BlockSpec((None, bq, LANES), lambda p, b, i: (b, i, p)),
        out_shape=jax.ShapeDtypeStruct((bsz, seq, w), _BF16),
        compiler_params=_params("parallel", "parallel", "parallel"),
        name="dil_attn_online" if online else "dil_attn",
    )(q, kt, v, bias)


def _band_b_kernel(sink2_ref, sinkp_ref, q_ref, *refs, online):
    bq = q_ref.shape[0]
    g = pl.program_id(1)
    grp = B_HEADS // B_KV_HEADS
    npc = BAND_KW // BAND_KB
    k_refs, v_refs, (bias_ref, o_ref) = refs[:npc], refs[npc:2 * npc], refs[2 * npc:]
    kt = jnp.concatenate([r[...] for r in k_refs], axis=1)
    v = jnp.concatenate([r[...] for r in v_refs], axis=0)
    q = q_ref[...]
    q4 = jnp.concatenate([q[:, i * HEAD_DIM:(i + 1) * HEAD_DIM] for i in range(grp)], axis=0)
    s = jnp.dot(q4, kt, preferred_element_type=_F32) + bias_ref[...]
    if online:
        m = jnp.max(s, axis=-1, keepdims=True)
        s = s - m
    acc = jnp.dot(jnp.exp2(s).astype(_BF16), v, preferred_element_type=_F32)
    for i in range(grp):
        a = acc[i * bq:(i + 1) * bq]
        if online:
            mi = m[i * bq:(i + 1) * bq]
            sk = sink2_ref[g * grp + i]
            mm = jnp.maximum(mi, sk)
            a = a * jnp.exp2(mi - mm)
            o = a / (pltpu.roll(a, HEAD_DIM, axis=1) + jnp.exp2(sk - mm))
        else:
            o = a / (pltpu.roll(a, HEAD_DIM, axis=1) + sinkp_ref[g * grp + i])
        o_ref[:, i * HEAD_DIM:(i + 1) * HEAD_DIM] = o[:, :HEAD_DIM].astype(_BF16)


def _band_b(q, kt, v, bias, sink2, *, online):
    bsz, seq_len, w = q.shape
    bq, kb = BAND_BQ, BAND_KB
    nb = seq_len // bq
    grp = B_HEADS // B_KV_HEADS
    gw = w // B_KV_HEADS
    blocks = _window_blocks(nb)
    kspec = lambda f: pl.BlockSpec((None, HEAD_DIM, kb), lambda b, g, i: (b, g, f(i)))
    vspec = lambda f: pl.BlockSpec((None, None, kb, LANES), lambda b, g, i: (b, g, f(i), 0))
    qspec = pl.BlockSpec((None, bq, gw), lambda b, g, i: (b, i, g))
    smem = pl.BlockSpec(memory_space=pltpu.SMEM)
    return pl.pallas_call(
        functools.partial(_band_b_kernel, online=online), grid=(bsz, B_KV_HEADS, nb),
        in_specs=[smem, smem, qspec] + [kspec(f) for f in blocks] + [vspec(f) for f in blocks]
        + [pl.BlockSpec((None, None, grp * bq, BAND_KW),
                        lambda b, g, i: (_edge_variant(i, nb), g, 0, 0))],
        out_specs=qspec,
        out_shape=jax.ShapeDtypeStruct((bsz, seq_len, w), _BF16),
        compiler_params=_params("parallel", "parallel", "parallel"),
        name="band_b_online" if online else "band_b",
    )(sink2, jnp.exp2(sink2), q, *([kt] * len(blocks)), *([v] * len(blocks)), bias)


def _dense_attn_kernel(qt_ref, k_ref, vt_ref, o_ref, *, kch, online):
    bq = qt_ref.shape[1]
    seq = k_ref.shape[0]
    for c in range(C_HEADS // C_KV_HEADS):
        qt = qt_ref[c * LANES:(c + 1) * LANES, :]
        acc = jnp.zeros((LANES, bq), _F32)
        m = jnp.full((1, bq), NEG_BIG, _F32)
        for j in range(seq // kch):
            ks = slice(j * kch, (j + 1) * kch)
            st = jnp.dot(k_ref[ks, :], qt, preferred_element_type=_F32)
            if online:
                m_new = jnp.maximum(m, jnp.max(st, axis=0, keepdims=True))
                acc = acc * jnp.exp2(m - m_new)
                st = st - m_new
                m = m_new
            acc = acc + jnp.dot(vt_ref[:, ks], jnp.exp2(st).astype(_BF16),
                                preferred_element_type=_F32)
        o = acc[:HEAD_DIM] / acc[HEAD_DIM:]
        o_ref[:, c * HEAD_DIM:(c + 1) * HEAD_DIM] = o.T.astype(_BF16)


def _dense_attn(qt, k, vt, *, online, bq=DENSE_BQ, kch=DENSE_KCH):
    bsz, _, seq = qt.shape
    bq, kch = min(bq, seq), min(kch, seq)
    grp = C_HEADS // C_KV_HEADS
    return pl.pallas_call(
        functools.partial(_dense_attn_kernel, kch=kch, online=online),
        grid=(bsz, C_KV_HEADS, seq // bq),
        in_specs=[pl.BlockSpec((None, grp * LANES, bq), lambda b, g, i: (b, g, i)),
                  pl.BlockSpec((None, None, seq, LANES), lambda b, g, i: (b, g, 0, 0)),
                  pl.BlockSpec((None, LANES, seq), lambda b, g, i: (b, g, 0))],
        out_specs=pl.BlockSpec((None, bq, grp * HEAD_DIM), lambda b, g, i: (b, i, g)),
        out_shape=jax.ShapeDtypeStruct((bsz, seq, C_HEADS * HEAD_DIM), _BF16),
        compiler_params=_params("parallel", "parallel", "parallel"),
        name="dense_attn_online" if online else "dense_attn",
    )(qt, k, vt)


def _pack_pair(a, b):
    wa = lax.bitcast_convert_type(a.astype(_BF16).astype(_F32), jnp.uint32) >> 16
    wb = lax.bitcast_convert_type(b.astype(_BF16).astype(_F32), jnp.uint32) & jnp.uint32(0xFFFF0000)
    return lax.bitcast_convert_type(wa | wb, jnp.int32)


def _unpack_pair(w):
    u = lax.bitcast_convert_type(w, jnp.uint32)
    return (lax.bitcast_convert_type(u << 16, _F32),
            lax.bitcast_convert_type(u & jnp.uint32(0xFFFF0000), _F32))


def _store_packed(ref, y):
    q = PACK_W
    for j in range(2):
        ref[j] = _pack_pair(y[:, 2 * j * q:(2 * j + 1) * q], y[:, (2 * j + 1) * q:(2 * j + 2) * q])


def _load_packed(ref):
    parts = []
    for j in range(2):
        parts += list(_unpack_pair(ref[j]))
    return jnp.concatenate(parts, axis=1)


def _sc_mesh():
    return plsc.VectorSubcoreMesh(core_axis_name="core", subcore_axis_name="subcore")


def _sc_gather(table, idx):
    n = idx.shape[0]
    d = table.shape[1]

    @pl.kernel(out_type=jax.ShapeDtypeStruct((n, d), table.dtype), mesh=_sc_mesh())
    def gather(x_hbm, i_hbm, o_hbm):
        def body(i_vmem, o_vmem):
            pltpu.sync_copy(x_hbm.at[i_vmem.at[0]], o_vmem)

        pltpu.emit_pipeline(
            body, grid=(n // SC_WINDOW,),
            in_specs=[pl.BlockSpec((1, SC_WINDOW), index_map=lambda i: (0, i))],
            out_specs=[pl.BlockSpec((SC_WINDOW, d), index_map=lambda i: (i, 0))],
            core_axis_name=("core", "subcore"),
            dimension_semantics=(pltpu.PARALLEL,),
        )(i_hbm, o_hbm)

    return gather(table, idx.reshape(1, n))


def _sc_scatter(src, idx, n_out, reps):
    n = idx.shape[0]
    r2, d = src.shape
    nb = r2 // 2 // SC_WINDOW

    @pl.kernel(out_type=jax.ShapeDtypeStruct((n_out, d), src.dtype), mesh=_sc_mesh())
    def scatter(x_hbm, i_hbm, o_hbm):
        def body(x_vmem, i_vmem):
            pltpu.sync_copy(x_vmem, o_hbm.at[i_vmem.at[0]])

        pltpu.emit_pipeline(
            body, grid=(n // SC_WINDOW,),
            in_specs=[pl.BlockSpec((SC_WINDOW, d),
                                   index_map=lambda i: ((i // (reps * nb)) * nb + i % nb, 0)),
                      pl.BlockSpec((1, SC_WINDOW), index_map=lambda i: (0, i))],
            out_specs=[],
            core_axis_name=("core", "subcore"),
            dimension_semantics=(pltpu.PARALLEL,),
        )(x_hbm, i_hbm)

    return scatter(src, idx.reshape(1, n))


def _route_tile(x, rs, g_ref, w_ref, b_ref, h_ref, route_ref, cnt_ref, carry_ref):
    tm = x.shape[0]
    ms = jnp.mean(x * x, axis=-1, keepdims=True)
    h = x * lax.rsqrt(ms + RMS_EPS) * g_ref[...]
    _store_packed(h_ref.at[:, rs], h)
    h_hi = h.astype(_BF16)
    h_lo = (h - h_hi.astype(_F32)).astype(_BF16)
    hw = (jnp.dot(h_hi, w_ref[...], preferred_element_type=_F32)
          + jnp.dot(h_lo, w_ref[...], preferred_element_type=_F32))
    logits = hw[:, :LANES] + hw[:, LANES:] + b_ref[...]
    lt = logits.T[:ROUTE_ROWS]
    row = lax.broadcasted_iota(jnp.int32, (ROUTE_ROWS, tm), 0)

    lg = jnp.where(row < MOE_GROUPS, lt, NEG_BIG)
    mg = jnp.max(lg, axis=0, keepdims=True)
    zg = jnp.sum(jnp.exp(lg - mg), axis=0, keepdims=True)
    p_grp = 1.0 / zg
    g_idx = jnp.min(jnp.where(lg == mg, row, ROUTE_ROWS), axis=0, keepdims=True)

    e_row = row - ROUTE_E0
    emask = (e_row >= 0) & (e_row < N_EXPERTS) & ((e_row >> 3) == g_idx)
    le = jnp.where(emask, lt, NEG_BIG)
    m1 = jnp.max(le, axis=0, keepdims=True)
    i1 = jnp.min(jnp.where(le == m1, row, ROUTE_ROWS), axis=0, keepdims=True)
    le2 = jnp.where(row == i1, NEG_BIG, le)
    m2 = jnp.max(le2, axis=0, keepdims=True)
    i2 = jnp.min(jnp.where(le2 == m2, row, ROUTE_ROWS), axis=0, keepdims=True)
    e21 = jnp.exp(m2 - m1)
    c1 = p_grp / (1.0 + e21)
    c2 = p_grp * e21 / (1.0 + e21)

    onehot = jnp.where(row == i1, 1.0, jnp.where(row == i2, 1.0, 0.0)).astype(_BF16)
    upper = (lax.broadcasted_iota(jnp.int32, (tm, tm), 0)
             <= lax.broadcasted_iota(jnp.int32, (tm, tm), 1)).astype(_BF16)
    cum = jnp.dot(onehot, upper, preferred_element_type=_F32) + carry_ref[...]
    r1 = jnp.sum(jnp.where(row == i1, cum, 0.0), axis=0, keepdims=True) - 1.0
    r2 = jnp.sum(jnp.where(row == i2, cum, 0.0), axis=0, keepdims=True) - 1.0
    total = jnp.max(cum, axis=1, keepdims=True)
    carry_ref[...] = total
    cnt_ref[...] = jnp.broadcast_to(total, cnt_ref.shape)

    rows = ((i1 - ROUTE_E0).astype(_F32), (i2 - ROUTE_E0).astype(_F32), c1, c2, r1, r2)
    rrow = lax.broadcasted_iota(jnp.int32, (8, tm), 0)
    route = jnp.zeros((8, tm), _F32)
    for k, val in enumerate(rows):
        route = jnp.where(rrow == k, val, route)
    route_ref[:, rs] = route


def _out_router_kernel(*refs, n_in):
    x_ref = refs[0]
    o_refs = refs[1:1 + n_in]
    w_refs = refs[1 + n_in:1 + 2 * n_in]
    g_ref, wr_ref, br_ref, xnew_ref = refs[1 + 2 * n_in:5 + 2 * n_in]
    route_refs = refs[5 + 2 * n_in:]
    carry_ref = route_refs[-1]

    @pl.when(pl.program_id(0) == 0)
    def _():
        carry_ref[...] = jnp.zeros(carry_ref.shape, _F32)

    rows = x_ref.shape[0] // ROUTE_CHAINS
    for r in range(ROUTE_CHAINS):
        rs = slice(r * rows, (r + 1) * rows)
        acc = x_ref[rs]
        for o_ref, w_ref in zip(o_refs, w_refs):
            acc = acc + jnp.dot(o_ref[rs], w_ref[...], preferred_element_type=_F32)
        xnew_ref[rs] = acc
        _route_tile(acc, rs, g_ref, wr_ref, br_ref, *route_refs)


def _out_router(x2, os_, ws, g, w_router, b_router, tm=1024):
    n, d = x2.shape
    w_hi = w_router.astype(_BF16)
    w_lo = (w_router - w_hi.astype(_F32)).astype(_BF16)
    const = lambda shape: pl.BlockSpec(shape, lambda i: (0,) * len(shape))
    row = pl.BlockSpec((tm, d), lambda i: (i, 0))
    in_specs = [row] + [pl.BlockSpec((tm, o.shape[1]), lambda i: (i, 0)) for o in os_]
    in_specs += [const(w.shape) for w in ws]
    in_specs += [const((1, d)), const((d, 2 * LANES)), const((1, LANES))]
    return pl.pallas_call(
        functools.partial(_out_router_kernel, n_in=len(os_)), grid=(n // tm,),
        in_specs=in_specs,
        out_specs=[row, pl.BlockSpec((2, tm, PACK_W), lambda i: (0, i, 0)),
                   pl.BlockSpec((8, tm), lambda i: (0, i)), const((ROUTE_ROWS, LANES))],
        out_shape=[jax.ShapeDtypeStruct((n, d), _F32),
                   jax.ShapeDtypeStruct((2, n, PACK_W), jnp.int32),
                   jax.ShapeDtypeStruct((8, n), _F32),
                   jax.ShapeDtypeStruct((ROUTE_ROWS, LANES), _F32)],
        scratch_shapes=[pltpu.VMEM((ROUTE_ROWS, 1), _F32)],
        compiler_params=_params("arbitrary"),
        name="out_router",
    )(x2, *os_, *ws, g.reshape(1, d), jnp.concatenate([w_hi, w_lo], axis=1), b_router)


def _expert_kernel(te_ref, nv_ref, xs_ref, wg_ref, wu_ref, wd_ref, ys_ref, wgu_s, wd_s):
    j = pl.program_id(0)
    prev = te_ref[jnp.maximum(j - 1, 0)]

    @pl.when((j == 0) | (te_ref[j] != prev))
    def _():
        wgu_s[:, :MOE_D_FF] = wg_ref[...].astype(_BF16)
        wgu_s[:, MOE_D_FF:] = wu_ref[...].astype(_BF16)
        wd_s[...] = wd_ref[...].astype(_BF16)

    @pl.when(j < nv_ref[0])
    def _():
        rows = xs_ref.shape[1] // MOE_CHAINS
        for r in range(MOE_CHAINS):
            rs = slice(r * rows, (r + 1) * rows)
            xs = _load_packed(xs_ref[:, rs]).astype(_BF16)
            au = jnp.dot(xs, wgu_s[...], preferred_element_type=_F32)
            a, u = au[:, :MOE_D_FF], au[:, MOE_D_FF:]
            act = (a * (1.0 / (1.0 + jnp.exp(-a))) * u).astype(_BF16)
            _store_packed(ys_ref.at[:, rs], jnp.dot(act, wd_s[...], preferred_element_type=_F32))


def _experts(tile_expert, n_valid, xs, w_gate, w_up, w_down, tm=MOE_TM):
    _, n_slots, pw = xs.shape
    n_tiles = n_slots // tm
    d, f = w_gate.shape[-2:]
    row = lambda j, te, nv: (0, jnp.maximum(jnp.minimum(j, nv[0] - 1), 0), 0)
    grid_spec = pltpu.PrefetchScalarGridSpec(
        num_scalar_prefetch=2, grid=(n_tiles,),
        in_specs=[pl.BlockSpec((2, tm, pw), row),
                  pl.BlockSpec((None, d, f), lambda j, te, nv: (te[j], 0, 0)),
                  pl.BlockSpec((None, d, f), lambda j, te, nv: (te[j], 0, 0)),
                  pl.BlockSpec((None, f, d), lambda j, te, nv: (te[j], 0, 0))],
        out_specs=pl.BlockSpec((2, tm, pw), row),
        scratch_shapes=[pltpu.VMEM((d, 2 * f), _BF16), pltpu.VMEM((f, d), _BF16)])
    return pl.pallas_call(
        _expert_kernel, grid_spec=grid_spec,
        out_shape=jax.ShapeDtypeStruct((2, n_slots, pw), jnp.int32),
        compiler_params=_params("arbitrary"),
        name="experts",
    )(tile_expert, n_valid, xs, w_gate, w_up, w_down)


def _combine_kernel(x_ref, y_ref, route_ref, o_ref):
    o_ref[...] = _combine_tile(x_ref, y_ref, route_ref)


def _combine(x2, y, route, tm=512):
    n, d = x2.shape
    row = pl.BlockSpec((tm, d), lambda i: (i, 0))
    return pl.pallas_call(
        _combine_kernel, grid=(n // tm,),
        in_specs=[row, pl.BlockSpec((2, 2, tm, PACK_W), lambda i: (0, 0, i, 0)),
                  pl.BlockSpec((8, tm), lambda i: (0, i))],
        out_specs=row,
        out_shape=jax.ShapeDtypeStruct((n, d), _F32),
        compiler_params=_params("parallel"),
        name="moe_combine",
    )(x2, y, route)


def _slot_kernel(offset_ref, route_ref, slot_ref, *, n_slots):
    expert = route_ref[0:2, :].astype(jnp.int32)
    pos = route_ref[4:6, :].astype(jnp.int32)
    for e in range(N_EXPERTS):
        pos = pos + jnp.where(expert == e, offset_ref[e], 0)
    slot_ref[0:2, :] = pos
    slot_ref[2:4, :] = pos + n_slots


def _slots(offset, route, n_slots):
    n = route.shape[1]
    return pl.pallas_call(
        functools.partial(_slot_kernel, n_slots=n_slots),
        in_specs=[pl.BlockSpec(memory_space=pltpu.SMEM),
                  pl.BlockSpec((8, n), lambda: (0, 0))],
        out_specs=pl.BlockSpec((4, n), lambda: (0, 0)),
        out_shape=jax.ShapeDtypeStruct((4, n), jnp.int32),
        name="moe_slots",
    )(offset, route)


def _router_params(wg, bg, we, be):
    d = wg.shape[0]
    w_router = jnp.zeros((d, LANES), _F32)
    w_router = w_router.at[:, :MOE_GROUPS].set(wg)
    w_router = w_router.at[:, ROUTE_E0:ROUTE_E0 + N_EXPERTS].set(
        jnp.moveaxis(we, 0, 1).reshape(d, N_EXPERTS))
    b_router = jnp.zeros((1, LANES), _F32)
    b_router = b_router.at[0, :MOE_GROUPS].set(bg)
    b_router = b_router.at[0, ROUTE_E0:ROUTE_E0 + N_EXPERTS].set(be.reshape(-1))
    return w_router, b_router


def _moe_experts(h, route, cnt, w_gate, w_up, w_down, layer):
    n = h.shape[1]
    d = w_gate.shape[-2]
    tm = MOE_TM
    counts = cnt[ROUTE_E0:ROUTE_E0 + N_EXPERTS, 0].astype(jnp.int32)
    tiles_per = (counts + tm - 1) // tm
    tiles_end = jnp.cumsum(tiles_per)
    offset = (tiles_end - tiles_per) * tm
    n_tiles = (2 * n) // tm + N_EXPERTS
    n_valid = tiles_end[-1]
    tile_ids = jnp.minimum(jnp.arange(n_tiles, dtype=jnp.int32), n_valid - 1)
    tile_expert = jnp.sum(tile_ids[:, None] >= tiles_end[None, :], axis=1).astype(jnp.int32)
    n_slots = n_tiles * tm
    slot = _slots(offset, route, n_slots).reshape(-1)

    xs = _sc_scatter(h.reshape(2 * n, PACK_W), slot, 2 * n_slots, reps=2)
    ys = _experts(tile_expert + layer * N_EXPERTS, n_valid.reshape(1),
                  xs.reshape(2, n_slots, PACK_W), w_gate.reshape(-1, d, MOE_D_FF),
                  w_up.reshape(-1, d, MOE_D_FF), w_down.reshape(-1, MOE_D_FF, d))
    y = _sc_gather(ys.reshape(2 * n_slots, PACK_W), slot)
    return y.reshape(2, 2, n, PACK_W)


def _proj_prep_pending(x2, pending, *args, **kwargs):
    outs = _proj_prep(x2, *args, pending=pending, **kwargs)
    return (x2, outs) if pending is None else (outs[0], outs[1:])


def _mixer_ab(x2, pending, router, bsz, seq, g, w_in, a_qn, a_kn, b_qn, b_kn, b_sink, w_out):
    aw = A_HEADS * HEAD_DIM
    bqw = B_HEADS * HEAD_DIM
    bkw = B_KV_HEADS * HEAD_DIM
    scale2 = QK_SCALE * LOG2E
    segs = [_Seg(0, aw, True, False, scale2, "plain"),
            _Seg(aw, aw, True, False, 1.0, "T"),
            _Seg(2 * aw, aw, False, False, 1.0, "heads_ones"),
            _Seg(3 * aw, bqw, True, False, scale2, "plain"),
            _Seg(3 * aw + bqw, bkw, True, False, 1.0, "T"),
            _Seg(3 * aw + bqw + bkw, bkw, False, False, 1.0, "heads_ones")]
    x2, (qa, ka, va, qb, kb, vb) = _proj_prep_pending(
        x2, pending, bsz, seq, g, w_in.astype(_BF16), segs, [a_qn, a_kn, None, b_qn, b_kn, None])

    bound_a = HEAD_DIM * scale2 * jnp.max(jnp.abs(a_qn)) * jnp.max(jnp.abs(a_kn))
    bound_b = HEAD_DIM * scale2 * jnp.max(jnp.abs(b_qn)) * jnp.max(jnp.abs(b_kn))
    static_ok = jnp.maximum(bound_a, bound_b) <= SHIFT_MAX
    shift_a = jnp.where(static_ok, bound_a, 0.0)
    shift_b = jnp.where(static_ok, bound_b, 0.0)
    slopes = _alibi_slopes(A_HEADS + B_HEADS)
    bias_a = _dil_bias(slopes[0::2], shift_a)
    bias_b = _band_bias(B_WINDOW, 1.0, slopes[1::2], shift_b, B_HEADS // B_KV_HEADS)
    sink2 = b_sink.astype(_F32) * LOG2E - shift_b

    def attend(online, qa, ka, va, qb, kb, vb, bias_a, bias_b, sink2):
        return (_dil_attn(qa, ka, va, bias_a, online=online),
                _band_b(qb, kb, vb, bias_b, sink2, online=online))

    oa, ob = lax.cond(static_ok, functools.partial(attend, False), functools.partial(attend, True),
                      qa, ka, va, qb, kb, vb, bias_a, bias_b, sink2)
    w_out = w_out.astype(_BF16)
    return _out_router(x2, [oa.reshape(-1, aw), ob.reshape(-1, bqw)], [w_out[:aw], w_out[aw:]],
                       *router)


def _mixer_c(x2, pending, router, bsz, seq, g, w_in, qn, kn, w_out):
    qw = C_HEADS * HEAD_DIM
    kvw = C_KV_HEADS * HEAD_DIM
    rope = _rope_tables(seq)
    bound = HEAD_DIM * QK_SCALE * LOG2E * jnp.max(jnp.abs(qn)) * jnp.max(jnp.abs(kn))
    static_ok = bound <= SHIFT_MAX
    shift = jnp.where(static_ok, bound, 0.0)
    segs = [_Seg(0, qw, True, True, QK_SCALE * LOG2E, "T_one"),
            _Seg(qw, kvw, True, True, 1.0, "heads_shift"),
            _Seg(qw + kvw, kvw, False, False, 1.0, "T_ones")]
    x2, (qt, k, vt) = _proj_prep_pending(x2, pending, bsz, seq, g, w_in.astype(_BF16), segs,
                                         [qn, kn, None], rope=rope, shift=shift)
    o = lax.cond(static_ok,
                 functools.partial(_dense_attn, online=False),
                 functools.partial(_dense_attn, online=True), qt, k, vt)
    return _out_router(x2, [o.reshape(-1, qw)], [w_out.astype(_BF16)], *router)


def kernel(x, mix_norm, ffn_norm, ab_w_in, a_q_norm, a_k_norm, b_q_norm, b_k_norm, b_sink, ab_w_out,
           c_w_in, c_q_norm, c_k_norm, c_w_out, moe_group_w, moe_group_b, moe_expert_w, moe_expert_b,
           moe_w_gate, moe_w_up, moe_w_down):
    bsz, seq, d = x.shape
    x2 = x.reshape(bsz * seq, d)
    depth = mix_norm.shape[0]
    pending = None
    for layer in range(depth):
        i = layer // 2
        router = (ffn_norm[layer],) + _router_params(moe_group_w[layer], moe_group_b[layer],
                                                     moe_expert_w[layer], moe_expert_b[layer])
        if layer % 2 == 0:
            x2, h, route, cnt = _mixer_ab(x2, pending, router, bsz, seq, mix_norm[layer], ab_w_in[i],
                                          a_q_norm[i], a_k_norm[i], b_q_norm[i], b_k_norm[i],
                                          b_sink[i], ab_w_out[i])
        else:
            x2, h, route, cnt = _mixer_c(x2, pending, router, bsz, seq, mix_norm[layer], c_w_in[i],
                                         c_q_norm[i], c_k_norm[i], c_w_out[i])
        pending = (_moe_experts(h, route, cnt, moe_w_gate, moe_w_up, moe_w_down, layer), route)
    return _combine(x2, *pending).reshape(bsz, seq, d)
```

```python
import functools
import math
from typing import NamedTuple

import jax
import jax.numpy as jnp
import numpy as np
from jax import lax
from jax.experimental import pallas as pl
from jax.experimental.pallas import tpu as pltpu
from jax.experimental.pallas import tpu_sc as plsc

HEAD_DIM = 64
LANES = 128
N_HEADS = 16
A_HEADS = 8
B_HEADS = 8
B_KV_HEADS = 2
C_HEADS = 16
C_KV_HEADS = 4
A_PATTERNS = ((128, 1), (512, 4), (2048, 16))
B_WINDOW = 128
GRID_W = 64
ROPE_THETA = 10000.0
ROPE_AXIS_DIM = HEAD_DIM // 2
ALIBI_MAX_BIAS = 8.0
RMS_EPS = 1e-6
MOE_GROUPS = 4
MOE_EXPERTS = 8
N_EXPERTS = MOE_GROUPS * MOE_EXPERTS
MOE_D_FF = 256
QK_SCALE = HEAD_DIM ** -0.5
LOG2E = math.log2(math.e)
SHIFT_MAX = 60.0
NEG_BIG = -1e30
VMEM_LIMIT = 52 * 1024 * 1024

BAND_BQ = 256
BAND_KB = 128
BAND_KW = BAND_BQ + 2 * BAND_KB
DIL_BQ = 256
DIL_SUBS = 8
DIL_REACH = max(w // 2 for w, _ in A_PATTERNS)
DIL_KW = DIL_BQ + 2 * DIL_REACH
DIL_CHUNK = 768
DENSE_BQ = 1024
DENSE_KCH = 2048
MOE_TM = 512
MOE_CHAINS = 1
ROUTE_E0 = 4
ROUTE_CHAINS = 1
ROUTE_ROWS = 48
PACK_W = 256
SC_WINDOW = 128

_BF16 = jnp.bfloat16
_F32 = jnp.float32


def _params(*sem):
    return pltpu.CompilerParams(dimension_semantics=sem, vmem_limit_bytes=VMEM_LIMIT)


def _alibi_slopes(n):
    return np.asarray(2.0 ** (-ALIBI_MAX_BIAS * np.arange(1, n + 1) / n), dtype=np.float32)


def _head_norm(y, gain, head_ones):
    y2 = y * y
    hi = y2.astype(_BF16)
    lo = (y2 - hi.astype(_F32)).astype(_BF16)
    ss = (jnp.dot(hi, head_ones, preferred_element_type=_F32)
          + jnp.dot(lo, head_ones, preferred_element_type=_F32))
    return y * lax.rsqrt(ss * (1.0 / HEAD_DIM) + RMS_EPS) * gain


def _rope(y, cos, sin):
    lane = lax.broadcasted_iota(jnp.int32, y.shape, 1)
    first = (lane % 32) < 16
    partner = jnp.where(first, pltpu.roll(y, LANES - 16, axis=1), pltpu.roll(y, 16, axis=1))
    return y * cos + partner * sin


class _Seg(NamedTuple):
    col0: int
    ncols: int
    norm: bool
    rope: bool
    scale: float
    mode: str


def _prep_slab(y, c, o_ref, seg, gain_ref, cos_ref, sin_ref, shift_ref, head_ones):
    rows = y.shape[0]
    if seg.norm:
        y = _head_norm(y, gain_ref[...], head_ones)
    if seg.rope:
        y = _rope(y, cos_ref[...], sin_ref[...])
    if seg.scale != 1.0:
        y = y * seg.scale
    sl = slice(c * LANES, (c + 1) * LANES)
    if seg.mode == "plain":
        o_ref[:, sl] = y.astype(_BF16)
    elif seg.mode == "T":
        o_ref[sl, :] = y.T.astype(_BF16)
    elif seg.mode in ("heads_ones", "heads_shift"):
        lane = lax.broadcasted_iota(jnp.int32, y.shape, 1)
        if seg.mode == "heads_ones":
            fill = jnp.ones(y.shape, _F32)
        else:
            fill = jnp.where(lane == HEAD_DIM, -shift_ref[0], 0.0)
        for k, yk in enumerate((y, pltpu.roll(y, HEAD_DIM, axis=1))):
            o_ref[2 * c + k] = jnp.where(lane < HEAD_DIM, yk, fill).astype(_BF16)
    else:
        assert seg.mode in ("T_one", "T_ones")
        yt = y.T
        if seg.mode == "T_one":
            row = lax.broadcasted_iota(jnp.int32, (HEAD_DIM, rows), 0)
            extra = jnp.where(row == 0, 1.0, 0.0)
        else:
            extra = jnp.ones((HEAD_DIM, rows), _F32)
        for k in range(2):
            ext = jnp.concatenate([yt[k * HEAD_DIM:(k + 1) * HEAD_DIM], extra], axis=0)
            o_ref[(2 * c + k) * LANES:(2 * c + k + 1) * LANES, :] = ext.astype(_BF16)


def _combine_tile(x_ref, y_ref, route_ref):
    coef = route_ref[...].T
    y0 = _load_packed(y_ref[:, 0])
    y1 = _load_packed(y_ref[:, 1])
    return x_ref[...] + coef[:, 2:3] * y0 + coef[:, 3:4] * y1


def _proj_prep_kernel(*refs, segs, combine):
    it = iter(refs)
    shift_ref = next(it) if any(s.mode == "heads_shift" for s in segs) else None
    x_ref, g_ref, w_ref = next(it), next(it), next(it)
    y_ref, route_ref = (next(it), next(it)) if combine else (None, None)
    gain_refs = [next(it) if s.norm else None for s in segs]
    cos_ref, sin_ref = (next(it), next(it)) if any(s.rope for s in segs) else (None, None)
    outs = list(it)
    if combine:
        x = _combine_tile(x_ref, y_ref, route_ref)
        outs.pop(0)[...] = x
    else:
        x = x_ref[...]
    ms = jnp.mean(x * x, axis=-1, keepdims=True)
    h = (x * lax.rsqrt(ms + RMS_EPS) * g_ref[...]).astype(_BF16)
    proj = jnp.dot(h, w_ref[...], preferred_element_type=_F32)
    head_ones = (lax.broadcasted_iota(jnp.int32, (LANES, LANES), 0) // HEAD_DIM
                 == lax.broadcasted_iota(jnp.int32, (LANES, LANES), 1) // HEAD_DIM).astype(_BF16)
    for seg, gain_ref, o_ref in zip(segs, gain_refs, outs):
        for c in range(seg.ncols // LANES):
            y = proj[:, seg.col0 + c * LANES:seg.col0 + (c + 1) * LANES]
            _prep_slab(y, c, o_ref, seg, gain_ref, cos_ref, sin_ref, shift_ref, head_ones)


def _proj_prep(x2, bsz, seq, g, w, segs, gains, rope=None, shift=None, pending=None, tm=512):
    n, d = x2.shape
    p = w.shape[1]
    nt = seq // tm
    assert seq % tm == 0
    const = lambda shape: pl.BlockSpec(shape, lambda i: (0,) * len(shape))
    row = pl.BlockSpec((tm, d), lambda i: (i, 0))
    ins, in_specs = [], []
    if shift is not None:
        ins.append(shift.reshape(1).astype(_F32))
        in_specs.append(pl.BlockSpec(memory_space=pltpu.SMEM))
    ins += [x2, g.reshape(1, d), w]
    in_specs += [row, const((1, d)), const((d, p))]
    if pending is not None:
        ins += list(pending)
        in_specs += [pl.BlockSpec((2, 2, tm, PACK_W), lambda i: (0, 0, i, 0)),
                     pl.BlockSpec((8, tm), lambda i: (0, i))]
    for seg, gain in zip(segs, gains):
        if seg.norm:
            ins.append(jnp.tile(gain.astype(_F32), LANES // HEAD_DIM).reshape(1, LANES))
            in_specs.append(const((1, LANES)))
    if rope is not None:
        ins += list(rope)
        in_specs += [pl.BlockSpec((tm, LANES), lambda i: (i % nt, 0))] * 2
    out_shape, out_specs = [], []
    if pending is not None:
        out_shape.append(jax.ShapeDtypeStruct((n, d), _F32))
        out_specs.append(row)
    for seg in segs:
        nc = seg.ncols
        if seg.mode == "plain":
            out_shape.append(jax.ShapeDtypeStruct((bsz, seq, nc), _BF16))
            out_specs.append(pl.BlockSpec((None, tm, nc), lambda i: (i // nt, i % nt, 0)))
        elif seg.mode in ("T", "T_one", "T_ones"):
            wout = nc if seg.mode == "T" else 2 * nc
            out_shape.append(jax.ShapeDtypeStruct((bsz, wout, seq), _BF16))
            out_specs.append(pl.BlockSpec((None, wout, tm), lambda i: (i // nt, 0, i % nt)))
        else:
            assert seg.mode in ("heads_ones", "heads_shift")
            nh = nc // HEAD_DIM
            out_shape.append(jax.ShapeDtypeStruct((bsz, nh, seq, LANES), _BF16))
            out_specs.append(pl.BlockSpec((None, nh, tm, LANES), lambda i: (i // nt, 0, i % nt, 0)))
    return pl.pallas_call(
        functools.partial(_proj_prep_kernel, segs=tuple(segs), combine=pending is not None),
        grid=(n // tm,),
        in_specs=in_specs, out_specs=out_specs, out_shape=out_shape,
        compiler_params=_params("parallel"),
        name="proj_prep",
    )(*ins)


def _rope_tables(seq):
    t = np.arange(seq)
    row = (t // GRID_W).astype(np.float32)
    col = (t % GRID_W).astype(np.float32)
    inv_freq = np.float32(ROPE_THETA) ** (-np.arange(0, ROPE_AXIS_DIM, 2, dtype=np.float32)
                                          / np.float32(ROPE_AXIS_DIM))
    ang_r = row[:, None] * inv_freq[None, :]
    ang_c = col[:, None] * inv_freq[None, :]
    cr, sr, cc, sc = np.cos(ang_r), np.sin(ang_r), np.cos(ang_c), np.sin(ang_c)
    cos = np.concatenate([cr, cr, cc, cc], axis=-1)
    sin = np.concatenate([-sr, sr, -sc, sc], axis=-1)
    return (jnp.asarray(np.tile(cos, (1, 2)), _F32), jnp.asarray(np.tile(sin, (1, 2)), _F32))


def _band_bias(hw, dist_scale, slopes, shift, stack):
    bq, kb = BAND_BQ, BAND_KB
    assert hw <= kb
    r = np.arange(bq)[:, None]
    c = np.arange(BAND_KW)[None, :]
    rel = np.abs(c - kb - r)
    dist = rel.astype(np.float32) * np.float32(dist_scale)
    ok = np.stack([(rel <= hw) & ~(first & (c < kb)) & ~(last & (c >= kb + bq))
                   for first, last in ((False, False), (True, False), (False, True), (True, True))])
    alibi = -(np.asarray(slopes, np.float32)[:, None, None] * dist[None]) * np.float32(LOG2E)
    bias = jnp.where(ok[:, None], jnp.asarray(alibi)[None] - shift, NEG_BIG)
    nh = len(slopes)
    return bias.reshape(4, nh // stack, stack * bq, BAND_KW)


def _edge_variant(i, nb):
    return jnp.where(i == 0, 1, 0) + jnp.where(i == nb - 1, 2, 0)


def _window_blocks(nb):
    per = BAND_BQ // BAND_KB
    last = nb * per - 1
    fns = [lambda i: jnp.maximum(per * i - 1, 0)]
    fns += [functools.partial(lambda j, i: per * i + j, j) for j in range(per)]
    fns += [lambda i: jnp.minimum(per * i + per, last)]
    return fns


def _dil_bias(slopes, shift):
    r = jnp.arange(DIL_BQ, dtype=jnp.int32)[:, None]
    u = jnp.arange(DIL_BQ + 4 * DIL_REACH, dtype=jnp.int32)[None, :]
    delta = u - 2 * DIL_REACH - r
    dist = jnp.abs(delta)
    mult = sum(((delta % d == 0) & (dist <= ((w // 2) // d) * d)).astype(_F32) for w, d in A_PATTERNS)
    alibi = -(jnp.asarray(slopes, _F32)[:, None, None] * dist.astype(_F32)[None]) * LOG2E
    return jnp.where((mult > 0)[None], jnp.log2(jnp.maximum(mult, 1.0))[None] + alibi - shift, NEG_BIG)


def _dil_attn_kernel(q_ref, kt_ref, v_ref, bias_ref, o_ref, *, online):
    bq = DIL_BQ
    seq = kt_ref.shape[1]
    for sub in range(q_ref.shape[0] // bq):
        qs = slice(sub * bq, (sub + 1) * bq)
        q0 = pl.program_id(2) * q_ref.shape[0] + sub * bq
        w0 = pl.multiple_of(jnp.clip(q0 - DIL_REACH, 0, seq - DIL_KW), bq)
        u0 = pl.multiple_of(w0 - q0 + 2 * DIL_REACH, bq)
        q = q_ref[qs, :]
        outs = []
        for j in range(2):
            rows = slice(j * HEAD_DIM, (j + 1) * HEAD_DIM)
            scores = []
            for c in range(DIL_KW // DIL_CHUNK):
                ks = pl.ds(pl.multiple_of(w0 + c * DIL_CHUNK, bq), DIL_CHUNK)
                us = pl.ds(pl.multiple_of(u0 + c * DIL_CHUNK, bq), DIL_CHUNK)
                scores.append(jnp.dot(q[:, rows], kt_ref[rows, ks], preferred_element_type=_F32)
                              + bias_ref[j, :, us])
            if online:
                m = functools.reduce(jnp.maximum,
                                     [jnp.max(s, axis=-1, keepdims=True) for s in scores])
                scores = [s - m for s in scores]
            acc = jnp.zeros((bq, LANES), _F32)
            for c, s in enumerate(scores):
                ks = pl.ds(pl.multiple_of(w0 + c * DIL_CHUNK, bq), DIL_CHUNK)
                acc += jnp.dot(jnp.exp2(s).astype(_BF16), v_ref[j, ks, :],
                               preferred_element_type=_F32)
            outs.append((acc / pltpu.roll(acc, HEAD_DIM, axis=1))[:, :HEAD_DIM])
        o_ref[qs, :] = jnp.concatenate(outs, axis=1).astype(_BF16)


def _dil_attn(q, kt, v, bias, *, online):
    bsz, seq, w = q.shape
    bq = DIL_BQ * DIL_SUBS
    assert seq >= DIL_KW and seq % bq == 0
    return pl.pallas_call(
        functools.partial(_dil_attn_kernel, online=online), grid=(A_HEADS // 2, bsz, seq // bq),
        in_specs=[pl.BlockSpec((None, bq, LANES), lambda p, b, i: (b, i, p)),
                  pl.BlockSpec((None, LANES, seq), lambda p, b, i: (b, p, 0)),
                  pl.BlockSpec((None, 2, seq, LANES), lambda p, b, i: (b, p, 0, 0)),
                  pl.BlockSpec((2, DIL_BQ, bias.shape[-1]), lambda p, b, i: (p, 0, 0))],
        out_specs=pl.BlockSpec((None, bq, LANES), lambda p, b, i: (b, i, p)),
        out_shape=jax.ShapeDtypeStruct((bsz, seq, w), _BF16),
        compiler_params=_params("parallel", "parallel", "parallel"),
        name="dil_attn_online" if online else "dil_attn",
    )(q, kt, v, bias)


def _band_b_kernel(sink2_ref, sinkp_ref, q_ref, *refs, online):
    bq = q_ref.shape[0]
    g = pl.program_id(1)
    grp = B_HEADS // B_KV_HEADS
    npc = BAND_KW // BAND_KB
    k_refs, v_refs, (bias_ref, o_ref) = refs[:npc], refs[npc:2 * npc], refs[2 * npc:]
    kt = jnp.concatenate([r[...] for r in k_refs], axis=1)
    v = jnp.concatenate([r[...] for r in v_refs], axis=0)
    q = q_ref[...]
    q4 = jnp.concatenate([q[:, i * HEAD_DIM:(i + 1) * HEAD_DIM] for i in range(grp)], axis=0)
    s = jnp.dot(q4, kt, preferred_element_type=_F32) + bias_ref[...]
    if online:
        m = jnp.max(s, axis=-1, keepdims=True)
        s = s - m
    acc = jnp.dot(jnp.exp2(s).astype(_BF16), v, preferred_element_type=_F32)
    for i in range(grp):
        a = acc[i * bq:(i + 1) * bq]
        if online:
            mi = m[i * bq:(i + 1) * bq]
            sk = sink2_ref[g * grp + i]
            mm = jnp.maximum(mi, sk)
            a = a * jnp.exp2(mi - mm)
            o = a / (pltpu.roll(a, HEAD_DIM, axis=1) + jnp.exp2(sk - mm))
        else:
            o = a / (pltpu.roll(a, HEAD_DIM, axis=1) + sinkp_ref[g * grp + i])
        o_ref[:, i * HEAD_DIM:(i + 1) * HEAD_DIM] = o[:, :HEAD_DIM].astype(_BF16)


def _band_b(q, kt, v, bias, sink2, *, online):
    bsz, seq_len, w = q.shape
    bq, kb = BAND_BQ, BAND_KB
    nb = seq_len // bq
    grp = B_HEADS // B_KV_HEADS
    gw = w // B_KV_HEADS
    blocks = _window_blocks(nb)
    kspec = lambda f: pl.BlockSpec((None, HEAD_DIM, kb), lambda b, g, i: (b, g, f(i)))
    vspec = lambda f: pl.BlockSpec((None, None, kb, LANES), lambda b, g, i: (b, g, f(i), 0))
    qspec = pl.BlockSpec((None, bq, gw), lambda b, g, i: (b, i, g))
    smem = pl.BlockSpec(memory_space=pltpu.SMEM)
    return pl.pallas_call(
        functools.partial(_band_b_kernel, online=online), grid=(bsz, B_KV_HEADS, nb),
        in_specs=[smem, smem, qspec] + [kspec(f) for f in blocks] + [vspec(f) for f in blocks]
        + [pl.BlockSpec((None, None, grp * bq, BAND_KW),
                        lambda b, g, i: (_edge_variant(i, nb), g, 0, 0))],
        out_specs=qspec,
        out_shape=jax.ShapeDtypeStruct((bsz, seq_len, w), _BF16),
        compiler_params=_params("parallel", "parallel", "parallel"),
        name="band_b_online" if online else "band_b",
    )(sink2, jnp.exp2(sink2), q, *([kt] * len(blocks)), *([v] * len(blocks)), bias)


def _dense_attn_kernel(qt_ref, k_ref, vt_ref, o_ref, *, kch, online):
    bq = qt_ref.shape[1]
    seq = k_ref.shape[0]
    for c in range(C_HEADS // C_KV_HEADS):
        qt = qt_ref[c * LANES:(c + 1) * LANES, :]
        acc = jnp.zeros((LANES, bq), _F32)
        m = jnp.full((1, bq), NEG_BIG, _F32)
        for j in range(seq // kch):
            ks = slice(j * kch, (j + 1) * kch)
            st = jnp.dot(k_ref[ks, :], qt, preferred_element_type=_F32)
            if online:
                m_new = jnp.maximum(m, jnp.max(st, axis=0, keepdims=True))
                acc = acc * jnp.exp2(m - m_new)
                st = st - m_new
                m = m_new
            acc = acc + jnp.dot(vt_ref[:, ks], jnp.exp2(st).astype(_BF16),
                                preferred_element_type=_F32)
        o = acc[:HEAD_DIM] / acc[HEAD_DIM:]
        o_ref[:, c * HEAD_DIM:(c + 1) * HEAD_DIM] = o.T.astype(_BF16)


def _dense_attn(qt, k, vt, *, online, bq=DENSE_BQ, kch=DENSE_KCH):
    bsz, _, seq = qt.shape
    bq, kch = min(bq, seq), min(kch, seq)
    grp = C_HEADS // C_KV_HEADS
    return pl.pallas_call(
        functools.partial(_dense_attn_kernel, kch=kch, online=online),
        grid=(bsz, C_KV_HEADS, seq // bq),
        in_specs=[pl.BlockSpec((None, grp * LANES, bq), lambda b, g, i: (b, g, i)),
                  pl.BlockSpec((None, None, seq, LANES), lambda b, g, i: (b, g, 0, 0)),
                  pl.BlockSpec((None, LANES, seq), lambda b, g, i: (b, g, 0))],
        out_specs=pl.BlockSpec((None, bq, grp * HEAD_DIM), lambda b, g, i: (b, i, g)),
        out_shape=jax.ShapeDtypeStruct((bsz, seq, C_HEADS * HEAD_DIM), _BF16),
        compiler_params=_params("parallel", "parallel", "parallel"),
        name="dense_attn_online" if online else "dense_attn",
    )(qt, k, vt)


def _pack_pair(a, b):
    wa = lax.bitcast_convert_type(a.astype(_BF16).astype(_F32), jnp.uint32) >> 16
    wb = lax.bitcast_convert_type(b.astype(_BF16).astype(_F32), jnp.uint32) & jnp.uint32(0xFFFF0000)
    return lax.bitcast_convert_type(wa | wb, jnp.int32)


def _unpack_pair(w):
    u = lax.bitcast_convert_type(w, jnp.uint32)
    return (lax.bitcast_convert_type(u << 16, _F32),
            lax.bitcast_convert_type(u & jnp.uint32(0xFFFF0000), _F32))


def _store_packed(ref, y):
    q = PACK_W
    for j in range(2):
        ref[j] = _pack_pair(y[:, 2 * j * q:(2 * j + 1) * q], y[:, (2 * j + 1) * q:(2 * j + 2) * q])


def _load_packed(ref):
    parts = []
    for j in range(2):
        parts += list(_unpack_pair(ref[j]))
    return jnp.concatenate(parts, axis=1)


def _sc_mesh():
    return plsc.VectorSubcoreMesh(core_axis_name="core", subcore_axis_name="subcore")


def _sc_gather(table, idx):
    n = idx.shape[0]
    d = table.shape[1]

    @pl.kernel(out_type=jax.ShapeDtypeStruct((n, d), table.dtype), mesh=_sc_mesh())
    def gather(x_hbm, i_hbm, o_hbm):
        def body(i_vmem, o_vmem):
            pltpu.sync_copy(x_hbm.at[i_vmem.at[0]], o_vmem)

        pltpu.emit_pipeline(
            body, grid=(n // SC_WINDOW,),
            in_specs=[pl.BlockSpec((1, SC_WINDOW), index_map=lambda i: (0, i))],
            out_specs=[pl.BlockSpec((SC_WINDOW, d), index_map=lambda i: (i, 0))],
            core_axis_name=("core", "subcore"),
            dimension_semantics=(pltpu.PARALLEL,),
        )(i_hbm, o_hbm)

    return gather(table, idx.reshape(1, n))


def _sc_scatter(src, idx, n_out, reps):
    n = idx.shape[0]
    r2, d = src.shape
    nb = r2 // 2 // SC_WINDOW

    @pl.kernel(out_type=jax.ShapeDtypeStruct((n_out, d), src.dtype), mesh=_sc_mesh())
    def scatter(x_hbm, i_hbm, o_hbm):
        def body(x_vmem, i_vmem):
            pltpu.sync_copy(x_vmem, o_hbm.at[i_vmem.at[0]])

        pltpu.emit_pipeline(
            body, grid=(n // SC_WINDOW,),
            in_specs=[pl.BlockSpec((SC_WINDOW, d),
                                   index_map=lambda i: ((i // (reps * nb)) * nb + i % nb, 0)),
                      pl.BlockSpec((1, SC_WINDOW), index_map=lambda i: (0, i))],
            out_specs=[],
            core_axis_name=("core", "subcore"),
            dimension_semantics=(pltpu.PARALLEL,),
        )(x_hbm, i_hbm)

    return scatter(src, idx.reshape(1, n))


def _route_tile(x, rs, g_ref, w_ref, b_ref, h_ref, route_ref, cnt_ref, carry_ref):
    tm = x.shape[0]
    ms = jnp.mean(x * x, axis=-1, keepdims=True)
    h = x * lax.rsqrt(ms + RMS_EPS) * g_ref[...]
    _store_packed(h_ref.at[:, rs], h)
    h_hi = h.astype(_BF16)
    h_lo = (h - h_hi.astype(_F32)).astype(_BF16)
    hw = (jnp.dot(h_hi, w_ref[...], preferred_element_type=_F32)
          + jnp.dot(h_lo, w_ref[...], preferred_element_type=_F32))
    logits = hw[:, :LANES] + hw[:, LANES:] + b_ref[...]
    lt = logits.T[:ROUTE_ROWS]
    row = lax.broadcasted_iota(jnp.int32, (ROUTE_ROWS, tm), 0)

    lg = jnp.where(row < MOE_GROUPS, lt, NEG_BIG)
    mg = jnp.max(lg, axis=0, keepdims=True)
    zg = jnp.sum(jnp.exp(lg - mg), axis=0, keepdims=True)
    p_grp = 1.0 / zg
    g_idx = jnp.min(jnp.where(lg == mg, row, ROUTE_ROWS), axis=0, keepdims=True)

    e_row = row - ROUTE_E0
    emask = (e_row >= 0) & (e_row < N_EXPERTS) & ((e_row >> 3) == g_idx)
    le = jnp.where(emask, lt, NEG_BIG)
    m1 = jnp.max(le, axis=0, keepdims=True)
    i1 = jnp.min(jnp.where(le == m1, row, ROUTE_ROWS), axis=0, keepdims=True)
    le2 = jnp.where(row == i1, NEG_BIG, le)
    m2 = jnp.max(le2, axis=0, keepdims=True)
    i2 = jnp.min(jnp.where(le2 == m2, row, ROUTE_ROWS), axis=0, keepdims=True)
    e21 = jnp.exp(m2 - m1)
    c1 = p_grp / (1.0 + e21)
    c2 = p_grp * e21 / (1.0 + e21)

    onehot = jnp.where(row == i1, 1.0, jnp.where(row == i2, 1.0, 0.0)).astype(_BF16)
    upper = (lax.broadcasted_iota(jnp.int32, (tm, tm), 0)
             <= lax.broadcasted_iota(jnp.int32, (tm, tm), 1)).astype(_BF16)
    cum = jnp.dot(onehot, upper, preferred_element_type=_F32) + carry_ref[...]
    r1 = jnp.sum(jnp.where(row == i1, cum, 0.0), axis=0, keepdims=True) - 1.0
    r2 = jnp.sum(jnp.where(row == i2, cum, 0.0), axis=0, keepdims=True) - 1.0
    total = jnp.max(cum, axis=1, keepdims=True)
    carry_ref[...] = total
    cnt_ref[...] = jnp.broadcast_to(total, cnt_ref.shape)

    rows = ((i1 - ROUTE_E0).astype(_F32), (i2 - ROUTE_E0).astype(_F32), c1, c2, r1, r2)
    rrow = lax.broadcasted_iota(jnp.int32, (8, tm), 0)
    route = jnp.zeros((8, tm), _F32)
    for k, val in enumerate(rows):
        route = jnp.where(rrow == k, val, route)
    route_ref[:, rs] = route


def _out_router_kernel(*refs, n_in):
    x_ref = refs[0]
    o_refs = refs[1:1 + n_in]
    w_refs = refs[1 + n_in:1 + 2 * n_in]
    g_ref, wr_ref, br_ref, xnew_ref = refs[1 + 2 * n_in:5 + 2 * n_in]
    route_refs = refs[5 + 2 * n_in:]
    carry_ref = route_refs[-1]

    @pl.when(pl.program_id(0) == 0)
    def _():
        carry_ref[...] = jnp.zeros(carry_ref.shape, _F32)

    rows = x_ref.shape[0] // ROUTE_CHAINS
    for r in range(ROUTE_CHAINS):
        rs = slice(r * rows, (r + 1) * rows)
        acc = x_ref[rs]
        for o_ref, w_ref in zip(o_refs, w_refs):
            acc = acc + jnp.dot(o_ref[rs], w_ref[...], preferred_element_type=_F32)
        xnew_ref[rs] = acc
        _route_tile(acc, rs, g_ref, wr_ref, br_ref, *route_refs)


def _out_router(x2, os_, ws, g, w_router, b_router, tm=1024):
    n, d = x2.shape
    w_hi = w_router.astype(_BF16)
    w_lo = (w_router - w_hi.astype(_F32)).astype(_BF16)
    const = lambda shape: pl.BlockSpec(shape, lambda i: (0,) * len(shape))
    row = pl.BlockSpec((tm, d), lambda i: (i, 0))
    in_specs = [row] + [pl.BlockSpec((tm, o.shape[1]), lambda i: (i, 0)) for o in os_]
    in_specs += [const(w.shape) for w in ws]
    in_specs += [const((1, d)), const((d, 2 * LANES)), const((1, LANES))]
    return pl.pallas_call(
        functools.partial(_out_router_kernel, n_in=len(os_)), grid=(n // tm,),
        in_specs=in_specs,
        out_specs=[row, pl.BlockSpec((2, tm, PACK_W), lambda i: (0, i, 0)),
                   pl.BlockSpec((8, tm), lambda i: (0, i)), const((ROUTE_ROWS, LANES))],
        out_shape=[jax.ShapeDtypeStruct((n, d), _F32),
                   jax.ShapeDtypeStruct((2, n, PACK_W), jnp.int32),
                   jax.ShapeDtypeStruct((8, n), _F32),
                   jax.ShapeDtypeStruct((ROUTE_ROWS, LANES), _F32)],
        scratch_shapes=[pltpu.VMEM((ROUTE_ROWS, 1), _F32)],
        compiler_params=_params("arbitrary"),
        name="out_router",
    )(x2, *os_, *ws, g.reshape(1, d), jnp.concatenate([w_hi, w_lo], axis=1), b_router)


def _expert_kernel(te_ref, nv_ref, xs_ref, wg_ref, wu_ref, wd_ref, ys_ref, wgu_s, wd_s):
    j = pl.program_id(0)
    prev = te_ref[jnp.maximum(j - 1, 0)]

    @pl.when((j == 0) | (te_ref[j] != prev))
    def _():
        wgu_s[:, :MOE_D_FF] = wg_ref[...].astype(_BF16)
        wgu_s[:, MOE_D_FF:] = wu_ref[...].astype(_BF16)
        wd_s[...] = wd_ref[...].astype(_BF16)

    @pl.when(j < nv_ref[0])
    def _():
        rows = xs_ref.shape[1] // MOE_CHAINS
        for r in range(MOE_CHAINS):
            rs = slice(r * rows, (r + 1) * rows)
            xs = _load_packed(xs_ref[:, rs]).astype(_BF16)
            au = jnp.dot(xs, wgu_s[...], preferred_element_type=_F32)
            a, u = au[:, :MOE_D_FF], au[:, MOE_D_FF:]
            act = (a * (1.0 / (1.0 + jnp.exp(-a))) * u).astype(_BF16)
            _store_packed(ys_ref.at[:, rs], jnp.dot(act, wd_s[...], preferred_element_type=_F32))


def _experts(tile_expert, n_valid, xs, w_gate, w_up, w_down, tm=MOE_TM):
    _, n_slots, pw = xs.shape
    n_tiles = n_slots // tm
    d, f = w_gate.shape[-2:]
    row = lambda j, te, nv: (0, jnp.maximum(jnp.minimum(j, nv[0] - 1), 0), 0)
    grid_spec = pltpu.PrefetchScalarGridSpec(
        num_scalar_prefetch=2, grid=(n_tiles,),
        in_specs=[pl.BlockSpec((2, tm, pw), row),
                  pl.BlockSpec((None, d, f), lambda j, te, nv: (te[j], 0, 0)),
                  pl.BlockSpec((None, d, f), lambda j, te, nv: (te[j], 0, 0)),
                  pl.BlockSpec((None, f, d), lambda j, te, nv: (te[j], 0, 0))],
        out_specs=pl.BlockSpec((2, tm, pw), row),
        scratch_shapes=[pltpu.VMEM((d, 2 * f), _BF16), pltpu.VMEM((f, d), _BF16)])
    return pl.pallas_call(
        _expert_kernel, grid_spec=grid_spec,
        out_shape=jax.ShapeDtypeStruct((2, n_slots, pw), jnp.int32),
        compiler_params=_params("arbitrary"),
        name="experts",
    )(tile_expert, n_valid, xs, w_gate, w_up, w_down)


def _combine_kernel(x_ref, y_ref, route_ref, o_ref):
    o_ref[...] = _combine_tile(x_ref, y_ref, route_ref)


def _combine(x2, y, route, tm=512):
    n, d = x2.shape
    row = pl.BlockSpec((tm, d), lambda i: (i, 0))
    return pl.pallas_call(
        _combine_kernel, grid=(n // tm,),
        in_specs=[row, pl.BlockSpec((2, 2, tm, PACK_W), lambda i: (0, 0, i, 0)),
                  pl.BlockSpec((8, tm), lambda i: (0, i))],
        out_specs=row,
        out_shape=jax.ShapeDtypeStruct((n, d), _F32),
        compiler_params=_params("parallel"),
        name="moe_combine",
    )(x2, y, route)


def _slot_kernel(offset_ref, route_ref, slot_ref, *, n_slots):
    expert = route_ref[0:2, :].astype(jnp.int32)
    pos = route_ref[4:6, :].astype(jnp.int32)
    for e in range(N_EXPERTS):
        pos = pos + jnp.where(expert == e, offset_ref[e], 0)
    slot_ref[0:2, :] = pos
    slot_ref[2:4, :] = pos + n_slots


def _slots(offset, route, n_slots):
    n = route.shape[1]
    return pl.pallas_call(
        functools.partial(_slot_kernel, n_slots=n_slots),
        in_specs=[pl.BlockSpec(memory_space=pltpu.SMEM),
                  pl.BlockSpec((8, n), lambda: (0, 0))],
        out_specs=pl.BlockSpec((4, n), lambda: (0, 0)),
        out_shape=jax.ShapeDtypeStruct((4, n), jnp.int32),
        name="moe_slots",
    )(offset, route)


def _router_params(wg, bg, we, be):
    d = wg.shape[0]
    w_router = jnp.zeros((d, LANES), _F32)
    w_router = w_router.at[:, :MOE_GROUPS].set(wg)
    w_router = w_router.at[:, ROUTE_E0:ROUTE_E0 + N_EXPERTS].set(
        jnp.moveaxis(we, 0, 1).reshape(d, N_EXPERTS))
    b_router = jnp.zeros((1, LANES), _F32)
    b_router = b_router.at[0, :MOE_GROUPS].set(bg)
    b_router = b_router.at[0, ROUTE_E0:ROUTE_E0 + N_EXPERTS].set(be.reshape(-1))
    return w_router, b_router


def _moe_experts(h, route, cnt, w_gate, w_up, w_down, layer):
    n = h.shape[1]
    d = w_gate.shape[-2]
    tm = MOE_TM
    counts = cnt[ROUTE_E0:ROUTE_E0 + N_EXPERTS, 0].astype(jnp.int32)
    tiles_per = (counts + tm - 1) // tm
    tiles_end = jnp.cumsum(tiles_per)
    offset = (tiles_end - tiles_per) * tm
    n_tiles = (2 * n) // tm + N_EXPERTS
    n_valid = tiles_end[-1]
    tile_ids = jnp.minimum(jnp.arange(n_tiles, dtype=jnp.int32), n_valid - 1)
    tile_expert = jnp.sum(tile_ids[:, None] >= tiles_end[None, :], axis=1).astype(jnp.int32)
    n_slots = n_tiles * tm
    slot = _slots(offset, route, n_slots).reshape(-1)

    xs = _sc_scatter(h.reshape(2 * n, PACK_W), slot, 2 * n_slots, reps=2)
    ys = _experts(tile_expert + layer * N_EXPERTS, n_valid.reshape(1),
                  xs.reshape(2, n_slots, PACK_W), w_gate.reshape(-1, d, MOE_D_FF),
                  w_up.reshape(-1, d, MOE_D_FF), w_down.reshape(-1, MOE_D_FF, d))
    y = _sc_gather(ys.reshape(2 * n_slots, PACK_W), slot)
    return y.reshape(2, 2, n, PACK_W)


def _proj_prep_pending(x2, pending, *args, **kwargs):
    outs = _proj_prep(x2, *args, pending=pending, **kwargs)
    return (x2, outs) if pending is None else (outs[0], outs[1:])


def _mixer_ab(x2, pending, router, bsz, seq, g, w_in, a_qn, a_kn, b_qn, b_kn, b_sink, w_out):
    aw = A_HEADS * HEAD_DIM
    bqw = B_HEADS * HEAD_DIM
    bkw = B_KV_HEADS * HEAD_DIM
    scale2 = QK_SCALE * LOG2E
    segs = [_Seg(0, aw, True, False, scale2, "plain"),
            _Seg(aw, aw, True, False, 1.0, "T"),
            _Seg(2 * aw, aw, False, False, 1.0, "heads_ones"),
            _Seg(3 * aw, bqw, True, False, scale2, "plain"),
            _Seg(3 * aw + bqw, bkw, True, False, 1.0, "T"),
            _Seg(3 * aw + bqw + bkw, bkw, False, False, 1.0, "heads_ones")]
    x2, (qa, ka, va, qb, kb, vb) = _proj_prep_pending(
        x2, pending, bsz, seq, g, w_in.astype(_BF16), segs, [a_qn, a_kn, None, b_qn, b_kn, None])

    bound_a = HEAD_DIM * scale2 * jnp.max(jnp.abs(a_qn)) * jnp.max(jnp.abs(a_kn))
    bound_b = HEAD_DIM * scale2 * jnp.max(jnp.abs(b_qn)) * jnp.max(jnp.abs(b_kn))
    static_ok = jnp.maximum(bound_a, bound_b) <= SHIFT_MAX
    shift_a = jnp.where(static_ok, bound_a, 0.0)
    shift_b = jnp.where(static_ok, bound_b, 0.0)
    slopes = _alibi_slopes(A_HEADS + B_HEADS)
    bias_a = _dil_bias(slopes[0::2], shift_a)
    bias_b = _band_bias(B_WINDOW, 1.0, slopes[1::2], shift_b, B_HEADS // B_KV_HEADS)
    sink2 = b_sink.astype(_F32) * LOG2E - shift_b

    def attend(online, qa, ka, va, qb, kb, vb, bias_a, bias_b, sink2):
        return (_dil_attn(qa, ka, va, bias_a, online=online),
                _band_b(qb, kb, vb, bias_b, sink2, online=online))

    oa, ob = lax.cond(static_ok, functools.partial(attend, False), functools.partial(attend, True),
                      qa, ka, va, qb, kb, vb, bias_a, bias_b, sink2)
    w_out = w_out.astype(_BF16)
    return _out_router(x2, [oa.reshape(-1, aw), ob.reshape(-1, bqw)], [w_out[:aw], w_out[aw:]],
                       *router)


def _mixer_c(x2, pending, router, bsz, seq, g, w_in, qn, kn, w_out):
    qw = C_HEADS * HEAD_DIM
    kvw = C_KV_HEADS * HEAD_DIM
    rope = _rope_tables(seq)
    bound = HEAD_DIM * QK_SCALE * LOG2E * jnp.max(jnp.abs(qn)) * jnp.max(jnp.abs(kn))
    static_ok = bound <= SHIFT_MAX
    shift = jnp.where(static_ok, bound, 0.0)
    segs = [_Seg(0, qw, True, True, QK_SCALE * LOG2E, "T_one"),
            _Seg(qw, kvw, True, True, 1.0, "heads_shift"),
            _Seg(qw + kvw, kvw, False, False, 1.0, "T_ones")]
    x2, (qt, k, vt) = _proj_prep_pending(x2, pending, bsz, seq, g, w_in.astype(_BF16), segs,
                                         [qn, kn, None], rope=rope, shift=shift)
    o = lax.cond(static_ok,
                 functools.partial(_dense_attn, online=False),
                 functools.partial(_dense_attn, online=True), qt, k, vt)
    return _out_router(x2, [o.reshape(-1, qw)], [w_out.astype(_BF16)], *router)


def kernel(x, mix_norm, ffn_norm, ab_w_in, a_q_norm, a_k_norm, b_q_norm, b_k_norm, b_sink, ab_w_out,
           c_w_in, c_q_norm, c_k_norm, c_w_out, moe_group_w, moe_group_b, moe_expert_w, moe_expert_b,
           moe_w_gate, moe_w_up, moe_w_down):
    bsz, seq, d = x.shape
    x2 = x.reshape(bsz * seq, d)
    depth = mix_norm.shape[0]
    pending = None
    for layer in range(depth):
        i = layer // 2
        router = (ffn_norm[layer],) + _router_params(moe_group_w[layer], moe_group_b[layer],
                                                     moe_expert_w[layer], moe_expert_b[layer])
        if layer % 2 == 0:
            x2, h, route, cnt = _mixer_ab(x2, pending, router, bsz, seq, mix_norm[layer], ab_w_in[i],
                                          a_q_norm[i], a_k_norm[i], b_q_norm[i], b_k_norm[i],
                                          b_sink[i], ab_w_out[i])
        else:
            x2, h, route, cnt = _mixer_c(x2, pending, router, bsz, seq, mix_norm[layer], c_w_in[i],
                                         c_q_norm[i], c_k_norm[i], c_w_out[i])
        pending = (_moe_experts(h, route, cnt, moe_w_gate, moe_w_up, moe_w_down, layer), route)
    return _combine(x2, *pending).reshape(bsz, seq, d)
```

```python
import functools
import math
from typing import NamedTuple

import jax
import jax.numpy as jnp
import numpy as np
from jax import lax
from jax.experimental import pallas as pl
from jax.experimental.pallas import tpu as pltpu
from jax.experimental.pallas import tpu_sc as plsc

HEAD_DIM = 64
LANES = 128
N_HEADS = 16
A_HEADS = 8
B_HEADS = 8
B_KV_HEADS = 2
C_HEADS = 16
C_KV_HEADS = 4
A_PATTERNS = ((128, 1), (512, 4), (2048, 16))
B_WINDOW = 128
GRID_W = 64
ROPE_THETA = 10000.0
ROPE_AXIS_DIM = HEAD_DIM // 2
ALIBI_MAX_BIAS = 8.0
RMS_EPS = 1e-6
MOE_GROUPS = 4
MOE_EXPERTS = 8
N_EXPERTS = MOE_GROUPS * MOE_EXPERTS
MOE_D_FF = 256
QK_SCALE = HEAD_DIM ** -0.5
LOG2E = math.log2(math.e)
SHIFT_MAX = 60.0
NEG_BIG = -1e30
VMEM_LIMIT = 52 * 1024 * 1024

BAND_BQ = 256
BAND_KB = 128
BAND_KW = BAND_BQ + 2 * BAND_KB
BAND_SUBS = 4
DIL_BQ = 256
DIL_SUBS = 8
DIL_REACH = max(w // 2 for w, _ in A_PATTERNS)
DIL_KW = DIL_BQ + 2 * DIL_REACH
DIL_CHUNK = 768
DENSE_BQ = 1024
DENSE_KCH = 2048
MOE_TM = 512
MOE_CHAINS = 1
ROUTE_E0 = 4
ROUTE_CHAINS = 1
ROUTE_ROWS = 48
PACK_W = 256
SC_WINDOW = 128

_BF16 = jnp.bfloat16
_F32 = jnp.float32


def _params(*sem):
    return pltpu.CompilerParams(dimension_semantics=sem, vmem_limit_bytes=VMEM_LIMIT)


def _alibi_slopes(n):
    return np.asarray(2.0 ** (-ALIBI_MAX_BIAS * np.arange(1, n + 1) / n), dtype=np.float32)


def _head_norm(y, gain, head_ones):
    y2 = y * y
    hi = y2.astype(_BF16)
    lo = (y2 - hi.astype(_F32)).astype(_BF16)
    ss = (jnp.dot(hi, head_ones, preferred_element_type=_F32)
          + jnp.dot(lo, head_ones, preferred_element_type=_F32))
    return y * lax.rsqrt(ss * (1.0 / HEAD_DIM) + RMS_EPS) * gain


def _rope(y, cos, sin):
    lane = lax.broadcasted_iota(jnp.int32, y.shape, 1)
    first = (lane % 32) < 16
    partner = jnp.where(first, pltpu.roll(y, LANES - 16, axis=1), pltpu.roll(y, 16, axis=1))
    return y * cos + partner * sin


class _Seg(NamedTuple):
    col0: int
    ncols: int
    norm: bool
    rope: bool
    scale: float
    mode: str


def _prep_slab(y, c, o_ref, seg, gain_ref, cos_ref, sin_ref, shift_ref, head_ones):
    rows = y.shape[0]
    if seg.norm:
        y = _head_norm(y, gain_ref[...], head_ones)
    if seg.rope:
        y = _rope(y, cos_ref[...], sin_ref[...])
    if seg.scale != 1.0:
        y = y * seg.scale
    sl = slice(c * LANES, (c + 1) * LANES)
    if seg.mode == "plain":
        o_ref[:, sl] = y.astype(_BF16)
    elif seg.mode == "T":
        o_ref[sl, :] = y.T.astype(_BF16)
    elif seg.mode in ("heads_ones", "heads_shift"):
        lane = lax.broadcasted_iota(jnp.int32, y.shape, 1)
        if seg.mode == "heads_ones":
            fill = jnp.ones(y.shape, _F32)
        else:
            fill = jnp.where(lane == HEAD_DIM, -shift_ref[0], 0.0)
        for k, yk in enumerate((y, pltpu.roll(y, HEAD_DIM, axis=1))):
            o_ref[2 * c + k] = jnp.where(lane < HEAD_DIM, yk, fill).astype(_BF16)
    else:
        assert seg.mode in ("T_one", "T_ones")
        yt = y.T
        if seg.mode == "T_one":
            row = lax.broadcasted_iota(jnp.int32, (HEAD_DIM, rows), 0)
            extra = jnp.where(row == 0, 1.0, 0.0)
        else:
            extra = jnp.ones((HEAD_DIM, rows), _F32)
        for k in range(2):
            ext = jnp.concatenate([yt[k * HEAD_DIM:(k + 1) * HEAD_DIM], extra], axis=0)
            o_ref[(2 * c + k) * LANES:(2 * c + k + 1) * LANES, :] = ext.astype(_BF16)


def _combine_tile(x_ref, y_ref, route_ref):
    coef = route_ref[...].T
    y0 = _load_packed(y_ref[:, 0])
    y1 = _load_packed(y_ref[:, 1])
    return x_ref[...] + coef[:, 2:3] * y0 + coef[:, 3:4] * y1


def _proj_prep_kernel(*refs, segs, combine):
    it = iter(refs)
    shift_ref = next(it) if any(s.mode == "heads_shift" for s in segs) else None
    x_ref, g_ref, w_ref = next(it), next(it), next(it)
    y_ref, route_ref = (next(it), next(it)) if combine else (None, None)
    gain_refs = [next(it) if s.norm else None for s in segs]
    cos_ref, sin_ref = (next(it), next(it)) if any(s.rope for s in segs) else (None, None)
    outs = list(it)
    if combine:
        x = _combine_tile(x_ref, y_ref, route_ref)
        outs.pop(0)[...] = x
    else:
        x = x_ref[...]
    ms = jnp.mean(x * x, axis=-1, keepdims=True)
    h = (x * lax.rsqrt(ms + RMS_EPS) * g_ref[...]).astype(_BF16)
    proj = jnp.dot(h, w_ref[...], preferred_element_type=_F32)
    head_ones = (lax.broadcasted_iota(jnp.int32, (LANES, LANES), 0) // HEAD_DIM
                 == lax.broadcasted_iota(jnp.int32, (LANES, LANES), 1) // HEAD_DIM).astype(_BF16)
    for seg, gain_ref, o_ref in zip(segs, gain_refs, outs):
        for c in range(seg.ncols // LANES):
            y = proj[:, seg.col0 + c * LANES:seg.col0 + (c + 1) * LANES]
            _prep_slab(y, c, o_ref, seg, gain_ref, cos_ref, sin_ref, shift_ref, head_ones)


def _proj_prep(x2, bsz, seq, g, w, segs, gains, rope=None, shift=None, pending=None, tm=512):
    n, d = x2.shape
    p = w.shape[1]
    nt = seq // tm
    assert seq % tm == 0
    const = lambda shape: pl.BlockSpec(shape, lambda i: (0,) * len(shape))
    row = pl.BlockSpec((tm, d), lambda i: (i, 0))
    ins, in_specs = [], []
    if shift is not None:
        ins.append(shift.reshape(1).astype(_F32))
        in_specs.append(pl.BlockSpec(memory_space=pltpu.SMEM))
    ins += [x2, g.reshape(1, d), w]
    in_specs += [row, const((1, d)), const((d, p))]
    if pending is not None:
        ins += list(pending)
        in_specs += [pl.BlockSpec((2, 2, tm, PACK_W), lambda i: (0, 0, i, 0)),
                     pl.BlockSpec((8, tm), lambda i: (0, i))]
    for seg, gain in zip(segs, gains):
        if seg.norm:
            ins.append(jnp.tile(gain.astype(_F32), LANES // HEAD_DIM).reshape(1, LANES))
            in_specs.append(const((1, LANES)))
    if rope is not None:
        ins += list(rope)
        in_specs += [pl.BlockSpec((tm, LANES), lambda i: (i % nt, 0))] * 2
    out_shape, out_specs = [], []
    if pending is not None:
        out_shape.append(jax.ShapeDtypeStruct((n, d), _F32))
        out_specs.append(row)
    for seg in segs:
        nc = seg.ncols
        if seg.mode == "plain":
            out_shape.append(jax.ShapeDtypeStruct((bsz, seq, nc), _BF16))
            out_specs.append(pl.BlockSpec((None, tm, nc), lambda i: (i // nt, i % nt, 0)))
        elif seg.mode in ("T", "T_one", "T_ones"):
            wout = nc if seg.mode == "T" else 2 * nc
            out_shape.append(jax.ShapeDtypeStruct((bsz, wout, seq), _BF16))
            out_specs.append(pl.BlockSpec((None, wout, tm), lambda i: (i // nt, 0, i % nt)))
        else:
            assert seg.mode in ("heads_ones", "heads_shift")
            nh = nc // HEAD_DIM
            out_shape.append(jax.ShapeDtypeStruct((bsz, nh, seq, LANES), _BF16))
            out_specs.append(pl.BlockSpec((None, nh, tm, LANES), lambda i: (i // nt, 0, i % nt, 0)))
    return pl.pallas_call(
        functools.partial(_proj_prep_kernel, segs=tuple(segs), combine=pending is not None),
        grid=(n // tm,),
        in_specs=in_specs, out_specs=out_specs, out_shape=out_shape,
        compiler_params=_params("parallel"),
        name="proj_prep",
    )(*ins)


def _rope_tables(seq):
    t = np.arange(seq)
    row = (t // GRID_W).astype(np.float32)
    col = (t % GRID_W).astype(np.float32)
    inv_freq = np.float32(ROPE_THETA) ** (-np.arange(0, ROPE_AXIS_DIM, 2, dtype=np.float32)
                                          / np.float32(ROPE_AXIS_DIM))
    ang_r = row[:, None] * inv_freq[None, :]
    ang_c = col[:, None] * inv_freq[None, :]
    cr, sr, cc, sc = np.cos(ang_r), np.sin(ang_r), np.cos(ang_c), np.sin(ang_c)
    cos = np.concatenate([cr, cr, cc, cc], axis=-1)
    sin = np.concatenate([-sr, sr, -sc, sc], axis=-1)
    return (jnp.asarray(np.tile(cos, (1, 2)), _F32), jnp.asarray(np.tile(sin, (1, 2)), _F32))


def _band_bias(hw, dist_scale, slopes, shift, stack):
    bq, kb = BAND_BQ, BAND_KB
    assert hw <= kb
    r = np.arange(bq)[:, None]
    c = np.arange(BAND_KW)[None, :]
    rel = np.abs(c - kb - r)
    dist = rel.astype(np.float32) * np.float32(dist_scale)
    ok = np.stack([(rel <= hw) & ~(first & (c < kb)) & ~(last & (c >= kb + bq))
                   for first, last in ((False, False), (True, False), (False, True), (True, True))])
    alibi = -(np.asarray(slopes, np.float32)[:, None, None] * dist[None]) * np.float32(LOG2E)
    bias = jnp.where(ok[:, None], jnp.asarray(alibi)[None] - shift, NEG_BIG)
    nh = len(slopes)
    return bias.reshape(4, nh // stack, stack * bq, BAND_KW)


def _window_blocks(nb):
    per = BAND_BQ * BAND_SUBS // BAND_KB
    last = nb * per - 1
    fns = [lambda i: jnp.maximum(per * i - 1, 0)]
    fns += [functools.partial(lambda j, i: per * i + j, j) for j in range(per)]
    fns += [lambda i: jnp.minimum(per * i + per, last)]
    return fns


def _dil_bias(slopes, shift):
    r = jnp.arange(DIL_BQ, dtype=jnp.int32)[:, None]
    u = jnp.arange(DIL_BQ + 4 * DIL_REACH, dtype=jnp.int32)[None, :]
    delta = u - 2 * DIL_REACH - r
    dist = jnp.abs(delta)
    mult = sum(((delta % d == 0) & (dist <= ((w // 2) // d) * d)).astype(_F32) for w, d in A_PATTERNS)
    alibi = -(jnp.asarray(slopes, _F32)[:, None, None] * dist.astype(_F32)[None]) * LOG2E
    return jnp.where((mult > 0)[None], jnp.log2(jnp.maximum(mult, 1.0))[None] + alibi - shift, NEG_BIG)


def _dil_attn_kernel(q_ref, kt_ref, v_ref, bias_ref, o_ref, *, online):
    bq = DIL_BQ
    seq = kt_ref.shape[1]
    for sub in range(q_ref.shape[0] // bq):
        qs = slice(sub * bq, (sub + 1) * bq)
        q0 = pl.program_id(2) * q_ref.shape[0] + sub * bq
        w0 = pl.multiple_of(jnp.clip(q0 - DIL_REACH, 0, seq - DIL_KW), bq)
        u0 = pl.multiple_of(w0 - q0 + 2 * DIL_REACH, bq)
        q = q_ref[qs, :]
        outs = []
        for j in range(2):
            rows = slice(j * HEAD_DIM, (j + 1) * HEAD_DIM)
            scores = []
            for c in range(DIL_KW // DIL_CHUNK):
                ks = pl.ds(pl.multiple_of(w0 + c * DIL_CHUNK, bq), DIL_CHUNK)
                us = pl.ds(pl.multiple_of(u0 + c * DIL_CHUNK, bq), DIL_CHUNK)
                scores.append(jnp.dot(q[:, rows], kt_ref[rows, ks], preferred_element_type=_F32)
                              + bias_ref[j, :, us])
            if online:
                m = functools.reduce(jnp.maximum,
                                     [jnp.max(s, axis=-1, keepdims=True) for s in scores])
                scores = [s - m for s in scores]
            acc = jnp.zeros((bq, LANES), _F32)
            for c, s in enumerate(scores):
                ks = pl.ds(pl.multiple_of(w0 + c * DIL_CHUNK, bq), DIL_CHUNK)
                acc += jnp.dot(jnp.exp2(s).astype(_BF16), v_ref[j, ks, :],
                               preferred_element_type=_F32)
            outs.append((acc / pltpu.roll(acc, HEAD_DIM, axis=1))[:, :HEAD_DIM])
        o_ref[qs, :] = jnp.concatenate(outs, axis=1).astype(_BF16)


def _dil_attn(q, kt, v, bias, *, online):
    bsz, seq, w = q.shape
    bq = DIL_BQ * DIL_SUBS
    assert seq >= DIL_KW and seq % bq == 0
    return pl.pallas_call(
        functools.partial(_dil_attn_kernel, online=online), grid=(A_HEADS // 2, bsz, seq // bq),
        in_specs=[pl.BlockSpec((None, bq, LANES), lambda p, b, i: (b, i, p)),
                  pl.BlockSpec((None, LANES, seq), lambda p, b, i: (b, p, 0)),
                  pl.BlockSpec((None, 2, seq, LANES), lambda p, b, i: (b, p, 0, 0)),
                  pl.BlockSpec((2, DIL_BQ, bias.shape[-1]), lambda p, b, i: (p, 0, 0))],
        out_specs=pl.BlockSpec((None, bq, LANES), lambda p, b, i: (b, i, p)),
        out_shape=jax.ShapeDtypeStruct((bsz, seq, w), _BF16),
        compiler_params=_params("parallel", "parallel", "parallel"),
        name="dil_attn_online" if online else "dil_attn",
    )(q, kt, v, bias)


def _band_b_kernel(sink2_ref, sinkp_ref, q_ref, *refs, online):
    bq = BAND_BQ
    g = pl.program_id(1)
    grp = B_HEADS // B_KV_HEADS
    npc = (BAND_BQ * BAND_SUBS + 2 * BAND_KB) // BAND_KB
    k_refs, v_refs = refs[:npc], refs[npc:2 * npc]
    bias_first, bias_mid, bias_last, o_ref = refs[2 * npc:]
    kt_all = jnp.concatenate([r[...] for r in k_refs], axis=1)
    v_all = jnp.concatenate([r[...] for r in v_refs], axis=0)
    for sub in range(BAND_SUBS):
        qs = slice(sub * bq, (sub + 1) * bq)
        ks = slice(sub * bq, sub * bq + BAND_KW)
        bias_ref = bias_first if sub == 0 else bias_last if sub == BAND_SUBS - 1 else bias_mid
        q = q_ref[qs, :]
        q4 = jnp.concatenate([q[:, i * HEAD_DIM:(i + 1) * HEAD_DIM] for i in range(grp)], axis=0)
        s = jnp.dot(q4, kt_all[:, ks], preferred_element_type=_F32) + bias_ref[...]
        if online:
            m = jnp.max(s, axis=-1, keepdims=True)
            s = s - m
        acc = jnp.dot(jnp.exp2(s).astype(_BF16), v_all[ks], preferred_element_type=_F32)
        for i in range(grp):
            a = acc[i * bq:(i + 1) * bq]
            if online:
                mi = m[i * bq:(i + 1) * bq]
                sk = sink2_ref[g * grp + i]
                mm = jnp.maximum(mi, sk)
                a = a * jnp.exp2(mi - mm)
                o = a / (pltpu.roll(a, HEAD_DIM, axis=1) + jnp.exp2(sk - mm))
            else:
                o = a / (pltpu.roll(a, HEAD_DIM, axis=1) + sinkp_ref[g * grp + i])
            o_ref[qs, i * HEAD_DIM:(i + 1) * HEAD_DIM] = o[:, :HEAD_DIM].astype(_BF16)


def _band_b(q, kt, v, bias, sink2, *, online):
    bsz, seq_len, w = q.shape
    assert BAND_SUBS >= 2
    bq, kb = BAND_BQ * BAND_SUBS, BAND_KB
    nb = seq_len // bq
    grp = B_HEADS // B_KV_HEADS
    gw = w // B_KV_HEADS
    blocks = _window_blocks(nb)
    kspec = lambda f: pl.BlockSpec((None, HEAD_DIM, kb), lambda b, g, i: (b, g, f(i)))
    vspec = lambda f: pl.BlockSpec((None, None, kb, LANES), lambda b, g, i: (b, g, f(i), 0))
    qspec = pl.BlockSpec((None, bq, gw), lambda b, g, i: (b, i, g))
    smem = pl.BlockSpec(memory_space=pltpu.SMEM)
    bspec = lambda f: pl.BlockSpec((None, None, grp * BAND_BQ, BAND_KW),
                                   lambda b, g, i: (f(i), g, 0, 0))
    variants = [lambda i: jnp.where(i == 0, 1, 0), lambda i: 0,
                lambda i: jnp.where(i == nb - 1, 2, 0)]
    return pl.pallas_call(
        functools.partial(_band_b_kernel, online=online), grid=(bsz, B_KV_HEADS, nb),
        in_specs=[smem, smem, qspec] + [kspec(f) for f in blocks] + [vspec(f) for f in blocks]
        + [bspec(f) for f in variants],
        out_specs=qspec,
        out_shape=jax.ShapeDtypeStruct((bsz, seq_len, w), _BF16),
        compiler_params=_params("parallel", "parallel", "parallel"),
        name="band_b_online" if online else "band_b",
    )(sink2, jnp.exp2(sink2), q, *([kt] * len(blocks)), *([v] * len(blocks)), bias, bias, bias)


def _dense_attn_kernel(qt_ref, k_ref, vt_ref, o_ref, *, kch, online):
    bq = qt_ref.shape[1]
    seq = k_ref.shape[0]
    for c in range(C_HEADS // C_KV_HEADS):
        qt = qt_ref[c * LANES:(c + 1) * LANES, :]
        acc = jnp.zeros((LANES, bq), _F32)
        m = jnp.full((1, bq), NEG_BIG, _F32)
        for j in range(seq // kch):
            ks = slice(j * kch, (j + 1) * kch)
            st = jnp.dot(k_ref[ks, :], qt, preferred_element_type=_F32)
            if online:
                m_new = jnp.maximum(m, jnp.max(st, axis=0, keepdims=True))
                acc = acc * jnp.exp2(m - m_new)
                st = st - m_new
                m = m_new
            acc = acc + jnp.dot(vt_ref[:, ks], jnp.exp2(st).astype(_BF16),
                                preferred_element_type=_F32)
        o = acc[:HEAD_DIM] / acc[HEAD_DIM:]
        o_ref[:, c * HEAD_DIM:(c + 1) * HEAD_DIM] = o.T.astype(_BF16)


def _dense_attn(qt, k, vt, *, online, bq=DENSE_BQ, kch=DENSE_KCH):
    bsz, _, seq = qt.shape
    bq, kch = min(bq, seq), min(kch, seq)
    grp = C_HEADS // C_KV_HEADS
    return pl.pallas_call(
        functools.partial(_dense_attn_kernel, kch=kch, online=online),
        grid=(bsz, C_KV_HEADS, seq // bq),
        in_specs=[pl.BlockSpec((None, grp * LANES, bq), lambda b, g, i: (b, g, i)),
                  pl.BlockSpec((None, None, seq, LANES), lambda b, g, i: (b, g, 0, 0)),
                  pl.BlockSpec((None, LANES, seq), lambda b, g, i: (b, g, 0))],
        out_specs=pl.BlockSpec((None, bq, grp * HEAD_DIM), lambda b, g, i: (b, i, g)),
        out_shape=jax.ShapeDtypeStruct((bsz, seq, C_HEADS * HEAD_DIM), _BF16),
        compiler_params=_params("parallel", "parallel", "parallel"),
        name="dense_attn_online" if online else "dense_attn",
    )(qt, k, vt)


def _pack_pair(a, b):
    wa = lax.bitcast_convert_type(a.astype(_BF16).astype(_F32), jnp.uint32) >> 16
    wb = lax.bitcast_convert_type(b.astype(_BF16).astype(_F32), jnp.uint32) & jnp.uint32(0xFFFF0000)
    return lax.bitcast_convert_type(wa | wb, jnp.int32)


def _unpack_pair(w):
    u = lax.bitcast_convert_type(w, jnp.uint32)
    return (lax.bitcast_convert_type(u << 16, _F32),
            lax.bitcast_convert_type(u & jnp.uint32(0xFFFF0000), _F32))


def _store_packed(ref, y):
    q = PACK_W
    for j in range(2):
        ref[j] = _pack_pair(y[:, 2 * j * q:(2 * j + 1) * q], y[:, (2 * j + 1) * q:(2 * j + 2) * q])


def _load_packed(ref):
    parts = []
    for j in range(2):
        parts += list(_unpack_pair(ref[j]))
    return jnp.concatenate(parts, axis=1)


def _sc_mesh():
    return plsc.VectorSubcoreMesh(core_axis_name="core", subcore_axis_name="subcore")


def _sc_gather(table, idx):
    n = idx.shape[0]
    d = table.shape[1]

    @pl.kernel(out_type=jax.ShapeDtypeStruct((n, d), table.dtype), mesh=_sc_mesh())
    def gather(x_hbm, i_hbm, o_hbm):
        def body(i_vmem, o_vmem):
            pltpu.sync_copy(x_hbm.at[i_vmem.at[0]], o_vmem)

        pltpu.emit_pipeline(
            body, grid=(n // SC_WINDOW,),
            in_specs=[pl.BlockSpec((1, SC_WINDOW), index_map=lambda i: (0, i))],
            out_specs=[pl.BlockSpec((SC_WINDOW, d), index_map=lambda i: (i, 0))],
            core_axis_name=("core", "subcore"),
            dimension_semantics=(pltpu.PARALLEL,),
        )(i_hbm, o_hbm)

    return gather(table, idx.reshape(1, n))


def _sc_scatter(src, idx, n_out, reps):
    n = idx.shape[0]
    r2, d = src.shape
    nb = r2 // 2 // SC_WINDOW

    @pl.kernel(out_type=jax.ShapeDtypeStruct((n_out, d), src.dtype), mesh=_sc_mesh())
    def scatter(x_hbm, i_hbm, o_hbm):
        def body(x_vmem, i_vmem):
            pltpu.sync_copy(x_vmem, o_hbm.at[i_vmem.at[0]])

        pltpu.emit_pipeline(
            body, grid=(n // SC_WINDOW,),
            in_specs=[pl.BlockSpec((SC_WINDOW, d),
                                   index_map=lambda i: ((i // (reps * nb)) * nb + i % nb, 0)),
                      pl.BlockSpec((1, SC_WINDOW), index_map=lambda i: (0, i))],
            out_specs=[],
            core_axis_name=("core", "subcore"),
            dimension_semantics=(pltpu.PARALLEL,),
        )(x_hbm, i_hbm)

    return scatter(src, idx.reshape(1, n))


def _route_tile(x, rs, g_ref, w_ref, b_ref, h_ref, route_ref, cnt_ref, carry_ref):
    tm = x.shape[0]
    ms = jnp.mean(x * x, axis=-1, keepdims=True)
    h = x * lax.rsqrt(ms + RMS_EPS) * g_ref[...]
    _store_packed(h_ref.at[:, rs], h)
    h_hi = h.astype(_BF16)
    h_lo = (h - h_hi.astype(_F32)).astype(_BF16)
    hw = (jnp.dot(h_hi, w_ref[...], preferred_element_type=_F32)
          + jnp.dot(h_lo, w_ref[...], preferred_element_type=_F32))
    logits = hw[:, :LANES] + hw[:, LANES:] + b_ref[...]
    lt = logits.T[:ROUTE_ROWS]
    row = lax.broadcasted_iota(jnp.int32, (ROUTE_ROWS, tm), 0)

    lg = jnp.where(row < MOE_GROUPS, lt, NEG_BIG)
    mg = jnp.max(lg, axis=0, keepdims=True)
    zg = jnp.sum(jnp.exp(lg - mg), axis=0, keepdims=True)
    p_grp = 1.0 / zg
    g_idx = jnp.min(jnp.where(lg == mg, row, ROUTE_ROWS), axis=0, keepdims=True)

    e_row = row - ROUTE_E0
    emask = (e_row >= 0) & (e_row < N_EXPERTS) & ((e_row >> 3) == g_idx)
    le = jnp.where(emask, lt, NEG_BIG)
    m1 = jnp.max(le, axis=0, keepdims=True)
    i1 = jnp.min(jnp.where(le == m1, row, ROUTE_ROWS), axis=0, keepdims=True)
    le2 = jnp.where(row == i1, NEG_BIG, le)
    m2 = jnp.max(le2, axis=0, keepdims=True)
    i2 = jnp.min(jnp.where(le2 == m2, row, ROUTE_ROWS), axis=0, keepdims=True)
    e21 = jnp.exp(m2 - m1)
    c1 = p_grp / (1.0 + e21)
    c2 = p_grp * e21 / (1.0 + e21)

    onehot = jnp.where(row == i1, 1.0, jnp.where(row == i2, 1.0, 0.0)).astype(_BF16)
    upper = (lax.broadcasted_iota(jnp.int32, (tm, tm), 0)
             <= lax.broadcasted_iota(jnp.int32, (tm, tm), 1)).astype(_BF16)
    cum = jnp.dot(onehot, upper, preferred_element_type=_F32) + carry_ref[...]
    r1 = jnp.sum(jnp.where(row == i1, cum, 0.0), axis=0, keepdims=True) - 1.0
    r2 = jnp.sum(jnp.where(row == i2, cum, 0.0), axis=0, keepdims=True) - 1.0
    total = jnp.max(cum, axis=1, keepdims=True)
    carry_ref[...] = total
    cnt_ref[...] = jnp.broadcast_to(total, cnt_ref.shape)

    rows = ((i1 - ROUTE_E0).astype(_F32), (i2 - ROUTE_E0).astype(_F32), c1, c2, r1, r2)
    rrow = lax.broadcasted_iota(jnp.int32, (8, tm), 0)
    route = jnp.zeros((8, tm), _F32)
    for k, val in enumerate(rows):
        route = jnp.where(rrow == k, val, route)
    route_ref[:, rs] = route


def _out_router_kernel(*refs, n_in):
    x_ref = refs[0]
    o_refs = refs[1:1 + n_in]
    w_refs = refs[1 + n_in:1 + 2 * n_in]
    g_ref, wr_ref, br_ref, xnew_ref = refs[1 + 2 * n_in:5 + 2 * n_in]
    route_refs = refs[5 + 2 * n_in:]
    carry_ref = route_refs[-1]

    @pl.when(pl.program_id(0) == 0)
    def _():
        carry_ref[...] = jnp.zeros(carry_ref.shape, _F32)

    rows = x_ref.shape[0] // ROUTE_CHAINS
    for r in range(ROUTE_CHAINS):
        rs = slice(r * rows, (r + 1) * rows)
        acc = x_ref[rs]
        for o_ref, w_ref in zip(o_refs, w_refs):
            acc = acc + jnp.dot(o_ref[rs], w_ref[...], preferred_element_type=_F32)
        xnew_ref[rs] = acc
        _route_tile(acc, rs, g_ref, wr_ref, br_ref, *route_refs)


def _out_router(x2, os_, ws, g, w_router, b_router, tm=1024):
    n, d = x2.shape
    w_hi = w_router.astype(_BF16)
    w_lo = (w_router - w_hi.astype(_F32)).astype(_BF16)
    const = lambda shape: pl.BlockSpec(shape, lambda i: (0,) * len(shape))
    row = pl.BlockSpec((tm, d), lambda i: (i, 0))
    in_specs = [row] + [pl.BlockSpec((tm, o.shape[1]), lambda i: (i, 0)) for o in os_]
    in_specs += [const(w.shape) for w in ws]
    in_specs += [const((1, d)), const((d, 2 * LANES)), const((1, LANES))]
    return pl.pallas_call(
        functools.partial(_out_router_kernel, n_in=len(os_)), grid=(n // tm,),
        in_specs=in_specs,
        out_specs=[row, pl.BlockSpec((2, tm, PACK_W), lambda i: (0, i, 0)),
                   pl.BlockSpec((8, tm), lambda i: (0, i)), const((ROUTE_ROWS, LANES))],
        out_shape=[jax.ShapeDtypeStruct((n, d), _F32),
                   jax.ShapeDtypeStruct((2, n, PACK_W), jnp.int32),
                   jax.ShapeDtypeStruct((8, n), _F32),
                   jax.ShapeDtypeStruct((ROUTE_ROWS, LANES), _F32)],
        scratch_shapes=[pltpu.VMEM((ROUTE_ROWS, 1), _F32)],
        compiler_params=_params("arbitrary"),
        name="out_router",
    )(x2, *os_, *ws, g.reshape(1, d), jnp.concatenate([w_hi, w_lo], axis=1), b_router)


def _expert_kernel(te_ref, nv_ref, xs_ref, wg_ref, wu_ref, wd_ref, ys_ref, wgu_s, wd_s):
    j = pl.program_id(0)
    prev = te_ref[jnp.maximum(j - 1, 0)]

    @pl.when((j == 0) | (te_ref[j] != prev))
    def _():
        wgu_s[:, :MOE_D_FF] = wg_ref[...].astype(_BF16)
        wgu_s[:, MOE_D_FF:] = wu_ref[...].astype(_BF16)
        wd_s[...] = wd_ref[...].astype(_BF16)

    @pl.when(j < nv_ref[0])
    def _():
        rows = xs_ref.shape[1] // MOE_CHAINS
        for r in range(MOE_CHAINS):
            rs = slice(r * rows, (r + 1) * rows)
            xs = _load_packed(xs_ref[:, rs]).astype(_BF16)
            au = jnp.dot(xs, wgu_s[...], preferred_element_type=_F32)
            a, u = au[:, :MOE_D_FF], au[:, MOE_D_FF:]
            act = (a * (1.0 / (1.0 + jnp.exp(-a))) * u).astype(_BF16)
            _store_packed(ys_ref.at[:, rs], jnp.dot(act, wd_s[...], preferred_element_type=_F32))


def _experts(tile_expert, n_valid, xs, w_gate, w_up, w_down, tm=MOE_TM):
    _, n_slots, pw = xs.shape
    n_tiles = n_slots // tm
    d, f = w_gate.shape[-2:]
    row = lambda j, te, nv: (0, jnp.maximum(jnp.minimum(j, nv[0] - 1), 0), 0)
    grid_spec = pltpu.PrefetchScalarGridSpec(
        num_scalar_prefetch=2, grid=(n_tiles,),
        in_specs=[pl.BlockSpec((2, tm, pw), row),
                  pl.BlockSpec((None, d, f), lambda j, te, nv: (te[j], 0, 0)),
                  pl.BlockSpec((None, d, f), lambda j, te, nv: (te[j], 0, 0)),
                  pl.BlockSpec((None, f, d), lambda j, te, nv: (te[j], 0, 0))],
        out_specs=pl.BlockSpec((2, tm, pw), row),
        scratch_shapes=[pltpu.VMEM((d, 2 * f), _BF16), pltpu.VMEM((f, d), _BF16)])
    return pl.pallas_call(
        _expert_kernel, grid_spec=grid_spec,
        out_shape=jax.ShapeDtypeStruct((2, n_slots, pw), jnp.int32),
        compiler_params=_params("arbitrary"),
        name="experts",
    )(tile_expert, n_valid, xs, w_gate, w_up, w_down)


def _combine_kernel(x_ref, y_ref, route_ref, o_ref):
    o_ref[...] = _combine_tile(x_ref, y_ref, route_ref)


def _combine(x2, y, route, tm=512):
    n, d = x2.shape
    row = pl.BlockSpec((tm, d), lambda i: (i, 0))
    return pl.pallas_call(
        _combine_kernel, grid=(n // tm,),
        in_specs=[row, pl.BlockSpec((2, 2, tm, PACK_W), lambda i: (0, 0, i, 0)),
                  pl.BlockSpec((8, tm), lambda i: (0, i))],
        out_specs=row,
        out_shape=jax.ShapeDtypeStruct((n, d), _F32),
        compiler_params=_params("parallel"),
        name="moe_combine",
    )(x2, y, route)


def _slot_kernel(offset_ref, route_ref, slot_ref, *, n_slots):
    expert = route_ref[0:2, :].astype(jnp.int32)
    pos = route_ref[4:6, :].astype(jnp.int32)
    for e in range(N_EXPERTS):
        pos = pos + jnp.where(expert == e, offset_ref[e], 0)
    slot_ref[0:2, :] = pos
    slot_ref[2:4, :] = pos + n_slots


def _slots(offset, route, n_slots):
    n = route.shape[1]
    return pl.pallas_call(
        functools.partial(_slot_kernel, n_slots=n_slots),
        in_specs=[pl.BlockSpec(memory_space=pltpu.SMEM),
                  pl.BlockSpec((8, n), lambda: (0, 0))],
        out_specs=pl.BlockSpec((4, n), lambda: (0, 0)),
        out_shape=jax.ShapeDtypeStruct((4, n), jnp.int32),
        name="moe_slots",
    )(offset, route)


def _router_params(wg, bg, we, be):
    d = wg.shape[0]
    w_router = jnp.zeros((d, LANES), _F32)
    w_router = w_router.at[:, :MOE_GROUPS].set(wg)
    w_router = w_router.at[:, ROUTE_E0:ROUTE_E0 + N_EXPERTS].set(
        jnp.moveaxis(we, 0, 1).reshape(d, N_EXPERTS))
    b_router = jnp.zeros((1, LANES), _F32)
    b_router = b_router.at[0, :MOE_GROUPS].set(bg)
    b_router = b_router.at[0, ROUTE_E0:ROUTE_E0 + N_EXPERTS].set(be.reshape(-1))
    return w_router, b_router


def _moe_experts(h, route, cnt, w_gate, w_up, w_down, layer):
    n = h.shape[1]
    d = w_gate.shape[-2]
    tm = MOE_TM
    counts = cnt[ROUTE_E0:ROUTE_E0 + N_EXPERTS, 0].astype(jnp.int32)
    tiles_per = (counts + tm - 1) // tm
    tiles_end = jnp.cumsum(tiles_per)
    offset = (tiles_end - tiles_per) * tm
    n_tiles = (2 * n) // tm + N_EXPERTS
    n_valid = tiles_end[-1]
    tile_ids = jnp.minimum(jnp.arange(n_tiles, dtype=jnp.int32), n_valid - 1)
    tile_expert = jnp.sum(tile_ids[:, None] >= tiles_end[None, :], axis=1).astype(jnp.int32)
    n_slots = n_tiles * tm
    slot = _slots(offset, route, n_slots).reshape(-1)

    xs = _sc_scatter(h.reshape(2 * n, PACK_W), slot, 2 * n_slots, reps=2)
    ys = _experts(tile_expert + layer * N_EXPERTS, n_valid.reshape(1),
                  xs.reshape(2, n_slots, PACK_W), w_gate.reshape(-1, d, MOE_D_FF),
                  w_up.reshape(-1, d, MOE_D_FF), w_down.reshape(-1, MOE_D_FF, d))
    y = _sc_gather(ys.reshape(2 * n_slots, PACK_W), slot)
    return y.reshape(2, 2, n, PACK_W)


def _proj_prep_pending(x2, pending, *args, **kwargs):
    outs = _proj_prep(x2, *args, pending=pending, **kwargs)
    return (x2, outs) if pending is None else (outs[0], outs[1:])


def _mixer_ab(x2, pending, router, bsz, seq, g, w_in, a_qn, a_kn, b_qn, b_kn, b_sink, w_out):
    aw = A_HEADS * HEAD_DIM
    bqw = B_HEADS * HEAD_DIM
    bkw = B_KV_HEADS * HEAD_DIM
    scale2 = QK_SCALE * LOG2E
    segs = [_Seg(0, aw, True, False, scale2, "plain"),
            _Seg(aw, aw, True, False, 1.0, "T"),
            _Seg(2 * aw, aw, False, False, 1.0, "heads_ones"),
            _Seg(3 * aw, bqw, True, False, scale2, "plain"),
            _Seg(3 * aw + bqw, bkw, True, False, 1.0, "T"),
            _Seg(3 * aw + bqw + bkw, bkw, False, False, 1.0, "heads_ones")]
    x2, (qa, ka, va, qb, kb, vb) = _proj_prep_pending(
        x2, pending, bsz, seq, g, w_in.astype(_BF16), segs, [a_qn, a_kn, None, b_qn, b_kn, None])

    bound_a = HEAD_DIM * scale2 * jnp.max(jnp.abs(a_qn)) * jnp.max(jnp.abs(a_kn))
    bound_b = HEAD_DIM * scale2 * jnp.max(jnp.abs(b_qn)) * jnp.max(jnp.abs(b_kn))
    static_ok = jnp.maximum(bound_a, bound_b) <= SHIFT_MAX
    shift_a = jnp.where(static_ok, bound_a, 0.0)
    shift_b = jnp.where(static_ok, bound_b, 0.0)
    slopes = _alibi_slopes(A_HEADS + B_HEADS)
    bias_a = _dil_bias(slopes[0::2], shift_a)
    bias_b = _band_bias(B_WINDOW, 1.0, slopes[1::2], shift_b, B_HEADS // B_KV_HEADS)
    sink2 = b_sink.astype(_F32) * LOG2E - shift_b

    def attend(online, qa, ka, va, qb, kb, vb, bias_a, bias_b, sink2):
        return (_dil_attn(qa, ka, va, bias_a, online=online),
                _band_b(qb, kb, vb, bias_b, sink2, online=online))

    oa, ob = lax.cond(static_ok, functools.partial(attend, False), functools.partial(attend, True),
                      qa, ka, va, qb, kb, vb, bias_a, bias_b, sink2)
    w_out = w_out.astype(_BF16)
    return _out_router(x2, [oa.reshape(-1, aw), ob.reshape(-1, bqw)], [w_out[:aw], w_out[aw:]],
                       *router)


def _mixer_c(x2, pending, router, bsz, seq, g, w_in, qn, kn, w_out):
    qw = C_HEADS * HEAD_DIM
    kvw = C_KV_HEADS * HEAD_DIM
    rope = _rope_tables(seq)
    bound = HEAD_DIM * QK_SCALE * LOG2E * jnp.max(jnp.abs(qn)) * jnp.max(jnp.abs(kn))
    static_ok = bound <= SHIFT_MAX
    shift = jnp.where(static_ok, bound, 0.0)
    segs = [_Seg(0, qw, True, True, QK_SCALE * LOG2E, "T_one"),
            _Seg(qw, kvw, True, True, 1.0, "heads_shift"),
            _Seg(qw + kvw, kvw, False, False, 1.0, "T_ones")]
    x2, (qt, k, vt) = _proj_prep_pending(x2, pending, bsz, seq, g, w_in.astype(_BF16), segs,
                                         [qn, kn, None], rope=rope, shift=shift)
    o = lax.cond(static_ok,
                 functools.partial(_dense_attn, online=False),
                 functools.partial(_dense_attn, online=True), qt, k, vt)
    return _out_router(x2, [o.reshape(-1, qw)], [w_out.astype(_BF16)], *router)


def kernel(x, mix_norm, ffn_norm, ab_w_in, a_q_norm, a_k_norm, b_q_norm, b_k_norm, b_sink, ab_w_out,
           c_w_in, c_q_norm, c_k_norm, c_w_out, moe_group_w, moe_group_b, moe_expert_w, moe_expert_b,
           moe_w_gate, moe_w_up, moe_w_down):
    bsz, seq, d = x.shape
    x2 = x.reshape(bsz * seq, d)
    depth = mix_norm.shape[0]
    pending = None
    for layer in range(depth):
        i = layer // 2
        router = (ffn_norm[layer],) + _router_params(moe_group_w[layer], moe_group_b[layer],
                                                     moe_expert_w[layer], moe_expert_b[layer])
        if layer % 2 == 0:
            x2, h, route, cnt = _mixer_ab(x2, pending, router, bsz, seq, mix_norm[layer], ab_w_in[i],
                                          a_q_norm[i], a_k_norm[i], b_q_norm[i], b_k_norm[i],
                                          b_sink[i], ab_w_out[i])
        else:
            x2, h, route, cnt = _mixer_c(x2, pending, router, bsz, seq, mix_norm[layer], c_w_in[i],
                                         c_q_norm[i], c_k_norm[i], c_w_out[i])
        pending = (_moe_experts(h, route, cnt, moe_w_gate, moe_w_up, moe_w_down, layer), route)
    return _combine(x2, *pending).reshape(bsz, seq, d)
```

```python
import functools
import math
from typing import NamedTuple

import jax
import jax.numpy as jnp
import numpy as np
from jax import lax
from jax.experimental import pallas as pl
from jax.experimental.pallas import tpu as pltpu
from jax.experimental.pallas import tpu_sc as plsc

HEAD_DIM = 64
LANES = 128
A_HEADS = 8
B_HEADS = 8
B_KV_HEADS = 2
C_HEADS = 16
C_KV_HEADS = 4
A_PATTERNS = ((128, 1), (512, 4), (2048, 16))
B_WINDOW = 128
GRID_W = 64
ROPE_THETA = 10000.0
ROPE_AXIS_DIM = HEAD_DIM // 2
ALIBI_MAX_BIAS = 8.0
RMS_EPS = 1e-6
MOE_GROUPS = 4
MOE_EXPERTS = 8
N_EXPERTS = MOE_GROUPS * MOE_EXPERTS
MOE_D_FF = 256
QK_SCALE = HEAD_DIM ** -0.5
LOG2E = math.log2(math.e)
SHIFT_MAX = 60.0
NEG_BIG = -1e30
VMEM_LIMIT = 52 * 1024 * 1024

BAND_BQ = 256
BAND_KB = 128
BAND_KW = BAND_BQ + 2 * BAND_KB
BAND_SUBS = 8
DIL_BQ = 256
DIL_SUBS = 8
DIL_REACH = max(w // 2 for w, _ in A_PATTERNS)
DIL_KW = DIL_BQ + 2 * DIL_REACH
DIL_CHUNK = 768
DENSE_BQ = 1024
DENSE_KCH = 2048
MOE_TM = 512
MOE_CHAINS = 1
ROUTE_E0 = 4
ROUTE_CHAINS = 1
ROUTE_ROWS = 48
PACK_W = 256
SC_WINDOW = 128

_BF16 = jnp.bfloat16
_F32 = jnp.float32


def _params(*sem):
    return pltpu.CompilerParams(dimension_semantics=sem, vmem_limit_bytes=VMEM_LIMIT)


def _alibi_slopes(n):
    return np.asarray(2.0 ** (-ALIBI_MAX_BIAS * np.arange(1, n + 1) / n), dtype=np.float32)


def _head_norm(y, gain, head_ones):
    y2 = y * y
    hi = y2.astype(_BF16)
    lo = (y2 - hi.astype(_F32)).astype(_BF16)
    ss = (jnp.dot(hi, head_ones, preferred_element_type=_F32)
          + jnp.dot(lo, head_ones, preferred_element_type=_F32))
    return y * lax.rsqrt(ss * (1.0 / HEAD_DIM) + RMS_EPS) * gain


def _rope(y, cos, sin):
    lane = lax.broadcasted_iota(jnp.int32, y.shape, 1)
    first = (lane % 32) < 16
    partner = jnp.where(first, pltpu.roll(y, LANES - 16, axis=1), pltpu.roll(y, 16, axis=1))
    return y * cos + partner * sin


class _Seg(NamedTuple):
    col0: int
    ncols: int
    norm: bool
    rope: bool
    scale: float
    mode: str


def _prep_slab(y, c, o_ref, seg, gain_ref, cos_ref, sin_ref, shift_ref, head_ones):
    rows = y.shape[0]
    if seg.norm:
        y = _head_norm(y, gain_ref[...], head_ones)
    if seg.rope:
        y = _rope(y, cos_ref[...], sin_ref[...])
    if seg.scale != 1.0:
        y = y * seg.scale
    sl = slice(c * LANES, (c + 1) * LANES)
    if seg.mode == "plain":
        o_ref[:, sl] = y.astype(_BF16)
    elif seg.mode == "T":
        o_ref[sl, :] = y.T.astype(_BF16)
    elif seg.mode in ("heads_ones", "heads_shift"):
        lane = lax.broadcasted_iota(jnp.int32, y.shape, 1)
        if seg.mode == "heads_ones":
            fill = jnp.ones(y.shape, _F32)
        else:
            fill = jnp.where(lane == HEAD_DIM, -shift_ref[0], 0.0)
        for k, yk in enumerate((y, pltpu.roll(y, HEAD_DIM, axis=1))):
            o_ref[2 * c + k] = jnp.where(lane < HEAD_DIM, yk, fill).astype(_BF16)
    else:
        assert seg.mode in ("T_one", "T_ones")
        yt = y.T
        if seg.mode == "T_one":
            row = lax.broadcasted_iota(jnp.int32, (HEAD_DIM, rows), 0)
            extra = jnp.where(row == 0, 1.0, 0.0)
        else:
            extra = jnp.ones((HEAD_DIM, rows), _F32)
        for k in range(2):
            ext = jnp.concatenate([yt[k * HEAD_DIM:(k + 1) * HEAD_DIM], extra], axis=0)
            o_ref[(2 * c + k) * LANES:(2 * c + k + 1) * LANES, :] = ext.astype(_BF16)


def _combine_tile(x_ref, y_ref, route_ref):
    coef = route_ref[...].T
    y0 = _load_packed(y_ref[:, 0])
    y1 = _load_packed(y_ref[:, 1])
    return x_ref[...] + coef[:, 2:3] * y0 + coef[:, 3:4] * y1


def _proj_prep_kernel(*refs, segs, combine):
    it = iter(refs)
    shift_ref = next(it) if any(s.mode == "heads_shift" for s in segs) else None
    x_ref, g_ref, w_ref = next(it), next(it), next(it)
    y_ref, route_ref = (next(it), next(it)) if combine else (None, None)
    gain_refs = [next(it) if s.norm else None for s in segs]
    cos_ref, sin_ref = (next(it), next(it)) if any(s.rope for s in segs) else (None, None)
    outs = list(it)
    if combine:
        x = _combine_tile(x_ref, y_ref, route_ref)
        outs.pop(0)[...] = x
    else:
        x = x_ref[...]
    ms = jnp.mean(x * x, axis=-1, keepdims=True)
    h = (x * lax.rsqrt(ms + RMS_EPS) * g_ref[...]).astype(_BF16)
    proj = jnp.dot(h, w_ref[...], preferred_element_type=_F32)
    head_ones = (lax.broadcasted_iota(jnp.int32, (LANES, LANES), 0) // HEAD_DIM
                 == lax.broadcasted_iota(jnp.int32, (LANES, LANES), 1) // HEAD_DIM).astype(_BF16)
    for seg, gain_ref, o_ref in zip(segs, gain_refs, outs):
        for c in range(seg.ncols // LANES):
            y = proj[:, seg.col0 + c * LANES:seg.col0 + (c + 1) * LANES]
            _prep_slab(y, c, o_ref, seg, gain_ref, cos_ref, sin_ref, shift_ref, head_ones)


def _proj_prep(x2, bsz, seq, g, w, segs, gains, rope=None, shift=None, pending=None, tm=512):
    n, d = x2.shape
    p = w.shape[1]
    nt = seq // tm
    assert seq % tm == 0
    const = lambda shape: pl.BlockSpec(shape, lambda i: (0,) * len(shape))
    row = pl.BlockSpec((tm, d), lambda i: (i, 0))
    ins, in_specs = [], []
    if shift is not None:
        ins.append(shift.reshape(1).astype(_F32))
        in_specs.append(pl.BlockSpec(memory_space=pltpu.SMEM))
    ins += [x2, g.reshape(1, d), w]
    in_specs += [row, const((1, d)), const((d, p))]
    if pending is not None:
        ins += list(pending)
        in_specs += [pl.BlockSpec((2, 2, tm, PACK_W), lambda i: (0, 0, i, 0)),
                     pl.BlockSpec((8, tm), lambda i: (0, i))]
    for seg, gain in zip(segs, gains):
        if seg.norm:
            ins.append(jnp.tile(gain.astype(_F32), LANES // HEAD_DIM).reshape(1, LANES))
            in_specs.append(const((1, LANES)))
    if rope is not None:
        ins += list(rope)
        in_specs += [pl.BlockSpec((tm, LANES), lambda i: (i % nt, 0))] * 2
    out_shape, out_specs = [], []
    if pending is not None:
        out_shape.append(jax.ShapeDtypeStruct((n, d), _F32))
        out_specs.append(row)
    for seg in segs:
        nc = seg.ncols
        if seg.mode == "plain":
            out_shape.append(jax.ShapeDtypeStruct((bsz, seq, nc), _BF16))
            out_specs.append(pl.BlockSpec((None, tm, nc), lambda i: (i // nt, i % nt, 0)))
        elif seg.mode in ("T", "T_one", "T_ones"):
            wout = nc if seg.mode == "T" else 2 * nc
            out_shape.append(jax.ShapeDtypeStruct((bsz, wout, seq), _BF16))
            out_specs.append(pl.BlockSpec((None, wout, tm), lambda i: (i // nt, 0, i % nt)))
        else:
            assert seg.mode in ("heads_ones", "heads_shift")
            nh = nc // HEAD_DIM
            out_shape.append(jax.ShapeDtypeStruct((bsz, nh, seq, LANES), _BF16))
            out_specs.append(pl.BlockSpec((None, nh, tm, LANES), lambda i: (i // nt, 0, i % nt, 0)))
    return pl.pallas_call(
        functools.partial(_proj_prep_kernel, segs=tuple(segs), combine=pending is not None),
        grid=(n // tm,),
        in_specs=in_specs, out_specs=out_specs, out_shape=out_shape,
        compiler_params=_params("parallel"),
        name="proj_prep",
    )(*ins)


def _rope_tables(seq):
    t = np.arange(seq)
    row = (t // GRID_W).astype(np.float32)
    col = (t % GRID_W).astype(np.float32)
    inv_freq = np.float32(ROPE_THETA) ** (-np.arange(0, ROPE_AXIS_DIM, 2, dtype=np.float32)
                                          / np.float32(ROPE_AXIS_DIM))
    ang_r = row[:, None] * inv_freq[None, :]
    ang_c = col[:, None] * inv_freq[None, :]
    cr, sr, cc, sc = np.cos(ang_r), np.sin(ang_r), np.cos(ang_c), np.sin(ang_c)
    cos = np.concatenate([cr, cr, cc, cc], axis=-1)
    sin = np.concatenate([-sr, sr, -sc, sc], axis=-1)
    return (jnp.asarray(np.tile(cos, (1, 2)), _F32), jnp.asarray(np.tile(sin, (1, 2)), _F32))


def _band_bias(hw, dist_scale, slopes, shift, stack):
    bq, kb = BAND_BQ, BAND_KB
    assert hw <= kb
    r = np.arange(bq)[:, None]
    c = np.arange(BAND_KW)[None, :]
    rel = np.abs(c - kb - r)
    dist = rel.astype(np.float32) * np.float32(dist_scale)
    ok = np.stack([(rel <= hw) & ~(first & (c < kb)) & ~(last & (c >= kb + bq))
                   for first, last in ((False, False), (True, False), (False, True), (True, True))])
    alibi = -(np.asarray(slopes, np.float32)[:, None, None] * dist[None]) * np.float32(LOG2E)
    bias = jnp.where(ok[:, None], jnp.asarray(alibi)[None] - shift, NEG_BIG)
    nh = len(slopes)
    return bias.reshape(4, nh // stack, stack * bq, BAND_KW)


def _window_blocks(nb):
    per = BAND_BQ * BAND_SUBS // BAND_KB
    last = nb * per - 1
    fns = [lambda i: jnp.maximum(per * i - 1, 0)]
    fns += [functools.partial(lambda j, i: per * i + j, j) for j in range(per)]
    fns += [lambda i: jnp.minimum(per * i + per, last)]
    return fns


def _dil_bias(slopes, shift):
    r = jnp.arange(DIL_BQ, dtype=jnp.int32)[:, None]
    u = jnp.arange(DIL_BQ + 4 * DIL_REACH, dtype=jnp.int32)[None, :]
    delta = u - 2 * DIL_REACH - r
    dist = jnp.abs(delta)
    mult = sum(((delta % d == 0) & (dist <= ((w // 2) // d) * d)).astype(_F32) for w, d in A_PATTERNS)
    alibi = -(jnp.asarray(slopes, _F32)[:, None, None] * dist.astype(_F32)[None]) * LOG2E
    return jnp.where((mult > 0)[None], jnp.log2(jnp.maximum(mult, 1.0))[None] + alibi - shift, NEG_BIG)


def _dil_attn_kernel(q_ref, kt_ref, v_ref, bias_ref, o_ref, *, online):
    bq = DIL_BQ
    seq = kt_ref.shape[1]
    for sub in range(q_ref.shape[0] // bq):
        qs = slice(sub * bq, (sub + 1) * bq)
        q0 = pl.program_id(2) * q_ref.shape[0] + sub * bq
        w0 = pl.multiple_of(jnp.clip(q0 - DIL_REACH, 0, seq - DIL_KW), bq)
        u0 = pl.multiple_of(w0 - q0 + 2 * DIL_REACH, bq)
        q = q_ref[qs, :]
        outs = []
        for j in range(2):
            rows = slice(j * HEAD_DIM, (j + 1) * HEAD_DIM)
            scores = []
            for c in range(DIL_KW // DIL_CHUNK):
                ks = pl.ds(pl.multiple_of(w0 + c * DIL_CHUNK, bq), DIL_CHUNK)
                us = pl.ds(pl.multiple_of(u0 + c * DIL_CHUNK, bq), DIL_CHUNK)
                scores.append(jnp.dot(q[:, rows], kt_ref[rows, ks], preferred_element_type=_F32)
                              + bias_ref[j, :, us])
            if online:
                m = functools.reduce(jnp.maximum,
                                     [jnp.max(s, axis=-1, keepdims=True) for s in scores])
                scores = [s - m for s in scores]
            acc = jnp.zeros((bq, LANES), _F32)
            for c, s in enumerate(scores):
                ks = pl.ds(pl.multiple_of(w0 + c * DIL_CHUNK, bq), DIL_CHUNK)
                acc += jnp.dot(jnp.exp2(s).astype(_BF16), v_ref[j, ks, :],
                               preferred_element_type=_F32)
            outs.append((acc / pltpu.roll(acc, HEAD_DIM, axis=1))[:, :HEAD_DIM])
        o_ref[qs, :] = jnp.concatenate(outs, axis=1).astype(_BF16)


def _dil_attn(q, kt, v, bias, *, online):
    bsz, seq, w = q.shape
    bq = DIL_BQ * DIL_SUBS
    assert seq >= DIL_KW and seq % bq == 0
    return pl.pallas_call(
        functools.partial(_dil_attn_kernel, online=online), grid=(A_HEADS // 2, bsz, seq // bq),
        in_specs=[pl.BlockSpec((None, bq, LANES), lambda p, b, i: (b, i, p)),
                  pl.BlockSpec((None, LANES, seq), lambda p, b, i: (b, p, 0)),
                  pl.BlockSpec((None, 2, seq, LANES), lambda p, b, i: (b, p, 0, 0)),
                  pl.BlockSpec((2, DIL_BQ, bias.shape[-1]), lambda p, b, i: (p, 0, 0))],
        out_specs=pl.BlockSpec((None, bq, LANES), lambda p, b, i: (b, i, p)),
        out_shape=jax.ShapeDtypeStruct((bsz, seq, w), _BF16),
        compiler_params=_params("parallel", "parallel", "parallel"),
        name="dil_attn_online" if online else "dil_attn",
    )(q, kt, v, bias)


def _band_b_kernel(sink2_ref, sinkp_ref, q_ref, *refs, online):
    bq = BAND_BQ
    g = pl.program_id(1)
    grp = B_HEADS // B_KV_HEADS
    npc = (BAND_BQ * BAND_SUBS + 2 * BAND_KB) // BAND_KB
    k_refs, v_refs = refs[:npc], refs[npc:2 * npc]
    bias_first, bias_mid, bias_last, o_ref = refs[2 * npc:]
    kt_all = jnp.concatenate([r[...] for r in k_refs], axis=1)
    v_all = jnp.concatenate([r[...] for r in v_refs], axis=0)
    for sub in range(BAND_SUBS):
        qs = slice(sub * bq, (sub + 1) * bq)
        ks = slice(sub * bq, sub * bq + BAND_KW)
        bias_ref = bias_first if sub == 0 else bias_last if sub == BAND_SUBS - 1 else bias_mid
        q = q_ref[qs, :]
        q4 = jnp.concatenate([q[:, i * HEAD_DIM:(i + 1) * HEAD_DIM] for i in range(grp)], axis=0)
        s = jnp.dot(q4, kt_all[:, ks], preferred_element_type=_F32) + bias_ref[...]
        if online:
            m = jnp.max(s, axis=-1, keepdims=True)
            s = s - m
        acc = jnp.dot(jnp.exp2(s).astype(_BF16), v_all[ks], preferred_element_type=_F32)
        for i in range(grp):
            a = acc[i * bq:(i + 1) * bq]
            if online:
                mi = m[i * bq:(i + 1) * bq]
                sk = sink2_ref[g * grp + i]
                mm = jnp.maximum(mi, sk)
                a = a * jnp.exp2(mi - mm)
                o = a / (pltpu.roll(a, HEAD_DIM, axis=1) + jnp.exp2(sk - mm))
            else:
                o = a / (pltpu.roll(a, HEAD_DIM, axis=1) + sinkp_ref[g * grp + i])
            o_ref[qs, i * HEAD_DIM:(i + 1) * HEAD_DIM] = o[:, :HEAD_DIM].astype(_BF16)


def _band_b(q, kt, v, bias, sink2, *, online):
    bsz, seq_len, w = q.shape
    assert BAND_SUBS >= 2
    bq, kb = BAND_BQ * BAND_SUBS, BAND_KB
    nb = seq_len // bq
    grp = B_HEADS // B_KV_HEADS
    gw = w // B_KV_HEADS
    blocks = _window_blocks(nb)
    kspec = lambda f: pl.BlockSpec((None, HEAD_DIM, kb), lambda b, g, i: (b, g, f(i)))
    vspec = lambda f: pl.BlockSpec((None, None, kb, LANES), lambda b, g, i: (b, g, f(i), 0))
    qspec = pl.BlockSpec((None, bq, gw), lambda b, g, i: (b, i, g))
    smem = pl.BlockSpec(memory_space=pltpu.SMEM)
    bspec = lambda f: pl.BlockSpec((None, None, grp * BAND_BQ, BAND_KW),
                                   lambda b, g, i: (f(i), g, 0, 0))
    variants = [lambda i: jnp.where(i == 0, 1, 0), lambda i: 0,
                lambda i: jnp.where(i == nb - 1, 2, 0)]
    return pl.pallas_call(
        functools.partial(_band_b_kernel, online=online), grid=(bsz, B_KV_HEADS, nb),
        in_specs=[smem, smem, qspec] + [kspec(f) for f in blocks] + [vspec(f) for f in blocks]
        + [bspec(f) for f in variants],
        out_specs=qspec,
        out_shape=jax.ShapeDtypeStruct((bsz, seq_len, w), _BF16),
        compiler_params=_params("parallel", "parallel", "parallel"),
        name="band_b_online" if online else "band_b",
    )(sink2, jnp.exp2(sink2), q, *([kt] * len(blocks)), *([v] * len(blocks)), bias, bias, bias)


def _dense_attn_kernel(qt_ref, k_ref, vt_ref, o_ref, *, kch, online):
    bq = qt_ref.shape[1]
    seq = k_ref.shape[0]
    for c in range(C_HEADS // C_KV_HEADS):
        qt = qt_ref[c * LANES:(c + 1) * LANES, :]
        acc = jnp.zeros((LANES, bq), _F32)
        m = jnp.full((1, bq), NEG_BIG, _F32)
        for j in range(seq // kch):
            ks = slice(j * kch, (j + 1) * kch)
            st = jnp.dot(k_ref[ks, :], qt, preferred_element_type=_F32)
            if online:
                m_new = jnp.maximum(m, jnp.max(st, axis=0, keepdims=True))
                acc = acc * jnp.exp2(m - m_new)
                st = st - m_new
                m = m_new
            acc = acc + jnp.dot(vt_ref[:, ks], jnp.exp2(st).astype(_BF16),
                                preferred_element_type=_F32)
        o = acc[:HEAD_DIM] / acc[HEAD_DIM:]
        o_ref[:, c * HEAD_DIM:(c + 1) * HEAD_DIM] = o.T.astype(_BF16)


def _dense_attn(qt, k, vt, *, online, bq=DENSE_BQ, kch=DENSE_KCH):
    bsz, _, seq = qt.shape
    bq, kch = min(bq, seq), min(kch, seq)
    grp = C_HEADS // C_KV_HEADS
    return pl.pallas_call(
        functools.partial(_dense_attn_kernel, kch=kch, online=online),
        grid=(bsz, C_KV_HEADS, seq // bq),
        in_specs=[pl.BlockSpec((None, grp * LANES, bq), lambda b, g, i: (b, g, i)),
                  pl.BlockSpec((None, None, seq, LANES), lambda b, g, i: (b, g, 0, 0)),
                  pl.BlockSpec((None, LANES, seq), lambda b, g, i: (b, g, 0))],
        out_specs=pl.BlockSpec((None, bq, grp * HEAD_DIM), lambda b, g, i: (b, i, g)),
        out_shape=jax.ShapeDtypeStruct((bsz, seq, C_HEADS * HEAD_DIM), _BF16),
        compiler_params=_params("parallel", "parallel", "parallel"),
        name="dense_attn_online" if online else "dense_attn",
    )(qt, k, vt)


def _pack_pair(a, b):
    wa = lax.bitcast_convert_type(a.astype(_BF16).astype(_F32), jnp.uint32) >> 16
    wb = lax.bitcast_convert_type(b.astype(_BF16).astype(_F32), jnp.uint32) & jnp.uint32(0xFFFF0000)
    return lax.bitcast_convert_type(wa | wb, jnp.int32)


def _unpack_pair(w):
    u = lax.bitcast_convert_type(w, jnp.uint32)
    return (lax.bitcast_convert_type(u << 16, _F32),
            lax.bitcast_convert_type(u & jnp.uint32(0xFFFF0000), _F32))


def _store_packed(ref, y):
    q = PACK_W
    for j in range(2):
        ref[j] = _pack_pair(y[:, 2 * j * q:(2 * j + 1) * q], y[:, (2 * j + 1) * q:(2 * j + 2) * q])


def _load_packed(ref):
    parts = []
    for j in range(2):
        parts += list(_unpack_pair(ref[j]))
    return jnp.concatenate(parts, axis=1)


def _sc_mesh():
    return plsc.VectorSubcoreMesh(core_axis_name="core", subcore_axis_name="subcore")


def _sc_gather(table, idx):
    n = idx.shape[0]
    d = table.shape[1]

    @pl.kernel(out_type=jax.ShapeDtypeStruct((n, d), table.dtype), mesh=_sc_mesh())
    def gather(x_hbm, i_hbm, o_hbm):
        def body(i_vmem, o_vmem):
            pltpu.sync_copy(x_hbm.at[i_vmem.at[0]], o_vmem)

        pltpu.emit_pipeline(
            body, grid=(n // SC_WINDOW,),
            in_specs=[pl.BlockSpec((1, SC_WINDOW), index_map=lambda i: (0, i))],
            out_specs=[pl.BlockSpec((SC_WINDOW, d), index_map=lambda i: (i, 0))],
            core_axis_name=("core", "subcore"),
            dimension_semantics=(pltpu.PARALLEL,),
        )(i_hbm, o_hbm)

    return gather(table, idx.reshape(1, n))


def _sc_scatter(src, idx, n_out, reps):
    n = idx.shape[0]
    r2, d = src.shape
    nb = r2 // 2 // SC_WINDOW

    @pl.kernel(out_type=jax.ShapeDtypeStruct((n_out, d), src.dtype), mesh=_sc_mesh())
    def scatter(x_hbm, i_hbm, o_hbm):
        def body(x_vmem, i_vmem):
            pltpu.sync_copy(x_vmem, o_hbm.at[i_vmem.at[0]])

        pltpu.emit_pipeline(
            body, grid=(n // SC_WINDOW,),
            in_specs=[pl.BlockSpec((SC_WINDOW, d),
                                   index_map=lambda i: ((i // (reps * nb)) * nb + i % nb, 0)),
                      pl.BlockSpec((1, SC_WINDOW), index_map=lambda i: (0, i))],
            out_specs=[],
            core_axis_name=("core", "subcore"),
            dimension_semantics=(pltpu.PARALLEL,),
        )(x_hbm, i_hbm)

    return scatter(src, idx.reshape(1, n))


def _route_tile(x, rs, g_ref, w_ref, b_ref, h_ref, route_ref, cnt_ref, carry_ref):
    tm = x.shape[0]
    ms = jnp.mean(x * x, axis=-1, keepdims=True)
    h = x * lax.rsqrt(ms + RMS_EPS) * g_ref[...]
    _store_packed(h_ref.at[:, rs], h)
    h_hi = h.astype(_BF16)
    h_lo = (h - h_hi.astype(_F32)).astype(_BF16)
    hw = (jnp.dot(h_hi, w_ref[...], preferred_element_type=_F32)
          + jnp.dot(h_lo, w_ref[...], preferred_element_type=_F32))
    logits = hw[:, :LANES] + hw[:, LANES:] + b_ref[...]
    lt = logits.T[:ROUTE_ROWS]
    row = lax.broadcasted_iota(jnp.int32, (ROUTE_ROWS, tm), 0)

    lg = jnp.where(row < MOE_GROUPS, lt, NEG_BIG)
    mg = jnp.max(lg, axis=0, keepdims=True)
    zg = jnp.sum(jnp.exp(lg - mg), axis=0, keepdims=True)
    p_grp = 1.0 / zg
    g_idx = jnp.min(jnp.where(lg == mg, row, ROUTE_ROWS), axis=0, keepdims=True)

    e_row = row - ROUTE_E0
    emask = (e_row >= 0) & (e_row < N_EXPERTS) & ((e_row >> 3) == g_idx)
    le = jnp.where(emask, lt, NEG_BIG)
    m1 = jnp.max(le, axis=0, keepdims=True)
    i1 = jnp.min(jnp.where(le == m1, row, ROUTE_ROWS), axis=0, keepdims=True)
    le2 = jnp.where(row == i1, NEG_BIG, le)
    m2 = jnp.max(le2, axis=0, keepdims=True)
    i2 = jnp.min(jnp.where(le2 == m2, row, ROUTE_ROWS), axis=0, keepdims=True)
    e21 = jnp.exp(m2 - m1)
    c1 = p_grp / (1.0 + e21)
    c2 = p_grp * e21 / (1.0 + e21)

    onehot = jnp.where(row == i1, 1.0, jnp.where(row == i2, 1.0, 0.0)).astype(_BF16)
    upper = (lax.broadcasted_iota(jnp.int32, (tm, tm), 0)
             <= lax.broadcasted_iota(jnp.int32, (tm, tm), 1)).astype(_BF16)
    cum = jnp.dot(onehot, upper, preferred_element_type=_F32) + carry_ref[...]
    r1 = jnp.sum(jnp.where(row == i1, cum, 0.0), axis=0, keepdims=True) - 1.0
    r2 = jnp.sum(jnp.where(row == i2, cum, 0.0), axis=0, keepdims=True) - 1.0
    total = jnp.max(cum, axis=1, keepdims=True)
    carry_ref[...] = total
    cnt_ref[...] = jnp.broadcast_to(total, cnt_ref.shape)

    rows = ((i1 - ROUTE_E0).astype(_F32), (i2 - ROUTE_E0).astype(_F32), c1, c2, r1, r2)
    rrow = lax.broadcasted_iota(jnp.int32, (8, tm), 0)
    route = jnp.zeros((8, tm), _F32)
    for k, val in enumerate(rows):
        route = jnp.where(rrow == k, val, route)
    route_ref[:, rs] = route


def _out_router_kernel(*refs, n_in):
    x_ref = refs[0]
    o_refs = refs[1:1 + n_in]
    w_refs = refs[1 + n_in:1 + 2 * n_in]
    g_ref, wr_ref, br_ref, xnew_ref = refs[1 + 2 * n_in:5 + 2 * n_in]
    route_refs = refs[5 + 2 * n_in:]
    carry_ref = route_refs[-1]

    @pl.when(pl.program_id(0) == 0)
    def _():
        carry_ref[...] = jnp.zeros(carry_ref.shape, _F32)

    rows = x_ref.shape[0] // ROUTE_CHAINS
    for r in range(ROUTE_CHAINS):
        rs = slice(r * rows, (r + 1) * rows)
        acc = x_ref[rs]
        for o_ref, w_ref in zip(o_refs, w_refs):
            acc = acc + jnp.dot(o_ref[rs], w_ref[...], preferred_element_type=_F32)
        xnew_ref[rs] = acc
        _route_tile(acc, rs, g_ref, wr_ref, br_ref, *route_refs)


def _out_router(x2, os_, ws, g, w_router, b_router, tm=1024):
    n, d = x2.shape
    w_hi = w_router.astype(_BF16)
    w_lo = (w_router - w_hi.astype(_F32)).astype(_BF16)
    const = lambda shape: pl.BlockSpec(shape, lambda i: (0,) * len(shape))
    row = pl.BlockSpec((tm, d), lambda i: (i, 0))
    in_specs = [row] + [pl.BlockSpec((tm, o.shape[1]), lambda i: (i, 0)) for o in os_]
    in_specs += [const(w.shape) for w in ws]
    in_specs += [const((1, d)), const((d, 2 * LANES)), const((1, LANES))]
    return pl.pallas_call(
        functools.partial(_out_router_kernel, n_in=len(os_)), grid=(n // tm,),
        in_specs=in_specs,
        out_specs=[row, pl.BlockSpec((2, tm, PACK_W), lambda i: (0, i, 0)),
                   pl.BlockSpec((8, tm), lambda i: (0, i)), const((ROUTE_ROWS, LANES))],
        out_shape=[jax.ShapeDtypeStruct((n, d), _F32),
                   jax.ShapeDtypeStruct((2, n, PACK_W), jnp.int32),
                   jax.ShapeDtypeStruct((8, n), _F32),
                   jax.ShapeDtypeStruct((ROUTE_ROWS, LANES), _F32)],
        scratch_shapes=[pltpu.VMEM((ROUTE_ROWS, 1), _F32)],
        compiler_params=_params("arbitrary"),
        name="out_router",
    )(x2, *os_, *ws, g.reshape(1, d), jnp.concatenate([w_hi, w_lo], axis=1), b_router)


def _expert_kernel(te_ref, nv_ref, xs_ref, wg_ref, wu_ref, wd_ref, ys_ref, wgu_s, wd_s):
    j = pl.program_id(0)
    prev = te_ref[jnp.maximum(j - 1, 0)]

    @pl.when((j == 0) | (te_ref[j] != prev))
    def _():
        wgu_s[:, :MOE_D_FF] = wg_ref[...].astype(_BF16)
        wgu_s[:, MOE_D_FF:] = wu_ref[...].astype(_BF16)
        wd_s[...] = wd_ref[...].astype(_BF16)

    @pl.when(j < nv_ref[0])
    def _():
        rows = xs_ref.shape[1] // MOE_CHAINS
        for r in range(MOE_CHAINS):
            rs = slice(r * rows, (r + 1) * rows)
            xs = _load_packed(xs_ref[:, rs]).astype(_BF16)
            au = jnp.dot(xs, wgu_s[...], preferred_element_type=_F32)
            a, u = au[:, :MOE_D_FF], au[:, MOE_D_FF:]
            act = (a * (1.0 / (1.0 + jnp.exp(-a))) * u).astype(_BF16)
            _store_packed(ys_ref.at[:, rs], jnp.dot(act, wd_s[...], preferred_element_type=_F32))


def _experts(tile_expert, n_valid, xs, w_gate, w_up, w_down, tm=MOE_TM):
    _, n_slots, pw = xs.shape
    n_tiles = n_slots // tm
    d, f = w_gate.shape[-2:]
    row = lambda j, te, nv: (0, jnp.maximum(jnp.minimum(j, nv[0] - 1), 0), 0)
    grid_spec = pltpu.PrefetchScalarGridSpec(
        num_scalar_prefetch=2, grid=(n_tiles,),
        in_specs=[pl.BlockSpec((2, tm, pw), row),
                  pl.BlockSpec((None, d, f), lambda j, te, nv: (te[j], 0, 0)),
                  pl.BlockSpec((None, d, f), lambda j, te, nv: (te[j], 0, 0)),
                  pl.BlockSpec((None, f, d), lambda j, te, nv: (te[j], 0, 0))],
        out_specs=pl.BlockSpec((2, tm, pw), row),
        scratch_shapes=[pltpu.VMEM((d, 2 * f), _BF16), pltpu.VMEM((f, d), _BF16)])
    return pl.pallas_call(
        _expert_kernel, grid_spec=grid_spec,
        out_shape=jax.ShapeDtypeStruct((2, n_slots, pw), jnp.int32),
        compiler_params=_params("arbitrary"),
        name="experts",
    )(tile_expert, n_valid, xs, w_gate, w_up, w_down)


def _combine_kernel(x_ref, y_ref, route_ref, o_ref):
    o_ref[...] = _combine_tile(x_ref, y_ref, route_ref)


def _combine(x2, y, route, tm=512):
    n, d = x2.shape
    row = pl.BlockSpec((tm, d), lambda i: (i, 0))
    return pl.pallas_call(
        _combine_kernel, grid=(n // tm,),
        in_specs=[row, pl.BlockSpec((2, 2, tm, PACK_W), lambda i: (0, 0, i, 0)),
                  pl.BlockSpec((8, tm), lambda i: (0, i))],
        out_specs=row,
        out_shape=jax.ShapeDtypeStruct((n, d), _F32),
        compiler_params=_params("parallel"),
        name="moe_combine",
    )(x2, y, route)


def _slot_kernel(offset_ref, route_ref, slot_ref, *, n_slots):
    expert = route_ref[0:2, :].astype(jnp.int32)
    pos = route_ref[4:6, :].astype(jnp.int32)
    for e in range(N_EXPERTS):
        pos = pos + jnp.where(expert == e, offset_ref[e], 0)
    slot_ref[0:2, :] = pos
    slot_ref[2:4, :] = pos + n_slots


def _slots(offset, route, n_slots):
    n = route.shape[1]
    return pl.pallas_call(
        functools.partial(_slot_kernel, n_slots=n_slots),
        in_specs=[pl.BlockSpec(memory_space=pltpu.SMEM),
                  pl.BlockSpec((8, n), lambda: (0, 0))],
        out_specs=pl.BlockSpec((4, n), lambda: (0, 0)),
        out_shape=jax.ShapeDtypeStruct((4, n), jnp.int32),
        name="moe_slots",
    )(offset, route)


def _router_params(wg, bg, we, be):
    d = wg.shape[0]
    w_router = jnp.zeros((d, LANES), _F32)
    w_router = w_router.at[:, :MOE_GROUPS].set(wg)
    w_router = w_router.at[:, ROUTE_E0:ROUTE_E0 + N_EXPERTS].set(
        jnp.moveaxis(we, 0, 1).reshape(d, N_EXPERTS))
    b_router = jnp.zeros((1, LANES), _F32)
    b_router = b_router.at[0, :MOE_GROUPS].set(bg)
    b_router = b_router.at[0, ROUTE_E0:ROUTE_E0 + N_EXPERTS].set(be.reshape(-1))
    return w_router, b_router


def _moe_experts(h, route, cnt, w_gate, w_up, w_down, layer):
    n = h.shape[1]
    d = w_gate.shape[-2]
    tm = MOE_TM
    counts = cnt[ROUTE_E0:ROUTE_E0 + N_EXPERTS, 0].astype(jnp.int32)
    tiles_per = (counts + tm - 1) // tm
    tiles_end = jnp.cumsum(tiles_per)
    offset = (tiles_end - tiles_per) * tm
    n_tiles = (2 * n) // tm + N_EXPERTS
    n_valid = tiles_end[-1]
    tile_ids = jnp.minimum(jnp.arange(n_tiles, dtype=jnp.int32), n_valid - 1)
    tile_expert = jnp.sum(tile_ids[:, None] >= tiles_end[None, :], axis=1).astype(jnp.int32)
    n_slots = n_tiles * tm
    slot = _slots(offset, route, n_slots).reshape(-1)

    xs = _sc_scatter(h.reshape(2 * n, PACK_W), slot, 2 * n_slots, reps=2)
    ys = _experts(tile_expert + layer * N_EXPERTS, n_valid.reshape(1),
                  xs.reshape(2, n_slots, PACK_W), w_gate.reshape(-1, d, MOE_D_FF),
                  w_up.reshape(-1, d, MOE_D_FF), w_down.reshape(-1, MOE_D_FF, d))
    y = _sc_gather(ys.reshape(2 * n_slots, PACK_W), slot)
    return y.reshape(2, 2, n, PACK_W)


def _proj_prep_pending(x2, pending, *args, **kwargs):
    outs = _proj_prep(x2, *args, pending=pending, **kwargs)
    return (x2, outs) if pending is None else (outs[0], outs[1:])


def _mixer_ab(x2, pending, router, bsz, seq, g, w_in, a_qn, a_kn, b_qn, b_kn, b_sink, w_out):
    aw = A_HEADS * HEAD_DIM
    bqw = B_HEADS * HEAD_DIM
    bkw = B_KV_HEADS * HEAD_DIM
    scale2 = QK_SCALE * LOG2E
    segs = [_Seg(0, aw, True, False, scale2, "plain"),
            _Seg(aw, aw, True, False, 1.0, "T"),
            _Seg(2 * aw, aw, False, False, 1.0, "heads_ones"),
            _Seg(3 * aw, bqw, True, False, scale2, "plain"),
            _Seg(3 * aw + bqw, bkw, True, False, 1.0, "T"),
            _Seg(3 * aw + bqw + bkw, bkw, False, False, 1.0, "heads_ones")]
    x2, (qa, ka, va, qb, kb, vb) = _proj_prep_pending(
        x2, pending, bsz, seq, g, w_in.astype(_BF16), segs, [a_qn, a_kn, None, b_qn, b_kn, None])

    bound_a = HEAD_DIM * scale2 * jnp.max(jnp.abs(a_qn)) * jnp.max(jnp.abs(a_kn))
    bound_b = HEAD_DIM * scale2 * jnp.max(jnp.abs(b_qn)) * jnp.max(jnp.abs(b_kn))
    static_ok = jnp.maximum(bound_a, bound_b) <= SHIFT_MAX
    shift_a = jnp.where(static_ok, bound_a, 0.0)
    shift_b = jnp.where(static_ok, bound_b, 0.0)
    slopes = _alibi_slopes(A_HEADS + B_HEADS)
    bias_a = _dil_bias(slopes[0::2], shift_a)
    bias_b = _band_bias(B_WINDOW, 1.0, slopes[1::2], shift_b, B_HEADS // B_KV_HEADS)
    sink2 = b_sink.astype(_F32) * LOG2E - shift_b

    def attend(online, qa, ka, va, qb, kb, vb, bias_a, bias_b, sink2):
        return (_dil_attn(qa, ka, va, bias_a, online=online),
                _band_b(qb, kb, vb, bias_b, sink2, online=online))

    oa, ob = lax.cond(static_ok, functools.partial(attend, False), functools.partial(attend, True),
                      qa, ka, va, qb, kb, vb, bias_a, bias_b, sink2)
    w_out = w_out.astype(_BF16)
    return _out_router(x2, [oa.reshape(-1, aw), ob.reshape(-1, bqw)], [w_out[:aw], w_out[aw:]],
                       *router)


def _mixer_c(x2, pending, router, bsz, seq, g, w_in, qn, kn, w_out):
    qw = C_HEADS * HEAD_DIM
    kvw = C_KV_HEADS * HEAD_DIM
    rope = _rope_tables(seq)
    bound = HEAD_DIM * QK_SCALE * LOG2E * jnp.max(jnp.abs(qn)) * jnp.max(jnp.abs(kn))
    static_ok = bound <= SHIFT_MAX
    shift = jnp.where(static_ok, bound, 0.0)
    segs = [_Seg(0, qw, True, True, QK_SCALE * LOG2E, "T_one"),
            _Seg(qw, kvw, True, True, 1.0, "heads_shift"),
            _Seg(qw + kvw, kvw, False, False, 1.0, "T_ones")]
    x2, (qt, k, vt) = _proj_prep_pending(x2, pending, bsz, seq, g, w_in.astype(_BF16), segs,
                                         [qn, kn, None], rope=rope, shift=shift)
    o = lax.cond(static_ok,
                 functools.partial(_dense_attn, online=False),
                 functools.partial(_dense_attn, online=True), qt, k, vt)
    return _out_router(x2, [o.reshape(-1, qw)], [w_out.astype(_BF16)], *router)


def kernel(x, mix_norm, ffn_norm, ab_w_in, a_q_norm, a_k_norm, b_q_norm, b_k_norm, b_sink, ab_w_out,
           c_w_in, c_q_norm, c_k_norm, c_w_out, moe_group_w, moe_group_b, moe_expert_w, moe_expert_b,
           moe_w_gate, moe_w_up, moe_w_down):
    bsz, seq, d = x.shape
    x2 = x.reshape(bsz * seq, d)
    depth = mix_norm.shape[0]
    pending = None
    for layer in range(depth):
        i = layer // 2
        router = (ffn_norm[layer],) + _router_params(moe_group_w[layer], moe_group_b[layer],
                                                     moe_expert_w[layer], moe_expert_b[layer])
        if layer % 2 == 0:
            x2, h, route, cnt = _mixer_ab(x2, pending, router, bsz, seq, mix_norm[layer], ab_w_in[i],
                                          a_q_norm[i], a_k_norm[i], b_q_norm[i], b_k_norm[i],
                                          b_sink[i], ab_w_out[i])
        else:
            x2, h, route, cnt = _mixer_c(x2, pending, router, bsz, seq, mix_norm[layer], c_w_in[i],
                                         c_q_norm[i], c_k_norm[i], c_w_out[i])
        pending = (_moe_experts(h, route, cnt, moe_w_gate, moe_w_up, moe_w_down, layer), route)
    return _combine(x2, *pending).reshape(bsz, seq, d)
```

```python
import functools
import math
from typing import NamedTuple

import jax
import jax.numpy as jnp
import numpy as np
from jax import lax
from jax.experimental import pallas as pl
from jax.experimental.pallas import tpu as pltpu
from jax.experimental.pallas import tpu_sc as plsc

HEAD_DIM = 64
LANES = 128
A_HEADS = 8
B_HEADS = 8
B_KV_HEADS = 2
C_HEADS = 16
C_KV_HEADS = 4
A_PATTERNS = ((128, 1), (512, 4), (2048, 16))
B_WINDOW = 128
GRID_W = 64
ROPE_THETA = 10000.0
ROPE_AXIS_DIM = HEAD_DIM // 2
ALIBI_MAX_BIAS = 8.0
RMS_EPS = 1e-6
MOE_GROUPS = 4
MOE_EXPERTS = 8
N_EXPERTS = MOE_GROUPS * MOE_EXPERTS
MOE_D_FF = 256
QK_SCALE = HEAD_DIM ** -0.5
LOG2E = math.log2(math.e)
SHIFT_MAX = 60.0
NEG_BIG = -1e30
VMEM_LIMIT = 52 * 1024 * 1024

BAND_BQ = 256
BAND_KB = 128
BAND_KW = BAND_BQ + 2 * BAND_KB
BAND_SUBS = 8
DIL_BQ = 256
DIL_SUBS = 8
DIL_REACH = max(w // 2 for w, _ in A_PATTERNS)
DIL_KW = DIL_BQ + 2 * DIL_REACH
DIL_CHUNK = 768
DENSE_BQ = 1024
DENSE_KCH = 2048
MOE_TM = 512
MOE_CHAINS = 1
ROUTE_E0 = 4
ROUTE_CHAINS = 1
ROUTE_ROWS = 48
PACK_W = 256
SC_WINDOW = 128

_BF16 = jnp.bfloat16
_F32 = jnp.float32


def _params(*sem):
    return pltpu.CompilerParams(dimension_semantics=sem, vmem_limit_bytes=VMEM_LIMIT)


def _alibi_slopes(n):
    return np.asarray(2.0 ** (-ALIBI_MAX_BIAS * np.arange(1, n + 1) / n), dtype=np.float32)


def _head_norm(y, gain, head_ones):
    y2 = y * y
    hi = y2.astype(_BF16)
    lo = (y2 - hi.astype(_F32)).astype(_BF16)
    ss = (jnp.dot(hi, head_ones, preferred_element_type=_F32)
          + jnp.dot(lo, head_ones, preferred_element_type=_F32))
    return y * lax.rsqrt(ss * (1.0 / HEAD_DIM) + RMS_EPS) * gain


def _rope(y, cos, sin):
    lane = lax.broadcasted_iota(jnp.int32, y.shape, 1)
    first = (lane % 32) < 16
    partner = jnp.where(first, pltpu.roll(y, LANES - 16, axis=1), pltpu.roll(y, 16, axis=1))
    return y * cos + partner * sin


class _Seg(NamedTuple):
    col0: int
    ncols: int
    norm: bool
    rope: bool
    scale: float
    mode: str


def _prep_slab(y, c, o_ref, seg, gain_ref, cos_ref, sin_ref, shift_ref, head_ones):
    rows = y.shape[0]
    if seg.norm:
        y = _head_norm(y, gain_ref[...], head_ones)
    if seg.rope:
        y = _rope(y, cos_ref[...], sin_ref[...])
    if seg.scale != 1.0:
        y = y * seg.scale
    sl = slice(c * LANES, (c + 1) * LANES)
    if seg.mode == "plain":
        o_ref[:, sl] = y.astype(_BF16)
    elif seg.mode == "T":
        o_ref[sl, :] = y.T.astype(_BF16)
    elif seg.mode in ("heads_ones", "heads_shift"):
        lane = lax.broadcasted_iota(jnp.int32, y.shape, 1)
        if seg.mode == "heads_ones":
            fill = jnp.ones(y.shape, _F32)
        else:
            fill = jnp.where(lane == HEAD_DIM, -shift_ref[0], 0.0)
        for k, yk in enumerate((y, pltpu.roll(y, HEAD_DIM, axis=1))):
            o_ref[2 * c + k] = jnp.where(lane < HEAD_DIM, yk, fill).astype(_BF16)
    else:
        assert seg.mode in ("T_one", "T_ones")
        yt = y.T
        if seg.mode == "T_one":
            row = lax.broadcasted_iota(jnp.int32, (HEAD_DIM, rows), 0)
            extra = jnp.where(row == 0, 1.0, 0.0)
        else:
            extra = jnp.ones((HEAD_DIM, rows), _F32)
        for k in range(2):
            ext = jnp.concatenate([yt[k * HEAD_DIM:(k + 1) * HEAD_DIM], extra], axis=0)
            o_ref[(2 * c + k) * LANES:(2 * c + k + 1) * LANES, :] = ext.astype(_BF16)


def _combine_tile(x_ref, y_ref, route_ref):
    coef = route_ref[...].T
    y0 = _load_packed(y_ref[:, 0])
    y1 = _load_packed(y_ref[:, 1])
    return x_ref[...] + coef[:, 2:3] * y0 + coef[:, 3:4] * y1


def _proj_prep_kernel(*refs, segs, combine):
    it = iter(refs)
    shift_ref = next(it) if any(s.mode == "heads_shift" for s in segs) else None
    x_ref, g_ref, w_ref = next(it), next(it), next(it)
    y_ref, route_ref = (next(it), next(it)) if combine else (None, None)
    gain_refs = [next(it) if s.norm else None for s in segs]
    cos_ref, sin_ref = (next(it), next(it)) if any(s.rope for s in segs) else (None, None)
    outs = list(it)
    if combine:
        x = _combine_tile(x_ref, y_ref, route_ref)
        outs.pop(0)[...] = x
    else:
        x = x_ref[...]
    ms = jnp.mean(x * x, axis=-1, keepdims=True)
    h = (x * lax.rsqrt(ms + RMS_EPS) * g_ref[...]).astype(_BF16)
    proj = jnp.dot(h, w_ref[...], preferred_element_type=_F32)
    head_ones = (lax.broadcasted_iota(jnp.int32, (LANES, LANES), 0) // HEAD_DIM
                 == lax.broadcasted_iota(jnp.int32, (LANES, LANES), 1) // HEAD_DIM).astype(_BF16)
    for seg, gain_ref, o_ref in zip(segs, gain_refs, outs):
        for c in range(seg.ncols // LANES):
            y = proj[:, seg.col0 + c * LANES:seg.col0 + (c + 1) * LANES]
            _prep_slab(y, c, o_ref, seg, gain_ref, cos_ref, sin_ref, shift_ref, head_ones)


def _proj_prep(x2, bsz, seq, g, w, segs, gains, rope=None, shift=None, pending=None, tm=512):
    n, d = x2.shape
    p = w.shape[1]
    nt = seq // tm
    assert seq % tm == 0
    const = lambda shape: pl.BlockSpec(shape, lambda i: (0,) * len(shape))
    row = pl.BlockSpec((tm, d), lambda i: (i, 0))
    ins, in_specs = [], []
    if shift is not None:
        ins.append(shift.reshape(1).astype(_F32))
        in_specs.append(pl.BlockSpec(memory_space=pltpu.SMEM))
    ins += [x2, g.reshape(1, d), w]
    in_specs += [row, const((1, d)), const((d, p))]
    if pending is not None:
        ins += list(pending)
        in_specs += [pl.BlockSpec((2, 2, tm, PACK_W), lambda i: (0, 0, i, 0)),
                     pl.BlockSpec((8, tm), lambda i: (0, i))]
    for seg, gain in zip(segs, gains):
        if seg.norm:
            ins.append(jnp.tile(gain.astype(_F32), LANES // HEAD_DIM).reshape(1, LANES))
            in_specs.append(const((1, LANES)))
    if rope is not None:
        ins += list(rope)
        in_specs += [pl.BlockSpec((tm, LANES), lambda i: (i % nt, 0))] * 2
    out_shape, out_specs = [], []
    if pending is not None:
        out_shape.append(jax.ShapeDtypeStruct((n, d), _F32))
        out_specs.append(row)
    for seg in segs:
        nc = seg.ncols
        if seg.mode == "plain":
            out_shape.append(jax.ShapeDtypeStruct((bsz, seq, nc), _BF16))
            out_specs.append(pl.BlockSpec((None, tm, nc), lambda i: (i // nt, i % nt, 0)))
        elif seg.mode in ("T", "T_one", "T_ones"):
            wout = nc if seg.mode == "T" else 2 * nc
            out_shape.append(jax.ShapeDtypeStruct((bsz, wout, seq), _BF16))
            out_specs.append(pl.BlockSpec((None, wout, tm), lambda i: (i // nt, 0, i % nt)))
        else:
            assert seg.mode in ("heads_ones", "heads_shift")
            nh = nc // HEAD_DIM
            out_shape.append(jax.ShapeDtypeStruct((bsz, nh, seq, LANES), _BF16))
            out_specs.append(pl.BlockSpec((None, nh, tm, LANES), lambda i: (i // nt, 0, i % nt, 0)))
    return pl.pallas_call(
        functools.partial(_proj_prep_kernel, segs=tuple(segs), combine=pending is not None),
        grid=(n // tm,),
        in_specs=in_specs, out_specs=out_specs, out_shape=out_shape,
        compiler_params=_params("parallel"),
        name="proj_prep",
    )(*ins)


def _rope_tables(seq):
    t = np.arange(seq)
    row = (t // GRID_W).astype(np.float32)
    col = (t % GRID_W).astype(np.float32)
    inv_freq = np.float32(ROPE_THETA) ** (-np.arange(0, ROPE_AXIS_DIM, 2, dtype=np.float32)
                                          / np.float32(ROPE_AXIS_DIM))
    ang_r = row[:, None] * inv_freq[None, :]
    ang_c = col[:, None] * inv_freq[None, :]
    cr, sr, cc, sc = np.cos(ang_r), np.sin(ang_r), np.cos(ang_c), np.sin(ang_c)
    cos = np.concatenate([cr, cr, cc, cc], axis=-1)
    sin = np.concatenate([-sr, sr, -sc, sc], axis=-1)
    return (jnp.asarray(np.tile(cos, (1, 2)), _F32), jnp.asarray(np.tile(sin, (1, 2)), _F32))


def _band_bias(hw, dist_scale, slopes, shift, stack):
    bq, kb = BAND_BQ, BAND_KB
    assert hw <= kb
    r = np.arange(bq)[:, None]
    c = np.arange(BAND_KW)[None, :]
    rel = np.abs(c - kb - r)
    dist = rel.astype(np.float32) * np.float32(dist_scale)
    ok = np.stack([(rel <= hw) & ~(first & (c < kb)) & ~(last & (c >= kb + bq))
                   for first, last in ((False, False), (True, False), (False, True), (True, True))])
    alibi = -(np.asarray(slopes, np.float32)[:, None, None] * dist[None]) * np.float32(LOG2E)
    bias = jnp.where(ok[:, None], jnp.asarray(alibi)[None] - shift, NEG_BIG)
    nh = len(slopes)
    return bias.reshape(4, nh // stack, stack * bq, BAND_KW)


def _window_blocks(nb):
    per = BAND_BQ * BAND_SUBS // BAND_KB
    last = nb * per - 1
    fns = [lambda i: jnp.maximum(per * i - 1, 0)]
    fns += [functools.partial(lambda j, i: per * i + j, j) for j in range(per)]
    fns += [lambda i: jnp.minimum(per * i + per, last)]
    return fns


def _dil_bias(slopes, shift):
    r = jnp.arange(DIL_BQ, dtype=jnp.int32)[:, None]
    u = jnp.arange(DIL_BQ + 4 * DIL_REACH, dtype=jnp.int32)[None, :]
    delta = u - 2 * DIL_REACH - r
    dist = jnp.abs(delta)
    mult = sum(((delta % d == 0) & (dist <= ((w // 2) // d) * d)).astype(_F32) for w, d in A_PATTERNS)
    alibi = -(jnp.asarray(slopes, _F32)[:, None, None] * dist.astype(_F32)[None]) * LOG2E
    return jnp.where((mult > 0)[None], jnp.log2(jnp.maximum(mult, 1.0))[None] + alibi - shift, NEG_BIG)


def _dil_attn_kernel(q_ref, kt_ref, v_ref, bias_ref, o_ref, *, online):
    bq = DIL_BQ
    seq = kt_ref.shape[1]
    for sub in range(q_ref.shape[0] // bq):
        qs = slice(sub * bq, (sub + 1) * bq)
        q0 = pl.program_id(2) * q_ref.shape[0] + sub * bq
        w0 = pl.multiple_of(jnp.clip(q0 - DIL_REACH, 0, seq - DIL_KW), bq)
        u0 = pl.multiple_of(w0 - q0 + 2 * DIL_REACH, bq)
        q = q_ref[qs, :]
        outs = []
        for j in range(2):
            rows = slice(j * HEAD_DIM, (j + 1) * HEAD_DIM)
            scores = []
            for c in range(DIL_KW // DIL_CHUNK):
                ks = pl.ds(pl.multiple_of(w0 + c * DIL_CHUNK, bq), DIL_CHUNK)
                us = pl.ds(pl.multiple_of(u0 + c * DIL_CHUNK, bq), DIL_CHUNK)
                scores.append(jnp.dot(q[:, rows], kt_ref[rows, ks], preferred_element_type=_F32)
                              + bias_ref[j, :, us])
            if online:
                m = functools.reduce(jnp.maximum,
                                     [jnp.max(s, axis=-1, keepdims=True) for s in scores])
                scores = [s - m for s in scores]
            acc = jnp.zeros((bq, LANES), _F32)
            for c, s in enumerate(scores):
                ks = pl.ds(pl.multiple_of(w0 + c * DIL_CHUNK, bq), DIL_CHUNK)
                acc += jnp.dot(jnp.exp2(s).astype(_BF16), v_ref[j, ks, :],
                               preferred_element_type=_F32)
            outs.append((acc / pltpu.roll(acc, HEAD_DIM, axis=1))[:, :HEAD_DIM])
        o_ref[qs, :] = jnp.concatenate(outs, axis=1).astype(_BF16)


def _dil_attn(q, kt, v, bias, *, online):
    bsz, seq, w = q.shape
    bq = DIL_BQ * DIL_SUBS
    assert seq >= DIL_KW and seq % bq == 0
    return pl.pallas_call(
        functools.partial(_dil_attn_kernel, online=online), grid=(A_HEADS // 2, bsz, seq // bq),
        in_specs=[pl.BlockSpec((None, bq, LANES), lambda p, b, i: (b, i, p)),
                  pl.BlockSpec((None, LANES, seq), lambda p, b, i: (b, p, 0)),
                  pl.BlockSpec((None, 2, seq, LANES), lambda p, b, i: (b, p, 0, 0)),
                  pl.BlockSpec((2, DIL_BQ, bias.shape[-1]), lambda p, b, i: (p, 0, 0))],
        out_specs=pl.BlockSpec((None, bq, LANES), lambda p, b, i: (b, i, p)),
        out_shape=jax.ShapeDtypeStruct((bsz, seq, w), _BF16),
        compiler_params=_params("parallel", "parallel", "parallel"),
        name="dil_attn_online" if online else "dil_attn",
    )(q, kt, v, bias)


def _band_b_kernel(sink2_ref, sinkp_ref, q_ref, *refs, online):
    bq = BAND_BQ
    g = pl.program_id(1)
    grp = B_HEADS // B_KV_HEADS
    npc = (BAND_BQ * BAND_SUBS + 2 * BAND_KB) // BAND_KB
    k_refs, v_refs = refs[:npc], refs[npc:2 * npc]
    bias_first, bias_mid, bias_last, o_ref = refs[2 * npc:]
    kt_all = jnp.concatenate([r[...] for r in k_refs], axis=1)
    v_all = jnp.concatenate([r[...] for r in v_refs], axis=0)
    for sub in range(BAND_SUBS):
        qs = slice(sub * bq, (sub + 1) * bq)
        ks = slice(sub * bq, sub * bq + BAND_KW)
        bias_ref = bias_first if sub == 0 else bias_last if sub == BAND_SUBS - 1 else bias_mid
        q = q_ref[qs, :]
        q4 = jnp.concatenate([q[:, i * HEAD_DIM:(i + 1) * HEAD_DIM] for i in range(grp)], axis=0)
        s = jnp.dot(q4, kt_all[:, ks], preferred_element_type=_F32) + bias_ref[...]
        if online:
            m = jnp.max(s, axis=-1, keepdims=True)
            s = s - m
        acc = jnp.dot(jnp.exp2(s).astype(_BF16), v_all[ks], preferred_element_type=_F32)
        for i in range(grp):
            a = acc[i * bq:(i + 1) * bq]
            if online:
                mi = m[i * bq:(i + 1) * bq]
                sk = sink2_ref[g * grp + i]
                mm = jnp.maximum(mi, sk)
                a = a * jnp.exp2(mi - mm)
                o = a / (pltpu.roll(a, HEAD_DIM, axis=1) + jnp.exp2(sk - mm))
            else:
                o = a / (pltpu.roll(a, HEAD_DIM, axis=1) + sinkp_ref[g * grp + i])
            o_ref[qs, i * HEAD_DIM:(i + 1) * HEAD_DIM] = o[:, :HEAD_DIM].astype(_BF16)


def _band_b(q, kt, v, bias, sink2, *, online):
    bsz, seq_len, w = q.shape
    assert BAND_SUBS >= 2
    bq, kb = BAND_BQ * BAND_SUBS, BAND_KB
    nb = seq_len // bq
    grp = B_HEADS // B_KV_HEADS
    gw = w // B_KV_HEADS
    blocks = _window_blocks(nb)
    kspec = lambda f: pl.BlockSpec((None, HEAD_DIM, kb), lambda b, g, i: (b, g, f(i)))
    vspec = lambda f: pl.BlockSpec((None, None, kb, LANES), lambda b, g, i: (b, g, f(i), 0))
    qspec = pl.BlockSpec((None, bq, gw), lambda b, g, i: (b, i, g))
    smem = pl.BlockSpec(memory_space=pltpu.SMEM)
    bspec = lambda f: pl.BlockSpec((None, None, grp * BAND_BQ, BAND_KW),
                                   lambda b, g, i: (f(i), g, 0, 0))
    variants = [lambda i: jnp.where(i == 0, 1, 0), lambda i: 0,
                lambda i: jnp.where(i == nb - 1, 2, 0)]
    return pl.pallas_call(
        functools.partial(_band_b_kernel, online=online), grid=(bsz, B_KV_HEADS, nb),
        in_specs=[smem, smem, qspec] + [kspec(f) for f in blocks] + [vspec(f) for f in blocks]
        + [bspec(f) for f in variants],
        out_specs=qspec,
        out_shape=jax.ShapeDtypeStruct((bsz, seq_len, w), _BF16),
        compiler_params=_params("parallel", "parallel", "parallel"),
        name="band_b_online" if online else "band_b",
    )(sink2, jnp.exp2(sink2), q, *([kt] * len(blocks)), *([v] * len(blocks)), bias, bias, bias)


def _dense_attn_kernel(qt_ref, k_ref, vt_ref, o_ref, *, kch, online):
    bq = qt_ref.shape[1]
    seq = k_ref.shape[0]
    for c in range(C_HEADS // C_KV_HEADS):
        qt = qt_ref[c * LANES:(c + 1) * LANES, :]
        acc = jnp.zeros((LANES, bq), _F32)
        m = jnp.full((1, bq), NEG_BIG, _F32)
        for j in range(seq // kch):
            ks = slice(j * kch, (j + 1) * kch)
            st = jnp.dot(k_ref[ks, :], qt, preferred_element_type=_F32)
            if online:
                m_new = jnp.maximum(m, jnp.max(st, axis=0, keepdims=True))
                acc = acc * jnp.exp2(m - m_new)
                st = st - m_new
                m = m_new
            acc = acc + jnp.dot(vt_ref[:, ks], jnp.exp2(st).astype(_BF16),
                                preferred_element_type=_F32)
        o = acc[:HEAD_DIM] / acc[HEAD_DIM:]
        o_ref[:, c * HEAD_DIM:(c + 1) * HEAD_DIM] = o.T.astype(_BF16)


def _dense_attn(qt, k, vt, *, online, bq=DENSE_BQ, kch=DENSE_KCH):
    bsz, _, seq = qt.shape
    bq, kch = min(bq, seq), min(kch, seq)
    grp = C_HEADS // C_KV_HEADS
    return pl.pallas_call(
        functools.partial(_dense_attn_kernel, kch=kch, online=online),
        grid=(bsz, C_KV_HEADS, seq // bq),
        in_specs=[pl.BlockSpec((None, grp * LANES, bq), lambda b, g, i: (b, g, i)),
                  pl.BlockSpec((None, None, seq, LANES), lambda b, g, i: (b, g, 0, 0)),
                  pl.BlockSpec((None, LANES, seq), lambda b, g, i: (b, g, 0))],
        out_specs=pl.BlockSpec((None, bq, grp * HEAD_DIM), lambda b, g, i: (b, i, g)),
        out_shape=jax.ShapeDtypeStruct((bsz, seq, C_HEADS * HEAD_DIM), _BF16),
        compiler_params=_params("parallel", "parallel", "parallel"),
        name="dense_attn_online" if online else "dense_attn",
    )(qt, k, vt)


def _pack_pair(a, b):
    wa = lax.bitcast_convert_type(a.astype(_BF16).astype(_F32), jnp.uint32) >> 16
    wb = lax.bitcast_convert_type(b.astype(_BF16).astype(_F32), jnp.uint32) & jnp.uint32(0xFFFF0000)
    return lax.bitcast_convert_type(wa | wb, jnp.int32)


def _unpack_pair(w):
    u = lax.bitcast_convert_type(w, jnp.uint32)
    return (lax.bitcast_convert_type(u << 16, _F32),
            lax.bitcast_convert_type(u & jnp.uint32(0xFFFF0000), _F32))


def _store_packed(ref, y):
    q = PACK_W
    for j in range(2):
        ref[j] = _pack_pair(y[:, 2 * j * q:(2 * j + 1) * q], y[:, (2 * j + 1) * q:(2 * j + 2) * q])


def _load_packed(ref):
    parts = []
    for j in range(2):
        parts += list(_unpack_pair(ref[j]))
    return jnp.concatenate(parts, axis=1)


def _sc_mesh():
    return plsc.VectorSubcoreMesh(core_axis_name="core", subcore_axis_name="subcore")


def _sc_gather(table, idx):
    n = idx.shape[0]
    d = table.shape[1]

    @pl.kernel(out_type=jax.ShapeDtypeStruct((n, d), table.dtype), mesh=_sc_mesh())
    def gather(x_hbm, i_hbm, o_hbm):
        def body(i_vmem, o_vmem):
            pltpu.sync_copy(x_hbm.at[i_vmem.at[0]], o_vmem)

        pltpu.emit_pipeline(
            body, grid=(n // SC_WINDOW,),
            in_specs=[pl.BlockSpec((1, SC_WINDOW), index_map=lambda i: (0, i))],
            out_specs=[pl.BlockSpec((SC_WINDOW, d), index_map=lambda i: (i, 0))],
            core_axis_name=("core", "subcore"),
            dimension_semantics=(pltpu.PARALLEL,),
        )(i_hbm, o_hbm)

    return gather(table, idx.reshape(1, n))


def _sc_scatter(src, idx, n_out, reps):
    n = idx.shape[0]
    r2, d = src.shape
    nb = r2 // 2 // SC_WINDOW

    @pl.kernel(out_type=jax.ShapeDtypeStruct((n_out, d), src.dtype), mesh=_sc_mesh())
    def scatter(x_hbm, i_hbm, o_hbm):
        def body(x_vmem, i_vmem):
            pltpu.sync_copy(x_vmem, o_hbm.at[i_vmem.at[0]])

        pltpu.emit_pipeline(
            body, grid=(n // SC_WINDOW,),
            in_specs=[pl.BlockSpec((SC_WINDOW, d),
                                   index_map=lambda i: ((i // (reps * nb)) * nb + i % nb, 0)),
                      pl.BlockSpec((1, SC_WINDOW), index_map=lambda i: (0, i))],
            out_specs=[],
            core_axis_name=("core", "subcore"),
            dimension_semantics=(pltpu.PARALLEL,),
        )(x_hbm, i_hbm)

    return scatter(src, idx.reshape(1, n))


def _route_tile(x, rs, g_ref, w_ref, b_ref, h_ref, route_ref, cnt_ref, carry_ref):
    tm = x.shape[0]
    ms = jnp.mean(x * x, axis=-1, keepdims=True)
    h = x * lax.rsqrt(ms + RMS_EPS) * g_ref[...]
    _store_packed(h_ref.at[:, rs], h)
    h_hi = h.astype(_BF16)
    h_lo = (h - h_hi.astype(_F32)).astype(_BF16)
    hw = (jnp.dot(h_hi, w_ref[...], preferred_element_type=_F32)
          + jnp.dot(h_lo, w_ref[...], preferred_element_type=_F32))
    logits = hw[:, :LANES] + hw[:, LANES:] + b_ref[...]
    lt = logits.T[:ROUTE_ROWS]
    row = lax.broadcasted_iota(jnp.int32, (ROUTE_ROWS, tm), 0)

    lg = jnp.where(row < MOE_GROUPS, lt, NEG_BIG)
    mg = jnp.max(lg, axis=0, keepdims=True)
    zg = jnp.sum(jnp.exp(lg - mg), axis=0, keepdims=True)
    p_grp = 1.0 / zg
    g_idx = jnp.min(jnp.where(lg == mg, row, ROUTE_ROWS), axis=0, keepdims=True)

    e_row = row - ROUTE_E0
    emask = (e_row >= 0) & (e_row < N_EXPERTS) & ((e_row >> 3) == g_idx)
    le = jnp.where(emask, lt, NEG_BIG)
    m1 = jnp.max(le, axis=0, keepdims=True)
    i1 = jnp.min(jnp.where(le == m1, row, ROUTE_ROWS), axis=0, keepdims=True)
    le2 = jnp.where(row == i1, NEG_BIG, le)
    m2 = jnp.max(le2, axis=0, keepdims=True)
    i2 = jnp.min(jnp.where(le2 == m2, row, ROUTE_ROWS), axis=0, keepdims=True)
    e21 = jnp.exp(m2 - m1)
    c1 = p_grp / (1.0 + e21)
    c2 = p_grp * e21 / (1.0 + e21)

    onehot = jnp.where(row == i1, 1.0, jnp.where(row == i2, 1.0, 0.0)).astype(_BF16)
    upper = (lax.broadcasted_iota(jnp.int32, (tm, tm), 0)
             <= lax.broadcasted_iota(jnp.int32, (tm, tm), 1)).astype(_BF16)
    cum = jnp.dot(onehot, upper, preferred_element_type=_F32) + carry_ref[...]
    r1 = jnp.sum(jnp.where(row == i1, cum, 0.0), axis=0, keepdims=True) - 1.0
    r2 = jnp.sum(jnp.where(row == i2, cum, 0.0), axis=0, keepdims=True) - 1.0
    total = jnp.max(cum, axis=1, keepdims=True)
    carry_ref[...] = total
    cnt_ref[...] = jnp.broadcast_to(total, cnt_ref.shape)

    rows = ((i1 - ROUTE_E0).astype(_F32), (i2 - ROUTE_E0).astype(_F32), c1, c2, r1, r2)
    rrow = lax.broadcasted_iota(jnp.int32, (8, tm), 0)
    route = jnp.zeros((8, tm), _F32)
    for k, val in enumerate(rows):
        route = jnp.where(rrow == k, val, route)
    route_ref[:, rs] = route


def _out_router_kernel(*refs, n_in):
    x_ref = refs[0]
    o_refs = refs[1:1 + n_in]
    w_refs = refs[1 + n_in:1 + 2 * n_in]
    g_ref, wr_ref, br_ref, xnew_ref = refs[1 + 2 * n_in:5 + 2 * n_in]
    route_refs = refs[5 + 2 * n_in:]
    carry_ref = route_refs[-1]

    @pl.when(pl.program_id(0) == 0)
    def _():
        carry_ref[...] = jnp.zeros(carry_ref.shape, _F32)

    rows = x_ref.shape[0] // ROUTE_CHAINS
    for r in range(ROUTE_CHAINS):
        rs = slice(r * rows, (r + 1) * rows)
        acc = x_ref[rs]
        for o_ref, w_ref in zip(o_refs, w_refs):
            acc = acc + jnp.dot(o_ref[rs], w_ref[...], preferred_element_type=_F32)
        xnew_ref[rs] = acc
        _route_tile(acc, rs, g_ref, wr_ref, br_ref, *route_refs)


def _out_router(x2, os_, ws, g, w_router, b_router, tm=1024):
    n, d = x2.shape
    w_hi = w_router.astype(_BF16)
    w_lo = (w_router - w_hi.astype(_F32)).astype(_BF16)
    const = lambda shape: pl.BlockSpec(shape, lambda i: (0,) * len(shape))
    row = pl.BlockSpec((tm, d), lambda i: (i, 0))
    in_specs = [row] + [pl.BlockSpec((tm, o.shape[1]), lambda i: (i, 0)) for o in os_]
    in_specs += [const(w.shape) for w in ws]
    in_specs += [const((1, d)), const((d, 2 * LANES)), const((1, LANES))]
    return pl.pallas_call(
        functools.partial(_out_router_kernel, n_in=len(os_)), grid=(n // tm,),
        in_specs=in_specs,
        out_specs=[row, pl.BlockSpec((2, tm, PACK_W), lambda i: (0, i, 0)),
                   pl.BlockSpec((8, tm), lambda i: (0, i)), const((ROUTE_ROWS, LANES))],
        out_shape=[jax.ShapeDtypeStruct((n, d), _F32),
                   jax.ShapeDtypeStruct((2, n, PACK_W), jnp.int32),
                   jax.ShapeDtypeStruct((8, n), _F32),
                   jax.ShapeDtypeStruct((ROUTE_ROWS, LANES), _F32)],
        scratch_shapes=[pltpu.VMEM((ROUTE_ROWS, 1), _F32)],
        compiler_params=_params("arbitrary"),
        name="out_router",
    )(x2, *os_, *ws, g.reshape(1, d), jnp.concatenate([w_hi, w_lo], axis=1), b_router)


def _expert_kernel(te_ref, nv_ref, nxt_ref, par_ref, xs_ref, wg_hbm, wu_hbm, wd_hbm, ys_ref,
                   wgu_s, wd_s, bg, bu, bd, sem):
    j = pl.program_id(0)
    prev = te_ref[jnp.maximum(j - 1, 0)]

    def copies(e, slot):
        return [pltpu.make_async_copy(src.at[e], dst.at[slot], sem.at[slot, k])
                for k, (src, dst) in enumerate(((wg_hbm, bg), (wu_hbm, bu), (wd_hbm, bd)))]

    @pl.when(j == 0)
    def _():
        for c in copies(te_ref[0], par_ref[0]):
            c.start()

    @pl.when((j == 0) | (te_ref[j] != prev))
    def _():
        slot = par_ref[j]
        for c in copies(te_ref[j], slot):
            c.wait()

        @pl.when(nxt_ref[j] >= 0)
        def _():
            for c in copies(nxt_ref[j], 1 - slot):
                c.start()

        wgu_s[:, :MOE_D_FF] = bg[slot].astype(_BF16)
        wgu_s[:, MOE_D_FF:] = bu[slot].astype(_BF16)
        wd_s[...] = bd[slot].astype(_BF16)

    @pl.when(j < nv_ref[0])
    def _():
        rows = xs_ref.shape[1] // MOE_CHAINS
        for r in range(MOE_CHAINS):
            rs = slice(r * rows, (r + 1) * rows)
            xs = _load_packed(xs_ref[:, rs]).astype(_BF16)
            au = jnp.dot(xs, wgu_s[...], preferred_element_type=_F32)
            a, u = au[:, :MOE_D_FF], au[:, MOE_D_FF:]
            act = (a * (1.0 / (1.0 + jnp.exp(-a))) * u).astype(_BF16)
            _store_packed(ys_ref.at[:, rs], jnp.dot(act, wd_s[...], preferred_element_type=_F32))


def _experts(tile_expert, n_valid, xs, w_gate, w_up, w_down, tm=MOE_TM):
    _, n_slots, pw = xs.shape
    n_tiles = n_slots // tm
    d, f = w_gate.shape[-2:]
    row = lambda j, te, nv, nxt, par: (0, jnp.maximum(jnp.minimum(j, nv[0] - 1), 0), 0)
    change = (tile_expert[1:] != tile_expert[:-1]).astype(jnp.int32)
    parity = jnp.concatenate([jnp.zeros((1,), jnp.int32), jnp.cumsum(change)]) % 2
    after = jnp.searchsorted(tile_expert, tile_expert, side="right")
    nxt = jnp.where(after < n_tiles, tile_expert[jnp.minimum(after, n_tiles - 1)], -1).astype(jnp.int32)
    hbm = pl.BlockSpec(memory_space=pl.ANY)
    grid_spec = pltpu.PrefetchScalarGridSpec(
        num_scalar_prefetch=4, grid=(n_tiles,),
        in_specs=[pl.BlockSpec((2, tm, pw), row), hbm, hbm, hbm],
        out_specs=pl.BlockSpec((2, tm, pw), row),
        scratch_shapes=[pltpu.VMEM((d, 2 * f), _BF16), pltpu.VMEM((f, d), _BF16),
                        pltpu.VMEM((2, d, f), _F32), pltpu.VMEM((2, d, f), _F32),
                        pltpu.VMEM((2, f, d), _F32), pltpu.SemaphoreType.DMA((2, 3))])
    return pl.pallas_call(
        _expert_kernel, grid_spec=grid_spec,
        out_shape=jax.ShapeDtypeStruct((2, n_slots, pw), jnp.int32),
        compiler_params=_params("arbitrary"),
        name="experts",
    )(tile_expert, n_valid, nxt, parity, xs, w_gate, w_up, w_down)


def _combine_kernel(x_ref, y_ref, route_ref, o_ref):
    o_ref[...] = _combine_tile(x_ref, y_ref, route_ref)


def _combine(x2, y, route, tm=512):
    n, d = x2.shape
    row = pl.BlockSpec((tm, d), lambda i: (i, 0))
    return pl.pallas_call(
        _combine_kernel, grid=(n // tm,),
        in_specs=[row, pl.BlockSpec((2, 2, tm, PACK_W), lambda i: (0, 0, i, 0)),
                  pl.BlockSpec((8, tm), lambda i: (0, i))],
        out_specs=row,
        out_shape=jax.ShapeDtypeStruct((n, d), _F32),
        compiler_params=_params("parallel"),
        name="moe_combine",
    )(x2, y, route)


def _slot_kernel(offset_ref, route_ref, slot_ref, *, n_slots):
    expert = route_ref[0:2, :].astype(jnp.int32)
    pos = route_ref[4:6, :].astype(jnp.int32)
    for e in range(N_EXPERTS):
        pos = pos + jnp.where(expert == e, offset_ref[e], 0)
    slot_ref[0:2, :] = pos
    slot_ref[2:4, :] = pos + n_slots


def _slots(offset, route, n_slots):
    n = route.shape[1]
    return pl.pallas_call(
        functools.partial(_slot_kernel, n_slots=n_slots),
        in_specs=[pl.BlockSpec(memory_space=pltpu.SMEM),
                  pl.BlockSpec((8, n), lambda: (0, 0))],
        out_specs=pl.BlockSpec((4, n), lambda: (0, 0)),
        out_shape=jax.ShapeDtypeStruct((4, n), jnp.int32),
        name="moe_slots",
    )(offset, route)


def _router_params(wg, bg, we, be):
    d = wg.shape[0]
    w_router = jnp.zeros((d, LANES), _F32)
    w_router = w_router.at[:, :MOE_GROUPS].set(wg)
    w_router = w_router.at[:, ROUTE_E0:ROUTE_E0 + N_EXPERTS].set(
        jnp.moveaxis(we, 0, 1).reshape(d, N_EXPERTS))
    b_router = jnp.zeros((1, LANES), _F32)
    b_router = b_router.at[0, :MOE_GROUPS].set(bg)
    b_router = b_router.at[0, ROUTE_E0:ROUTE_E0 + N_EXPERTS].set(be.reshape(-1))
    return w_router, b_router


def _moe_experts(h, route, cnt, w_gate, w_up, w_down, layer):
    n = h.shape[1]
    d = w_gate.shape[-2]
    tm = MOE_TM
    counts = cnt[ROUTE_E0:ROUTE_E0 + N_EXPERTS, 0].astype(jnp.int32)
    tiles_per = (counts + tm - 1) // tm
    tiles_end = jnp.cumsum(tiles_per)
    offset = (tiles_end - tiles_per) * tm
    n_tiles = (2 * n) // tm + N_EXPERTS
    n_valid = tiles_end[-1]
    tile_ids = jnp.minimum(jnp.arange(n_tiles, dtype=jnp.int32), n_valid - 1)
    tile_expert = jnp.sum(tile_ids[:, None] >= tiles_end[None, :], axis=1).astype(jnp.int32)
    n_slots = n_tiles * tm
    slot = _slots(offset, route, n_slots).reshape(-1)

    xs = _sc_scatter(h.reshape(2 * n, PACK_W), slot, 2 * n_slots, reps=2)
    ys = _experts(tile_expert + layer * N_EXPERTS, n_valid.reshape(1),
                  xs.reshape(2, n_slots, PACK_W), w_gate.reshape(-1, d, MOE_D_FF),
                  w_up.reshape(-1, d, MOE_D_FF), w_down.reshape(-1, MOE_D_FF, d))
    y = _sc_gather(ys.reshape(2 * n_slots, PACK_W), slot)
    return y.reshape(2, 2, n, PACK_W)


def _proj_prep_pending(x2, pending, *args, **kwargs):
    outs = _proj_prep(x2, *args, pending=pending, **kwargs)
    return (x2, outs) if pending is None else (outs[0], outs[1:])


def _mixer_ab(x2, pending, router, bsz, seq, g, w_in, a_qn, a_kn, b_qn, b_kn, b_sink, w_out):
    aw = A_HEADS * HEAD_DIM
    bqw = B_HEADS * HEAD_DIM
    bkw = B_KV_HEADS * HEAD_DIM
    scale2 = QK_SCALE * LOG2E
    segs = [_Seg(0, aw, True, False, scale2, "plain"),
            _Seg(aw, aw, True, False, 1.0, "T"),
            _Seg(2 * aw, aw, False, False, 1.0, "heads_ones"),
            _Seg(3 * aw, bqw, True, False, scale2, "plain"),
            _Seg(3 * aw + bqw, bkw, True, False, 1.0, "T"),
            _Seg(3 * aw + bqw + bkw, bkw, False, False, 1.0, "heads_ones")]
    x2, (qa, ka, va, qb, kb, vb) = _proj_prep_pending(
        x2, pending, bsz, seq, g, w_in.astype(_BF16), segs, [a_qn, a_kn, None, b_qn, b_kn, None])

    bound_a = HEAD_DIM * scale2 * jnp.max(jnp.abs(a_qn)) * jnp.max(jnp.abs(a_kn))
    bound_b = HEAD_DIM * scale2 * jnp.max(jnp.abs(b_qn)) * jnp.max(jnp.abs(b_kn))
    static_ok = jnp.maximum(bound_a, bound_b) <= SHIFT_MAX
    shift_a = jnp.where(static_ok, bound_a, 0.0)
    shift_b = jnp.where(static_ok, bound_b, 0.0)
    slopes = _alibi_slopes(A_HEADS + B_HEADS)
    bias_a = _dil_bias(slopes[0::2], shift_a)
    bias_b = _band_bias(B_WINDOW, 1.0, slopes[1::2], shift_b, B_HEADS // B_KV_HEADS)
    sink2 = b_sink.astype(_F32) * LOG2E - shift_b

    def attend(online, qa, ka, va, qb, kb, vb, bias_a, bias_b, sink2):
        return (_dil_attn(qa, ka, va, bias_a, online=online),
                _band_b(qb, kb, vb, bias_b, sink2, online=online))

    oa, ob = lax.cond(static_ok, functools.partial(attend, False), functools.partial(attend, True),
                      qa, ka, va, qb, kb, vb, bias_a, bias_b, sink2)
    w_out = w_out.astype(_BF16)
    return _out_router(x2, [oa.reshape(-1, aw), ob.reshape(-1, bqw)], [w_out[:aw], w_out[aw:]],
                       *router)


def _mixer_c(x2, pending, router, bsz, seq, g, w_in, qn, kn, w_out):
    qw = C_HEADS * HEAD_DIM
    kvw = C_KV_HEADS * HEAD_DIM
    rope = _rope_tables(seq)
    bound = HEAD_DIM * QK_SCALE * LOG2E * jnp.max(jnp.abs(qn)) * jnp.max(jnp.abs(kn))
    static_ok = bound <= SHIFT_MAX
    shift = jnp.where(static_ok, bound, 0.0)
    segs = [_Seg(0, qw, True, True, QK_SCALE * LOG2E, "T_one"),
            _Seg(qw, kvw, True, True, 1.0, "heads_shift"),
            _Seg(qw + kvw, kvw, False, False, 1.0, "T_ones")]
    x2, (qt, k, vt) = _proj_prep_pending(x2, pending, bsz, seq, g, w_in.astype(_BF16), segs,
                                         [qn, kn, None], rope=rope, shift=shift)
    o = lax.cond(static_ok,
                 functools.partial(_dense_attn, online=False),
                 functools.partial(_dense_attn, online=True), qt, k, vt)
    return _out_router(x2, [o.reshape(-1, qw)], [w_out.astype(_BF16)], *router)


def kernel(x, mix_norm, ffn_norm, ab_w_in, a_q_norm, a_k_norm, b_q_norm, b_k_norm, b_sink, ab_w_out,
           c_w_in, c_q_norm, c_k_norm, c_w_out, moe_group_w, moe_group_b, moe_expert_w, moe_expert_b,
           moe_w_gate, moe_w_up, moe_w_down):
    bsz, seq, d = x.shape
    x2 = x.reshape(bsz * seq, d)
    depth = mix_norm.shape[0]
    pending = None
    for layer in range(depth):
        i = layer // 2
        router = (ffn_norm[layer],) + _router_params(moe_group_w[layer], moe_group_b[layer],
                                                     moe_expert_w[layer], moe_expert_b[layer])
        if layer % 2 == 0:
            x2, h, route, cnt = _mixer_ab(x2, pending, router, bsz, seq, mix_norm[layer], ab_w_in[i],
                                          a_q_norm[i], a_k_norm[i], b_q_norm[i], b_k_norm[i],
                                          b_sink[i], ab_w_out[i])
        else:
            x2, h, route, cnt = _mixer_c(x2, pending, router, bsz, seq, mix_norm[layer], c_w_in[i],
                                         c_q_norm[i], c_k_norm[i], c_w_out[i])
        pending = (_moe_experts(h, route, cnt, moe_w_gate, moe_w_up, moe_w_down, layer), route)
    return _combine(x2, *pending).reshape(bsz, seq, d)
```

```python
import functools
import math
from typing import NamedTuple

import jax
import jax.numpy as jnp
import numpy as np
from jax import lax
from jax.experimental import pallas as pl
from jax.experimental.pallas import tpu as pltpu
from jax.experimental.pallas import tpu_sc as plsc

HEAD_DIM = 64
LANES = 128
A_HEADS = 8
B_HEADS = 8
B_KV_HEADS = 2
C_HEADS = 16
C_KV_HEADS = 4
A_PATTERNS = ((128, 1), (512, 4), (2048, 16))
B_WINDOW = 128
GRID_W = 64
ROPE_THETA = 10000.0
ROPE_AXIS_DIM = HEAD_DIM // 2
ALIBI_MAX_BIAS = 8.0
RMS_EPS = 1e-6
MOE_GROUPS = 4
MOE_EXPERTS = 8
N_EXPERTS = MOE_GROUPS * MOE_EXPERTS
MOE_D_FF = 256
QK_SCALE = HEAD_DIM ** -0.5
LOG2E = math.log2(math.e)
SHIFT_MAX = 60.0
NEG_BIG = -1e30
VMEM_LIMIT = 52 * 1024 * 1024

BAND_BQ = 256
BAND_KB = 128
BAND_KW = BAND_BQ + 2 * BAND_KB
BAND_SUBS = 8
DIL_BQ = 256
DIL_SUBS = 8
DIL_REACH = max(w // 2 for w, _ in A_PATTERNS)
DIL_KW = DIL_BQ + 2 * DIL_REACH
DIL_CHUNK = 768
DENSE_BQ = 1024
DENSE_KCH = 2048
MOE_TM = 512
MOE_STEP_TILES = 2
ROUTE_E0 = 4
ROUTE_CHAINS = 1
ROUTE_ROWS = 48
PACK_W = 256
SC_WINDOW = 128

_BF16 = jnp.bfloat16
_F32 = jnp.float32


def _params(*sem):
    return pltpu.CompilerParams(dimension_semantics=sem, vmem_limit_bytes=VMEM_LIMIT)


def _alibi_slopes(n):
    return np.asarray(2.0 ** (-ALIBI_MAX_BIAS * np.arange(1, n + 1) / n), dtype=np.float32)


def _head_norm(y, gain, head_ones):
    y2 = y * y
    hi = y2.astype(_BF16)
    lo = (y2 - hi.astype(_F32)).astype(_BF16)
    ss = (jnp.dot(hi, head_ones, preferred_element_type=_F32)
          + jnp.dot(lo, head_ones, preferred_element_type=_F32))
    return y * lax.rsqrt(ss * (1.0 / HEAD_DIM) + RMS_EPS) * gain


def _rope(y, cos, sin):
    lane = lax.broadcasted_iota(jnp.int32, y.shape, 1)
    first = (lane % 32) < 16
    partner = jnp.where(first, pltpu.roll(y, LANES - 16, axis=1), pltpu.roll(y, 16, axis=1))
    return y * cos + partner * sin


class _Seg(NamedTuple):
    col0: int
    ncols: int
    norm: bool
    rope: bool
    scale: float
    mode: str


def _prep_slab(y, c, o_ref, seg, gain_ref, cos_ref, sin_ref, shift_ref, head_ones):
    rows = y.shape[0]
    if seg.norm:
        y = _head_norm(y, gain_ref[...], head_ones)
    if seg.rope:
        y = _rope(y, cos_ref[...], sin_ref[...])
    if seg.scale != 1.0:
        y = y * seg.scale
    sl = slice(c * LANES, (c + 1) * LANES)
    if seg.mode == "plain":
        o_ref[:, sl] = y.astype(_BF16)
    elif seg.mode == "T":
        o_ref[sl, :] = y.T.astype(_BF16)
    elif seg.mode in ("heads_ones", "heads_shift"):
        lane = lax.broadcasted_iota(jnp.int32, y.shape, 1)
        if seg.mode == "heads_ones":
            fill = jnp.ones(y.shape, _F32)
        else:
            fill = jnp.where(lane == HEAD_DIM, -shift_ref[0], 0.0)
        for k, yk in enumerate((y, pltpu.roll(y, HEAD_DIM, axis=1))):
            o_ref[2 * c + k] = jnp.where(lane < HEAD_DIM, yk, fill).astype(_BF16)
    else:
        assert seg.mode in ("T_one", "T_ones")
        yt = y.T
        if seg.mode == "T_one":
            row = lax.broadcasted_iota(jnp.int32, (HEAD_DIM, rows), 0)
            extra = jnp.where(row == 0, 1.0, 0.0)
        else:
            extra = jnp.ones((HEAD_DIM, rows), _F32)
        for k in range(2):
            ext = jnp.concatenate([yt[k * HEAD_DIM:(k + 1) * HEAD_DIM], extra], axis=0)
            o_ref[(2 * c + k) * LANES:(2 * c + k + 1) * LANES, :] = ext.astype(_BF16)


def _combine_tile(x_ref, y_ref, route_ref):
    coef = route_ref[...].T
    y0 = _load_packed(y_ref[:, 0])
    y1 = _load_packed(y_ref[:, 1])
    return x_ref[...] + coef[:, 2:3] * y0 + coef[:, 3:4] * y1


def _proj_prep_kernel(*refs, segs, combine):
    it = iter(refs)
    shift_ref = next(it) if any(s.mode == "heads_shift" for s in segs) else None
    x_ref, g_ref, w_ref = next(it), next(it), next(it)
    y_ref, route_ref = (next(it), next(it)) if combine else (None, None)
    gain_refs = [next(it) if s.norm else None for s in segs]
    cos_ref, sin_ref = (next(it), next(it)) if any(s.rope for s in segs) else (None, None)
    outs = list(it)
    if combine:
        x = _combine_tile(x_ref, y_ref, route_ref)
        outs.pop(0)[...] = x
    else:
        x = x_ref[...]
    ms = jnp.mean(x * x, axis=-1, keepdims=True)
    h = (x * lax.rsqrt(ms + RMS_EPS) * g_ref[...]).astype(_BF16)
    proj = jnp.dot(h, w_ref[...], preferred_element_type=_F32)
    head_ones = (lax.broadcasted_iota(jnp.int32, (LANES, LANES), 0) // HEAD_DIM
                 == lax.broadcasted_iota(jnp.int32, (LANES, LANES), 1) // HEAD_DIM).astype(_BF16)
    for seg, gain_ref, o_ref in zip(segs, gain_refs, outs):
        for c in range(seg.ncols // LANES):
            y = proj[:, seg.col0 + c * LANES:seg.col0 + (c + 1) * LANES]
            _prep_slab(y, c, o_ref, seg, gain_ref, cos_ref, sin_ref, shift_ref, head_ones)


def _proj_prep(x2, bsz, seq, g, w, segs, gains, rope=None, shift=None, pending=None, tm=512):
    n, d = x2.shape
    p = w.shape[1]
    nt = seq // tm
    assert seq % tm == 0
    const = lambda shape: pl.BlockSpec(shape, lambda i: (0,) * len(shape))
    row = pl.BlockSpec((tm, d), lambda i: (i, 0))
    ins, in_specs = [], []
    if shift is not None:
        ins.append(shift.reshape(1).astype(_F32))
        in_specs.append(pl.BlockSpec(memory_space=pltpu.SMEM))
    ins += [x2, g.reshape(1, d), w]
    in_specs += [row, const((1, d)), const((d, p))]
    if pending is not None:
        ins += list(pending)
        in_specs += [pl.BlockSpec((2, 2, tm, PACK_W), lambda i: (0, 0, i, 0)),
                     pl.BlockSpec((8, tm), lambda i: (0, i))]
    for seg, gain in zip(segs, gains):
        if seg.norm:
            ins.append(jnp.tile(gain.astype(_F32), LANES // HEAD_DIM).reshape(1, LANES))
            in_specs.append(const((1, LANES)))
    if rope is not None:
        ins += list(rope)
        in_specs += [pl.BlockSpec((tm, LANES), lambda i: (i % nt, 0))] * 2
    out_shape, out_specs = [], []
    if pending is not None:
        out_shape.append(jax.ShapeDtypeStruct((n, d), _F32))
        out_specs.append(row)
    for seg in segs:
        nc = seg.ncols
        if seg.mode == "plain":
            out_shape.append(jax.ShapeDtypeStruct((bsz, seq, nc), _BF16))
            out_specs.append(pl.BlockSpec((None, tm, nc), lambda i: (i // nt, i % nt, 0)))
        elif seg.mode in ("T", "T_one", "T_ones"):
            wout = nc if seg.mode == "T" else 2 * nc
            out_shape.append(jax.ShapeDtypeStruct((bsz, wout, seq), _BF16))
            out_specs.append(pl.BlockSpec((None, wout, tm), lambda i: (i // nt, 0, i % nt)))
        else:
            assert seg.mode in ("heads_ones", "heads_shift")
            nh = nc // HEAD_DIM
            out_shape.append(jax.ShapeDtypeStruct((bsz, nh, seq, LANES), _BF16))
            out_specs.append(pl.BlockSpec((None, nh, tm, LANES), lambda i: (i // nt, 0, i % nt, 0)))
    return pl.pallas_call(
        functools.partial(_proj_prep_kernel, segs=tuple(segs), combine=pending is not None),
        grid=(n // tm,),
        in_specs=in_specs, out_specs=out_specs, out_shape=out_shape,
        compiler_params=_params("parallel"),
        name="proj_prep",
    )(*ins)


def _rope_tables(seq):
    t = np.arange(seq)
    row = (t // GRID_W).astype(np.float32)
    col = (t % GRID_W).astype(np.float32)
    inv_freq = np.float32(ROPE_THETA) ** (-np.arange(0, ROPE_AXIS_DIM, 2, dtype=np.float32)
                                          / np.float32(ROPE_AXIS_DIM))
    ang_r = row[:, None] * inv_freq[None, :]
    ang_c = col[:, None] * inv_freq[None, :]
    cr, sr, cc, sc = np.cos(ang_r), np.sin(ang_r), np.cos(ang_c), np.sin(ang_c)
    cos = np.concatenate([cr, cr, cc, cc], axis=-1)
    sin = np.concatenate([-sr, sr, -sc, sc], axis=-1)
    return (jnp.asarray(np.tile(cos, (1, 2)), _F32), jnp.asarray(np.tile(sin, (1, 2)), _F32))


def _band_bias(hw, dist_scale, slopes, shift, stack):
    bq, kb = BAND_BQ, BAND_KB
    assert hw <= kb
    r = np.arange(bq)[:, None]
    c = np.arange(BAND_KW)[None, :]
    rel = np.abs(c - kb - r)
    dist = rel.astype(np.float32) * np.float32(dist_scale)
    ok = np.stack([(rel <= hw) & ~(first & (c < kb)) & ~(last & (c >= kb + bq))
                   for first, last in ((False, False), (True, False), (False, True), (True, True))])
    alibi = -(np.asarray(slopes, np.float32)[:, None, None] * dist[None]) * np.float32(LOG2E)
    bias = jnp.where(ok[:, None], jnp.asarray(alibi)[None] - shift, NEG_BIG)
    nh = len(slopes)
    return bias.reshape(4, nh // stack, stack * bq, BAND_KW)


def _window_blocks(nb):
    per = BAND_BQ * BAND_SUBS // BAND_KB
    last = nb * per - 1
    fns = [lambda i: jnp.maximum(per * i - 1, 0)]
    fns += [functools.partial(lambda j, i: per * i + j, j) for j in range(per)]
    fns += [lambda i: jnp.minimum(per * i + per, last)]
    return fns


def _dil_bias(slopes, shift):
    r = jnp.arange(DIL_BQ, dtype=jnp.int32)[:, None]
    u = jnp.arange(DIL_BQ + 4 * DIL_REACH, dtype=jnp.int32)[None, :]
    delta = u - 2 * DIL_REACH - r
    dist = jnp.abs(delta)
    mult = sum(((delta % d == 0) & (dist <= ((w // 2) // d) * d)).astype(_F32) for w, d in A_PATTERNS)
    alibi = -(jnp.asarray(slopes, _F32)[:, None, None] * dist.astype(_F32)[None]) * LOG2E
    return jnp.where((mult > 0)[None], jnp.log2(jnp.maximum(mult, 1.0))[None] + alibi - shift, NEG_BIG)


def _dil_attn_kernel(q_ref, kt_ref, v_ref, bias_ref, o_ref, *, online):
    bq = DIL_BQ
    seq = kt_ref.shape[1]
    for sub in range(q_ref.shape[0] // bq):
        qs = slice(sub * bq, (sub + 1) * bq)
        q0 = pl.program_id(2) * q_ref.shape[0] + sub * bq
        w0 = pl.multiple_of(jnp.clip(q0 - DIL_REACH, 0, seq - DIL_KW), bq)
        u0 = pl.multiple_of(w0 - q0 + 2 * DIL_REACH, bq)
        q = q_ref[qs, :]
        outs = []
        for j in range(2):
            rows = slice(j * HEAD_DIM, (j + 1) * HEAD_DIM)
            scores = []
            for c in range(DIL_KW // DIL_CHUNK):
                ks = pl.ds(pl.multiple_of(w0 + c * DIL_CHUNK, bq), DIL_CHUNK)
                us = pl.ds(pl.multiple_of(u0 + c * DIL_CHUNK, bq), DIL_CHUNK)
                scores.append(jnp.dot(q[:, rows], kt_ref[rows, ks], preferred_element_type=_F32)
                              + bias_ref[j, :, us])
            if online:
                m = functools.reduce(jnp.maximum,
                                     [jnp.max(s, axis=-1, keepdims=True) for s in scores])
                scores = [s - m for s in scores]
            acc = jnp.zeros((bq, LANES), _F32)
            for c, s in enumerate(scores):
                ks = pl.ds(pl.multiple_of(w0 + c * DIL_CHUNK, bq), DIL_CHUNK)
                acc += jnp.dot(jnp.exp2(s).astype(_BF16), v_ref[j, ks, :],
                               preferred_element_type=_F32)
            outs.append((acc / pltpu.roll(acc, HEAD_DIM, axis=1))[:, :HEAD_DIM])
        o_ref[qs, :] = jnp.concatenate(outs, axis=1).astype(_BF16)


def _dil_attn(q, kt, v, bias, *, online):
    bsz, seq, w = q.shape
    bq = DIL_BQ * DIL_SUBS
    assert seq >= DIL_KW and seq % bq == 0
    return pl.pallas_call(
        functools.partial(_dil_attn_kernel, online=online), grid=(A_HEADS // 2, bsz, seq // bq),
        in_specs=[pl.BlockSpec((None, bq, LANES), lambda p, b, i: (b, i, p)),
                  pl.BlockSpec((None, LANES, seq), lambda p, b, i: (b, p, 0)),
                  pl.BlockSpec((None, 2, seq, LANES), lambda p, b, i: (b, p, 0, 0)),
                  pl.BlockSpec((2, DIL_BQ, bias.shape[-1]), lambda p, b, i: (p, 0, 0))],
        out_specs=pl.BlockSpec((None, bq, LANES), lambda p, b, i: (b, i, p)),
        out_shape=jax.ShapeDtypeStruct((bsz, seq, w), _BF16),
        compiler_params=_params("parallel", "parallel", "parallel"),
        name="dil_attn_online" if online else "dil_attn",
    )(q, kt, v, bias)


def _band_b_kernel(sink2_ref, sinkp_ref, q_ref, *refs, online):
    bq = BAND_BQ
    g = pl.program_id(1)
    grp = B_HEADS // B_KV_HEADS
    npc = (BAND_BQ * BAND_SUBS + 2 * BAND_KB) // BAND_KB
    k_refs, v_refs = refs[:npc], refs[npc:2 * npc]
    bias_first, bias_mid, bias_last, o_ref = refs[2 * npc:]
    kt_all = jnp.concatenate([r[...] for r in k_refs], axis=1)
    v_all = jnp.concatenate([r[...] for r in v_refs], axis=0)
    for sub in range(BAND_SUBS):
        qs = slice(sub * bq, (sub + 1) * bq)
        ks = slice(sub * bq, sub * bq + BAND_KW)
        bias_ref = bias_first if sub == 0 else bias_last if sub == BAND_SUBS - 1 else bias_mid
        q = q_ref[qs, :]
        q4 = jnp.concatenate([q[:, i * HEAD_DIM:(i + 1) * HEAD_DIM] for i in range(grp)], axis=0)
        s = jnp.dot(q4, kt_all[:, ks], preferred_element_type=_F32) + bias_ref[...]
        if online:
            m = jnp.max(s, axis=-1, keepdims=True)
            s = s - m
        acc = jnp.dot(jnp.exp2(s).astype(_BF16), v_all[ks], preferred_element_type=_F32)
        for i in range(grp):
            a = acc[i * bq:(i + 1) * bq]
            if online:
                mi = m[i * bq:(i + 1) * bq]
                sk = sink2_ref[g * grp + i]
                mm = jnp.maximum(mi, sk)
                a = a * jnp.exp2(mi - mm)
                o = a / (pltpu.roll(a, HEAD_DIM, axis=1) + jnp.exp2(sk - mm))
            else:
                o = a / (pltpu.roll(a, HEAD_DIM, axis=1) + sinkp_ref[g * grp + i])
            o_ref[qs, i * HEAD_DIM:(i + 1) * HEAD_DIM] = o[:, :HEAD_DIM].astype(_BF16)


def _band_b(q, kt, v, bias, sink2, *, online):
    bsz, seq_len, w = q.shape
    assert BAND_SUBS >= 2
    bq, kb = BAND_BQ * BAND_SUBS, BAND_KB
    nb = seq_len // bq
    grp = B_HEADS // B_KV_HEADS
    gw = w // B_KV_HEADS
    blocks = _window_blocks(nb)
    kspec = lambda f: pl.BlockSpec((None, HEAD_DIM, kb), lambda b, g, i: (b, g, f(i)))
    vspec = lambda f: pl.BlockSpec((None, None, kb, LANES), lambda b, g, i: (b, g, f(i), 0))
    qspec = pl.BlockSpec((None, bq, gw), lambda b, g, i: (b, i, g))
    smem = pl.BlockSpec(memory_space=pltpu.SMEM)
    bspec = lambda f: pl.BlockSpec((None, None, grp * BAND_BQ, BAND_KW),
                                   lambda b, g, i: (f(i), g, 0, 0))
    variants = [lambda i: jnp.where(i == 0, 1, 0), lambda i: 0,
                lambda i: jnp.where(i == nb - 1, 2, 0)]
    return pl.pallas_call(
        functools.partial(_band_b_kernel, online=online), grid=(bsz, B_KV_HEADS, nb),
        in_specs=[smem, smem, qspec] + [kspec(f) for f in blocks] + [vspec(f) for f in blocks]
        + [bspec(f) for f in variants],
        out_specs=qspec,
        out_shape=jax.ShapeDtypeStruct((bsz, seq_len, w), _BF16),
        compiler_params=_params("parallel", "parallel", "parallel"),
        name="band_b_online" if online else "band_b",
    )(sink2, jnp.exp2(sink2), q, *([kt] * len(blocks)), *([v] * len(blocks)), bias, bias, bias)


def _dense_attn_kernel(qt_ref, k_ref, vt_ref, o_ref, *, kch, online):
    bq = qt_ref.shape[1]
    seq = k_ref.shape[0]
    for c in range(C_HEADS // C_KV_HEADS):
        qt = qt_ref[c * LANES:(c + 1) * LANES, :]
        acc = jnp.zeros((LANES, bq), _F32)
        m = jnp.full((1, bq), NEG_BIG, _F32)
        for j in range(seq // kch):
            ks = slice(j * kch, (j + 1) * kch)
            st = jnp.dot(k_ref[ks, :], qt, preferred_element_type=_F32)
            if online:
                m_new = jnp.maximum(m, jnp.max(st, axis=0, keepdims=True))
                acc = acc * jnp.exp2(m - m_new)
                st = st - m_new
                m = m_new
            acc = acc + jnp.dot(vt_ref[:, ks], jnp.exp2(st).astype(_BF16),
                                preferred_element_type=_F32)
        o = acc[:HEAD_DIM] / acc[HEAD_DIM:]
        o_ref[:, c * HEAD_DIM:(c + 1) * HEAD_DIM] = o.T.astype(_BF16)


def _dense_attn(qt, k, vt, *, online, bq=DENSE_BQ, kch=DENSE_KCH):
    bsz, _, seq = qt.shape
    bq, kch = min(bq, seq), min(kch, seq)
    grp = C_HEADS // C_KV_HEADS
    return pl.pallas_call(
        functools.partial(_dense_attn_kernel, kch=kch, online=online),
        grid=(bsz, C_KV_HEADS, seq // bq),
        in_specs=[pl.BlockSpec((None, grp * LANES, bq), lambda b, g, i: (b, g, i)),
                  pl.BlockSpec((None, None, seq, LANES), lambda b, g, i: (b, g, 0, 0)),
                  pl.BlockSpec((None, LANES, seq), lambda b, g, i: (b, g, 0))],
        out_specs=pl.BlockSpec((None, bq, grp * HEAD_DIM), lambda b, g, i: (b, i, g)),
        out_shape=jax.ShapeDtypeStruct((bsz, seq, C_HEADS * HEAD_DIM), _BF16),
        compiler_params=_params("parallel", "parallel", "parallel"),
        name="dense_attn_online" if online else "dense_attn",
    )(qt, k, vt)


def _pack_pair(a, b):
    wa = lax.bitcast_convert_type(a.astype(_BF16).astype(_F32), jnp.uint32) >> 16
    wb = lax.bitcast_convert_type(b.astype(_BF16).astype(_F32), jnp.uint32) & jnp.uint32(0xFFFF0000)
    return lax.bitcast_convert_type(wa | wb, jnp.int32)


def _unpack_pair(w):
    u = lax.bitcast_convert_type(w, jnp.uint32)
    return (lax.bitcast_convert_type(u << 16, _F32),
            lax.bitcast_convert_type(u & jnp.uint32(0xFFFF0000), _F32))


def _store_packed(ref, y):
    q = PACK_W
    for j in range(2):
        ref[j] = _pack_pair(y[:, 2 * j * q:(2 * j + 1) * q], y[:, (2 * j + 1) * q:(2 * j + 2) * q])


def _load_packed(ref):
    parts = []
    for j in range(2):
        parts += list(_unpack_pair(ref[j]))
    return jnp.concatenate(parts, axis=1)


def _sc_mesh():
    return plsc.VectorSubcoreMesh(core_axis_name="core", subcore_axis_name="subcore")


def _sc_gather(table, idx):
    n = idx.shape[0]
    d = table.shape[1]

    @pl.kernel(out_type=jax.ShapeDtypeStruct((n, d), table.dtype), mesh=_sc_mesh())
    def gather(x_hbm, i_hbm, o_hbm):
        def body(i_vmem, o_vmem):
            pltpu.sync_copy(x_hbm.at[i_vmem.at[0]], o_vmem)

        pltpu.emit_pipeline(
            body, grid=(n // SC_WINDOW,),
            in_specs=[pl.BlockSpec((1, SC_WINDOW), index_map=lambda i: (0, i))],
            out_specs=[pl.BlockSpec((SC_WINDOW, d), index_map=lambda i: (i, 0))],
            core_axis_name=("core", "subcore"),
            dimension_semantics=(pltpu.PARALLEL,),
        )(i_hbm, o_hbm)

    return gather(table, idx.reshape(1, n))


def _sc_scatter(src, idx, n_out, reps):
    n = idx.shape[0]
    r2, d = src.shape
    nb = r2 // 2 // SC_WINDOW

    @pl.kernel(out_type=jax.ShapeDtypeStruct((n_out, d), src.dtype), mesh=_sc_mesh())
    def scatter(x_hbm, i_hbm, o_hbm):
        def body(x_vmem, i_vmem):
            pltpu.sync_copy(x_vmem, o_hbm.at[i_vmem.at[0]])

        pltpu.emit_pipeline(
            body, grid=(n // SC_WINDOW,),
            in_specs=[pl.BlockSpec((SC_WINDOW, d),
                                   index_map=lambda i: ((i // (reps * nb)) * nb + i % nb, 0)),
                      pl.BlockSpec((1, SC_WINDOW), index_map=lambda i: (0, i))],
            out_specs=[],
            core_axis_name=("core", "subcore"),
            dimension_semantics=(pltpu.PARALLEL,),
        )(x_hbm, i_hbm)

    return scatter(src, idx.reshape(1, n))


def _route_tile(x, rs, g_ref, w_ref, b_ref, h_ref, route_ref, cnt_ref, carry_ref):
    tm = x.shape[0]
    ms = jnp.mean(x * x, axis=-1, keepdims=True)
    h = x * lax.rsqrt(ms + RMS_EPS) * g_ref[...]
    _store_packed(h_ref.at[:, rs], h)
    h_hi = h.astype(_BF16)
    h_lo = (h - h_hi.astype(_F32)).astype(_BF16)
    hw = (jnp.dot(h_hi, w_ref[...], preferred_element_type=_F32)
          + jnp.dot(h_lo, w_ref[...], preferred_element_type=_F32))
    logits = hw[:, :LANES] + hw[:, LANES:] + b_ref[...]
    lt = logits.T[:ROUTE_ROWS]
    row = lax.broadcasted_iota(jnp.int32, (ROUTE_ROWS, tm), 0)

    lg = jnp.where(row < MOE_GROUPS, lt, NEG_BIG)
    mg = jnp.max(lg, axis=0, keepdims=True)
    zg = jnp.sum(jnp.exp(lg - mg), axis=0, keepdims=True)
    p_grp = 1.0 / zg
    g_idx = jnp.min(jnp.where(lg == mg, row, ROUTE_ROWS), axis=0, keepdims=True)

    e_row = row - ROUTE_E0
    emask = (e_row >= 0) & (e_row < N_EXPERTS) & ((e_row >> 3) == g_idx)
    le = jnp.where(emask, lt, NEG_BIG)
    m1 = jnp.max(le, axis=0, keepdims=True)
    i1 = jnp.min(jnp.where(le == m1, row, ROUTE_ROWS), axis=0, keepdims=True)
    le2 = jnp.where(row == i1, NEG_BIG, le)
    m2 = jnp.max(le2, axis=0, keepdims=True)
    i2 = jnp.min(jnp.where(le2 == m2, row, ROUTE_ROWS), axis=0, keepdims=True)
    e21 = jnp.exp(m2 - m1)
    c1 = p_grp / (1.0 + e21)
    c2 = p_grp * e21 / (1.0 + e21)

    onehot = jnp.where(row == i1, 1.0, jnp.where(row == i2, 1.0, 0.0)).astype(_BF16)
    upper = (lax.broadcasted_iota(jnp.int32, (tm, tm), 0)
             <= lax.broadcasted_iota(jnp.int32, (tm, tm), 1)).astype(_BF16)
    cum = jnp.dot(onehot, upper, preferred_element_type=_F32) + carry_ref[...]
    r1 = jnp.sum(jnp.where(row == i1, cum, 0.0), axis=0, keepdims=True) - 1.0
    r2 = jnp.sum(jnp.where(row == i2, cum, 0.0), axis=0, keepdims=True) - 1.0
    total = jnp.max(cum, axis=1, keepdims=True)
    carry_ref[...] = total
    cnt_ref[...] = jnp.broadcast_to(total, cnt_ref.shape)

    rows = ((i1 - ROUTE_E0).astype(_F32), (i2 - ROUTE_E0).astype(_F32), c1, c2, r1, r2)
    rrow = lax.broadcasted_iota(jnp.int32, (8, tm), 0)
    route = jnp.zeros((8, tm), _F32)
    for k, val in enumerate(rows):
        route = jnp.where(rrow == k, val, route)
    route_ref[:, rs] = route


def _out_router_kernel(*refs, n_in):
    x_ref = refs[0]
    o_refs = refs[1:1 + n_in]
    w_refs = refs[1 + n_in:1 + 2 * n_in]
    g_ref, wr_ref, br_ref, xnew_ref = refs[1 + 2 * n_in:5 + 2 * n_in]
    route_refs = refs[5 + 2 * n_in:]
    carry_ref = route_refs[-1]

    @pl.when(pl.program_id(0) == 0)
    def _():
        carry_ref[...] = jnp.zeros(carry_ref.shape, _F32)

    rows = x_ref.shape[0] // ROUTE_CHAINS
    for r in range(ROUTE_CHAINS):
        rs = slice(r * rows, (r + 1) * rows)
        acc = x_ref[rs]
        for o_ref, w_ref in zip(o_refs, w_refs):
            acc = acc + jnp.dot(o_ref[rs], w_ref[...], preferred_element_type=_F32)
        xnew_ref[rs] = acc
        _route_tile(acc, rs, g_ref, wr_ref, br_ref, *route_refs)


def _out_router(x2, os_, ws, g, w_router, b_router, tm=1024):
    n, d = x2.shape
    w_hi = w_router.astype(_BF16)
    w_lo = (w_router - w_hi.astype(_F32)).astype(_BF16)
    const = lambda shape: pl.BlockSpec(shape, lambda i: (0,) * len(shape))
    row = pl.BlockSpec((tm, d), lambda i: (i, 0))
    in_specs = [row] + [pl.BlockSpec((tm, o.shape[1]), lambda i: (i, 0)) for o in os_]
    in_specs += [const(w.shape) for w in ws]
    in_specs += [const((1, d)), const((d, 2 * LANES)), const((1, LANES))]
    return pl.pallas_call(
        functools.partial(_out_router_kernel, n_in=len(os_)), grid=(n // tm,),
        in_specs=in_specs,
        out_specs=[row, pl.BlockSpec((2, tm, PACK_W), lambda i: (0, i, 0)),
                   pl.BlockSpec((8, tm), lambda i: (0, i)), const((ROUTE_ROWS, LANES))],
        out_shape=[jax.ShapeDtypeStruct((n, d), _F32),
                   jax.ShapeDtypeStruct((2, n, PACK_W), jnp.int32),
                   jax.ShapeDtypeStruct((8, n), _F32),
                   jax.ShapeDtypeStruct((ROUTE_ROWS, LANES), _F32)],
        scratch_shapes=[pltpu.VMEM((ROUTE_ROWS, 1), _F32)],
        compiler_params=_params("arbitrary"),
        name="out_router",
    )(x2, *os_, *ws, g.reshape(1, d), jnp.concatenate([w_hi, w_lo], axis=1), b_router)


def _expert_kernel(te_ref, nv_ref, nxt_ref, par_ref, xs_ref, wg_hbm, wu_hbm, wd_hbm, ys_ref,
                   wgu_s, wd_s, bg, bu, bd, sem):
    def copies(e, slot):
        return [pltpu.make_async_copy(src.at[e], dst.at[slot], sem.at[slot, k])
                for k, (src, dst) in enumerate(((wg_hbm, bg), (wu_hbm, bu), (wd_hbm, bd)))]

    @pl.when(pl.program_id(0) == 0)
    def _():
        for c in copies(te_ref[0], par_ref[0]):
            c.start()

    for t in range(MOE_STEP_TILES):
        _expert_tile(pl.program_id(0) * MOE_STEP_TILES + t, slice(t * MOE_TM, (t + 1) * MOE_TM),
                     copies, te_ref, nv_ref, nxt_ref, par_ref, xs_ref, ys_ref,
                     wgu_s, wd_s, bg, bu, bd)


def _expert_tile(j, ts, copies, te_ref, nv_ref, nxt_ref, par_ref, xs_ref, ys_ref,
                 wgu_s, wd_s, bg, bu, bd):
    prev = te_ref[jnp.maximum(j - 1, 0)]

    @pl.when((j == 0) | (te_ref[j] != prev))
    def _():
        slot = par_ref[j]
        for c in copies(te_ref[j], slot):
            c.wait()

        @pl.when(nxt_ref[j] >= 0)
        def _():
            for c in copies(nxt_ref[j], 1 - slot):
                c.start()

        wgu_s[:, :MOE_D_FF] = bg[slot].astype(_BF16)
        wgu_s[:, MOE_D_FF:] = bu[slot].astype(_BF16)
        wd_s[...] = bd[slot].astype(_BF16)

    @pl.when(j < nv_ref[0])
    def _():
        xs = _load_packed(xs_ref[:, ts]).astype(_BF16)
        au = jnp.dot(xs, wgu_s[...], preferred_element_type=_F32)
        a, u = au[:, :MOE_D_FF], au[:, MOE_D_FF:]
        act = (a * (1.0 / (1.0 + jnp.exp(-a))) * u).astype(_BF16)
        _store_packed(ys_ref.at[:, ts], jnp.dot(act, wd_s[...], preferred_element_type=_F32))


def _experts(tile_expert, n_valid, xs, w_gate, w_up, w_down, tm=MOE_TM):
    _, n_slots, pw = xs.shape
    n_tiles = n_slots // tm
    d, f = w_gate.shape[-2:]
    st = MOE_STEP_TILES
    assert n_tiles % st == 0
    row = lambda s, te, nv, nxt, par: (0, jnp.maximum(jnp.minimum(s, (nv[0] - 1) // st), 0), 0)
    change = (tile_expert[1:] != tile_expert[:-1]).astype(jnp.int32)
    parity = jnp.concatenate([jnp.zeros((1,), jnp.int32), jnp.cumsum(change)]) % 2
    after = jnp.searchsorted(tile_expert, tile_expert, side="right")
    nxt = jnp.where(after < n_tiles, tile_expert[jnp.minimum(after, n_tiles - 1)], -1).astype(jnp.int32)
    hbm = pl.BlockSpec(memory_space=pl.ANY)
    grid_spec = pltpu.PrefetchScalarGridSpec(
        num_scalar_prefetch=4, grid=(n_tiles // st,),
        in_specs=[pl.BlockSpec((2, st * tm, pw), row), hbm, hbm, hbm],
        out_specs=pl.BlockSpec((2, st * tm, pw), row),
        scratch_shapes=[pltpu.VMEM((d, 2 * f), _BF16), pltpu.VMEM((f, d), _BF16),
                        pltpu.VMEM((2, d, f), _F32), pltpu.VMEM((2, d, f), _F32),
                        pltpu.VMEM((2, f, d), _F32), pltpu.SemaphoreType.DMA((2, 3))])
    return pl.pallas_call(
        _expert_kernel, grid_spec=grid_spec,
        out_shape=jax.ShapeDtypeStruct((2, n_slots, pw), jnp.int32),
        compiler_params=_params("arbitrary"),
        name="experts",
    )(tile_expert, n_valid, nxt, parity, xs, w_gate, w_up, w_down)


def _combine_kernel(x_ref, y_ref, route_ref, o_ref):
    o_ref[...] = _combine_tile(x_ref, y_ref, route_ref)


def _combine(x2, y, route, tm=512):
    n, d = x2.shape
    row = pl.BlockSpec((tm, d), lambda i: (i, 0))
    return pl.pallas_call(
        _combine_kernel, grid=(n // tm,),
        in_specs=[row, pl.BlockSpec((2, 2, tm, PACK_W), lambda i: (0, 0, i, 0)),
                  pl.BlockSpec((8, tm), lambda i: (0, i))],
        out_specs=row,
        out_shape=jax.ShapeDtypeStruct((n, d), _F32),
        compiler_params=_params("parallel"),
        name="moe_combine",
    )(x2, y, route)


def _slot_kernel(offset_ref, route_ref, slot_ref, *, n_slots):
    expert = route_ref[0:2, :].astype(jnp.int32)
    pos = route_ref[4:6, :].astype(jnp.int32)
    for e in range(N_EXPERTS):
        pos = pos + jnp.where(expert == e, offset_ref[e], 0)
    slot_ref[0:2, :] = pos
    slot_ref[2:4, :] = pos + n_slots


def _slots(offset, route, n_slots):
    n = route.shape[1]
    return pl.pallas_call(
        functools.partial(_slot_kernel, n_slots=n_slots),
        in_specs=[pl.BlockSpec(memory_space=pltpu.SMEM),
                  pl.BlockSpec((8, n), lambda: (0, 0))],
        out_specs=pl.BlockSpec((4, n), lambda: (0, 0)),
        out_shape=jax.ShapeDtypeStruct((4, n), jnp.int32),
        name="moe_slots",
    )(offset, route)


def _router_params(wg, bg, we, be):
    d = wg.shape[0]
    w_router = jnp.zeros((d, LANES), _F32)
    w_router = w_router.at[:, :MOE_GROUPS].set(wg)
    w_router = w_router.at[:, ROUTE_E0:ROUTE_E0 + N_EXPERTS].set(
        jnp.moveaxis(we, 0, 1).reshape(d, N_EXPERTS))
    b_router = jnp.zeros((1, LANES), _F32)
    b_router = b_router.at[0, :MOE_GROUPS].set(bg)
    b_router = b_router.at[0, ROUTE_E0:ROUTE_E0 + N_EXPERTS].set(be.reshape(-1))
    return w_router, b_router


def _moe_experts(h, route, cnt, w_gate, w_up, w_down, layer):
    n = h.shape[1]
    d = w_gate.shape[-2]
    tm = MOE_TM
    counts = cnt[ROUTE_E0:ROUTE_E0 + N_EXPERTS, 0].astype(jnp.int32)
    tiles_per = (counts + tm - 1) // tm
    tiles_end = jnp.cumsum(tiles_per)
    offset = (tiles_end - tiles_per) * tm
    n_tiles = (2 * n) // tm + N_EXPERTS
    n_valid = tiles_end[-1]
    tile_ids = jnp.minimum(jnp.arange(n_tiles, dtype=jnp.int32), n_valid - 1)
    tile_expert = jnp.sum(tile_ids[:, None] >= tiles_end[None, :], axis=1).astype(jnp.int32)
    n_slots = n_tiles * tm
    slot = _slots(offset, route, n_slots).reshape(-1)

    xs = _sc_scatter(h.reshape(2 * n, PACK_W), slot, 2 * n_slots, reps=2)
    ys = _experts(tile_expert + layer * N_EXPERTS, n_valid.reshape(1),
                  xs.reshape(2, n_slots, PACK_W), w_gate.reshape(-1, d, MOE_D_FF),
                  w_up.reshape(-1, d, MOE_D_FF), w_down.reshape(-1, MOE_D_FF, d))
    y = _sc_gather(ys.reshape(2 * n_slots, PACK_W), slot)
    return y.reshape(2, 2, n, PACK_W)


def _proj_prep_pending(x2, pending, *args, **kwargs):
    outs = _proj_prep(x2, *args, pending=pending, **kwargs)
    return (x2, outs) if pending is None else (outs[0], outs[1:])


def _mixer_ab(x2, pending, router, bsz, seq, g, w_in, a_qn, a_kn, b_qn, b_kn, b_sink, w_out):
    aw = A_HEADS * HEAD_DIM
    bqw = B_HEADS * HEAD_DIM
    bkw = B_KV_HEADS * HEAD_DIM
    scale2 = QK_SCALE * LOG2E
    segs = [_Seg(0, aw, True, False, scale2, "plain"),
            _Seg(aw, aw, True, False, 1.0, "T"),
            _Seg(2 * aw, aw, False, False, 1.0, "heads_ones"),
            _Seg(3 * aw, bqw, True, False, scale2, "plain"),
            _Seg(3 * aw + bqw, bkw, True, False, 1.0, "T"),
            _Seg(3 * aw + bqw + bkw, bkw, False, False, 1.0, "heads_ones")]
    x2, (qa, ka, va, qb, kb, vb) = _proj_prep_pending(
        x2, pending, bsz, seq, g, w_in.astype(_BF16), segs, [a_qn, a_kn, None, b_qn, b_kn, None])

    bound_a = HEAD_DIM * scale2 * jnp.max(jnp.abs(a_qn)) * jnp.max(jnp.abs(a_kn))
    bound_b = HEAD_DIM * scale2 * jnp.max(jnp.abs(b_qn)) * jnp.max(jnp.abs(b_kn))
    static_ok = jnp.maximum(bound_a, bound_b) <= SHIFT_MAX
    shift_a = jnp.where(static_ok, bound_a, 0.0)
    shift_b = jnp.where(static_ok, bound_b, 0.0)
    slopes = _alibi_slopes(A_HEADS + B_HEADS)
    bias_a = _dil_bias(slopes[0::2], shift_a)
    bias_b = _band_bias(B_WINDOW, 1.0, slopes[1::2], shift_b, B_HEADS // B_KV_HEADS)
    sink2 = b_sink.astype(_F32) * LOG2E - shift_b

    def attend(online, qa, ka, va, qb, kb, vb, bias_a, bias_b, sink2):
        return (_dil_attn(qa, ka, va, bias_a, online=online),
                _band_b(qb, kb, vb, bias_b, sink2, online=online))

    oa, ob = lax.cond(static_ok, functools.partial(attend, False), functools.partial(attend, True),
                      qa, ka, va, qb, kb, vb, bias_a, bias_b, sink2)
    w_out = w_out.astype(_BF16)
    return _out_router(x2, [oa.reshape(-1, aw), ob.reshape(-1, bqw)], [w_out[:aw], w_out[aw:]],
                       *router)


def _mixer_c(x2, pending, router, bsz, seq, g, w_in, qn, kn, w_out):
    qw = C_HEADS * HEAD_DIM
    kvw = C_KV_HEADS * HEAD_DIM
    rope = _rope_tables(seq)
    bound = HEAD_DIM * QK_SCALE * LOG2E * jnp.max(jnp.abs(qn)) * jnp.max(jnp.abs(kn))
    static_ok = bound <= SHIFT_MAX
    shift = jnp.where(static_ok, bound, 0.0)
    segs = [_Seg(0, qw, True, True, QK_SCALE * LOG2E, "T_one"),
            _Seg(qw, kvw, True, True, 1.0, "heads_shift"),
            _Seg(qw + kvw, kvw, False, False, 1.0, "T_ones")]
    x2, (qt, k, vt) = _proj_prep_pending(x2, pending, bsz, seq, g, w_in.astype(_BF16), segs,
                                         [qn, kn, None], rope=rope, shift=shift)
    o = lax.cond(static_ok,
                 functools.partial(_dense_attn, online=False),
                 functools.partial(_dense_attn, online=True), qt, k, vt)
    return _out_router(x2, [o.reshape(-1, qw)], [w_out.astype(_BF16)], *router)


def kernel(x, mix_norm, ffn_norm, ab_w_in, a_q_norm, a_k_norm, b_q_norm, b_k_norm, b_sink, ab_w_out,
           c_w_in, c_q_norm, c_k_norm, c_w_out, moe_group_w, moe_group_b, moe_expert_w, moe_expert_b,
           moe_w_gate, moe_w_up, moe_w_down):
    bsz, seq, d = x.shape
    x2 = x.reshape(bsz * seq, d)
    depth = mix_norm.shape[0]
    pending = None
    for layer in range(depth):
        i = layer // 2
        router = (ffn_norm[layer],) + _router_params(moe_group_w[layer], moe_group_b[layer],
                                                     moe_expert_w[layer], moe_expert_b[layer])
        if layer % 2 == 0:
            x2, h, route, cnt = _mixer_ab(x2, pending, router, bsz, seq, mix_norm[layer], ab_w_in[i],
                                          a_q_norm[i], a_k_norm[i], b_q_norm[i], b_k_norm[i],
                                          b_sink[i], ab_w_out[i])
        else:
            x2, h, route, cnt = _mixer_c(x2, pending, router, bsz, seq, mix_norm[layer], c_w_in[i],
                                         c_q_norm[i], c_k_norm[i], c_w_out[i])
        pending = (_moe_experts(h, route, cnt, moe_w_gate, moe_w_up, moe_w_down, layer), route)
    return _combine(x2, *pending).reshape(bsz, seq, d)
```

```python
import functools
import math
from typing import NamedTuple

import jax
import jax.numpy as jnp
import numpy as np
from jax import lax
from jax.experimental import pallas as pl
from jax.experimental.pallas import tpu as pltpu
from jax.experimental.pallas import tpu_sc as plsc

HEAD_DIM = 64
LANES = 128
A_HEADS = 8
B_HEADS = 8
B_KV_HEADS = 2
C_HEADS = 16
C_KV_HEADS = 4
A_PATTERNS = ((128, 1), (512, 4), (2048, 16))
B_WINDOW = 128
GRID_W = 64
ROPE_THETA = 10000.0
ROPE_AXIS_DIM = HEAD_DIM // 2
ALIBI_MAX_BIAS = 8.0
RMS_EPS = 1e-6
MOE_GROUPS = 4
MOE_EXPERTS = 8
N_EXPERTS = MOE_GROUPS * MOE_EXPERTS
MOE_D_FF = 256
QK_SCALE = HEAD_DIM ** -0.5
LOG2E = math.log2(math.e)
SHIFT_MAX = 60.0
NEG_BIG = -1e30
VMEM_LIMIT = 52 * 1024 * 1024

BAND_BQ = 256
BAND_KB = 128
BAND_KW = BAND_BQ + 2 * BAND_KB
BAND_SUBS = 8
DIL_BQ = 256
DIL_SUBS = 8
DIL_REACH = max(w // 2 for w, _ in A_PATTERNS)
DIL_KW = DIL_BQ + 2 * DIL_REACH
DIL_CHUNK = 768
DENSE_BQ = 1024
DENSE_KCH = 2048
MOE_TM = 512
MOE_STEP_TILES = 4
ROUTE_E0 = 4
ROUTE_CHAINS = 1
ROUTE_ROWS = 48
PACK_W = 256
SC_WINDOW = 128

_BF16 = jnp.bfloat16
_F32 = jnp.float32


def _params(*sem):
    return pltpu.CompilerParams(dimension_semantics=sem, vmem_limit_bytes=VMEM_LIMIT)


def _alibi_slopes(n):
    return np.asarray(2.0 ** (-ALIBI_MAX_BIAS * np.arange(1, n + 1) / n), dtype=np.float32)


def _head_norm(y, gain, head_ones):
    y2 = y * y
    hi = y2.astype(_BF16)
    lo = (y2 - hi.astype(_F32)).astype(_BF16)
    ss = (jnp.dot(hi, head_ones, preferred_element_type=_F32)
          + jnp.dot(lo, head_ones, preferred_element_type=_F32))
    return y * lax.rsqrt(ss * (1.0 / HEAD_DIM) + RMS_EPS) * gain


def _rope(y, cos, sin):
    lane = lax.broadcasted_iota(jnp.int32, y.shape, 1)
    first = (lane % 32) < 16
    partner = jnp.where(first, pltpu.roll(y, LANES - 16, axis=1), pltpu.roll(y, 16, axis=1))
    return y * cos + partner * sin


class _Seg(NamedTuple):
    col0: int
    ncols: int
    norm: bool
    rope: bool
    scale: float
    mode: str


def _prep_slab(y, c, o_ref, seg, gain_ref, cos_ref, sin_ref, shift_ref, head_ones):
    rows = y.shape[0]
    if seg.norm:
        y = _head_norm(y, gain_ref[...], head_ones)
    if seg.rope:
        y = _rope(y, cos_ref[...], sin_ref[...])
    if seg.scale != 1.0:
        y = y * seg.scale
    sl = slice(c * LANES, (c + 1) * LANES)
    if seg.mode == "plain":
        o_ref[:, sl] = y.astype(_BF16)
    elif seg.mode == "T":
        o_ref[sl, :] = y.T.astype(_BF16)
    elif seg.mode in ("heads_ones", "heads_shift"):
        lane = lax.broadcasted_iota(jnp.int32, y.shape, 1)
        if seg.mode == "heads_ones":
            fill = jnp.ones(y.shape, _F32)
        else:
            fill = jnp.where(lane == HEAD_DIM, -shift_ref[0], 0.0)
        for k, yk in enumerate((y, pltpu.roll(y, HEAD_DIM, axis=1))):
            o_ref[2 * c + k] = jnp.where(lane < HEAD_DIM, yk, fill).astype(_BF16)
    else:
        assert seg.mode in ("T_one", "T_ones")
        yt = y.T
        if seg.mode == "T_one":
            row = lax.broadcasted_iota(jnp.int32, (HEAD_DIM, rows), 0)
            extra = jnp.where(row == 0, 1.0, 0.0)
        else:
            extra = jnp.ones((HEAD_DIM, rows), _F32)
        for k in range(2):
            ext = jnp.concatenate([yt[k * HEAD_DIM:(k + 1) * HEAD_DIM], extra], axis=0)
            o_ref[(2 * c + k) * LANES:(2 * c + k + 1) * LANES, :] = ext.astype(_BF16)


def _combine_tile(x_ref, y_ref, route_ref):
    coef = route_ref[...].T
    y0 = _load_packed(y_ref[:, 0])
    y1 = _load_packed(y_ref[:, 1])
    return x_ref[...] + coef[:, 2:3] * y0 + coef[:, 3:4] * y1


def _proj_prep_kernel(*refs, segs, combine):
    it = iter(refs)
    shift_ref = next(it) if any(s.mode == "heads_shift" for s in segs) else None
    x_ref, g_ref, w_ref = next(it), next(it), next(it)
    y_ref, route_ref = (next(it), next(it)) if combine else (None, None)
    gain_refs = [next(it) if s.norm else None for s in segs]
    cos_ref, sin_ref = (next(it), next(it)) if any(s.rope for s in segs) else (None, None)
    outs = list(it)
    if combine:
        x = _combine_tile(x_ref, y_ref, route_ref)
        outs.pop(0)[...] = x
    else:
        x = x_ref[...]
    ms = jnp.mean(x * x, axis=-1, keepdims=True)
    h = (x * lax.rsqrt(ms + RMS_EPS) * g_ref[...]).astype(_BF16)
    proj = jnp.dot(h, w_ref[...], preferred_element_type=_F32)
    head_ones = (lax.broadcasted_iota(jnp.int32, (LANES, LANES), 0) // HEAD_DIM
                 == lax.broadcasted_iota(jnp.int32, (LANES, LANES), 1) // HEAD_DIM).astype(_BF16)
    for seg, gain_ref, o_ref in zip(segs, gain_refs, outs):
        for c in range(seg.ncols // LANES):
            y = proj[:, seg.col0 + c * LANES:seg.col0 + (c + 1) * LANES]
            _prep_slab(y, c, o_ref, seg, gain_ref, cos_ref, sin_ref, shift_ref, head_ones)


def _proj_prep(x2, bsz, seq, g, w, segs, gains, rope=None, shift=None, pending=None, tm=512):
    n, d = x2.shape
    p = w.shape[1]
    nt = seq // tm
    assert seq % tm == 0
    const = lambda shape: pl.BlockSpec(shape, lambda i: (0,) * len(shape))
    row = pl.BlockSpec((tm, d), lambda i: (i, 0))
    ins, in_specs = [], []
    if shift is not None:
        ins.append(shift.reshape(1).astype(_F32))
        in_specs.append(pl.BlockSpec(memory_space=pltpu.SMEM))
    ins += [x2, g.reshape(1, d), w]
    in_specs += [row, const((1, d)), const((d, p))]
    if pending is not None:
        ins += list(pending)
        in_specs += [pl.BlockSpec((2, 2, tm, PACK_W), lambda i: (0, 0, i, 0)),
                     pl.BlockSpec((8, tm), lambda i: (0, i))]
    for seg, gain in zip(segs, gains):
        if seg.norm:
            ins.append(jnp.tile(gain.astype(_F32), LANES // HEAD_DIM).reshape(1, LANES))
            in_specs.append(const((1, LANES)))
    if rope is not None:
        ins += list(rope)
        in_specs += [pl.BlockSpec((tm, LANES), lambda i: (i % nt, 0))] * 2
    out_shape, out_specs = [], []
    if pending is not None:
        out_shape.append(jax.ShapeDtypeStruct((n, d), _F32))
        out_specs.append(row)
    for seg in segs:
        nc = seg.ncols
        if seg.mode == "plain":
            out_shape.append(jax.ShapeDtypeStruct((bsz, seq, nc), _BF16))
            out_specs.append(pl.BlockSpec((None, tm, nc), lambda i: (i // nt, i % nt, 0)))
        elif seg.mode in ("T", "T_one", "T_ones"):
            wout = nc if seg.mode == "T" else 2 * nc
            out_shape.append(jax.ShapeDtypeStruct((bsz, wout, seq), _BF16))
            out_specs.append(pl.BlockSpec((None, wout, tm), lambda i: (i // nt, 0, i % nt)))
        else:
            assert seg.mode in ("heads_ones", "heads_shift")
            nh = nc // HEAD_DIM
            out_shape.append(jax.ShapeDtypeStruct((bsz, nh, seq, LANES), _BF16))
            out_specs.append(pl.BlockSpec((None, nh, tm, LANES), lambda i: (i // nt, 0, i % nt, 0)))
    return pl.pallas_call(
        functools.partial(_proj_prep_kernel, segs=tuple(segs), combine=pending is not None),
        grid=(n // tm,),
        in_specs=in_specs, out_specs=out_specs, out_shape=out_shape,
        compiler_params=_params("parallel"),
        name="proj_prep",
    )(*ins)


def _rope_tables(seq):
    t = np.arange(seq)
    row = (t // GRID_W).astype(np.float32)
    col = (t % GRID_W).astype(np.float32)
    inv_freq = np.float32(ROPE_THETA) ** (-np.arange(0, ROPE_AXIS_DIM, 2, dtype=np.float32)
                                          / np.float32(ROPE_AXIS_DIM))
    ang_r = row[:, None] * inv_freq[None, :]
    ang_c = col[:, None] * inv_freq[None, :]
    cr, sr, cc, sc = np.cos(ang_r), np.sin(ang_r), np.cos(ang_c), np.sin(ang_c)
    cos = np.concatenate([cr, cr, cc, cc], axis=-1)
    sin = np.concatenate([-sr, sr, -sc, sc], axis=-1)
    return (jnp.asarray(np.tile(cos, (1, 2)), _F32), jnp.asarray(np.tile(sin, (1, 2)), _F32))


def _band_bias(hw, dist_scale, slopes, shift, stack):
    bq, kb = BAND_BQ, BAND_KB
    assert hw <= kb
    r = np.arange(bq)[:, None]
    c = np.arange(BAND_KW)[None, :]
    rel = np.abs(c - kb - r)
    dist = rel.astype(np.float32) * np.float32(dist_scale)
    ok = np.stack([(rel <= hw) & ~(first & (c < kb)) & ~(last & (c >= kb + bq))
                   for first, last in ((False, False), (True, False), (False, True), (True, True))])
    alibi = -(np.asarray(slopes, np.float32)[:, None, None] * dist[None]) * np.float32(LOG2E)
    bias = jnp.where(ok[:, None], jnp.asarray(alibi)[None] - shift, NEG_BIG)
    nh = len(slopes)
    return bias.reshape(4, nh // stack, stack * bq, BAND_KW)


def _window_blocks(nb):
    per = BAND_BQ * BAND_SUBS // BAND_KB
    last = nb * per - 1
    fns = [lambda i: jnp.maximum(per * i - 1, 0)]
    fns += [functools.partial(lambda j, i: per * i + j, j) for j in range(per)]
    fns += [lambda i: jnp.minimum(per * i + per, last)]
    return fns


def _dil_bias(slopes, shift):
    r = jnp.arange(DIL_BQ, dtype=jnp.int32)[:, None]
    u = jnp.arange(DIL_BQ + 4 * DIL_REACH, dtype=jnp.int32)[None, :]
    delta = u - 2 * DIL_REACH - r
    dist = jnp.abs(delta)
    mult = sum(((delta % d == 0) & (dist <= ((w // 2) // d) * d)).astype(_F32) for w, d in A_PATTERNS)
    alibi = -(jnp.asarray(slopes, _F32)[:, None, None] * dist.astype(_F32)[None]) * LOG2E
    return jnp.where((mult > 0)[None], jnp.log2(jnp.maximum(mult, 1.0))[None] + alibi - shift, NEG_BIG)


def _dil_attn_kernel(q_ref, kt_ref, v_ref, bias_ref, o_ref, *, online):
    bq = DIL_BQ
    seq = kt_ref.shape[1]
    for sub in range(q_ref.shape[0] // bq):
        qs = slice(sub * bq, (sub + 1) * bq)
        q0 = pl.program_id(2) * q_ref.shape[0] + sub * bq
        w0 = pl.multiple_of(jnp.clip(q0 - DIL_REACH, 0, seq - DIL_KW), bq)
        u0 = pl.multiple_of(w0 - q0 + 2 * DIL_REACH, bq)
        q = q_ref[qs, :]
        outs = []
        for j in range(2):
            rows = slice(j * HEAD_DIM, (j + 1) * HEAD_DIM)
            scores = []
            for c in range(DIL_KW // DIL_CHUNK):
                ks = pl.ds(pl.multiple_of(w0 + c * DIL_CHUNK, bq), DIL_CHUNK)
                us = pl.ds(pl.multiple_of(u0 + c * DIL_CHUNK, bq), DIL_CHUNK)
                scores.append(jnp.dot(q[:, rows], kt_ref[rows, ks], preferred_element_type=_F32)
                              + bias_ref[j, :, us])
            if online:
                m = functools.reduce(jnp.maximum,
                                     [jnp.max(s, axis=-1, keepdims=True) for s in scores])
                scores = [s - m for s in scores]
            acc = jnp.zeros((bq, LANES), _F32)
            for c, s in enumerate(scores):
                ks = pl.ds(pl.multiple_of(w0 + c * DIL_CHUNK, bq), DIL_CHUNK)
                acc += jnp.dot(jnp.exp2(s).astype(_BF16), v_ref[j, ks, :],
                               preferred_element_type=_F32)
            outs.append((acc / pltpu.roll(acc, HEAD_DIM, axis=1))[:, :HEAD_DIM])
        o_ref[qs, :] = jnp.concatenate(outs, axis=1).astype(_BF16)


def _dil_attn(q, kt, v, bias, *, online):
    bsz, seq, w = q.shape
    bq = DIL_BQ * DIL_SUBS
    assert seq >= DIL_KW and seq % bq == 0
    return pl.pallas_call(
        functools.partial(_dil_attn_kernel, online=online), grid=(A_HEADS // 2, bsz, seq // bq),
        in_specs=[pl.BlockSpec((None, bq, LANES), lambda p, b, i: (b, i, p)),
                  pl.BlockSpec((None, LANES, seq), lambda p, b, i: (b, p, 0)),
                  pl.BlockSpec((None, 2, seq, LANES), lambda p, b, i: (b, p, 0, 0)),
                  pl.BlockSpec((2, DIL_BQ, bias.shape[-1]), lambda p, b, i: (p, 0, 0))],
        out_specs=pl.BlockSpec((None, bq, LANES), lambda p, b, i: (b, i, p)),
        out_shape=jax.ShapeDtypeStruct((bsz, seq, w), _BF16),
        compiler_params=_params("parallel", "parallel", "parallel"),
        name="dil_attn_online" if online else "dil_attn",
    )(q, kt, v, bias)


def _band_b_kernel(sink2_ref, sinkp_ref, q_ref, *refs, online):
    bq = BAND_BQ
    g = pl.program_id(1)
    grp = B_HEADS // B_KV_HEADS
    npc = (BAND_BQ * BAND_SUBS + 2 * BAND_KB) // BAND_KB
    k_refs, v_refs = refs[:npc], refs[npc:2 * npc]
    bias_first, bias_mid, bias_last, o_ref = refs[2 * npc:]
    kt_all = jnp.concatenate([r[...] for r in k_refs], axis=1)
    v_all = jnp.concatenate([r[...] for r in v_refs], axis=0)
    for sub in range(BAND_SUBS):
        qs = slice(sub * bq, (sub + 1) * bq)
        ks = slice(sub * bq, sub * bq + BAND_KW)
        bias_ref = bias_first if sub == 0 else bias_last if sub == BAND_SUBS - 1 else bias_mid
        q = q_ref[qs, :]
        q4 = jnp.concatenate([q[:, i * HEAD_DIM:(i + 1) * HEAD_DIM] for i in range(grp)], axis=0)
        s = jnp.dot(q4, kt_all[:, ks], preferred_element_type=_F32) + bias_ref[...]
        if online:
            m = jnp.max(s, axis=-1, keepdims=True)
            s = s - m
        acc = jnp.dot(jnp.exp2(s).astype(_BF16), v_all[ks], preferred_element_type=_F32)
        for i in range(grp):
            a = acc[i * bq:(i + 1) * bq]
            if online:
                mi = m[i * bq:(i + 1) * bq]
                sk = sink2_ref[g * grp + i]
                mm = jnp.maximum(mi, sk)
                a = a * jnp.exp2(mi - mm)
                o = a / (pltpu.roll(a, HEAD_DIM, axis=1) + jnp.exp2(sk - mm))
            else:
                o = a / (pltpu.roll(a, HEAD_DIM, axis=1) + sinkp_ref[g * grp + i])
            o_ref[qs, i * HEAD_DIM:(i + 1) * HEAD_DIM] = o[:, :HEAD_DIM].astype(_BF16)


def _band_b(q, kt, v, bias, sink2, *, online):
    bsz, seq_len, w = q.shape
    assert BAND_SUBS >= 2
    bq, kb = BAND_BQ * BAND_SUBS, BAND_KB
    nb = seq_len // bq
    grp = B_HEADS // B_KV_HEADS
    gw = w // B_KV_HEADS
    blocks = _window_blocks(nb)
    kspec = lambda f: pl.BlockSpec((None, HEAD_DIM, kb), lambda b, g, i: (b, g, f(i)))
    vspec = lambda f: pl.BlockSpec((None, None, kb, LANES), lambda b, g, i: (b, g, f(i), 0))
    qspec = pl.BlockSpec((None, bq, gw), lambda b, g, i: (b, i, g))
    smem = pl.BlockSpec(memory_space=pltpu.SMEM)
    bspec = lambda f: pl.BlockSpec((None, None, grp * BAND_BQ, BAND_KW),
                                   lambda b, g, i: (f(i), g, 0, 0))
    variants = [lambda i: jnp.where(i == 0, 1, 0), lambda i: 0,
                lambda i: jnp.where(i == nb - 1, 2, 0)]
    return pl.pallas_call(
        functools.partial(_band_b_kernel, online=online), grid=(bsz, B_KV_HEADS, nb),
        in_specs=[smem, smem, qspec] + [kspec(f) for f in blocks] + [vspec(f) for f in blocks]
        + [bspec(f) for f in variants],
        out_specs=qspec,
        out_shape=jax.ShapeDtypeStruct((bsz, seq_len, w), _BF16),
        compiler_params=_params("parallel", "parallel", "parallel"),
        name="band_b_online" if online else "band_b",
    )(sink2, jnp.exp2(sink2), q, *([kt] * len(blocks)), *([v] * len(blocks)), bias, bias, bias)


def _dense_attn_kernel(qt_ref, k_ref, vt_ref, o_ref, *, kch, online):
    bq = qt_ref.shape[1]
    seq = k_ref.shape[0]
    for c in range(C_HEADS // C_KV_HEADS):
        qt = qt_ref[c * LANES:(c + 1) * LANES, :]
        acc = jnp.zeros((LANES, bq), _F32)
        m = jnp.full((1, bq), NEG_BIG, _F32)
        for j in range(seq // kch):
            ks = slice(j * kch, (j + 1) * kch)
            st = jnp.dot(k_ref[ks, :], qt, preferred_element_type=_F32)
            if online:
                m_new = jnp.maximum(m, jnp.max(st, axis=0, keepdims=True))
                acc = acc * jnp.exp2(m - m_new)
                st = st - m_new
                m = m_new
            acc = acc + jnp.dot(vt_ref[:, ks], jnp.exp2(st).astype(_BF16),
                                preferred_element_type=_F32)
        o = acc[:HEAD_DIM] / acc[HEAD_DIM:]
        o_ref[:, c * HEAD_DIM:(c + 1) * HEAD_DIM] = o.T.astype(_BF16)


def _dense_attn(qt, k, vt, *, online, bq=DENSE_BQ, kch=DENSE_KCH):
    bsz, _, seq = qt.shape
    bq, kch = min(bq, seq), min(kch, seq)
    grp = C_HEADS // C_KV_HEADS
    return pl.pallas_call(
        functools.partial(_dense_attn_kernel, kch=kch, online=online),
        grid=(bsz, C_KV_HEADS, seq // bq),
        in_specs=[pl.BlockSpec((None, grp * LANES, bq), lambda b, g, i: (b, g, i)),
                  pl.BlockSpec((None, None, seq, LANES), lambda b, g, i: (b, g, 0, 0)),
                  pl.BlockSpec((None, LANES, seq), lambda b, g, i: (b, g, 0))],
        out_specs=pl.BlockSpec((None, bq, grp * HEAD_DIM), lambda b, g, i: (b, i, g)),
        out_shape=jax.ShapeDtypeStruct((bsz, seq, C_HEADS * HEAD_DIM), _BF16),
        compiler_params=_params("parallel", "parallel", "parallel"),
        name="dense_attn_online" if online else "dense_attn",
    )(qt, k, vt)


def _pack_pair(a, b):
    wa = lax.bitcast_convert_type(a.astype(_BF16).astype(_F32), jnp.uint32) >> 16
    wb = lax.bitcast_convert_type(b.astype(_BF16).astype(_F32), jnp.uint32) & jnp.uint32(0xFFFF0000)
    return lax.bitcast_convert_type(wa | wb, jnp.int32)


def _unpack_pair(w):
    u = lax.bitcast_convert_type(w, jnp.uint32)
    return (lax.bitcast_convert_type(u << 16, _F32),
            lax.bitcast_convert_type(u & jnp.uint32(0xFFFF0000), _F32))


def _store_packed(ref, y):
    q = PACK_W
    for j in range(2):
        ref[j] = _pack_pair(y[:, 2 * j * q:(2 * j + 1) * q], y[:, (2 * j + 1) * q:(2 * j + 2) * q])


def _load_packed(ref):
    parts = []
    for j in range(2):
        parts += list(_unpack_pair(ref[j]))
    return jnp.concatenate(parts, axis=1)


def _sc_mesh():
    return plsc.VectorSubcoreMesh(core_axis_name="core", subcore_axis_name="subcore")


def _sc_gather(table, idx):
    n = idx.shape[0]
    d = table.shape[1]

    @pl.kernel(out_type=jax.ShapeDtypeStruct((n, d), table.dtype), mesh=_sc_mesh())
    def gather(x_hbm, i_hbm, o_hbm):
        def body(i_vmem, o_vmem):
            pltpu.sync_copy(x_hbm.at[i_vmem.at[0]], o_vmem)

        pltpu.emit_pipeline(
            body, grid=(n // SC_WINDOW,),
            in_specs=[pl.BlockSpec((1, SC_WINDOW), index_map=lambda i: (0, i))],
            out_specs=[pl.BlockSpec((SC_WINDOW, d), index_map=lambda i: (i, 0))],
            core_axis_name=("core", "subcore"),
            dimension_semantics=(pltpu.PARALLEL,),
        )(i_hbm, o_hbm)

    return gather(table, idx.reshape(1, n))


def _sc_scatter(src, idx, n_out, reps):
    n = idx.shape[0]
    r2, d = src.shape
    nb = r2 // 2 // SC_WINDOW

    @pl.kernel(out_type=jax.ShapeDtypeStruct((n_out, d), src.dtype), mesh=_sc_mesh())
    def scatter(x_hbm, i_hbm, o_hbm):
        def body(x_vmem, i_vmem):
            pltpu.sync_copy(x_vmem, o_hbm.at[i_vmem.at[0]])

        pltpu.emit_pipeline(
            body, grid=(n // SC_WINDOW,),
            in_specs=[pl.BlockSpec((SC_WINDOW, d),
                                   index_map=lambda i: ((i // (reps * nb)) * nb + i % nb, 0)),
                      pl.BlockSpec((1, SC_WINDOW), index_map=lambda i: (0, i))],
            out_specs=[],
            core_axis_name=("core", "subcore"),
            dimension_semantics=(pltpu.PARALLEL,),
        )(x_hbm, i_hbm)

    return scatter(src, idx.reshape(1, n))


def _route_tile(x, rs, g_ref, w_ref, b_ref, h_ref, route_ref, cnt_ref, carry_ref):
    tm = x.shape[0]
    ms = jnp.mean(x * x, axis=-1, keepdims=True)
    h = x * lax.rsqrt(ms + RMS_EPS) * g_ref[...]
    _store_packed(h_ref.at[:, rs], h)
    h_hi = h.astype(_BF16)
    h_lo = (h - h_hi.astype(_F32)).astype(_BF16)
    hw = (jnp.dot(h_hi, w_ref[...], preferred_element_type=_F32)
          + jnp.dot(h_lo, w_ref[...], preferred_element_type=_F32))
    logits = hw[:, :LANES] + hw[:, LANES:] + b_ref[...]
    lt = logits.T[:ROUTE_ROWS]
    row = lax.broadcasted_iota(jnp.int32, (ROUTE_ROWS, tm), 0)

    lg = jnp.where(row < MOE_GROUPS, lt, NEG_BIG)
    mg = jnp.max(lg, axis=0, keepdims=True)
    zg = jnp.sum(jnp.exp(lg - mg), axis=0, keepdims=True)
    p_grp = 1.0 / zg
    g_idx = jnp.min(jnp.where(lg == mg, row, ROUTE_ROWS), axis=0, keepdims=True)

    e_row = row - ROUTE_E0
    emask = (e_row >= 0) & (e_row < N_EXPERTS) & ((e_row >> 3) == g_idx)
    le = jnp.where(emask, lt, NEG_BIG)
    m1 = jnp.max(le, axis=0, keepdims=True)
    i1 = jnp.min(jnp.where(le == m1, row, ROUTE_ROWS), axis=0, keepdims=True)
    le2 = jnp.where(row == i1, NEG_BIG, le)
    m2 = jnp.max(le2, axis=0, keepdims=True)
    i2 = jnp.min(jnp.where(le2 == m2, row, ROUTE_ROWS), axis=0, keepdims=True)
    e21 = jnp.exp(m2 - m1)
    c1 = p_grp / (1.0 + e21)
    c2 = p_grp * e21 / (1.0 + e21)

    onehot = jnp.where(row == i1, 1.0, jnp.where(row == i2, 1.0, 0.0)).astype(_BF16)
    upper = (lax.broadcasted_iota(jnp.int32, (tm, tm), 0)
             <= lax.broadcasted_iota(jnp.int32, (tm, tm), 1)).astype(_BF16)
    cum = jnp.dot(onehot, upper, preferred_element_type=_F32) + carry_ref[...]
    r1 = jnp.sum(jnp.where(row == i1, cum, 0.0), axis=0, keepdims=True) - 1.0
    r2 = jnp.sum(jnp.where(row == i2, cum, 0.0), axis=0, keepdims=True) - 1.0
    total = jnp.max(cum, axis=1, keepdims=True)
    carry_ref[...] = total
    cnt_ref[...] = jnp.broadcast_to(total, cnt_ref.shape)

    rows = ((i1 - ROUTE_E0).astype(_F32), (i2 - ROUTE_E0).astype(_F32), c1, c2, r1, r2)
    rrow = lax.broadcasted_iota(jnp.int32, (8, tm), 0)
    route = jnp.zeros((8, tm), _F32)
    for k, val in enumerate(rows):
        route = jnp.where(rrow == k, val, route)
    route_ref[:, rs] = route


def _out_router_kernel(*refs, n_in):
    x_ref = refs[0]
    o_refs = refs[1:1 + n_in]
    w_refs = refs[1 + n_in:1 + 2 * n_in]
    g_ref, wr_ref, br_ref, xnew_ref = refs[1 + 2 * n_in:5 + 2 * n_in]
    route_refs = refs[5 + 2 * n_in:]
    carry_ref = route_refs[-1]

    @pl.when(pl.program_id(0) == 0)
    def _():
        carry_ref[...] = jnp.zeros(carry_ref.shape, _F32)

    rows = x_ref.shape[0] // ROUTE_CHAINS
    for r in range(ROUTE_CHAINS):
        rs = slice(r * rows, (r + 1) * rows)
        acc = x_ref[rs]
        for o_ref, w_ref in zip(o_refs, w_refs):
            acc = acc + jnp.dot(o_ref[rs], w_ref[...], preferred_element_type=_F32)
        xnew_ref[rs] = acc
        _route_tile(acc, rs, g_ref, wr_ref, br_ref, *route_refs)


def _out_router(x2, os_, ws, g, w_router, b_router, tm=1024):
    n, d = x2.shape
    w_hi = w_router.astype(_BF16)
    w_lo = (w_router - w_hi.astype(_F32)).astype(_BF16)
    const = lambda shape: pl.BlockSpec(shape, lambda i: (0,) * len(shape))
    row = pl.BlockSpec((tm, d), lambda i: (i, 0))
    in_specs = [row] + [pl.BlockSpec((tm, o.shape[1]), lambda i: (i, 0)) for o in os_]
    in_specs += [const(w.shape) for w in ws]
    in_specs += [const((1, d)), const((d, 2 * LANES)), const((1, LANES))]
    return pl.pallas_call(
        functools.partial(_out_router_kernel, n_in=len(os_)), grid=(n // tm,),
        in_specs=in_specs,
        out_specs=[row, pl.BlockSpec((2, tm, PACK_W), lambda i: (0, i, 0)),
                   pl.BlockSpec((8, tm), lambda i: (0, i)), const((ROUTE_ROWS, LANES))],
        out_shape=[jax.ShapeDtypeStruct((n, d), _F32),
                   jax.ShapeDtypeStruct((2, n, PACK_W), jnp.int32),
                   jax.ShapeDtypeStruct((8, n), _F32),
                   jax.ShapeDtypeStruct((ROUTE_ROWS, LANES), _F32)],
        scratch_shapes=[pltpu.VMEM((ROUTE_ROWS, 1), _F32)],
        compiler_params=_params("arbitrary"),
        name="out_router",
    )(x2, *os_, *ws, g.reshape(1, d), jnp.concatenate([w_hi, w_lo], axis=1), b_router)


def _expert_kernel(te_ref, nv_ref, nxt_ref, par_ref, xs_ref, wg_hbm, wu_hbm, wd_hbm, ys_ref,
                   wgu_s, wd_s, bg, bu, bd, sem):
    def copies(e, slot):
        return [pltpu.make_async_copy(src.at[e], dst.at[slot], sem.at[slot, k])
                for k, (src, dst) in enumerate(((wg_hbm, bg), (wu_hbm, bu), (wd_hbm, bd)))]

    @pl.when(pl.program_id(0) == 0)
    def _():
        for c in copies(te_ref[0], par_ref[0]):
            c.start()

    for t in range(MOE_STEP_TILES):
        _expert_tile(pl.program_id(0) * MOE_STEP_TILES + t, slice(t * MOE_TM, (t + 1) * MOE_TM),
                     copies, te_ref, nv_ref, nxt_ref, par_ref, xs_ref, ys_ref,
                     wgu_s, wd_s, bg, bu, bd)


def _expert_tile(j, ts, copies, te_ref, nv_ref, nxt_ref, par_ref, xs_ref, ys_ref,
                 wgu_s, wd_s, bg, bu, bd):
    prev = te_ref[jnp.maximum(j - 1, 0)]

    @pl.when((j == 0) | (te_ref[j] != prev))
    def _():
        slot = par_ref[j]
        for c in copies(te_ref[j], slot):
            c.wait()

        @pl.when(nxt_ref[j] >= 0)
        def _():
            for c in copies(nxt_ref[j], 1 - slot):
                c.start()

        wgu_s[:, :MOE_D_FF] = bg[slot].astype(_BF16)
        wgu_s[:, MOE_D_FF:] = bu[slot].astype(_BF16)
        wd_s[...] = bd[slot].astype(_BF16)

    @pl.when(j < nv_ref[0])
    def _():
        xs = _load_packed(xs_ref[:, ts]).astype(_BF16)
        au = jnp.dot(xs, wgu_s[...], preferred_element_type=_F32)
        a, u = au[:, :MOE_D_FF], au[:, MOE_D_FF:]
        act = (a * (1.0 / (1.0 + jnp.exp(-a))) * u).astype(_BF16)
        _store_packed(ys_ref.at[:, ts], jnp.dot(act, wd_s[...], preferred_element_type=_F32))


def _experts(tile_expert, n_valid, xs, w_gate, w_up, w_down, tm=MOE_TM):
    _, n_slots, pw = xs.shape
    n_tiles = n_slots // tm
    d, f = w_gate.shape[-2:]
    st = MOE_STEP_TILES
    assert n_tiles % st == 0
    row = lambda s, te, nv, nxt, par: (0, jnp.maximum(jnp.minimum(s, (nv[0] - 1) // st), 0), 0)
    change = (tile_expert[1:] != tile_expert[:-1]).astype(jnp.int32)
    parity = jnp.concatenate([jnp.zeros((1,), jnp.int32), jnp.cumsum(change)]) % 2
    after = jnp.searchsorted(tile_expert, tile_expert, side="right")
    nxt = jnp.where(after < n_tiles, tile_expert[jnp.minimum(after, n_tiles - 1)], -1).astype(jnp.int32)
    hbm = pl.BlockSpec(memory_space=pl.ANY)
    grid_spec = pltpu.PrefetchScalarGridSpec(
        num_scalar_prefetch=4, grid=(n_tiles // st,),
        in_specs=[pl.BlockSpec((2, st * tm, pw), row), hbm, hbm, hbm],
        out_specs=pl.BlockSpec((2, st * tm, pw), row),
        scratch_shapes=[pltpu.VMEM((d, 2 * f), _BF16), pltpu.VMEM((f, d), _BF16),
                        pltpu.VMEM((2, d, f), _F32), pltpu.VMEM((2, d, f), _F32),
                        pltpu.VMEM((2, f, d), _F32), pltpu.SemaphoreType.DMA((2, 3))])
    return pl.pallas_call(
        _expert_kernel, grid_spec=grid_spec,
        out_shape=jax.ShapeDtypeStruct((2, n_slots, pw), jnp.int32),
        compiler_params=_params("arbitrary"),
        name="experts",
    )(tile_expert, n_valid, nxt, parity, xs, w_gate, w_up, w_down)


def _combine_kernel(x_ref, y_ref, route_ref, o_ref):
    o_ref[...] = _combine_tile(x_ref, y_ref, route_ref)


def _combine(x2, y, route, tm=512):
    n, d = x2.shape
    row = pl.BlockSpec((tm, d), lambda i: (i, 0))
    return pl.pallas_call(
        _combine_kernel, grid=(n // tm,),
        in_specs=[row, pl.BlockSpec((2, 2, tm, PACK_W), lambda i: (0, 0, i, 0)),
                  pl.BlockSpec((8, tm), lambda i: (0, i))],
        out_specs=row,
        out_shape=jax.ShapeDtypeStruct((n, d), _F32),
        compiler_params=_params("parallel"),
        name="moe_combine",
    )(x2, y, route)


def _slot_kernel(offset_ref, route_ref, slot_ref, *, n_slots):
    expert = route_ref[0:2, :].astype(jnp.int32)
    pos = route_ref[4:6, :].astype(jnp.int32)
    for e in range(N_EXPERTS):
        pos = pos + jnp.where(expert == e, offset_ref[e], 0)
    slot_ref[0:2, :] = pos
    slot_ref[2:4, :] = pos + n_slots


def _slots(offset, route, n_slots):
    n = route.shape[1]
    return pl.pallas_call(
        functools.partial(_slot_kernel, n_slots=n_slots),
        in_specs=[pl.BlockSpec(memory_space=pltpu.SMEM),
                  pl.BlockSpec((8, n), lambda: (0, 0))],
        out_specs=pl.BlockSpec((4, n), lambda: (0, 0)),
        out_shape=jax.ShapeDtypeStruct((4, n), jnp.int32),
        name="moe_slots",
    )(offset, route)


def _router_params(wg, bg, we, be):
    d = wg.shape[0]
    w_router = jnp.zeros((d, LANES), _F32)
    w_router = w_router.at[:, :MOE_GROUPS].set(wg)
    w_router = w_router.at[:, ROUTE_E0:ROUTE_E0 + N_EXPERTS].set(
        jnp.moveaxis(we, 0, 1).reshape(d, N_EXPERTS))
    b_router = jnp.zeros((1, LANES), _F32)
    b_router = b_router.at[0, :MOE_GROUPS].set(bg)
    b_router = b_router.at[0, ROUTE_E0:ROUTE_E0 + N_EXPERTS].set(be.reshape(-1))
    return w_router, b_router


def _moe_experts(h, route, cnt, w_gate, w_up, w_down, layer):
    n = h.shape[1]
    d = w_gate.shape[-2]
    tm = MOE_TM
    counts = cnt[ROUTE_E0:ROUTE_E0 + N_EXPERTS, 0].astype(jnp.int32)
    tiles_per = (counts + tm - 1) // tm
    tiles_end = jnp.cumsum(tiles_per)
    offset = (tiles_end - tiles_per) * tm
    n_tiles = (2 * n) // tm + N_EXPERTS
    n_valid = tiles_end[-1]
    tile_ids = jnp.minimum(jnp.arange(n_tiles, dtype=jnp.int32), n_valid - 1)
    tile_expert = jnp.sum(tile_ids[:, None] >= tiles_end[None, :], axis=1).astype(jnp.int32)
    n_slots = n_tiles * tm
    slot = _slots(offset, route, n_slots).reshape(-1)

    xs = _sc_scatter(h.reshape(2 * n, PACK_W), slot, 2 * n_slots, reps=2)
    ys = _experts(tile_expert + layer * N_EXPERTS, n_valid.reshape(1),
                  xs.reshape(2, n_slots, PACK_W), w_gate.reshape(-1, d, MOE_D_FF),
                  w_up.reshape(-1, d, MOE_D_FF), w_down.reshape(-1, MOE_D_FF, d))
    y = _sc_gather(ys.reshape(2 * n_slots, PACK_W), slot)
    return y.reshape(2, 2, n, PACK_W)


def _proj_prep_pending(x2, pending, *args, **kwargs):
    outs = _proj_prep(x2, *args, pending=pending, **kwargs)
    return (x2, outs) if pending is None else (outs[0], outs[1:])


def _mixer_ab(x2, pending, router, bsz, seq, g, w_in, a_qn, a_kn, b_qn, b_kn, b_sink, w_out):
    aw = A_HEADS * HEAD_DIM
    bqw = B_HEADS * HEAD_DIM
    bkw = B_KV_HEADS * HEAD_DIM
    scale2 = QK_SCALE * LOG2E
    segs = [_Seg(0, aw, True, False, scale2, "plain"),
            _Seg(aw, aw, True, False, 1.0, "T"),
            _Seg(2 * aw, aw, False, False, 1.0, "heads_ones"),
            _Seg(3 * aw, bqw, True, False, scale2, "plain"),
            _Seg(3 * aw + bqw, bkw, True, False, 1.0, "T"),
            _Seg(3 * aw + bqw + bkw, bkw, False, False, 1.0, "heads_ones")]
    x2, (qa, ka, va, qb, kb, vb) = _proj_prep_pending(
        x2, pending, bsz, seq, g, w_in.astype(_BF16), segs, [a_qn, a_kn, None, b_qn, b_kn, None])

    bound_a = HEAD_DIM * scale2 * jnp.max(jnp.abs(a_qn)) * jnp.max(jnp.abs(a_kn))
    bound_b = HEAD_DIM * scale2 * jnp.max(jnp.abs(b_qn)) * jnp.max(jnp.abs(b_kn))
    static_ok = jnp.maximum(bound_a, bound_b) <= SHIFT_MAX
    shift_a = jnp.where(static_ok, bound_a, 0.0)
    shift_b = jnp.where(static_ok, bound_b, 0.0)
    slopes = _alibi_slopes(A_HEADS + B_HEADS)
    bias_a = _dil_bias(slopes[0::2], shift_a)
    bias_b = _band_bias(B_WINDOW, 1.0, slopes[1::2], shift_b, B_HEADS // B_KV_HEADS)
    sink2 = b_sink.astype(_F32) * LOG2E - shift_b

    def attend(online, qa, ka, va, qb, kb, vb, bias_a, bias_b, sink2):
        return (_dil_attn(qa, ka, va, bias_a, online=online),
                _band_b(qb, kb, vb, bias_b, sink2, online=online))

    oa, ob = lax.cond(static_ok, functools.partial(attend, False), functools.partial(attend, True),
                      qa, ka, va, qb, kb, vb, bias_a, bias_b, sink2)
    w_out = w_out.astype(_BF16)
    return _out_router(x2, [oa.reshape(-1, aw), ob.reshape(-1, bqw)], [w_out[:aw], w_out[aw:]],
                       *router)


def _mixer_c(x2, pending, router, bsz, seq, g, w_in, qn, kn, w_out):
    qw = C_HEADS * HEAD_DIM
    kvw = C_KV_HEADS * HEAD_DIM
    rope = _rope_tables(seq)
    bound = HEAD_DIM * QK_SCALE * LOG2E * jnp.max(jnp.abs(qn)) * jnp.max(jnp.abs(kn))
    static_ok = bound <= SHIFT_MAX
    shift = jnp.where(static_ok, bound, 0.0)
    segs = [_Seg(0, qw, True, True, QK_SCALE * LOG2E, "T_one"),
            _Seg(qw, kvw, True, True, 1.0, "heads_shift"),
            _Seg(qw + kvw, kvw, False, False, 1.0, "T_ones")]
    x2, (qt, k, vt) = _proj_prep_pending(x2, pending, bsz, seq, g, w_in.astype(_BF16), segs,
                                         [qn, kn, None], rope=rope, shift=shift)
    o = lax.cond(static_ok,
                 functools.partial(_dense_attn, online=False),
                 functools.partial(_dense_attn, online=True), qt, k, vt)
    return _out_router(x2, [o.reshape(-1, qw)], [w_out.astype(_BF16)], *router)


def kernel(x, mix_norm, ffn_norm, ab_w_in, a_q_norm, a_k_norm, b_q_norm, b_k_norm, b_sink, ab_w_out,
           c_w_in, c_q_norm, c_k_norm, c_w_out, moe_group_w, moe_group_b, moe_expert_w, moe_expert_b,
           moe_w_gate, moe_w_up, moe_w_down):
    bsz, seq, d = x.shape
    x2 = x.reshape(bsz * seq, d)
    depth = mix_norm.shape[0]
    pending = None
    for layer in range(depth):
        i = layer // 2
        router = (ffn_norm[layer],) + _router_params(moe_group_w[layer], moe_group_b[layer],
                                                     moe_expert_w[layer], moe_expert_b[layer])
        if layer % 2 == 0:
            x2, h, route, cnt = _mixer_ab(x2, pending, router, bsz, seq, mix_norm[layer], ab_w_in[i],
                                          a_q_norm[i], a_k_norm[i], b_q_norm[i], b_k_norm[i],
                                          b_sink[i], ab_w_out[i])
        else:
            x2, h, route, cnt = _mixer_c(x2, pending, router, bsz, seq, mix_norm[layer], c_w_in[i],
                                         c_q_norm[i], c_k_norm[i], c_w_out[i])
        pending = (_moe_experts(h, route, cnt, moe_w_gate, moe_w_up, moe_w_down, layer), route)
    return _combine(x2, *pending).reshape(bsz, seq, d)
```
